```python
import jax, jax.numpy as jnp
from jax import lax
import numpy as np

D_MODEL = 1024
BATCH = 32
SEQ = 256
DEPTH = 1
DEC_BATCH = 8
DEC_SEQ = 2048
PAST_LEN = 512

GRID_W = 64
N_RET_HEADS = 8
RET_DK = 64
RET_DV = 128
D_RET_QK = N_RET_HEADS * RET_DK
D_RET_V = N_RET_HEADS * RET_DV
RET_CHUNK = 128
D_CONV = 1024
CONV_W = 3
N_EXPERTS = 16
EC_CAPACITY_FACTOR = 2
EXPERT_FF = 1024
RMS_EPS = 1e-6
GN_EPS = 1e-5
ROPE_BASE = 10000.0
SPLIT_POINTS = (
    D_RET_QK,
    2 * D_RET_QK,
    2 * D_RET_QK + D_RET_V,
    2 * D_RET_QK + 2 * D_RET_V,
    2 * D_RET_QK + 2 * D_RET_V + D_CONV,
    2 * D_RET_QK + 2 * D_RET_V + 2 * D_CONV,
    2 * D_RET_QK + 2 * D_RET_V + 3 * D_CONV,
    2 * D_RET_QK + 2 * D_RET_V + 3 * D_CONV + D_MODEL,
)
D_IN_TOTAL = 2 * D_RET_QK + 2 * D_RET_V + 3 * D_CONV + 2 * D_MODEL

kernel_name = 'hybrid_retention_shortconv_ecmoe_diffusion_step'


def rmsnorm(x, g):
    x32 = x.astype(jnp.float32)
    y = x32 * lax.rsqrt(jnp.mean(x32 * x32, axis=-1, keepdims=True) + RMS_EPS)
    return (y * g.astype(jnp.float32)).astype(x.dtype)


def rot_half(x, ang):
    x1, x2 = jnp.split(x, 2, axis=-1)
    cos, sin = jnp.cos(ang), jnp.sin(ang)
    return jnp.concatenate([x1 * cos - x2 * sin, x1 * sin + x2 * cos], axis=-1)


def rope_2d(x):
    l = x.shape[2]
    rows = l // GRID_W
    r_idx = jnp.broadcast_to(jnp.arange(rows, dtype=jnp.float32)[:, None], (rows, GRID_W)).reshape(-1)
    c_idx = jnp.broadcast_to(jnp.arange(GRID_W, dtype=jnp.float32)[None, :], (rows, GRID_W)).reshape(-1)
    n_freq = RET_DK // 4
    inv = ROPE_BASE ** (-jnp.arange(n_freq, dtype=jnp.float32) / n_freq)
    xr, xc = jnp.split(x.astype(jnp.float32), 2, axis=-1)
    out = jnp.concatenate([rot_half(xr, r_idx[:, None] * inv), rot_half(xc, c_idx[:, None] * inv)], axis=-1)
    return out.astype(x.dtype)


def retention_chunkwise(q, k, v, log_gamma, s0):
    b, h, l, dk = q.shape
    dv = v.shape[-1]
    nc = l // RET_CHUNK
    qc = q.astype(jnp.float32).reshape(b, h, nc, RET_CHUNK, dk)
    kc = k.astype(jnp.float32).reshape(b, h, nc, RET_CHUNK, dk)
    vc = v.astype(jnp.float32).reshape(b, h, nc, RET_CHUNK, dv)
    pos = jnp.arange(RET_CHUNK, dtype=jnp.float32)
    lg = log_gamma[:, None]
    diff = pos[:, None] - pos[None, :]
    decay = jnp.where(diff[None] >= 0, jnp.exp(lg[:, :, None] * jnp.maximum(diff, 0.0)[None]), 0.0)
    xi = jnp.exp(lg * (pos + 1.0))
    zeta = jnp.exp(lg * (RET_CHUNK - 1.0 - pos))
    chunk_decay = jnp.exp(log_gamma * RET_CHUNK)
    scores = jnp.einsum('bhnid,bhnjd->bhnij', qc, kc) * decay[:, None]
    o_intra = jnp.einsum('bhnij,bhnje->bhnie', scores, vc)
    kv = jnp.einsum('bhnjd,bhnje->nbhde', kc * zeta[:, None, :, None], vc)

    def step(r, kv_n):
        return chunk_decay[:, None, None] * r + kv_n, r

    s_fin, r_prev = lax.scan(step, s0.astype(jnp.float32), kv)
    o_cross = jnp.einsum('bhnid,nbhde->bhnie', qc * xi[:, None, :, None], r_prev)
    return (o_intra + o_cross).reshape(b, h, l, dv), s_fin


def bidir_retention(q, k, v, lg_f, lg_b, s0_f, s0_b):
    o_f, s_f = retention_chunkwise(q, k, v, lg_f, s0_f)
    o_b, s_b = retention_chunkwise(jnp.flip(q, 2), jnp.flip(k, 2), jnp.flip(v, 2), lg_b, s0_b)
    return o_f + jnp.flip(o_b, 2), s_f, s_b


def head_norm(o, g):
    mu = jnp.mean(o, axis=-1, keepdims=True)
    var = jnp.mean(jnp.square(o - mu), axis=-1, keepdims=True)
    y = (o - mu) * lax.rsqrt(var + GN_EPS)
    b, h, l, dv = o.shape
    return y.transpose(0, 2, 1, 3).reshape(b, l, h * dv) * g.astype(jnp.float32)


def short_conv3(u, w):
    up = jnp.pad(u, ((0, 0), (1, 1), (0, 0)))
    return up[:, :-2] * w[0] + up[:, 1:-1] * w[1] + up[:, 2:] * w[2]


def mixer(h, s0_f, s0_b, use_rope, w_in, dec_f, dec_b, gn_g, conv_w, w_ret_o, w_conv_o, w_o):
    b, l, _ = h.shape
    proj = h @ w_in
    q, k, v, g_ret, cb, cc, cx, ga, gb = jnp.split(proj, SPLIT_POINTS, axis=-1)
    q = q.reshape(b, l, N_RET_HEADS, RET_DK).transpose(0, 2, 1, 3)
    k = k.reshape(b, l, N_RET_HEADS, RET_DK).transpose(0, 2, 1, 3) * (RET_DK ** -0.5)
    v = v.reshape(b, l, N_RET_HEADS, RET_DV).transpose(0, 2, 1, 3)
    if use_rope:
        q = rope_2d(q)
        k = rope_2d(k)
    lg_f = -jax.nn.softplus(-dec_f.astype(jnp.float32))
    lg_b = -jax.nn.softplus(-dec_b.astype(jnp.float32))
    o, s_f, s_b = bidir_retention(q, k, v, lg_f, lg_b, s0_f, s0_b)
    o = head_norm(o, gn_g).astype(h.dtype)
    y_ret = (jax.nn.silu(g_ret) * o) @ w_ret_o
    u = short_conv3(cc * cx, conv_w)
    y_conv = (cb * u) @ w_conv_o
    merged = jax.nn.sigmoid(ga) * y_ret + jax.nn.sigmoid(gb) * y_conv
    return merged @ w_o, s_f, s_b


def expert_choice_ffn(h, w_router, w_gate, w_up, w_down):
    b, l, d = h.shape
    t = b * l
    cap = EC_CAPACITY_FACTOR * t // N_EXPERTS
    hf = h.reshape(t, d)
    aff = jax.nn.softmax((hf @ w_router).astype(jnp.float32), axis=-1)
    vals, idx = lax.top_k(aff.T, cap)
    xs = hf[idx]
    hid = jax.nn.silu(jnp.einsum('ecd,edf->ecf', xs, w_gate)) * jnp.einsum('ecd,edf->ecf', xs, w_up)
    out = jnp.einsum('ecf,efd->ecd', hid, w_down) * vals[..., None].astype(h.dtype)
    y = jnp.zeros_like(hf).at[idx.reshape(-1)].add(out.reshape(-1, d))
    return y.reshape(b, l, d)


def trunk_layer(x, cond, s0_f, s0_b, use_rope, w_ada, b_ada, n_pre_mix, n_post_mix, n_pre_ffn, n_post_ffn,
                w_in, dec_f, dec_b, gn_g, conv_w, w_ret_o, w_conv_o, w_o, w_router, w_gate, w_up, w_down):
    mod = (jax.nn.silu(cond) @ w_ada + b_ada)[:, None, :]
    sh1, sc1, g1, sh2, sc2, g2 = jnp.split(mod, 6, axis=-1)
    h = rmsnorm(x, n_pre_mix) * (1.0 + sc1) + sh1
    m, s_f, s_b = mixer(h, s0_f, s0_b, use_rope, w_in, dec_f, dec_b, gn_g, conv_w, w_ret_o, w_conv_o, w_o)
    x = x + g1 * rmsnorm(m, n_post_mix)
    h = rmsnorm(x, n_pre_ffn) * (1.0 + sc2) + sh2
    f = expert_choice_ffn(h, w_router, w_gate, w_up, w_down)
    x = x + g2 * rmsnorm(f, n_post_ffn)
    return x, s_f, s_b


def setup_inputs(seed: int = 0) -> dict:
    key = jax.random.key(seed)
    ks = jax.random.split(key, 26)
    f32 = jnp.float32

    def nrm(k, shape, scale):
        return jax.random.normal(k, shape, f32) * scale

    base_logit = jnp.asarray(np.log(2.0 ** (5 + np.arange(N_RET_HEADS)) - 1.0), f32)
    st_shape = (DEC_BATCH, DEPTH, N_RET_HEADS, RET_DK, RET_DV)
    return {
        'x_prompt': nrm(ks[0], (BATCH, SEQ, D_MODEL), 1.0),
        'x_sample': nrm(ks[1], (DEC_BATCH, DEC_SEQ, D_MODEL), 1.0),
        'state_ret_fwd': nrm(ks[2], st_shape, 1.0),
        'state_ret_bwd': nrm(ks[3], st_shape, 1.0),
        'c': nrm(ks[4], (DEC_BATCH, D_MODEL), 1.0),
        'c_ctx': nrm(ks[5], (D_MODEL,), 1.0),
        'w_ada': nrm(ks[6], (DEPTH, D_MODEL, 6 * D_MODEL), 0.5 * D_MODEL ** -0.5),
        'b_ada': nrm(ks[7], (DEPTH, 6 * D_MODEL), 0.02),
        'norm_pre_mix': 1.0 + nrm(ks[8], (DEPTH, D_MODEL), 0.05),
        'norm_post_mix': 1.0 + nrm(ks[9], (DEPTH, D_MODEL), 0.05),
        'norm_pre_ffn': 1.0 + nrm(ks[10], (DEPTH, D_MODEL), 0.05),
        'norm_post_ffn': 1.0 + nrm(ks[11], (DEPTH, D_MODEL), 0.05),
        'w_in': nrm(ks[12], (DEPTH, D_MODEL, D_IN_TOTAL), D_MODEL ** -0.5),
        'ret_decay_fwd': base_logit + nrm(ks[13], (DEPTH, N_RET_HEADS), 0.05),
        'ret_decay_bwd': base_logit + nrm(ks[14], (DEPTH, N_RET_HEADS), 0.05),
        'ret_norm_g': 1.0 + nrm(ks[15], (DEPTH, D_RET_V), 0.05),
        'conv_w': nrm(ks[16], (DEPTH, CONV_W, D_CONV), CONV_W ** -0.5),
        'w_ret_o': nrm(ks[17], (DEPTH, D_RET_V, D_MODEL), D_RET_V ** -0.5),
        'w_conv_o': nrm(ks[18], (DEPTH, D_CONV, D_MODEL), D_CONV ** -0.5),
        'w_o': nrm(ks[19], (DEPTH, D_MODEL, D_MODEL), D_MODEL ** -0.5),
        'w_router': nrm(ks[20], (DEPTH, D_MODEL, N_EXPERTS), D_MODEL ** -0.5),
        'w_gate': nrm(ks[21], (DEPTH, N_EXPERTS, D_MODEL, EXPERT_FF), D_MODEL ** -0.5),
        'w_up': nrm(ks[22], (DEPTH, N_EXPERTS, D_MODEL, EXPERT_FF), D_MODEL ** -0.5),
        'w_down': nrm(ks[23], (DEPTH, N_EXPERTS, EXPERT_FF, D_MODEL), EXPERT_FF ** -0.5),
    }


def reference(x_prompt, x_sample, state_ret_fwd, state_ret_bwd, c, c_ctx, w_ada, b_ada,
              norm_pre_mix, norm_post_mix, norm_pre_ffn, norm_post_ffn, w_in, ret_decay_fwd, ret_decay_bwd,
              ret_norm_g, conv_w, w_ret_o, w_conv_o, w_o, w_router, w_gate, w_up, w_down):
    xp = x_prompt
    xs = x_sample
    new_f = []
    new_b = []
    for l in range(DEPTH):
        lw = (w_ada[l], b_ada[l], norm_pre_mix[l], norm_post_mix[l], norm_pre_ffn[l], norm_post_ffn[l],
              w_in[l], ret_decay_fwd[l], ret_decay_bwd[l], ret_norm_g[l], conv_w[l], w_ret_o[l], w_conv_o[l],
              w_o[l], w_router[l], w_gate[l], w_up[l], w_down[l])
        zeros = jnp.zeros((xp.shape[0], N_RET_HEADS, RET_DK, RET_DV), jnp.float32)
        xp, s_f, s_b = trunk_layer(xp, c_ctx[None, :], zeros, zeros, False, *lw)
        new_f.append(s_f.astype(x_prompt.dtype))
        new_b.append(s_b.astype(x_prompt.dtype))
        xs, _, _ = trunk_layer(xs, c, state_ret_fwd[:, l], state_ret_bwd[:, l], True, *lw)
    new_state_ret_fwd = jnp.stack(new_f, axis=1)
    new_state_ret_bwd = jnp.stack(new_b, axis=1)
    return (xp, xs, new_state_ret_fwd, new_state_ret_bwd)
```

```python
import functools

import jax
import jax.numpy as jnp
import numpy as np
from jax import lax
from jax.experimental import pallas as pl
from jax.experimental.pallas import tpu as pltpu

F32 = jnp.float32
BF16 = jnp.bfloat16

D_MODEL = 1024
N_HEADS = 8
DK = 64
DV = 128
CHUNK = 128
GRID_W = 64
N_EXPERTS = 16
CAPACITY_FACTOR = 2
D_IN_TOTAL = 8192
RMS_EPS = 1e-6
GN_EPS = 1e-5
ROPE_BASE = 10000.0
N_COND_ROWS = 16
CTX_COND_ROW = 8
ROUTER_LANES = 128
VMEM_LIMIT = 48 * 1024 * 1024


def _sigmoid(x):
    return 1.0 / (1.0 + jnp.exp(-x))


def _rms(x, g):
    return x * lax.rsqrt(jnp.mean(x * x, axis=-1, keepdims=True) + RMS_EPS) * g


def _ada_kernel(c_ref, w_ref, b_ref, o_ref):
    c = c_ref[...]
    s = c * _sigmoid(c)
    o_ref[...] = jnp.dot(s, w_ref[...], preferred_element_type=F32,
                         precision=lax.Precision.HIGHEST) + b_ref[...]


def _ada(cond, w_ada, b_ada):
    n = w_ada.shape[1]
    tn = 1024
    return pl.pallas_call(
        _ada_kernel,
        grid=(n // tn,),
        in_specs=[pl.BlockSpec((N_COND_ROWS, D_MODEL), lambda j: (0, 0)),
                  pl.BlockSpec((D_MODEL, tn), lambda j: (0, j)),
                  pl.BlockSpec((1, tn), lambda j: (0, j))],
        out_specs=pl.BlockSpec((N_COND_ROWS, tn), lambda j: (0, j)),
        out_shape=jax.ShapeDtypeStruct((N_COND_ROWS, n), F32),
        compiler_params=pltpu.CompilerParams(dimension_semantics=("arbitrary",),
                                             vmem_limit_bytes=VMEM_LIMIT),
        name="ada_mod",
    )(cond, w_ada, b_ada)


def _inproj_kernel(x_ref, g_ref, sh_ref, sc_ref, w_ref, o_ref, h_ref):
    @pl.when(pl.program_id(1) == 0)
    def _():
        h = _rms(x_ref[...], g_ref[...]) * (1.0 + sc_ref[0]) + sh_ref[0]
        h_ref[...] = h.astype(BF16)

    o_ref[...] = jnp.dot(h_ref[...], w_ref[...], preferred_element_type=F32).astype(BF16)


def _inproj(x, mod3, g, w_bf16, cond_row_fn):
    t = x.shape[0]
    tm, tn = 1024, 2048
    return pl.pallas_call(
        _inproj_kernel,
        grid=(t // tm, D_IN_TOTAL // tn),
        in_specs=[pl.BlockSpec((tm, D_MODEL), lambda i, j: (i, 0)),
                  pl.BlockSpec((1, D_MODEL), lambda i, j: (0, 0)),
                  pl.BlockSpec((1, 1, D_MODEL), lambda i, j: (cond_row_fn(i * tm), 0, 0)),
                  pl.BlockSpec((1, 1, D_MODEL), lambda i, j: (cond_row_fn(i * tm), 0, 1)),
                  pl.BlockSpec((D_MODEL, tn), lambda i, j: (0, j))],
        out_specs=pl.BlockSpec((tm, tn), lambda i, j: (i, j)),
        out_shape=jax.ShapeDtypeStruct((t, D_IN_TOTAL), BF16),
        scratch_shapes=[pltpu.VMEM((tm, D_MODEL), BF16)],
        compiler_params=pltpu.CompilerParams(dimension_semantics=("arbitrary", "arbitrary"),
                                             vmem_limit_bytes=VMEM_LIMIT),
        name="inproj",
    )(x, g, mod3, mod3, w_bf16)


def _ret_kernel(*refs, nc, use_rope, has_init, emit_state):
    it = iter(refs)
    lg_ref = next(it)
    q_ref, k_ref, v_ref, gr_ref, gn_ref = (next(it) for _ in range(5))
    cos_ref = sin_ref = s0f_ref = s0b_ref = sf_ref = sb_ref = None
    if use_rope:
        cos_ref, sin_ref = next(it), next(it)
    if has_init:
        s0f_ref, s0b_ref = next(it), next(it)
    o_ref = next(it)
    if emit_state:
        sf_ref, sb_ref = next(it), next(it)
    qr, kr, rf_all, r0 = (next(it) for _ in range(4))

    p = pl.program_id(1)
    lgf_a, lgf_b = lg_ref[0, 2 * p], lg_ref[0, 2 * p + 1]
    lgb_a, lgb_b = lg_ref[1, 2 * p], lg_ref[1, 2 * p + 1]

    ri = lax.broadcasted_iota(jnp.int32, (CHUNK, 2 * DK), 0)
    ci = lax.broadcasted_iota(jnp.int32, (CHUNK, 2 * DK), 1)
    lane_a = ci < DK
    rowf = ri.astype(F32)
    diff = rowf - ci.astype(F32)
    lgf_lane = jnp.where(lane_a, lgf_a, lgf_b)
    lgb_lane = jnp.where(lane_a, lgb_a, lgb_b)
    xi_f = jnp.exp(lgf_lane * (rowf + 1.0))
    xi_b = jnp.exp(lgb_lane * (CHUNK - rowf))
    zeta_f = jnp.exp(lgf_lane * (CHUNK - 1.0 - rowf))
    zeta_b = jnp.exp(lgb_lane * rowf)

    def decay_matrix(lgf, lgb):
        return jnp.where(diff > 0, jnp.exp(lgf * diff),
                         jnp.where(diff < 0, jnp.exp(lgb * (-diff)), 2.0))

    dm_a = decay_matrix(lgf_a, lgb_a)
    dm_b = decay_matrix(lgf_b, lgb_b)

    r2 = lax.broadcasted_iota(jnp.int32, (2 * DK, 2 * DV), 0)
    c2 = lax.broadcasted_iota(jnp.int32, (2 * DK, 2 * DV), 1)
    top = r2 < DK
    blk = (top == (c2 < DV)).astype(F32)
    cd_f = jnp.exp(jnp.where(top, lgf_a, lgf_b) * float(CHUNK)) * blk
    cd_b = jnp.exp(jnp.where(top, lgb_a, lgb_b) * float(CHUNK)) * blk

    def rope(x, rows):
        fwd = pltpu.roll(x, 2 * DK - 16, 1)
        bwd = pltpu.roll(x, 16, 1)
        swapped = jnp.where((ci & 31) < 16, fwd, bwd)
        return x * cos_ref[rows, :] + swapped * sin_ref[rows, :]

    def prep(n, carry):
        rows = pl.ds(pl.multiple_of(n * CHUNK, CHUNK), CHUNK)
        q = q_ref[rows, :].astype(F32)
        k = k_ref[rows, :].astype(F32) * (DK ** -0.5)
        if use_rope:
            q = rope(q, rows)
            k = rope(k, rows)
        qr[rows, :] = q
        kr[rows, :] = k
        return carry

    lax.fori_loop(0, nc, prep, 0)

    def load_state(s_ref):
        r0[...] = jnp.zeros_like(r0)
        r0[0:DK, 0:DV] = s_ref[0, 0, 0].astype(F32)
        r0[DK:2 * DK, DV:2 * DV] = s_ref[0, 0, 1].astype(F32)
        return r0[...]

    def kv_update(k, zeta, v):
        kz_t = (k * zeta).T.astype(BF16)
        return jnp.dot(kz_t, v, preferred_element_type=F32) * blk

    def fwd_body(n, rf):
        rows = pl.ds(pl.multiple_of(n * CHUNK, CHUNK), CHUNK)
        rf_all[n] = rf.astype(BF16)
        return cd_f * rf + kv_update(kr[rows, :], zeta_f, v_ref[rows, :])

    rf0 = load_state(s0f_ref) if has_init else jnp.zeros((2 * DK, 2 * DV), F32)
    rf_fin = lax.fori_loop(0, nc, fwd_body, rf0)

    gn = gn_ref[...]

    def head_norm(o):
        mu = jnp.mean(o, axis=-1, keepdims=True)
        d = o - mu
        var = jnp.mean(d * d, axis=-1, keepdims=True)
        return d * lax.rsqrt(var + GN_EPS)

    def bwd_body(t, rb):
        n = nc - 1 - t
        rows = pl.ds(pl.multiple_of(n * CHUNK, CHUNK), CHUNK)
        q = qr[rows, :]
        k = kr[rows, :]
        v = v_ref[rows, :]
        q16 = q.astype(BF16)
        k16 = k.astype(BF16)
        zero = jnp.zeros_like(q16)
        dn = (((1,), (1,)), ((), ()))
        s_a = lax.dot_general(jnp.where(lane_a, q16, zero), k16, dn, preferred_element_type=F32)
        s_b = lax.dot_general(jnp.where(lane_a, zero, q16), k16, dn, preferred_element_type=F32)
        o_a = jnp.dot((s_a * dm_a).astype(BF16), v[:, 0:DV], preferred_element_type=F32)
        o_b = jnp.dot((s_b * dm_b).astype(BF16), v[:, DV:2 * DV], preferred_element_type=F32)
        oc = jnp.dot((q * xi_f).astype(BF16), rf_all[n], preferred_element_type=F32)
        oc = oc + jnp.dot((q * xi_b).astype(BF16), rb.astype(BF16), preferred_element_type=F32)
        y = jnp.concatenate([head_norm(o_a + oc[:, 0:DV]), head_norm(o_b + oc[:, DV:2 * DV])], axis=1)
        g = gr_ref[rows, :].astype(F32)
        o_ref[rows, :] = (g * _sigmoid(g) * (y * gn)).astype(BF16)
        return cd_b * rb + kv_update(k, zeta_b, v)

    rb0 = load_state(s0b_ref) if has_init else jnp.zeros((2 * DK, 2 * DV), F32)
    rb_fin = lax.fori_loop(0, nc, bwd_body, rb0)

    if emit_state:
        for s_ref, r in ((sf_ref, rf_fin), (sb_ref, rb_fin)):
            s_ref[0, 0, 0] = r[0:DK, 0:DV]
            s_ref[0, 0, 1] = r[DK:2 * DK, DV:2 * DV]


def _retention(proj, lgs, gn_g, seq, *, rope_tabs=None, init=None, emit_state):
    t = proj.shape[0]
    b = t // seq
    nc = seq // CHUNK
    use_rope = rope_tabs is not None
    has_init = init is not None
    pairs = N_HEADS // 2
    qk_blocks = (N_HEADS * DK) // (2 * DK)
    in_specs = [pl.BlockSpec(memory_space=pltpu.SMEM),
                pl.BlockSpec((seq, 2 * DK), lambda i, p: (i, p)),
                pl.BlockSpec((seq, 2 * DK), lambda i, p: (i, qk_blocks + p)),
                pl.BlockSpec((seq, 2 * DV), lambda i, p: (i, qk_blocks + p)),
                pl.BlockSpec((seq, 2 * DV), lambda i, p: (i, 2 * qk_blocks + p)),
                pl.BlockSpec((1, 2 * DV), lambda i, p: (0, p))]
    args = [lgs, proj, proj, proj, proj, gn_g]
    if use_rope:
        in_specs += [pl.BlockSpec((seq, 2 * DK), lambda i, p: (0, 0))] * 2
        args += list(rope_tabs)
    state_spec = pl.BlockSpec((1, 1, 2, DK, DV), lambda i, p: (i, 0, p, 0, 0))
    if has_init:
        in_specs += [state_spec, state_spec]
        args += list(init)
    out_specs = [pl.BlockSpec((seq, 2 * DV), lambda i, p: (i, p))]
    out_shape = [jax.ShapeDtypeStruct((t, N_HEADS * DV), BF16)]
    if emit_state:
        out_specs += [state_spec, state_spec]
        out_shape += [jax.ShapeDtypeStruct((b, 1, N_HEADS, DK, DV), F32)] * 2
    return pl.pallas_call(
        functools.partial(_ret_kernel, nc=nc, use_rope=use_rope, has_init=has_init, emit_state=emit_state),
        grid=(b, pairs),
        in_specs=in_specs,
        out_specs=out_specs,
        out_shape=out_shape,
        scratch_shapes=[pltpu.VMEM((seq, 2 * DK), F32), pltpu.VMEM((seq, 2 * DK), F32),
                        pltpu.VMEM((nc, 2 * DK, 2 * DV), BF16), pltpu.VMEM((2 * DK, 2 * DV), F32)],
        compiler_params=pltpu.CompilerParams(dimension_semantics=("arbitrary", "arbitrary"),
                                             vmem_limit_bytes=VMEM_LIMIT),
        name="retention",
    )(*args)


def _rope_tables(seq):
    n_freq = DK // 4
    pos = np.arange(seq)
    inv = jnp.asarray(ROPE_BASE, F32) ** (-jnp.arange(n_freq, dtype=F32) / n_freq)
    ang_r = jnp.asarray(pos // GRID_W, F32)[:, None] * inv
    ang_c = jnp.asarray(pos % GRID_W, F32)[:, None] * inv
    cos = jnp.concatenate([jnp.cos(ang_r)] * 2 + [jnp.cos(ang_c)] * 2, axis=1)
    sin = jnp.concatenate([-jnp.sin(ang_r), jnp.sin(ang_r), -jnp.sin(ang_c), jnp.sin(ang_c)], axis=1)
    return jnp.tile(cos, (1, 2)), jnp.tile(sin, (1, 2))


def _mix_kernel(og_ref, cb_ref, cc_ref, cx_ref, ccp_ref, cxp_ref, ccn_ref, cxn_ref, ga_ref, gb_ref, x_ref,
                cw_ref, wro_ref, wco_ref, wo_ref, wr_ref, npost_ref, npre_ref, g1_ref, sh2_ref, sc2_ref,
                x1_ref, h2_ref, aff_ref, *, tm, seq):
    i = pl.program_id(0)
    has_prev = jnp.where((i * tm) % seq != 0, 1.0, 0.0)
    has_next = jnp.where(((i + 1) * tm) % seq != 0, 1.0, 0.0)
    halo = ccp_ref.shape[0]
    prod = cc_ref[...].astype(F32) * cx_ref[...].astype(F32)
    prev_row = ccp_ref[halo - 1:halo, :].astype(F32) * cxp_ref[halo - 1:halo, :].astype(F32) * has_prev
    next_row = ccn_ref[0:1, :].astype(F32) * cxn_ref[0:1, :].astype(F32) * has_next
    row = lax.broadcasted_iota(jnp.int32, (tm, 1), 0)
    up = jnp.where(row == 0, prev_row, pltpu.roll(prod, 1, 0))
    dn = jnp.where(row == tm - 1, next_row, pltpu.roll(prod, tm - 1, 0))
    u = up * cw_ref[0:1, :] + prod * cw_ref[1:2, :] + dn * cw_ref[2:3, :]
    y_conv = jnp.dot((cb_ref[...].astype(F32) * u).astype(BF16), wco_ref[...], preferred_element_type=F32)
    y_ret = jnp.dot(og_ref[...], wro_ref[...], preferred_element_type=F32)
    merged = _sigmoid(ga_ref[...].astype(F32)) * y_ret + _sigmoid(gb_ref[...].astype(F32)) * y_conv
    m = jnp.dot(merged.astype(BF16), wo_ref[...], preferred_element_type=F32)
    x1 = x_ref[...] + g1_ref[0] * _rms(m, npost_ref[...])
    x1_ref[...] = x1
    h2 = _rms(x1, npre_ref[...]) * (1.0 + sc2_ref[0]) + sh2_ref[0]
    h2_ref[...] = h2.astype(BF16)
    h_hi = h2.astype(BF16)
    h_lo = (h2 - h_hi.astype(F32)).astype(BF16)
    wr = wr_ref[...]
    w_hi = wr.astype(BF16)
    w_lo = (wr - w_hi.astype(F32)).astype(BF16)
    logits = (jnp.dot(h_hi, w_hi, preferred_element_type=F32)
              + jnp.dot(h_lo, w_hi, preferred_element_type=F32)
              + jnp.dot(h_hi, w_lo, preferred_element_type=F32))
    lane = lax.broadcasted_iota(jnp.int32, (tm, ROUTER_LANES), 1)
    logits = jnp.where(lane < N_EXPERTS, logits, -jnp.inf)
    e = jnp.exp(logits - jnp.max(logits, axis=-1, keepdims=True))
    aff_ref[...] = e / jnp.sum(e, axis=-1, keepdims=True)


def _mix(og, proj, x, mod3, conv_w, wro, wco, wo, wr_pad, npost, npre, seq, cond_row_fn):
    t = x.shape[0]
    tm = 256
    halo = 16
    hb = tm // halo
    last_halo = t // halo - 1
    col = lambda c: (lambda i: (i, c))
    row_vec = pl.BlockSpec((1, D_MODEL), lambda i: (0, 0))
    wspec = pl.BlockSpec((D_MODEL, D_MODEL), lambda i: (0, 0))
    modspec = lambda c: pl.BlockSpec((1, 1, D_MODEL), lambda i: (cond_row_fn(i * tm), 0, c))
    tile = lambda c: pl.BlockSpec((tm, D_MODEL), col(c))
    prev = lambda c: pl.BlockSpec((halo, D_MODEL), lambda i: (jnp.maximum(i * hb - 1, 0), c))
    nxt = lambda c: pl.BlockSpec((halo, D_MODEL), lambda i: (jnp.minimum((i + 1) * hb, last_halo), c))
    return pl.pallas_call(
        functools.partial(_mix_kernel, tm=tm, seq=seq),
        grid=(t // tm,),
        in_specs=[tile(0), tile(3), tile(4), tile(5), prev(4), prev(5), nxt(4), nxt(5), tile(6), tile(7),
                  tile(0),
                  pl.BlockSpec((3, D_MODEL), lambda i: (0, 0)), wspec, wspec, wspec,
                  pl.BlockSpec((D_MODEL, ROUTER_LANES), lambda i: (0, 0)),
                  row_vec, row_vec, modspec(2), modspec(3), modspec(4)],
        out_specs=[tile(0), tile(0), pl.BlockSpec((tm, ROUTER_LANES), lambda i: (i, 0))],
        out_shape=[jax.ShapeDtypeStruct((t, D_MODEL), F32),
                   jax.ShapeDtypeStruct((t, D_MODEL), BF16),
                   jax.ShapeDtypeStruct((t, ROUTER_LANES), F32)],
        compiler_params=pltpu.CompilerParams(dimension_semantics=("arbitrary",),
                                             vmem_limit_bytes=VMEM_LIMIT),
        name="mix_out",
    )(og, proj, proj, proj, proj, proj, proj, proj, proj, proj, x,
      conv_w, wro, wco, wo, wr_pad, npost, npre, mod3, mod3, mod3)


def _expert_kernel(xs_ref, val_ref, wg_ref, wu_ref, wd_ref, o_ref):
    x = xs_ref[0]
    g = jnp.dot(x, wg_ref[0], preferred_element_type=F32)
    u = jnp.dot(x, wu_ref[0], preferred_element_type=F32)
    hid = (g * _sigmoid(g) * u).astype(BF16)
    o_ref[0] = jnp.dot(hid, wd_ref[0], preferred_element_type=F32) * val_ref[0]


def _experts(xs, vals, wg, wu, wd):
    e, cap, d = xs.shape
    ff = wg.shape[2]
    tr = 512
    return pl.pallas_call(
        _expert_kernel,
        grid=(e, cap // tr),
        in_specs=[pl.BlockSpec((1, tr, d), lambda a, r: (a, r, 0)),
                  pl.BlockSpec((1, tr, 1), lambda a, r: (a, r, 0)),
                  pl.BlockSpec((1, d, ff), lambda a, r: (a, 0, 0)),
                  pl.BlockSpec((1, d, ff), lambda a, r: (a, 0, 0)),
                  pl.BlockSpec((1, ff, d), lambda a, r: (a, 0, 0))],
        out_specs=pl.BlockSpec((1, tr, d), lambda a, r: (a, r, 0)),
        out_shape=jax.ShapeDtypeStruct((e, cap, d), F32),
        compiler_params=pltpu.CompilerParams(dimension_semantics=("arbitrary", "arbitrary"),
                                             vmem_limit_bytes=VMEM_LIMIT),
        name="experts",
    )(xs, vals, wg, wu, wd)


def _final_kernel(x1_ref, f_ref, g_ref, g2_ref, o_ref):
    o_ref[...] = x1_ref[...] + g2_ref[0] * _rms(f_ref[...], g_ref[...])


def _final(x1, f, mod3, npost, cond_row_fn):
    t = x1.shape[0]
    tm = 512
    tile = pl.BlockSpec((tm, D_MODEL), lambda i: (i, 0))
    return pl.pallas_call(
        _final_kernel,
        grid=(t // tm,),
        in_specs=[tile, tile, pl.BlockSpec((1, D_MODEL), lambda i: (0, 0)),
                  pl.BlockSpec((1, 1, D_MODEL), lambda i: (cond_row_fn(i * tm), 0, 5))],
        out_specs=tile,
        out_shape=jax.ShapeDtypeStruct((t, D_MODEL), F32),
        compiler_params=pltpu.CompilerParams(dimension_semantics=("arbitrary",),
                                             vmem_limit_bytes=VMEM_LIMIT),
        name="ffn_residual",
    )(x1, f, npost, mod3)


def _trunk(x3, mod3, cond_row_fn, weights, lgs, *, rope_tabs, init, emit_state):
    (n_pre_mix, n_post_mix, n_pre_ffn, n_post_ffn, w_in, gn_g, conv_w, wro, wco, wo, wr_pad, wg, wu, wd) = weights
    b, seq, d = x3.shape
    t = b * seq
    x = x3.reshape(t, d)
    proj = _inproj(x, mod3, n_pre_mix, w_in, cond_row_fn)
    ret = _retention(proj, lgs, gn_g, seq, rope_tabs=rope_tabs, init=init, emit_state=emit_state)
    og = ret[0]
    x1, h2, aff = _mix(og, proj, x, mod3, conv_w, wro, wco, wo, wr_pad, n_post_mix, n_pre_ffn, seq, cond_row_fn)
    cap = CAPACITY_FACTOR * t // N_EXPERTS
    vals, idx = lax.top_k(aff[:, :N_EXPERTS].T, cap)
    xs = h2[idx]
    out = _experts(xs, vals[..., None], wg, wu, wd)
    f = jnp.zeros((t, d), F32).at[idx.reshape(-1)].add(out.reshape(-1, d))
    y = _final(x1, f, mod3, n_post_ffn, cond_row_fn)
    return y.reshape(b, seq, d), ret[1:]


def kernel(x_prompt, x_sample, state_ret_fwd, state_ret_bwd, c, c_ctx, w_ada, b_ada, norm_pre_mix, norm_post_mix,
           norm_pre_ffn, norm_post_ffn, w_in, ret_decay_fwd, ret_decay_bwd, ret_norm_g, conv_w, w_ret_o, w_conv_o,
           w_o, w_router, w_gate, w_up, w_down):
    depth = w_ada.shape[0]
    assert depth == 1
    dec_b, dec_seq = x_sample.shape[0], x_sample.shape[1]
    xp, xs = x_prompt, x_sample
    l = 0
    cond = jnp.zeros((N_COND_ROWS, D_MODEL), F32).at[0:dec_b].set(c).at[CTX_COND_ROW].set(c_ctx)
    mod3 = _ada(cond, w_ada[l], b_ada[l][None, :]).reshape(N_COND_ROWS, 1, 6 * D_MODEL)
    lgs = jnp.stack([-jax.nn.softplus(-ret_decay_fwd[l].astype(F32)),
                     -jax.nn.softplus(-ret_decay_bwd[l].astype(F32))])
    wr_pad = jnp.pad(w_router[l], ((0, 0), (0, ROUTER_LANES - N_EXPERTS)))
    weights = (norm_pre_mix[l][None, :], norm_post_mix[l][None, :], norm_pre_ffn[l][None, :],
               norm_post_ffn[l][None, :], w_in[l].astype(BF16), ret_norm_g[l][None, :], conv_w[l],
               w_ret_o[l].astype(BF16), w_conv_o[l].astype(BF16), w_o[l].astype(BF16), wr_pad,
               w_gate[l].astype(BF16), w_up[l].astype(BF16), w_down[l].astype(BF16))
    yp, (s_f, s_b) = _trunk(xp, mod3, lambda r: CTX_COND_ROW, weights, lgs,
                            rope_tabs=None, init=None, emit_state=True)
    ys, _ = _trunk(xs, mod3, lambda r: r // dec_seq, weights, lgs,
                   rope_tabs=_rope_tables(dec_seq), init=(state_ret_fwd, state_ret_bwd), emit_state=False)
    return (yp, ys, s_f, s_b)
```

```python
import functools

import jax
import jax.numpy as jnp
import numpy as np
from jax import lax
from jax.experimental import pallas as pl
from jax.experimental.pallas import tpu as pltpu

F32 = jnp.float32
BF16 = jnp.bfloat16

D_MODEL = 1024
N_HEADS = 8
DK = 64
DV = 128
CHUNK = 128
GRID_W = 64
N_EXPERTS = 16
CAPACITY_FACTOR = 2
D_IN_TOTAL = 8192
RMS_EPS = 1e-6
GN_EPS = 1e-5
ROPE_BASE = 10000.0
N_COND_ROWS = 16
CTX_COND_ROW = 8
ROUTER_LANES = 128
VMEM_LIMIT = 48 * 1024 * 1024


def _sigmoid(x):
    return 1.0 / (1.0 + jnp.exp(-x))


def _rms(x, g):
    return x * lax.rsqrt(jnp.mean(x * x, axis=-1, keepdims=True) + RMS_EPS) * g


def _ada_kernel(c_ref, w_ref, b_ref, o_ref):
    c = c_ref[...]
    s = c * _sigmoid(c)
    o_ref[...] = jnp.dot(s, w_ref[...], preferred_element_type=F32,
                         precision=lax.Precision.HIGHEST) + b_ref[...]


def _ada(cond, w_ada, b_ada):
    n = w_ada.shape[1]
    tn = 1024
    return pl.pallas_call(
        _ada_kernel,
        grid=(n // tn,),
        in_specs=[pl.BlockSpec((N_COND_ROWS, D_MODEL), lambda j: (0, 0)),
                  pl.BlockSpec((D_MODEL, tn), lambda j: (0, j)),
                  pl.BlockSpec((1, tn), lambda j: (0, j))],
        out_specs=pl.BlockSpec((N_COND_ROWS, tn), lambda j: (0, j)),
        out_shape=jax.ShapeDtypeStruct((N_COND_ROWS, n), F32),
        compiler_params=pltpu.CompilerParams(dimension_semantics=("arbitrary",),
                                             vmem_limit_bytes=VMEM_LIMIT),
        name="ada_mod",
    )(cond, w_ada, b_ada)


def _inproj_kernel(x_ref, g_ref, sh_ref, sc_ref, w_ref, o_ref, h_ref):
    @pl.when(pl.program_id(1) == 0)
    def _():
        h = _rms(x_ref[...], g_ref[...]) * (1.0 + sc_ref[0]) + sh_ref[0]
        h_ref[...] = h.astype(BF16)

    o_ref[...] = jnp.dot(h_ref[...], w_ref[...], preferred_element_type=F32).astype(BF16)


def _inproj(x, mod3, g, w_bf16, cond_row_fn):
    t = x.shape[0]
    tm, tn = 1024, 2048
    return pl.pallas_call(
        _inproj_kernel,
        grid=(t // tm, D_IN_TOTAL // tn),
        in_specs=[pl.BlockSpec((tm, D_MODEL), lambda i, j: (i, 0)),
                  pl.BlockSpec((1, D_MODEL), lambda i, j: (0, 0)),
                  pl.BlockSpec((1, 1, D_MODEL), lambda i, j: (cond_row_fn(i * tm), 0, 0)),
                  pl.BlockSpec((1, 1, D_MODEL), lambda i, j: (cond_row_fn(i * tm), 0, 1)),
                  pl.BlockSpec((D_MODEL, tn), lambda i, j: (0, j))],
        out_specs=pl.BlockSpec((tm, tn), lambda i, j: (i, j)),
        out_shape=jax.ShapeDtypeStruct((t, D_IN_TOTAL), BF16),
        scratch_shapes=[pltpu.VMEM((tm, D_MODEL), BF16)],
        compiler_params=pltpu.CompilerParams(dimension_semantics=("arbitrary", "arbitrary"),
                                             vmem_limit_bytes=VMEM_LIMIT),
        name="inproj",
    )(x, g, mod3, mod3, w_bf16)


def _ret_kernel(*refs, nc, use_rope, has_init, emit_state):
    it = iter(refs)
    lg_ref = next(it)
    q_ref, k_ref, v_ref, gr_ref, gn_ref = (next(it) for _ in range(5))
    cos_ref = sin_ref = s0f_ref = s0b_ref = sf_ref = sb_ref = None
    if use_rope:
        cos_ref, sin_ref = next(it), next(it)
    if has_init:
        s0f_ref, s0b_ref = next(it), next(it)
    o_ref = next(it)
    if emit_state:
        sf_ref, sb_ref = next(it), next(it)
    qr, kr, rf_all, r0 = (next(it) for _ in range(4))

    p = pl.program_id(1)
    lgf_a, lgf_b = lg_ref[0, 2 * p], lg_ref[0, 2 * p + 1]
    lgb_a, lgb_b = lg_ref[1, 2 * p], lg_ref[1, 2 * p + 1]

    ri = lax.broadcasted_iota(jnp.int32, (CHUNK, 2 * DK), 0)
    ci = lax.broadcasted_iota(jnp.int32, (CHUNK, 2 * DK), 1)
    lane_a = ci < DK
    rowf = ri.astype(F32)
    diff = rowf - ci.astype(F32)
    lgf_lane = jnp.where(lane_a, lgf_a, lgf_b)
    lgb_lane = jnp.where(lane_a, lgb_a, lgb_b)
    xi_f = jnp.exp(lgf_lane * (rowf + 1.0))
    xi_b = jnp.exp(lgb_lane * (CHUNK - rowf))
    zeta_f = jnp.exp(lgf_lane * (CHUNK - 1.0 - rowf))
    zeta_b = jnp.exp(lgb_lane * rowf)

    def decay_matrix(lgf, lgb):
        return jnp.where(diff > 0, jnp.exp(lgf * diff),
                         jnp.where(diff < 0, jnp.exp(lgb * (-diff)), 2.0))

    dm_a = decay_matrix(lgf_a, lgb_a)
    dm_b = decay_matrix(lgf_b, lgb_b)

    r2 = lax.broadcasted_iota(jnp.int32, (2 * DK, 2 * DV), 0)
    c2 = lax.broadcasted_iota(jnp.int32, (2 * DK, 2 * DV), 1)
    top = r2 < DK
    blk = (top == (c2 < DV)).astype(F32)
    cd_f = jnp.exp(jnp.where(top, lgf_a, lgf_b) * float(CHUNK)) * blk
    cd_b = jnp.exp(jnp.where(top, lgb_a, lgb_b) * float(CHUNK)) * blk

    def rope(x, rows):
        fwd = pltpu.roll(x, 2 * DK - 16, 1)
        bwd = pltpu.roll(x, 16, 1)
        swapped = jnp.where((ci & 31) < 16, fwd, bwd)
        return x * cos_ref[rows, :] + swapped * sin_ref[rows, :]

    def prep(n, carry):
        rows = pl.ds(pl.multiple_of(n * CHUNK, CHUNK), CHUNK)
        q = q_ref[rows, :].astype(F32)
        k = k_ref[rows, :].astype(F32) * (DK ** -0.5)
        if use_rope:
            q = rope(q, rows)
            k = rope(k, rows)
        qr[rows, :] = q
        kr[rows, :] = k
        return carry

    lax.fori_loop(0, nc, prep, 0)

    def load_state(s_ref):
        r0[...] = jnp.zeros_like(r0)
        r0[0:DK, 0:DV] = s_ref[0, 0, 0].astype(F32)
        r0[DK:2 * DK, DV:2 * DV] = s_ref[0, 0, 1].astype(F32)
        return r0[...]

    def kv_update(k, zeta, v):
        kz_t = (k * zeta).T.astype(BF16)
        return jnp.dot(kz_t, v, preferred_element_type=F32) * blk

    def fwd_body(n, rf):
        rows = pl.ds(pl.multiple_of(n * CHUNK, CHUNK), CHUNK)
        rf_all[n] = rf.astype(BF16)
        return cd_f * rf + kv_update(kr[rows, :], zeta_f, v_ref[rows, :])

    rf0 = load_state(s0f_ref) if has_init else jnp.zeros((2 * DK, 2 * DV), F32)
    rf_fin = lax.fori_loop(0, nc, fwd_body, rf0)

    gn = gn_ref[...]

    def head_norm(o):
        mu = jnp.mean(o, axis=-1, keepdims=True)
        d = o - mu
        var = jnp.mean(d * d, axis=-1, keepdims=True)
        return d * lax.rsqrt(var + GN_EPS)

    def bwd_body(t, rb):
        n = nc - 1 - t
        rows = pl.ds(pl.multiple_of(n * CHUNK, CHUNK), CHUNK)
        q = qr[rows, :]
        k = kr[rows, :]
        v = v_ref[rows, :]
        q16 = q.astype(BF16)
        k16 = k.astype(BF16)
        zero = jnp.zeros_like(q16)
        dn = (((1,), (1,)), ((), ()))
        s_a = lax.dot_general(jnp.where(lane_a, q16, zero), k16, dn, preferred_element_type=F32)
        s_b = lax.dot_general(jnp.where(lane_a, zero, q16), k16, dn, preferred_element_type=F32)
        o_a = jnp.dot((s_a * dm_a).astype(BF16), v[:, 0:DV], preferred_element_type=F32)
        o_b = jnp.dot((s_b * dm_b).astype(BF16), v[:, DV:2 * DV], preferred_element_type=F32)
        oc = jnp.dot((q * xi_f).astype(BF16), rf_all[n], preferred_element_type=F32)
        oc = oc + jnp.dot((q * xi_b).astype(BF16), rb.astype(BF16), preferred_element_type=F32)
        y = jnp.concatenate([head_norm(o_a + oc[:, 0:DV]), head_norm(o_b + oc[:, DV:2 * DV])], axis=1)
        g = gr_ref[rows, :].astype(F32)
        o_ref[rows, :] = (g * _sigmoid(g) * (y * gn)).astype(BF16)
        return cd_b * rb + kv_update(k, zeta_b, v)

    rb0 = load_state(s0b_ref) if has_init else jnp.zeros((2 * DK, 2 * DV), F32)
    rb_fin = lax.fori_loop(0, nc, bwd_body, rb0)

    if emit_state:
        for s_ref, r in ((sf_ref, rf_fin), (sb_ref, rb_fin)):
            s_ref[0, 0, 0] = r[0:DK, 0:DV]
            s_ref[0, 0, 1] = r[DK:2 * DK, DV:2 * DV]


def _retention(proj, lgs, gn_g, seq, *, rope_tabs=None, init=None, emit_state):
    t = proj.shape[0]
    b = t // seq
    nc = seq // CHUNK
    use_rope = rope_tabs is not None
    has_init = init is not None
    pairs = N_HEADS // 2
    qk_blocks = (N_HEADS * DK) // (2 * DK)
    in_specs = [pl.BlockSpec(memory_space=pltpu.SMEM),
                pl.BlockSpec((seq, 2 * DK), lambda i, p: (i, p)),
                pl.BlockSpec((seq, 2 * DK), lambda i, p: (i, qk_blocks + p)),
                pl.BlockSpec((seq, 2 * DV), lambda i, p: (i, qk_blocks + p)),
                pl.BlockSpec((seq, 2 * DV), lambda i, p: (i, 2 * qk_blocks + p)),
                pl.BlockSpec((1, 2 * DV), lambda i, p: (0, p))]
    args = [lgs, proj, proj, proj, proj, gn_g]
    if use_rope:
        in_specs += [pl.BlockSpec((seq, 2 * DK), lambda i, p: (0, 0))] * 2
        args += list(rope_tabs)
    state_spec = pl.BlockSpec((1, 1, 2, DK, DV), lambda i, p: (i, 0, p, 0, 0))
    if has_init:
        in_specs += [state_spec, state_spec]
        args += list(init)
    out_specs = [pl.BlockSpec((seq, 2 * DV), lambda i, p: (i, p))]
    out_shape = [jax.ShapeDtypeStruct((t, N_HEADS * DV), BF16)]
    if emit_state:
        out_specs += [state_spec, state_spec]
        out_shape += [jax.ShapeDtypeStruct((b, 1, N_HEADS, DK, DV), F32)] * 2
    return pl.pallas_call(
        functools.partial(_ret_kernel, nc=nc, use_rope=use_rope, has_init=has_init, emit_state=emit_state),
        grid=(b, pairs),
        in_specs=in_specs,
        out_specs=out_specs,
        out_shape=out_shape,
        scratch_shapes=[pltpu.VMEM((seq, 2 * DK), F32), pltpu.VMEM((seq, 2 * DK), F32),
                        pltpu.VMEM((nc, 2 * DK, 2 * DV), BF16), pltpu.VMEM((2 * DK, 2 * DV), F32)],
        compiler_params=pltpu.CompilerParams(dimension_semantics=("arbitrary", "arbitrary"),
                                             vmem_limit_bytes=VMEM_LIMIT),
        name="retention",
    )(*args)


def _rope_tables(seq):
    n_freq = DK // 4
    pos = np.arange(seq)
    inv = jnp.asarray(ROPE_BASE, F32) ** (-jnp.arange(n_freq, dtype=F32) / n_freq)
    ang_r = jnp.asarray(pos // GRID_W, F32)[:, None] * inv
    ang_c = jnp.asarray(pos % GRID_W, F32)[:, None] * inv
    cos = jnp.concatenate([jnp.cos(ang_r)] * 2 + [jnp.cos(ang_c)] * 2, axis=1)
    sin = jnp.concatenate([-jnp.sin(ang_r), jnp.sin(ang_r), -jnp.sin(ang_c), jnp.sin(ang_c)], axis=1)
    return jnp.tile(cos, (1, 2)), jnp.tile(sin, (1, 2))


def _mix_kernel(og_ref, cb_ref, cc_ref, cx_ref, ccp_ref, cxp_ref, ccn_ref, cxn_ref, ga_ref, gb_ref, x_ref,
                cw_ref, wro_ref, wco_ref, wo_ref, wr_ref, npost_ref, npre_ref, g1_ref, sh2_ref, sc2_ref,
                x1_ref, h2_ref, aff_ref, *, tm, seq):
    i = pl.program_id(0)
    has_prev = jnp.where((i * tm) % seq != 0, 1.0, 0.0)
    has_next = jnp.where(((i + 1) * tm) % seq != 0, 1.0, 0.0)
    halo = ccp_ref.shape[0]
    prod = cc_ref[...].astype(F32) * cx_ref[...].astype(F32)
    prev_row = ccp_ref[halo - 1:halo, :].astype(F32) * cxp_ref[halo - 1:halo, :].astype(F32) * has_prev
    next_row = ccn_ref[0:1, :].astype(F32) * cxn_ref[0:1, :].astype(F32) * has_next
    row = lax.broadcasted_iota(jnp.int32, (tm, 1), 0)
    up = jnp.where(row == 0, prev_row, pltpu.roll(prod, 1, 0))
    dn = jnp.where(row == tm - 1, next_row, pltpu.roll(prod, tm - 1, 0))
    u = up * cw_ref[0:1, :] + prod * cw_ref[1:2, :] + dn * cw_ref[2:3, :]
    y_conv = jnp.dot((cb_ref[...].astype(F32) * u).astype(BF16), wco_ref[...], preferred_element_type=F32)
    y_ret = jnp.dot(og_ref[...], wro_ref[...], preferred_element_type=F32)
    merged = _sigmoid(ga_ref[...].astype(F32)) * y_ret + _sigmoid(gb_ref[...].astype(F32)) * y_conv
    m = jnp.dot(merged.astype(BF16), wo_ref[...], preferred_element_type=F32)
    x1 = x_ref[...] + g1_ref[0] * _rms(m, npost_ref[...])
    x1_ref[...] = x1
    h2 = _rms(x1, npre_ref[...]) * (1.0 + sc2_ref[0]) + sh2_ref[0]
    h2_ref[...] = h2.astype(BF16)
    h_hi = h2.astype(BF16)
    h_lo = (h2 - h_hi.astype(F32)).astype(BF16)
    wr = wr_ref[...]
    w_hi = wr.astype(BF16)
    w_lo = (wr - w_hi.astype(F32)).astype(BF16)
    logits = (jnp.dot(h_hi, w_hi, preferred_element_type=F32)
              + jnp.dot(h_lo, w_hi, preferred_element_type=F32)
              + jnp.dot(h_hi, w_lo, preferred_element_type=F32))
    lane = lax.broadcasted_iota(jnp.int32, (tm, ROUTER_LANES), 1)
    logits = jnp.where(lane < N_EXPERTS, logits, -jnp.inf)
    e = jnp.exp(logits - jnp.max(logits, axis=-1, keepdims=True))
    aff = e / jnp.sum(e, axis=-1, keepdims=True)
    aff_ref[...] = aff.T[0:N_EXPERTS, :]


def _mix(og, proj, x, mod3, conv_w, wro, wco, wo, wr_pad, npost, npre, seq, cond_row_fn):
    t = x.shape[0]
    tm = 256
    halo = 16
    hb = tm // halo
    last_halo = t // halo - 1
    col = lambda c: (lambda i: (i, c))
    row_vec = pl.BlockSpec((1, D_MODEL), lambda i: (0, 0))
    wspec = pl.BlockSpec((D_MODEL, D_MODEL), lambda i: (0, 0))
    modspec = lambda c: pl.BlockSpec((1, 1, D_MODEL), lambda i: (cond_row_fn(i * tm), 0, c))
    tile = lambda c: pl.BlockSpec((tm, D_MODEL), col(c))
    prev = lambda c: pl.BlockSpec((halo, D_MODEL), lambda i: (jnp.maximum(i * hb - 1, 0), c))
    nxt = lambda c: pl.BlockSpec((halo, D_MODEL), lambda i: (jnp.minimum((i + 1) * hb, last_halo), c))
    return pl.pallas_call(
        functools.partial(_mix_kernel, tm=tm, seq=seq),
        grid=(t // tm,),
        in_specs=[tile(0), tile(3), tile(4), tile(5), prev(4), prev(5), nxt(4), nxt(5), tile(6), tile(7),
                  tile(0),
                  pl.BlockSpec((3, D_MODEL), lambda i: (0, 0)), wspec, wspec, wspec,
                  pl.BlockSpec((D_MODEL, ROUTER_LANES), lambda i: (0, 0)),
                  row_vec, row_vec, modspec(2), modspec(3), modspec(4)],
        out_specs=[tile(0), tile(0), pl.BlockSpec((N_EXPERTS, tm), lambda i: (0, i))],
        out_shape=[jax.ShapeDtypeStruct((t, D_MODEL), F32),
                   jax.ShapeDtypeStruct((t, D_MODEL), BF16),
                   jax.ShapeDtypeStruct((N_EXPERTS, t), F32)],
        compiler_params=pltpu.CompilerParams(dimension_semantics=("arbitrary",),
                                             vmem_limit_bytes=VMEM_LIMIT),
        name="mix_out",
    )(og, proj, proj, proj, proj, proj, proj, proj, proj, proj, x,
      conv_w, wro, wco, wo, wr_pad, npost, npre, mod3, mod3, mod3)


def _expert_kernel(xs_ref, wg_ref, wu_ref, wd_ref, o_ref):
    x = xs_ref[0]
    g = jnp.dot(x, wg_ref[0], preferred_element_type=F32)
    u = jnp.dot(x, wu_ref[0], preferred_element_type=F32)
    hid = (g * _sigmoid(g) * u).astype(BF16)
    o_ref[0] = jnp.dot(hid, wd_ref[0], preferred_element_type=F32).astype(BF16)


def _experts(xs, wg, wu, wd):
    e, cap, d = xs.shape
    ff = wg.shape[2]
    tr = 512
    return pl.pallas_call(
        _expert_kernel,
        grid=(e, cap // tr),
        in_specs=[pl.BlockSpec((1, tr, d), lambda a, r: (a, r, 0)),
                  pl.BlockSpec((1, d, ff), lambda a, r: (a, 0, 0)),
                  pl.BlockSpec((1, d, ff), lambda a, r: (a, 0, 0)),
                  pl.BlockSpec((1, ff, d), lambda a, r: (a, 0, 0))],
        out_specs=pl.BlockSpec((1, tr, d), lambda a, r: (a, r, 0)),
        out_shape=jax.ShapeDtypeStruct((e, cap, d), BF16),
        compiler_params=pltpu.CompilerParams(dimension_semantics=("arbitrary", "arbitrary"),
                                             vmem_limit_bytes=VMEM_LIMIT),
        name="experts",
    )(xs, wg, wu, wd)


def _route_kernel(aff_ref, pos_ref, excl_ref, *, cap):
    ne, nblk, lanes = aff_ref.shape
    bits = pltpu.bitcast(aff_ref[...], jnp.int32)

    def count(mask):
        m = jnp.where(mask, 1.0, 0.0)
        return jnp.sum(jnp.sum(m, axis=1, keepdims=True), axis=2, keepdims=True)

    def bit_step(it, thr):
        cand = thr | jnp.left_shift(jnp.int32(1), 30 - it)
        return jnp.where(count(bits >= cand) >= cap, cand, thr)

    thr = lax.fori_loop(0, 31, bit_step, jnp.zeros((ne, 1, 1), jnp.int32))
    gt = bits > thr
    eq = bits == thr
    need = cap - count(gt)

    li = lax.broadcasted_iota(jnp.int32, (lanes, lanes), 0)
    lj = lax.broadcasted_iota(jnp.int32, (lanes, lanes), 1)
    incl_lanes = jnp.where(li <= lj, 1.0, 0.0).astype(BF16)
    all_lanes = jnp.ones((lanes, lanes), BF16)
    bi = lax.broadcasted_iota(jnp.int32, (nblk, nblk), 0)
    bj = lax.broadcasted_iota(jnp.int32, (nblk, nblk), 1)
    earlier_blocks = jnp.where(bj < bi, 1.0, 0.0).astype(BF16)

    def ranks(mask):
        m = jnp.where(mask, 1.0, 0.0).reshape(ne * nblk, lanes).astype(BF16)
        incl = jnp.dot(m, incl_lanes, preferred_element_type=F32).reshape(ne, nblk, lanes)
        tot = jnp.dot(m, all_lanes, preferred_element_type=F32).reshape(ne, nblk, lanes).astype(BF16)
        excl = jnp.stack([jnp.dot(earlier_blocks, tot[i], preferred_element_type=F32) for i in range(ne)])
        return excl, incl

    ex_eq, in_eq = ranks(eq)
    sel = gt | (eq & (ex_eq + in_eq - 1.0 < need))
    ex_sel, in_sel = ranks(sel)
    pos_ref[...] = jnp.where(sel, ex_sel + in_sel - 1.0, -1.0).astype(jnp.int32)
    excl_ref[...] = ex_sel.astype(jnp.int32)


def _route(aff_t, cap):
    ne, t = aff_t.shape
    nblk = t // 128
    shape = (ne, nblk, 128)
    full = pl.BlockSpec(shape, lambda i: (0, 0, 0))
    pos, excl = pl.pallas_call(
        functools.partial(_route_kernel, cap=cap),
        grid=(1,),
        in_specs=[full],
        out_specs=[full, full],
        out_shape=[jax.ShapeDtypeStruct(shape, jnp.int32)] * 2,
        compiler_params=pltpu.CompilerParams(dimension_semantics=("arbitrary",),
                                             vmem_limit_bytes=VMEM_LIMIT),
        name="route",
    )(aff_t.reshape(shape))
    base = jnp.concatenate([excl[:, :, 0], jnp.full((ne, 1), cap, jnp.int32)], axis=1)
    return pos.reshape(ne, t), base


ROUTE_TILE = 256
ROUTE_WINDOW = 64
SLOT_ALIGN = 16
DISPATCH_GROUP = 4


def _window(base_ref, e, blk, w, cap):
    lo = base_ref[e, blk] // SLOT_ALIGN + w * (ROUTE_WINDOW // SLOT_ALIGN)
    return lo * SLOT_ALIGN, jnp.minimum(lo, (cap - ROUTE_WINDOW) // SLOT_ALIGN) * SLOT_ALIGN


def _n_windows(base_ref, e, blk, cap):
    lo, _ = _window(base_ref, e, blk, 0, cap)
    end = base_ref[e, blk + ROUTE_TILE // 128]
    return (end - lo + ROUTE_WINDOW - 1) // ROUTE_WINDOW


def _dispatch_kernel(base_ref, h_ref, pos_ref, xs_ref, *, cap):
    g, j = pl.program_id(0), pl.program_id(1)
    blk = j * (ROUTE_TILE // 128)

    @pl.when(j == 0)
    def _():
        xs_ref[...] = jnp.zeros_like(xs_ref)

    h = h_ref[...]
    ri = lax.broadcasted_iota(jnp.int32, (ROUTE_WINDOW, ROUTE_TILE), 0)

    def onehot(prow, lo, off, first):
        hit = prow - off == ri
        if not first:
            hit = hit & (prow >= lo)
        return jnp.where(hit, 1.0, 0.0).astype(BF16)

    def add_rows(ge, off, rows):
        sl = pl.ds(pl.multiple_of(off, SLOT_ALIGN), ROUTE_WINDOW)
        xs_ref[ge, sl, :] = xs_ref[ge, sl, :] + rows.astype(BF16)

    prows, offs = [], []
    for ge in range(DISPATCH_GROUP):
        e = g * DISPATCH_GROUP + ge
        prows.append(pos_ref[pl.ds(e, 1), :])
        offs.append(_window(base_ref, e, blk, 0, cap))
    sel = jnp.concatenate([onehot(prows[ge], *offs[ge], True) for ge in range(DISPATCH_GROUP)], axis=0)
    rows = jnp.dot(sel, h, preferred_element_type=F32)
    for ge in range(DISPATCH_GROUP):
        add_rows(ge, offs[ge][1], rows[ge * ROUTE_WINDOW:(ge + 1) * ROUTE_WINDOW])

    for ge in range(DISPATCH_GROUP):
        e = g * DISPATCH_GROUP + ge

        def extra(w, carry, ge=ge, e=e):
            lo, off = _window(base_ref, e, blk, w, cap)
            add_rows(ge, off, jnp.dot(onehot(prows[ge], lo, off, False), h, preferred_element_type=F32))
            return carry

        lax.fori_loop(1, _n_windows(base_ref, e, blk, cap), extra, 0)


def _dispatch(base, h2, pos, cap):
    t, d = h2.shape
    ne = pos.shape[0]
    return pl.pallas_call(
        functools.partial(_dispatch_kernel, cap=cap),
        grid_spec=pltpu.PrefetchScalarGridSpec(
            num_scalar_prefetch=1,
            grid=(ne // DISPATCH_GROUP, t // ROUTE_TILE),
            in_specs=[pl.BlockSpec((ROUTE_TILE, d), lambda g, j, b: (j, 0)),
                      pl.BlockSpec((ne, ROUTE_TILE), lambda g, j, b: (0, j))],
            out_specs=pl.BlockSpec((DISPATCH_GROUP, cap, d), lambda g, j, b: (g, 0, 0))),
        out_shape=jax.ShapeDtypeStruct((ne, cap, d), BF16),
        compiler_params=pltpu.CompilerParams(dimension_semantics=("arbitrary", "arbitrary"),
                                             vmem_limit_bytes=VMEM_LIMIT),
        name="dispatch",
    )(base, h2, pos)


def _combine_kernel(base_ref, *refs, cap):
    win_refs = refs[:N_EXPERTS]
    pos_ref, aff_ref, x1_ref, g_ref, g2_ref, eo_ref, o_ref, f_ref, wbuf, sem = refs[N_EXPERTS:]
    blk = pl.program_id(0) * (ROUTE_TILE // 128)
    ri = lax.broadcasted_iota(jnp.int32, (ROUTE_WINDOW, ROUTE_TILE), 0)
    tn = (((0,), (0,)), ((), ()))

    def weights(e, lo, off, first):
        prow = pos_ref[e:e + 1, :]
        hit = prow - off == ri
        if not first:
            hit = hit & (prow >= lo)
        return jnp.where(hit, aff_ref[e:e + 1, :], 0.0).astype(BF16)

    q = jnp.concatenate([weights(e, *_window(base_ref, e, blk, 0, cap), True) for e in range(N_EXPERTS)], axis=0)
    win = jnp.concatenate([w[0] for w in win_refs], axis=0)
    f_ref[...] = lax.dot_general(q, win, tn, preferred_element_type=F32)

    for e in range(N_EXPERTS):
        def extra(w, carry, e=e):
            lo, off = _window(base_ref, e, blk, w, cap)
            cp = pltpu.make_async_copy(eo_ref.at[e, pl.ds(pl.multiple_of(off, SLOT_ALIGN), ROUTE_WINDOW), :],
                                       wbuf, sem)
            cp.start()
            cp.wait()
            f_ref[...] += lax.dot_general(weights(e, lo, off, False), wbuf[...], tn, preferred_element_type=F32)
            return carry

        lax.fori_loop(1, _n_windows(base_ref, e, blk, cap), extra, 0)

    o_ref[...] = x1_ref[...] + g2_ref[0] * _rms(f_ref[...], g_ref[...])


def _combine(base, eo, pos, aff_t, x1, mod3, npost, cond_row_fn, cap):
    t, d = x1.shape
    ne = pos.shape[0]

    def win_spec(e):
        return pl.BlockSpec((pl.Element(1), pl.Element(ROUTE_WINDOW), pl.Element(d)),
                            lambda j, b: (e, _window(b, e, j * (ROUTE_TILE // 128), 0, cap)[1], 0))

    tile = pl.BlockSpec((ROUTE_TILE, d), lambda j, b: (j, 0))
    etile = pl.BlockSpec((ne, ROUTE_TILE), lambda j, b: (0, j))
    return pl.pallas_call(
        functools.partial(_combine_kernel, cap=cap),
        grid_spec=pltpu.PrefetchScalarGridSpec(
            num_scalar_prefetch=1,
            grid=(t // ROUTE_TILE,),
            in_specs=[win_spec(e) for e in range(ne)] + [
                etile, etile, tile,
                pl.BlockSpec((1, d), lambda j, b: (0, 0)),
                pl.BlockSpec((1, 1, d), lambda j, b: (cond_row_fn(j * ROUTE_TILE), 0, 5)),
                pl.BlockSpec(memory_space=pl.ANY)],
            out_specs=tile,
            scratch_shapes=[pltpu.VMEM((ROUTE_TILE, d), F32), pltpu.VMEM((ROUTE_WINDOW, d), BF16),
                            pltpu.SemaphoreType.DMA(())]),
        out_shape=jax.ShapeDtypeStruct((t, d), F32),
        compiler_params=pltpu.CompilerParams(dimension_semantics=("arbitrary",),
                                             vmem_limit_bytes=VMEM_LIMIT),
        name="combine",
    )(base, *([eo] * ne), pos, aff_t, x1, npost, mod3, eo)


def _trunk(x3, mod3, cond_row_fn, weights, lgs, *, rope_tabs, init, emit_state):
    (n_pre_mix, n_post_mix, n_pre_ffn, n_post_ffn, w_in, gn_g, conv_w, wro, wco, wo, wr_pad, wg, wu, wd) = weights
    b, seq, d = x3.shape
    t = b * seq
    x = x3.reshape(t, d)
    proj = _inproj(x, mod3, n_pre_mix, w_in, cond_row_fn)
    ret = _retention(proj, lgs, gn_g, seq, rope_tabs=rope_tabs, init=init, emit_state=emit_state)
    og = ret[0]
    x1, h2, aff_t = _mix(og, proj, x, mod3, conv_w, wro, wco, wo, wr_pad, n_post_mix, n_pre_ffn, seq, cond_row_fn)
    cap = CAPACITY_FACTOR * t // N_EXPERTS
    pos, base = _route(aff_t, cap)
    eo = _experts(_dispatch(base, h2, pos, cap), wg, wu, wd)
    y = _combine(base, eo, pos, aff_t, x1, mod3, n_post_ffn, cond_row_fn, cap)
    return y.reshape(b, seq, d), ret[1:]


def kernel(x_prompt, x_sample, state_ret_fwd, state_ret_bwd, c, c_ctx, w_ada, b_ada, norm_pre_mix, norm_post_mix,
           norm_pre_ffn, norm_post_ffn, w_in, ret_decay_fwd, ret_decay_bwd, ret_norm_g, conv_w, w_ret_o, w_conv_o,
           w_o, w_router, w_gate, w_up, w_down):
    depth = w_ada.shape[0]
    assert depth == 1
    dec_b, dec_seq = x_sample.shape[0], x_sample.shape[1]
    xp, xs = x_prompt, x_sample
    l = 0
    cond = jnp.zeros((N_COND_ROWS, D_MODEL), F32).at[0:dec_b].set(c).at[CTX_COND_ROW].set(c_ctx)
    mod3 = _ada(cond, w_ada[l], b_ada[l][None, :]).reshape(N_COND_ROWS, 1, 6 * D_MODEL)
    lgs = jnp.stack([-jax.nn.softplus(-ret_decay_fwd[l].astype(F32)),
                     -jax.nn.softplus(-ret_decay_bwd[l].astype(F32))])
    wr_pad = jnp.pad(w_router[l], ((0, 0), (0, ROUTER_LANES - N_EXPERTS)))
    weights = (norm_pre_mix[l][None, :], norm_post_mix[l][None, :], norm_pre_ffn[l][None, :],
               norm_post_ffn[l][None, :], w_in[l].astype(BF16), ret_norm_g[l][None, :], conv_w[l],
               w_ret_o[l].astype(BF16), w_conv_o[l].astype(BF16), w_o[l].astype(BF16), wr_pad,
               w_gate[l].astype(BF16), w_up[l].astype(BF16), w_down[l].astype(BF16))
    yp, (s_f, s_b) = _trunk(xp, mod3, lambda r: CTX_COND_ROW, weights, lgs,
                            rope_tabs=None, init=None, emit_state=True)
    ys, _ = _trunk(xs, mod3, lambda r: r // dec_seq, weights, lgs,
                   rope_tabs=_rope_tables(dec_seq), init=(state_ret_fwd, state_ret_bwd), emit_state=False)
    return (yp, ys, s_f, s_b)
```

```python
import functools

import jax
import jax.numpy as jnp
import numpy as np
from jax import lax
from jax.experimental import pallas as pl
from jax.experimental.pallas import tpu as pltpu

F32 = jnp.float32
BF16 = jnp.bfloat16

D_MODEL = 1024
N_HEADS = 8
DK = 64
DV = 128
CHUNK = 128
GRID_W = 64
N_EXPERTS = 16
CAPACITY_FACTOR = 2
D_IN_TOTAL = 8192
RMS_EPS = 1e-6
GN_EPS = 1e-5
ROPE_BASE = 10000.0
N_COND_ROWS = 16
CTX_COND_ROW = 8
ROUTER_LANES = 128
RET_CHUNKS_PER_STEP = 16
VMEM_LIMIT = 48 * 1024 * 1024
EXPERT_ROWS_PER_STEP = 1024
EXPERT_SUB_ROWS = 512
EXPERT_VMEM_LIMIT = 56 * 1024 * 1024


def _sigmoid(x):
    return 1.0 / (1.0 + jnp.exp(-x))


def _rms(x, g):
    return x * lax.rsqrt(jnp.mean(x * x, axis=-1, keepdims=True) + RMS_EPS) * g


def _ada_kernel(c_ref, w_ref, b_ref, o_ref):
    c = c_ref[...]
    s = c * _sigmoid(c)
    o_ref[...] = jnp.dot(s, w_ref[...], preferred_element_type=F32,
                         precision=lax.Precision.HIGHEST) + b_ref[...]


def _ada(cond, w_ada, b_ada):
    n = w_ada.shape[1]
    tn = 1024
    return pl.pallas_call(
        _ada_kernel,
        grid=(n // tn,),
        in_specs=[pl.BlockSpec((N_COND_ROWS, D_MODEL), lambda j: (0, 0)),
                  pl.BlockSpec((D_MODEL, tn), lambda j: (0, j)),
                  pl.BlockSpec((1, tn), lambda j: (0, j))],
        out_specs=pl.BlockSpec((N_COND_ROWS, tn), lambda j: (0, j)),
        out_shape=jax.ShapeDtypeStruct((N_COND_ROWS, n), F32),
        compiler_params=pltpu.CompilerParams(dimension_semantics=("arbitrary",),
                                             vmem_limit_bytes=VMEM_LIMIT),
        name="ada_mod",
    )(cond, w_ada, b_ada)


def _inproj_kernel(x_ref, g_ref, sh_ref, sc_ref, w_ref, o_ref, h_ref):
    @pl.when(pl.program_id(1) == 0)
    def _():
        h = _rms(x_ref[...], g_ref[...]) * (1.0 + sc_ref[0]) + sh_ref[0]
        h_ref[...] = h.astype(BF16)

    o_ref[...] = jnp.dot(h_ref[...], w_ref[...], preferred_element_type=F32).astype(BF16)


def _inproj(x, mod3, g, w_bf16, cond_row_fn):
    t = x.shape[0]
    tm, tn = 1024, 2048
    return pl.pallas_call(
        _inproj_kernel,
        grid=(t // tm, D_IN_TOTAL // tn),
        in_specs=[pl.BlockSpec((tm, D_MODEL), lambda i, j: (i, 0)),
                  pl.BlockSpec((1, D_MODEL), lambda i, j: (0, 0)),
                  pl.BlockSpec((1, 1, D_MODEL), lambda i, j: (cond_row_fn(i * tm), 0, 0)),
                  pl.BlockSpec((1, 1, D_MODEL), lambda i, j: (cond_row_fn(i * tm), 0, 1)),
                  pl.BlockSpec((D_MODEL, tn), lambda i, j: (0, j))],
        out_specs=pl.BlockSpec((tm, tn), lambda i, j: (i, j)),
        out_shape=jax.ShapeDtypeStruct((t, D_IN_TOTAL), BF16),
        scratch_shapes=[pltpu.VMEM((tm, D_MODEL), BF16)],
        compiler_params=pltpu.CompilerParams(dimension_semantics=("arbitrary", "arbitrary"),
                                             vmem_limit_bytes=VMEM_LIMIT),
        name="inproj",
    )(x, g, mod3, mod3, w_bf16)


def _ret_kernel(*refs, nc, bb, use_rope, has_init, emit_state):
    it = iter(refs)
    lg_ref = next(it)
    q_ref, k_ref, v_ref, gr_ref, gn_ref = (next(it) for _ in range(5))
    cos_ref = sin_ref = s0f_ref = s0b_ref = sf_ref = sb_ref = None
    if use_rope:
        cos_ref, sin_ref = next(it), next(it)
    if has_init:
        s0f_ref, s0b_ref = next(it), next(it)
    o_ref = next(it)
    if emit_state:
        sf_ref, sb_ref = next(it), next(it)
    k16_s, kvf_s, kvb_s, rcat_s, r0 = (next(it) for _ in range(5))

    p = pl.program_id(1)
    lgf_a, lgf_b = lg_ref[0, 2 * p], lg_ref[0, 2 * p + 1]
    lgb_a, lgb_b = lg_ref[1, 2 * p], lg_ref[1, 2 * p + 1]

    ri = lax.broadcasted_iota(jnp.int32, (CHUNK, 2 * DK), 0)
    ci = lax.broadcasted_iota(jnp.int32, (CHUNK, 2 * DK), 1)
    lane_a = ci < DK
    rowf = ri.astype(F32)
    diff = rowf - ci.astype(F32)
    lgf_lane = jnp.where(lane_a, lgf_a, lgf_b)
    lgb_lane = jnp.where(lane_a, lgb_a, lgb_b)
    xi_f = jnp.exp(lgf_lane * (rowf + 1.0))
    xi_b = jnp.exp(lgb_lane * (CHUNK - rowf))
    zeta_f = jnp.exp(lgf_lane * (CHUNK - 1.0 - rowf))
    zeta_b = jnp.exp(lgb_lane * rowf)

    def decay_matrix(lgf, lgb):
        return jnp.where(diff > 0, jnp.exp(lgf * diff),
                         jnp.where(diff < 0, jnp.exp(lgb * (-diff)), 2.0))

    dm_a = decay_matrix(lgf_a, lgb_a)
    dm_b = decay_matrix(lgf_b, lgb_b)

    r2 = lax.broadcasted_iota(jnp.int32, (2 * DK, 2 * DV), 0)
    c2 = lax.broadcasted_iota(jnp.int32, (2 * DK, 2 * DV), 1)
    top = r2 < DK
    blk = (top == (c2 < DV)).astype(F32)
    cd_f = jnp.exp(jnp.where(top, lgf_a, lgf_b) * float(CHUNK)) * blk
    cd_b = jnp.exp(jnp.where(top, lgb_a, lgb_b) * float(CHUNK)) * blk

    def chunk_rows(c):
        return pl.ds(pl.multiple_of(c * CHUNK, CHUNK), CHUNK)

    def rope(x, c):
        rows = chunk_rows(lax.rem(c, nc))
        fwd = pltpu.roll(x, 2 * DK - 16, 1)
        bwd = pltpu.roll(x, 16, 1)
        swapped = jnp.where((ci & 31) < 16, fwd, bwd)
        return x * cos_ref[rows, :] + swapped * sin_ref[rows, :]

    def kv_body(c, carry):
        rows = chunk_rows(c)
        k = k_ref[rows, :].astype(F32) * (DK ** -0.5)
        if use_rope:
            k = rope(k, c)
        k16_s[rows, :] = k.astype(BF16)
        kz_t = jnp.concatenate([k * zeta_f, k * zeta_b], axis=1).T.astype(BF16)
        kv = jnp.dot(kz_t, v_ref[rows, :], preferred_element_type=F32)
        kvf_s[c] = kv[0:2 * DK] * blk
        kvb_s[c] = kv[2 * DK:4 * DK] * blk
        return carry

    lax.fori_loop(0, bb * nc, kv_body, 0, unroll=2)

    def load_state(s_ref, s):
        r0[...] = jnp.zeros_like(r0)
        r0[0:DK, 0:DV] = s_ref[s, 0, 0].astype(F32)
        r0[DK:2 * DK, DV:2 * DV] = s_ref[s, 0, 1].astype(F32)
        return r0[...]

    for s in range(bb):
        def fwd_body(n, rf, s=s):
            c = s * nc + n
            rcat_s[c, 0:2 * DK, :] = rf.astype(BF16)
            return cd_f * rf + kvf_s[c]

        def bwd_body(t, rb, s=s):
            c = s * nc + nc - 1 - t
            rcat_s[c, 2 * DK:4 * DK, :] = rb.astype(BF16)
            return cd_b * rb + kvb_s[c]

        zeros = jnp.zeros((2 * DK, 2 * DV), F32)
        rf_fin = lax.fori_loop(0, nc, fwd_body, load_state(s0f_ref, s) if has_init else zeros)
        rb_fin = lax.fori_loop(0, nc, bwd_body, load_state(s0b_ref, s) if has_init else zeros)
        if emit_state:
            for s_ref, r in ((sf_ref, rf_fin), (sb_ref, rb_fin)):
                s_ref[s, 0, 0] = r[0:DK, 0:DV]
                s_ref[s, 0, 1] = r[DK:2 * DK, DV:2 * DV]

    gn = gn_ref[...]

    def head_norm(o):
        mu = jnp.mean(o, axis=-1, keepdims=True)
        d = o - mu
        var = jnp.mean(d * d, axis=-1, keepdims=True)
        return d * lax.rsqrt(var + GN_EPS)

    def out_body(c, carry):
        rows = chunk_rows(c)
        q = q_ref[rows, :].astype(F32)
        if use_rope:
            q = rope(q, c)
        q16 = q.astype(BF16)
        zero = jnp.zeros_like(q16)
        v = v_ref[rows, :]
        q_ab = jnp.concatenate([jnp.where(lane_a, q16, zero), jnp.where(lane_a, zero, q16)], axis=0)
        s = lax.dot_general(q_ab, k16_s[rows, :], (((1,), (1,)), ((), ())), preferred_element_type=F32)
        o_a = jnp.dot((s[0:CHUNK] * dm_a).astype(BF16), v[:, 0:DV], preferred_element_type=F32)
        o_b = jnp.dot((s[CHUNK:2 * CHUNK] * dm_b).astype(BF16), v[:, DV:2 * DV], preferred_element_type=F32)
        qx = jnp.concatenate([(q * xi_f).astype(BF16), (q * xi_b).astype(BF16)], axis=1)
        oc = jnp.dot(qx, rcat_s[c], preferred_element_type=F32)
        y = jnp.concatenate([head_norm(o_a + oc[:, 0:DV]), head_norm(o_b + oc[:, DV:2 * DV])], axis=1)
        g = gr_ref[rows, :].astype(F32)
        o_ref[rows, :] = (g * _sigmoid(g) * (y * gn)).astype(BF16)
        return carry

    lax.fori_loop(0, bb * nc, out_body, 0, unroll=2)


def _retention(proj, lgs, gn_g, seq, *, rope_tabs=None, init=None, emit_state):
    t = proj.shape[0]
    b = t // seq
    nc = seq // CHUNK
    bb = max(1, RET_CHUNKS_PER_STEP // nc)
    rows = bb * seq
    use_rope = rope_tabs is not None
    has_init = init is not None
    pairs = N_HEADS // 2
    qk_blocks = (N_HEADS * DK) // (2 * DK)
    in_specs = [pl.BlockSpec(memory_space=pltpu.SMEM),
                pl.BlockSpec((rows, 2 * DK), lambda i, p: (i, p)),
                pl.BlockSpec((rows, 2 * DK), lambda i, p: (i, qk_blocks + p)),
                pl.BlockSpec((rows, 2 * DV), lambda i, p: (i, qk_blocks + p)),
                pl.BlockSpec((rows, 2 * DV), lambda i, p: (i, 2 * qk_blocks + p)),
                pl.BlockSpec((1, 2 * DV), lambda i, p: (0, p))]
    args = [lgs, proj, proj, proj, proj, gn_g]
    if use_rope:
        in_specs += [pl.BlockSpec((seq, 2 * DK), lambda i, p: (0, 0))] * 2
        args += list(rope_tabs)
    state_spec = pl.BlockSpec((bb, 1, 2, DK, DV), lambda i, p: (i, 0, p, 0, 0))
    if has_init:
        in_specs += [state_spec, state_spec]
        args += list(init)
    out_specs = [pl.BlockSpec((rows, 2 * DV), lambda i, p: (i, p))]
    out_shape = [jax.ShapeDtypeStruct((t, N_HEADS * DV), BF16)]
    if emit_state:
        out_specs += [state_spec, state_spec]
        out_shape += [jax.ShapeDtypeStruct((b, 1, N_HEADS, DK, DV), F32)] * 2
    return pl.pallas_call(
        functools.partial(_ret_kernel, nc=nc, bb=bb, use_rope=use_rope, has_init=has_init,
                          emit_state=emit_state),
        grid=(b // bb, pairs),
        in_specs=in_specs,
        out_specs=out_specs,
        out_shape=out_shape,
        scratch_shapes=[pltpu.VMEM((rows, 2 * DK), BF16),
                        pltpu.VMEM((bb * nc, 2 * DK, 2 * DV), F32),
                        pltpu.VMEM((bb * nc, 2 * DK, 2 * DV), F32),
                        pltpu.VMEM((bb * nc, 4 * DK, 2 * DV), BF16),
                        pltpu.VMEM((2 * DK, 2 * DV), F32)],
        compiler_params=pltpu.CompilerParams(dimension_semantics=("arbitrary", "arbitrary"),
                                             vmem_limit_bytes=VMEM_LIMIT),
        name="retention",
    )(*args)


def _rope_tables(seq):
    n_freq = DK // 4
    pos = np.arange(seq)
    inv = jnp.asarray(ROPE_BASE, F32) ** (-jnp.arange(n_freq, dtype=F32) / n_freq)
    ang_r = jnp.asarray(pos // GRID_W, F32)[:, None] * inv
    ang_c = jnp.asarray(pos % GRID_W, F32)[:, None] * inv
    cos = jnp.concatenate([jnp.cos(ang_r)] * 2 + [jnp.cos(ang_c)] * 2, axis=1)
    sin = jnp.concatenate([-jnp.sin(ang_r), jnp.sin(ang_r), -jnp.sin(ang_c), jnp.sin(ang_c)], axis=1)
    return jnp.tile(cos, (1, 2)), jnp.tile(sin, (1, 2))


def _mix_kernel(og_ref, cb_ref, cc_ref, cx_ref, ccp_ref, cxp_ref, ccn_ref, cxn_ref, ga_ref, gb_ref, x_ref,
                cw_ref, wro_ref, wco_ref, wo_ref, wr_ref, npost_ref, npre_ref, g1_ref, sh2_ref, sc2_ref,
                x1_ref, h2_ref, aff_ref, *, tm, seq):
    i = pl.program_id(0)
    has_prev = jnp.where((i * tm) % seq != 0, 1.0, 0.0)
    has_next = jnp.where(((i + 1) * tm) % seq != 0, 1.0, 0.0)
    halo = ccp_ref.shape[0]
    prod = cc_ref[...].astype(F32) * cx_ref[...].astype(F32)
    prev_row = ccp_ref[halo - 1:halo, :].astype(F32) * cxp_ref[halo - 1:halo, :].astype(F32) * has_prev
    next_row = ccn_ref[0:1, :].astype(F32) * cxn_ref[0:1, :].astype(F32) * has_next
    row = lax.broadcasted_iota(jnp.int32, (tm, 1), 0)
    up = jnp.where(row == 0, prev_row, pltpu.roll(prod, 1, 0))
    dn = jnp.where(row == tm - 1, next_row, pltpu.roll(prod, tm - 1, 0))
    u = up * cw_ref[0:1, :] + prod * cw_ref[1:2, :] + dn * cw_ref[2:3, :]
    y_conv = jnp.dot((cb_ref[...].astype(F32) * u).astype(BF16), wco_ref[...], preferred_element_type=F32)
    y_ret = jnp.dot(og_ref[...], wro_ref[...], preferred_element_type=F32)
    merged = _sigmoid(ga_ref[...].astype(F32)) * y_ret + _sigmoid(gb_ref[...].astype(F32)) * y_conv
    m = jnp.dot(merged.astype(BF16), wo_ref[...], preferred_element_type=F32)
    x1 = x_ref[...] + g1_ref[0] * _rms(m, npost_ref[...])
    x1_ref[...] = x1
    h2 = _rms(x1, npre_ref[...]) * (1.0 + sc2_ref[0]) + sh2_ref[0]
    h2_ref[...] = h2.astype(BF16)
    h_hi = h2.astype(BF16)
    h_lo = (h2 - h_hi.astype(F32)).astype(BF16)
    wr = wr_ref[...]
    w_hi = wr.astype(BF16)
    w_lo = (wr - w_hi.astype(F32)).astype(BF16)
    logits = (jnp.dot(h_hi, w_hi, preferred_element_type=F32)
              + jnp.dot(h_lo, w_hi, preferred_element_type=F32)
              + jnp.dot(h_hi, w_lo, preferred_element_type=F32))
    lane = lax.broadcasted_iota(jnp.int32, (tm, ROUTER_LANES), 1)
    logits = jnp.where(lane < N_EXPERTS, logits, -jnp.inf)
    e = jnp.exp(logits - jnp.max(logits, axis=-1, keepdims=True))
    aff = e / jnp.sum(e, axis=-1, keepdims=True)
    aff_ref[...] = aff.T[0:N_EXPERTS, :]


def _mix(og, proj, x, mod3, conv_w, wro, wco, wo, wr_pad, npost, npre, seq, cond_row_fn):
    t = x.shape[0]
    tm = 256
    halo = 16
    hb = tm // halo
    last_halo = t // halo - 1
    col = lambda c: (lambda i: (i, c))
    row_vec = pl.BlockSpec((1, D_MODEL), lambda i: (0, 0))
    wspec = pl.BlockSpec((D_MODEL, D_MODEL), lambda i: (0, 0))
    modspec = lambda c: pl.BlockSpec((1, 1, D_MODEL), lambda i: (cond_row_fn(i * tm), 0, c))
    tile = lambda c: pl.BlockSpec((tm, D_MODEL), col(c))
    prev = lambda c: pl.BlockSpec((halo, D_MODEL), lambda i: (jnp.maximum(i * hb - 1, 0), c))
    nxt = lambda c: pl.BlockSpec((halo, D_MODEL), lambda i: (jnp.minimum((i + 1) * hb, last_halo), c))
    return pl.pallas_call(
        functools.partial(_mix_kernel, tm=tm, seq=seq),
        grid=(t // tm,),
        in_specs=[tile(0), tile(3), tile(4), tile(5), prev(4), prev(5), nxt(4), nxt(5), tile(6), tile(7),
                  tile(0),
                  pl.BlockSpec((3, D_MODEL), lambda i: (0, 0)), wspec, wspec, wspec,
                  pl.BlockSpec((D_MODEL, ROUTER_LANES), lambda i: (0, 0)),
                  row_vec, row_vec, modspec(2), modspec(3), modspec(4)],
        out_specs=[tile(0), tile(0), pl.BlockSpec((N_EXPERTS, tm), lambda i: (0, i))],
        out_shape=[jax.ShapeDtypeStruct((t, D_MODEL), F32),
                   jax.ShapeDtypeStruct((t, D_MODEL), BF16),
                   jax.ShapeDtypeStruct((N_EXPERTS, t), F32)],
        compiler_params=pltpu.CompilerParams(dimension_semantics=("arbitrary",),
                                             vmem_limit_bytes=VMEM_LIMIT),
        name="mix_out",
    )(og, proj, proj, proj, proj, proj, proj, proj, proj, proj, x,
      conv_w, wro, wco, wo, wr_pad, npost, npre, mod3, mod3, mod3)


def _expert_kernel(xs_ref, wg_ref, wu_ref, wd_ref, o_ref, *, sub):
    wg = wg_ref[0].astype(BF16)
    wu = wu_ref[0].astype(BF16)
    wd = wd_ref[0].astype(BF16)
    for r in range(xs_ref.shape[1] // sub):
        rows = slice(r * sub, (r + 1) * sub)
        x = xs_ref[0, rows, :]
        g = jnp.dot(x, wg, preferred_element_type=F32)
        u = jnp.dot(x, wu, preferred_element_type=F32)
        hid = (g * _sigmoid(g) * u).astype(BF16)
        o_ref[0, rows, :] = jnp.dot(hid, wd, preferred_element_type=F32).astype(BF16)


def _experts(xs, wg, wu, wd):
    e, cap, d = xs.shape
    ff = wg.shape[2]
    tr = min(cap, EXPERT_ROWS_PER_STEP)
    return pl.pallas_call(
        functools.partial(_expert_kernel, sub=EXPERT_SUB_ROWS),
        grid=(e, cap // tr),
        in_specs=[pl.BlockSpec((1, tr, d), lambda a, r: (a, r, 0)),
                  pl.BlockSpec((1, d, ff), lambda a, r: (a, 0, 0)),
                  pl.BlockSpec((1, d, ff), lambda a, r: (a, 0, 0)),
                  pl.BlockSpec((1, ff, d), lambda a, r: (a, 0, 0))],
        out_specs=pl.BlockSpec((1, tr, d), lambda a, r: (a, r, 0)),
        out_shape=jax.ShapeDtypeStruct((e, cap, d), BF16),
        compiler_params=pltpu.CompilerParams(dimension_semantics=("arbitrary", "arbitrary"),
                                             vmem_limit_bytes=EXPERT_VMEM_LIMIT),
        name="experts",
    )(xs, wg, wu, wd)


def _route_kernel(aff_ref, pos_ref, excl_ref, *, cap):
    ne, nblk, lanes = aff_ref.shape
    bits = pltpu.bitcast(aff_ref[...], jnp.int32)

    def count(mask):
        m = jnp.where(mask, 1.0, 0.0)
        return jnp.sum(jnp.sum(m, axis=1, keepdims=True), axis=2, keepdims=True)

    def bit_step(it, thr):
        cand = thr | jnp.left_shift(jnp.int32(1), 30 - it)
        return jnp.where(count(bits >= cand) >= cap, cand, thr)

    thr = lax.fori_loop(0, 31, bit_step, jnp.zeros((ne, 1, 1), jnp.int32))
    gt = bits > thr
    eq = bits == thr
    need = cap - count(gt)

    li = lax.broadcasted_iota(jnp.int32, (lanes, lanes), 0)
    lj = lax.broadcasted_iota(jnp.int32, (lanes, lanes), 1)
    incl_lanes = jnp.where(li <= lj, 1.0, 0.0).astype(BF16)
    all_lanes = jnp.ones((lanes, lanes), BF16)
    bi = lax.broadcasted_iota(jnp.int32, (nblk, nblk), 0)
    bj = lax.broadcasted_iota(jnp.int32, (nblk, nblk), 1)
    earlier_blocks = jnp.where(bj < bi, 1.0, 0.0).astype(BF16)

    def ranks(mask):
        m = jnp.where(mask, 1.0, 0.0).reshape(ne * nblk, lanes).astype(BF16)
        incl = jnp.dot(m, incl_lanes, preferred_element_type=F32).reshape(ne, nblk, lanes)
        tot = jnp.dot(m, all_lanes, preferred_element_type=F32).reshape(ne, nblk, lanes).astype(BF16)
        excl = jnp.stack([jnp.dot(earlier_blocks, tot[i], preferred_element_type=F32) for i in range(ne)])
        return excl, incl

    ex_eq, in_eq = ranks(eq)
    sel = gt | (eq & (ex_eq + in_eq - 1.0 < need))
    ex_sel, in_sel = ranks(sel)
    pos_ref[...] = jnp.where(sel, ex_sel + in_sel - 1.0, -1.0).astype(jnp.int32)
    excl_ref[...] = ex_sel.astype(jnp.int32)


def _route(aff_t, cap):
    ne, t = aff_t.shape
    nblk = t // 128
    shape = (ne, nblk, 128)
    full = pl.BlockSpec(shape, lambda i: (0, 0, 0))
    pos, excl = pl.pallas_call(
        functools.partial(_route_kernel, cap=cap),
        grid=(1,),
        in_specs=[full],
        out_specs=[full, full],
        out_shape=[jax.ShapeDtypeStruct(shape, jnp.int32)] * 2,
        compiler_params=pltpu.CompilerParams(dimension_semantics=("arbitrary",),
                                             vmem_limit_bytes=VMEM_LIMIT),
        name="route",
    )(aff_t.reshape(shape))
    base = jnp.concatenate([excl[:, :, 0], jnp.full((ne, 1), cap, jnp.int32)], axis=1)
    return pos.reshape(ne, t), base


ROUTE_TILE = 256
ROUTE_WINDOW = 64
SLOT_ALIGN = 16
DISPATCH_GROUP = 4
DISPATCH_TILES_PER_STEP = 4


def _window(base_ref, e, blk, w, cap):
    lo = base_ref[e, blk] // SLOT_ALIGN + w * (ROUTE_WINDOW // SLOT_ALIGN)
    return lo * SLOT_ALIGN, jnp.minimum(lo, (cap - ROUTE_WINDOW) // SLOT_ALIGN) * SLOT_ALIGN


def _n_windows(base_ref, e, blk, cap):
    lo, _ = _window(base_ref, e, blk, 0, cap)
    end = base_ref[e, blk + ROUTE_TILE // 128]
    return (end - lo + ROUTE_WINDOW - 1) // ROUTE_WINDOW


def _dispatch_kernel(base_ref, h_ref, pos_ref, xs_ref, *, cap):
    g, j = pl.program_id(0), pl.program_id(1)

    @pl.when(j == 0)
    def _():
        xs_ref[...] = jnp.zeros_like(xs_ref)

    ri = lax.broadcasted_iota(jnp.int32, (ROUTE_WINDOW, ROUTE_TILE), 0)

    def onehot(prow, lo, off, first):
        hit = prow - off == ri
        if not first:
            hit = hit & (prow >= lo)
        return jnp.where(hit, 1.0, 0.0).astype(BF16)

    def add_rows(ge, off, rows):
        sl = pl.ds(pl.multiple_of(off, SLOT_ALIGN), ROUTE_WINDOW)
        xs_ref[ge, sl, :] = xs_ref[ge, sl, :] + rows.astype(BF16)

    for sub in range(DISPATCH_TILES_PER_STEP):
        blk = (j * DISPATCH_TILES_PER_STEP + sub) * (ROUTE_TILE // 128)
        tok = slice(sub * ROUTE_TILE, (sub + 1) * ROUTE_TILE)
        h = h_ref[tok, :]
        prows, offs = [], []
        for ge in range(DISPATCH_GROUP):
            e = g * DISPATCH_GROUP + ge
            prows.append(pos_ref[pl.ds(e, 1), tok])
            offs.append(_window(base_ref, e, blk, 0, cap))
        sel = jnp.concatenate([onehot(prows[ge], *offs[ge], True) for ge in range(DISPATCH_GROUP)], axis=0)
        rows = jnp.dot(sel, h, preferred_element_type=F32)
        for ge in range(DISPATCH_GROUP):
            add_rows(ge, offs[ge][1], rows[ge * ROUTE_WINDOW:(ge + 1) * ROUTE_WINDOW])

        for ge in range(DISPATCH_GROUP):
            e = g * DISPATCH_GROUP + ge

            def extra(w, carry, ge=ge, e=e, blk=blk, h=h, prow=prows[ge]):
                lo, off = _window(base_ref, e, blk, w, cap)
                add_rows(ge, off, jnp.dot(onehot(prow, lo, off, False), h, preferred_element_type=F32))
                return carry

            lax.fori_loop(1, _n_windows(base_ref, e, blk, cap), extra, 0)


def _dispatch(base, h2, pos, cap):
    t, d = h2.shape
    ne = pos.shape[0]
    step = ROUTE_TILE * DISPATCH_TILES_PER_STEP
    return pl.pallas_call(
        functools.partial(_dispatch_kernel, cap=cap),
        grid_spec=pltpu.PrefetchScalarGridSpec(
            num_scalar_prefetch=1,
            grid=(ne // DISPATCH_GROUP, t // step),
            in_specs=[pl.BlockSpec((step, d), lambda g, j, b: (j, 0)),
                      pl.BlockSpec((ne, step), lambda g, j, b: (0, j))],
            out_specs=pl.BlockSpec((DISPATCH_GROUP, cap, d), lambda g, j, b: (g, 0, 0))),
        out_shape=jax.ShapeDtypeStruct((ne, cap, d), BF16),
        compiler_params=pltpu.CompilerParams(dimension_semantics=("arbitrary", "arbitrary"),
                                             vmem_limit_bytes=VMEM_LIMIT),
        name="dispatch",
    )(base, h2, pos)


def _combine_kernel(base_ref, *refs, cap):
    win_refs = refs[:N_EXPERTS]
    pos_ref, aff_ref, x1_ref, g_ref, g2_ref, eo_ref, o_ref, f_ref, wbuf, sem = refs[N_EXPERTS:]
    blk = pl.program_id(0) * (ROUTE_TILE // 128)
    ri = lax.broadcasted_iota(jnp.int32, (ROUTE_WINDOW, ROUTE_TILE), 0)
    tn = (((0,), (0,)), ((), ()))

    def weights(e, lo, off, first):
        prow = pos_ref[e:e + 1, :]
        hit = prow - off == ri
        if not first:
            hit = hit & (prow >= lo)
        return jnp.where(hit, aff_ref[e:e + 1, :], 0.0).astype(BF16)

    q = jnp.concatenate([weights(e, *_window(base_ref, e, blk, 0, cap), True) for e in range(N_EXPERTS)], axis=0)
    win = jnp.concatenate([w[0] for w in win_refs], axis=0)
    f_ref[...] = lax.dot_general(q, win, tn, preferred_element_type=F32)

    for e in range(N_EXPERTS):
        def extra(w, carry, e=e):
            lo, off = _window(base_ref, e, blk, w, cap)
            cp = pltpu.make_async_copy(eo_ref.at[e, pl.ds(pl.multiple_of(off, SLOT_ALIGN), ROUTE_WINDOW), :],
                                       wbuf, sem)
            cp.start()
            cp.wait()
            f_ref[...] += lax.dot_general(weights(e, lo, off, False), wbuf[...], tn, preferred_element_type=F32)
            return carry

        lax.fori_loop(1, _n_windows(base_ref, e, blk, cap), extra, 0)

    o_ref[...] = x1_ref[...] + g2_ref[0] * _rms(f_ref[...], g_ref[...])


def _combine(base, eo, pos, aff_t, x1, mod3, npost, cond_row_fn, cap):
    t, d = x1.shape
    ne = pos.shape[0]

    def win_spec(e):
        return pl.BlockSpec((pl.Element(1), pl.Element(ROUTE_WINDOW), pl.Element(d)),
                            lambda j, b: (e, _window(b, e, j * (ROUTE_TILE // 128), 0, cap)[1], 0))

    tile = pl.BlockSpec((ROUTE_TILE, d), lambda j, b: (j, 0))
    etile = pl.BlockSpec((ne, ROUTE_TILE), lambda j, b: (0, j))
    return pl.pallas_call(
        functools.partial(_combine_kernel, cap=cap),
        grid_spec=pltpu.PrefetchScalarGridSpec(
            num_scalar_prefetch=1,
            grid=(t // ROUTE_TILE,),
            in_specs=[win_spec(e) for e in range(ne)] + [
                etile, etile, tile,
                pl.BlockSpec((1, d), lambda j, b: (0, 0)),
                pl.BlockSpec((1, 1, d), lambda j, b: (cond_row_fn(j * ROUTE_TILE), 0, 5)),
                pl.BlockSpec(memory_space=pl.ANY)],
            out_specs=tile,
            scratch_shapes=[pltpu.VMEM((ROUTE_TILE, d), F32), pltpu.VMEM((ROUTE_WINDOW, d), BF16),
                            pltpu.SemaphoreType.DMA(())]),
        out_shape=jax.ShapeDtypeStruct((t, d), F32),
        compiler_params=pltpu.CompilerParams(dimension_semantics=("arbitrary",),
                                             vmem_limit_bytes=VMEM_LIMIT),
        name="combine",
    )(base, *([eo] * ne), pos, aff_t, x1, npost, mod3, eo)


def _trunk(x3, mod3, cond_row_fn, weights, lgs, *, rope_tabs, init, emit_state):
    (n_pre_mix, n_post_mix, n_pre_ffn, n_post_ffn, w_in, gn_g, conv_w, wro, wco, wo, wr_pad, wg, wu, wd) = weights
    b, seq, d = x3.shape
    t = b * seq
    x = x3.reshape(t, d)
    proj = _inproj(x, mod3, n_pre_mix, w_in, cond_row_fn)
    ret = _retention(proj, lgs, gn_g, seq, rope_tabs=rope_tabs, init=init, emit_state=emit_state)
    og = ret[0]
    x1, h2, aff_t = _mix(og, proj, x, mod3, conv_w, wro, wco, wo, wr_pad, n_post_mix, n_pre_ffn, seq, cond_row_fn)
    cap = CAPACITY_FACTOR * t // N_EXPERTS
    pos, base = _route(aff_t, cap)
    eo = _experts(_dispatch(base, h2, pos, cap), wg, wu, wd)
    y = _combine(base, eo, pos, aff_t, x1, mod3, n_post_ffn, cond_row_fn, cap)
    return y.reshape(b, seq, d), ret[1:]


def kernel(x_prompt, x_sample, state_ret_fwd, state_ret_bwd, c, c_ctx, w_ada, b_ada, norm_pre_mix, norm_post_mix,
           norm_pre_ffn, norm_post_ffn, w_in, ret_decay_fwd, ret_decay_bwd, ret_norm_g, conv_w, w_ret_o, w_conv_o,
           w_o, w_router, w_gate, w_up, w_down):
    depth = w_ada.shape[0]
    assert depth == 1
    dec_b, dec_seq = x_sample.shape[0], x_sample.shape[1]
    xp, xs = x_prompt, x_sample
    l = 0
    cond = jnp.zeros((N_COND_ROWS, D_MODEL), F32).at[0:dec_b].set(c).at[CTX_COND_ROW].set(c_ctx)
    mod3 = _ada(cond, w_ada[l], b_ada[l][None, :]).reshape(N_COND_ROWS, 1, 6 * D_MODEL)
    lgs = jnp.stack([-jax.nn.softplus(-ret_decay_fwd[l].astype(F32)),
                     -jax.nn.softplus(-ret_decay_bwd[l].astype(F32))])
    wr_pad = jnp.pad(w_router[l], ((0, 0), (0, ROUTER_LANES - N_EXPERTS)))
    weights = (norm_pre_mix[l][None, :], norm_post_mix[l][None, :], norm_pre_ffn[l][None, :],
               norm_post_ffn[l][None, :], w_in[l].astype(BF16), ret_norm_g[l][None, :], conv_w[l],
               w_ret_o[l].astype(BF16), w_conv_o[l].astype(BF16), w_o[l].astype(BF16), wr_pad,
               w_gate[l], w_up[l], w_down[l])
    yp, (s_f, s_b) = _trunk(xp, mod3, lambda r: CTX_COND_ROW, weights, lgs,
                            rope_tabs=None, init=None, emit_state=True)
    ys, _ = _trunk(xs, mod3, lambda r: r // dec_seq, weights, lgs,
                   rope_tabs=_rope_tables(dec_seq), init=(state_ret_fwd, state_ret_bwd), emit_state=False)
    return (yp, ys, s_f, s_b)
```

```python
import functools

import jax
import jax.numpy as jnp
import numpy as np
from jax import lax
from jax.experimental import pallas as pl
from jax.experimental.pallas import tpu as pltpu

F32 = jnp.float32
BF16 = jnp.bfloat16

D_MODEL = 1024
N_HEADS = 8
DK = 64
DV = 128
CHUNK = 128
GRID_W = 64
N_EXPERTS = 16
CAPACITY_FACTOR = 2
D_IN_TOTAL = 8192
RMS_EPS = 1e-6
GN_EPS = 1e-5
ROPE_BASE = 10000.0
N_COND_ROWS = 16
CTX_COND_ROW = 8
ROUTER_LANES = 128
RET_CHUNKS_PER_STEP = 16
MIX_TILE = 512
MIX_SUB_TILE = 256
VMEM_LIMIT = 48 * 1024 * 1024
EXPERT_ROWS_PER_STEP = 1024
EXPERT_SUB_ROWS = 512
EXPERT_VMEM_LIMIT = 56 * 1024 * 1024


def _sigmoid(x):
    return 1.0 / (1.0 + jnp.exp(-x))


def _rms(x, g):
    return x * lax.rsqrt(jnp.mean(x * x, axis=-1, keepdims=True) + RMS_EPS) * g


def _ada_kernel(c_ref, w_ref, b_ref, o_ref):
    c = c_ref[...]
    s = c * _sigmoid(c)
    o_ref[...] = jnp.dot(s, w_ref[...], preferred_element_type=F32,
                         precision=lax.Precision.HIGHEST) + b_ref[...]


def _ada(cond, w_ada, b_ada):
    n = w_ada.shape[1]
    tn = 1024
    return pl.pallas_call(
        _ada_kernel,
        grid=(n // tn,),
        in_specs=[pl.BlockSpec((N_COND_ROWS, D_MODEL), lambda j: (0, 0)),
                  pl.BlockSpec((D_MODEL, tn), lambda j: (0, j)),
                  pl.BlockSpec((1, tn), lambda j: (0, j))],
        out_specs=pl.BlockSpec((N_COND_ROWS, tn), lambda j: (0, j)),
        out_shape=jax.ShapeDtypeStruct((N_COND_ROWS, n), F32),
        compiler_params=pltpu.CompilerParams(dimension_semantics=("arbitrary",),
                                             vmem_limit_bytes=VMEM_LIMIT),
        name="ada_mod",
    )(cond, w_ada, b_ada)


def _inproj_kernel(x_ref, g_ref, sh_ref, sc_ref, w_ref, o_ref, h_ref):
    @pl.when(pl.program_id(1) == 0)
    def _():
        h = _rms(x_ref[...], g_ref[...]) * (1.0 + sc_ref[0]) + sh_ref[0]
        h_ref[...] = h.astype(BF16)

    o_ref[...] = jnp.dot(h_ref[...], w_ref[...], preferred_element_type=F32).astype(BF16)


def _inproj(x, mod3, g, w_bf16, cond_row_fn):
    t = x.shape[0]
    tm, tn = 1024, 2048
    return pl.pallas_call(
        _inproj_kernel,
        grid=(t // tm, D_IN_TOTAL // tn),
        in_specs=[pl.BlockSpec((tm, D_MODEL), lambda i, j: (i, 0)),
                  pl.BlockSpec((1, D_MODEL), lambda i, j: (0, 0)),
                  pl.BlockSpec((1, 1, D_MODEL), lambda i, j: (cond_row_fn(i * tm), 0, 0)),
                  pl.BlockSpec((1, 1, D_MODEL), lambda i, j: (cond_row_fn(i * tm), 0, 1)),
                  pl.BlockSpec((D_MODEL, tn), lambda i, j: (0, j))],
        out_specs=pl.BlockSpec((tm, tn), lambda i, j: (i, j)),
        out_shape=jax.ShapeDtypeStruct((t, D_IN_TOTAL), BF16),
        scratch_shapes=[pltpu.VMEM((tm, D_MODEL), BF16)],
        compiler_params=pltpu.CompilerParams(dimension_semantics=("arbitrary", "arbitrary"),
                                             vmem_limit_bytes=VMEM_LIMIT),
        name="inproj",
    )(x, g, mod3, mod3, w_bf16)


def _ret_kernel(*refs, nc, bb, use_rope, has_init, emit_state):
    it = iter(refs)
    lg_ref = next(it)
    q_ref, k_ref, v_ref, gr_ref, gn_ref = (next(it) for _ in range(5))
    cos_ref = sin_ref = s0f_ref = s0b_ref = sf_ref = sb_ref = None
    if use_rope:
        cos_ref, sin_ref = next(it), next(it)
    if has_init:
        s0f_ref, s0b_ref = next(it), next(it)
    o_ref = next(it)
    if emit_state:
        sf_ref, sb_ref = next(it), next(it)
    k16_s, kvf_s, kvb_s, rcat_s, r0 = (next(it) for _ in range(5))

    p = pl.program_id(1)
    lgf_a, lgf_b = lg_ref[0, 2 * p], lg_ref[0, 2 * p + 1]
    lgb_a, lgb_b = lg_ref[1, 2 * p], lg_ref[1, 2 * p + 1]

    ri = lax.broadcasted_iota(jnp.int32, (CHUNK, 2 * DK), 0)
    ci = lax.broadcasted_iota(jnp.int32, (CHUNK, 2 * DK), 1)
    lane_a = ci < DK
    rowf = ri.astype(F32)
    diff = rowf - ci.astype(F32)
    lgf_lane = jnp.where(lane_a, lgf_a, lgf_b)
    lgb_lane = jnp.where(lane_a, lgb_a, lgb_b)
    xi_f = jnp.exp(lgf_lane * (rowf + 1.0))
    xi_b = jnp.exp(lgb_lane * (CHUNK - rowf))
    zeta_f = jnp.exp(lgf_lane * (CHUNK - 1.0 - rowf))
    zeta_b = jnp.exp(lgb_lane * rowf)

    def decay_matrix(lgf, lgb):
        return jnp.where(diff > 0, jnp.exp(lgf * diff),
                         jnp.where(diff < 0, jnp.exp(lgb * (-diff)), 2.0))

    dm_a = decay_matrix(lgf_a, lgb_a)
    dm_b = decay_matrix(lgf_b, lgb_b)

    r2 = lax.broadcasted_iota(jnp.int32, (2 * DK, 2 * DV), 0)
    c2 = lax.broadcasted_iota(jnp.int32, (2 * DK, 2 * DV), 1)
    top = r2 < DK
    blk = (top == (c2 < DV)).astype(F32)
    cd_f = jnp.exp(jnp.where(top, lgf_a, lgf_b) * float(CHUNK)) * blk
    cd_b = jnp.exp(jnp.where(top, lgb_a, lgb_b) * float(CHUNK)) * blk

    def chunk_rows(c):
        return pl.ds(pl.multiple_of(c * CHUNK, CHUNK), CHUNK)

    def rope(x, c):
        rows = chunk_rows(lax.rem(c, nc))
        fwd = pltpu.roll(x, 2 * DK - 16, 1)
        bwd = pltpu.roll(x, 16, 1)
        swapped = jnp.where((ci & 31) < 16, fwd, bwd)
        return x * cos_ref[rows, :] + swapped * sin_ref[rows, :]

    def kv_body(c, carry):
        rows = chunk_rows(c)
        k = k_ref[rows, :].astype(F32) * (DK ** -0.5)
        if use_rope:
            k = rope(k, c)
        k16_s[rows, :] = k.astype(BF16)
        kz_t = jnp.concatenate([k * zeta_f, k * zeta_b], axis=1).T.astype(BF16)
        kv = jnp.dot(kz_t, v_ref[rows, :], preferred_element_type=F32)
        kvf_s[c] = kv[0:2 * DK] * blk
        kvb_s[c] = kv[2 * DK:4 * DK] * blk
        return carry

    lax.fori_loop(0, bb * nc, kv_body, 0, unroll=2)

    def load_state(s_ref, s):
        r0[...] = jnp.zeros_like(r0)
        r0[0:DK, 0:DV] = s_ref[s, 0, 0].astype(F32)
        r0[DK:2 * DK, DV:2 * DV] = s_ref[s, 0, 1].astype(F32)
        return r0[...]

    for s in range(bb):
        def fwd_body(n, rf, s=s):
            c = s * nc + n
            rcat_s[c, 0:2 * DK, :] = rf.astype(BF16)
            return cd_f * rf + kvf_s[c]

        def bwd_body(t, rb, s=s):
            c = s * nc + nc - 1 - t
            rcat_s[c, 2 * DK:4 * DK, :] = rb.astype(BF16)
            return cd_b * rb + kvb_s[c]

        zeros = jnp.zeros((2 * DK, 2 * DV), F32)
        rf_fin = lax.fori_loop(0, nc, fwd_body, load_state(s0f_ref, s) if has_init else zeros)
        rb_fin = lax.fori_loop(0, nc, bwd_body, load_state(s0b_ref, s) if has_init else zeros)
        if emit_state:
            for s_ref, r in ((sf_ref, rf_fin), (sb_ref, rb_fin)):
                s_ref[s, 0, 0] = r[0:DK, 0:DV]
                s_ref[s, 0, 1] = r[DK:2 * DK, DV:2 * DV]

    gn = gn_ref[...]

    def head_norm(o):
        mu = jnp.mean(o, axis=-1, keepdims=True)
        d = o - mu
        var = jnp.mean(d * d, axis=-1, keepdims=True)
        return d * lax.rsqrt(var + GN_EPS)

    def out_body(c, carry):
        rows = chunk_rows(c)
        q = q_ref[rows, :].astype(F32)
        if use_rope:
            q = rope(q, c)
        q16 = q.astype(BF16)
        zero = jnp.zeros_like(q16)
        v = v_ref[rows, :]
        q_ab = jnp.concatenate([jnp.where(lane_a, q16, zero), jnp.where(lane_a, zero, q16)], axis=0)
        s = lax.dot_general(q_ab, k16_s[rows, :], (((1,), (1,)), ((), ())), preferred_element_type=F32)
        o_a = jnp.dot((s[0:CHUNK] * dm_a).astype(BF16), v[:, 0:DV], preferred_element_type=F32)
        o_b = jnp.dot((s[CHUNK:2 * CHUNK] * dm_b).astype(BF16), v[:, DV:2 * DV], preferred_element_type=F32)
        qx = jnp.concatenate([(q * xi_f).astype(BF16), (q * xi_b).astype(BF16)], axis=1)
        oc = jnp.dot(qx, rcat_s[c], preferred_element_type=F32)
        y = jnp.concatenate([head_norm(o_a + oc[:, 0:DV]), head_norm(o_b + oc[:, DV:2 * DV])], axis=1)
        g = gr_ref[rows, :].astype(F32)
        o_ref[rows, :] = (g * _sigmoid(g) * (y * gn)).astype(BF16)
        return carry

    lax.fori_loop(0, bb * nc, out_body, 0, unroll=2)


def _retention(proj, lgs, gn_g, seq, *, rope_tabs=None, init=None, emit_state):
    t = proj.shape[0]
    b = t // seq
    nc = seq // CHUNK
    bb = max(1, RET_CHUNKS_PER_STEP // nc)
    rows = bb * seq
    use_rope = rope_tabs is not None
    has_init = init is not None
    pairs = N_HEADS // 2
    qk_blocks = (N_HEADS * DK) // (2 * DK)
    in_specs = [pl.BlockSpec(memory_space=pltpu.SMEM),
                pl.BlockSpec((rows, 2 * DK), lambda i, p: (i, p)),
                pl.BlockSpec((rows, 2 * DK), lambda i, p: (i, qk_blocks + p)),
                pl.BlockSpec((rows, 2 * DV), lambda i, p: (i, qk_blocks + p)),
                pl.BlockSpec((rows, 2 * DV), lambda i, p: (i, 2 * qk_blocks + p)),
                pl.BlockSpec((1, 2 * DV), lambda i, p: (0, p))]
    args = [lgs, proj, proj, proj, proj, gn_g]
    if use_rope:
        in_specs += [pl.BlockSpec((seq, 2 * DK), lambda i, p: (0, 0))] * 2
        args += list(rope_tabs)
    state_spec = pl.BlockSpec((bb, 1, 2, DK, DV), lambda i, p: (i, 0, p, 0, 0))
    if has_init:
        in_specs += [state_spec, state_spec]
        args += list(init)
    out_specs = [pl.BlockSpec((rows, 2 * DV), lambda i, p: (i, p))]
    out_shape = [jax.ShapeDtypeStruct((t, N_HEADS * DV), BF16)]
    if emit_state:
        out_specs += [state_spec, state_spec]
        out_shape += [jax.ShapeDtypeStruct((b, 1, N_HEADS, DK, DV), F32)] * 2
    return pl.pallas_call(
        functools.partial(_ret_kernel, nc=nc, bb=bb, use_rope=use_rope, has_init=has_init,
                          emit_state=emit_state),
        grid=(b // bb, pairs),
        in_specs=in_specs,
        out_specs=out_specs,
        out_shape=out_shape,
        scratch_shapes=[pltpu.VMEM((rows, 2 * DK), BF16),
                        pltpu.VMEM((bb * nc, 2 * DK, 2 * DV), F32),
                        pltpu.VMEM((bb * nc, 2 * DK, 2 * DV), F32),
                        pltpu.VMEM((bb * nc, 4 * DK, 2 * DV), BF16),
                        pltpu.VMEM((2 * DK, 2 * DV), F32)],
        compiler_params=pltpu.CompilerParams(dimension_semantics=("arbitrary", "arbitrary"),
                                             vmem_limit_bytes=VMEM_LIMIT),
        name="retention",
    )(*args)


def _rope_tables(seq):
    n_freq = DK // 4
    pos = np.arange(seq)
    inv = jnp.asarray(ROPE_BASE, F32) ** (-jnp.arange(n_freq, dtype=F32) / n_freq)
    ang_r = jnp.asarray(pos // GRID_W, F32)[:, None] * inv
    ang_c = jnp.asarray(pos % GRID_W, F32)[:, None] * inv
    cos = jnp.concatenate([jnp.cos(ang_r)] * 2 + [jnp.cos(ang_c)] * 2, axis=1)
    sin = jnp.concatenate([-jnp.sin(ang_r), jnp.sin(ang_r), -jnp.sin(ang_c), jnp.sin(ang_c)], axis=1)
    return jnp.tile(cos, (1, 2)), jnp.tile(sin, (1, 2))


def _mix_kernel(og_ref, cb_ref, cc_ref, cx_ref, ccp_ref, cxp_ref, ccn_ref, cxn_ref, ga_ref, gb_ref, x_ref,
                cw_ref, wro_ref, wco_ref, wo_ref, wr_ref, npost_ref, npre_ref, g1_ref, sh2_ref, sc2_ref,
                x1_ref, h2_ref, aff_ref, *, tm, sub, seq):
    i = pl.program_id(0)
    halo = ccp_ref.shape[0]
    wr = wr_ref[...]
    w_hi = wr.astype(BF16)
    w_lo = (wr - w_hi.astype(F32)).astype(BF16)
    row = lax.broadcasted_iota(jnp.int32, (sub, 1), 0)
    lane = lax.broadcasted_iota(jnp.int32, (sub, ROUTER_LANES), 1)

    def product_row(cc, cx, r):
        return cc[r:r + 1, :].astype(F32) * cx[r:r + 1, :].astype(F32)

    for s in range(tm // sub):
        r0 = s * sub
        rows = slice(r0, r0 + sub)
        has_prev = jnp.where((i * tm + r0) % seq != 0, 1.0, 0.0)
        has_next = jnp.where((i * tm + r0 + sub) % seq != 0, 1.0, 0.0)
        prev_row = product_row(ccp_ref, cxp_ref, halo - 1) if s == 0 else product_row(cc_ref, cx_ref, r0 - 1)
        next_row = (product_row(ccn_ref, cxn_ref, 0) if r0 + sub == tm
                    else product_row(cc_ref, cx_ref, r0 + sub))
        prod = cc_ref[rows, :].astype(F32) * cx_ref[rows, :].astype(F32)
        up = jnp.where(row == 0, prev_row * has_prev, pltpu.roll(prod, 1, 0))
        dn = jnp.where(row == sub - 1, next_row * has_next, pltpu.roll(prod, sub - 1, 0))
        u = up * cw_ref[0:1, :] + prod * cw_ref[1:2, :] + dn * cw_ref[2:3, :]
        y_conv = jnp.dot((cb_ref[rows, :].astype(F32) * u).astype(BF16), wco_ref[...],
                         preferred_element_type=F32)
        y_ret = jnp.dot(og_ref[rows, :], wro_ref[...], preferred_element_type=F32)
        merged = (_sigmoid(ga_ref[rows, :].astype(F32)) * y_ret
                  + _sigmoid(gb_ref[rows, :].astype(F32)) * y_conv)
        m = jnp.dot(merged.astype(BF16), wo_ref[...], preferred_element_type=F32)
        x1 = x_ref[rows, :] + g1_ref[0] * _rms(m, npost_ref[...])
        x1_ref[rows, :] = x1
        h2 = _rms(x1, npre_ref[...]) * (1.0 + sc2_ref[0]) + sh2_ref[0]
        h_hi = h2.astype(BF16)
        h2_ref[rows, :] = h_hi
        h_lo = (h2 - h_hi.astype(F32)).astype(BF16)
        logits = (jnp.dot(h_hi, w_hi, preferred_element_type=F32)
                  + jnp.dot(h_lo, w_hi, preferred_element_type=F32)
                  + jnp.dot(h_hi, w_lo, preferred_element_type=F32))
        logits = jnp.where(lane < N_EXPERTS, logits, -jnp.inf)
        e = jnp.exp(logits - jnp.max(logits, axis=-1, keepdims=True))
        aff = e / jnp.sum(e, axis=-1, keepdims=True)
        aff_ref[:, rows] = aff.T[0:N_EXPERTS, :]


def _mix(og, proj, x, mod3, conv_w, wro, wco, wo, wr_pad, npost, npre, seq, cond_row_fn):
    t = x.shape[0]
    tm = MIX_TILE
    halo = 16
    hb = tm // halo
    last_halo = t // halo - 1
    col = lambda c: (lambda i: (i, c))
    row_vec = pl.BlockSpec((1, D_MODEL), lambda i: (0, 0))
    wspec = pl.BlockSpec((D_MODEL, D_MODEL), lambda i: (0, 0))
    modspec = lambda c: pl.BlockSpec((1, 1, D_MODEL), lambda i: (cond_row_fn(i * tm), 0, c))
    tile = lambda c: pl.BlockSpec((tm, D_MODEL), col(c))
    prev = lambda c: pl.BlockSpec((halo, D_MODEL), lambda i: (jnp.maximum(i * hb - 1, 0), c))
    nxt = lambda c: pl.BlockSpec((halo, D_MODEL), lambda i: (jnp.minimum((i + 1) * hb, last_halo), c))
    return pl.pallas_call(
        functools.partial(_mix_kernel, tm=tm, sub=MIX_SUB_TILE, seq=seq),
        grid=(t // tm,),
        in_specs=[tile(0), tile(3), tile(4), tile(5), prev(4), prev(5), nxt(4), nxt(5), tile(6), tile(7),
                  tile(0),
                  pl.BlockSpec((3, D_MODEL), lambda i: (0, 0)), wspec, wspec, wspec,
                  pl.BlockSpec((D_MODEL, ROUTER_LANES), lambda i: (0, 0)),
                  row_vec, row_vec, modspec(2), modspec(3), modspec(4)],
        out_specs=[tile(0), tile(0), pl.BlockSpec((N_EXPERTS, tm), lambda i: (0, i))],
        out_shape=[jax.ShapeDtypeStruct((t, D_MODEL), F32),
                   jax.ShapeDtypeStruct((t, D_MODEL), BF16),
                   jax.ShapeDtypeStruct((N_EXPERTS, t), F32)],
        compiler_params=pltpu.CompilerParams(dimension_semantics=("arbitrary",),
                                             vmem_limit_bytes=VMEM_LIMIT),
        name="mix_out",
    )(og, proj, proj, proj, proj, proj, proj, proj, proj, proj, x,
      conv_w, wro, wco, wo, wr_pad, npost, npre, mod3, mod3, mod3)


def _expert_kernel(xs_ref, wg_ref, wu_ref, wd_ref, o_ref, *, sub):
    wg = wg_ref[0].astype(BF16)
    wu = wu_ref[0].astype(BF16)
    wd = wd_ref[0].astype(BF16)
    for r in range(xs_ref.shape[1] // sub):
        rows = slice(r * sub, (r + 1) * sub)
        x = xs_ref[0, rows, :]
        g = jnp.dot(x, wg, preferred_element_type=F32)
        u = jnp.dot(x, wu, preferred_element_type=F32)
        hid = (g * _sigmoid(g) * u).astype(BF16)
        o_ref[0, rows, :] = jnp.dot(hid, wd, preferred_element_type=F32).astype(BF16)


def _experts(xs, wg, wu, wd):
    e, cap, d = xs.shape
    ff = wg.shape[2]
    tr = min(cap, EXPERT_ROWS_PER_STEP)
    return pl.pallas_call(
        functools.partial(_expert_kernel, sub=EXPERT_SUB_ROWS),
        grid=(e, cap // tr),
        in_specs=[pl.BlockSpec((1, tr, d), lambda a, r: (a, r, 0)),
                  pl.BlockSpec((1, d, ff), lambda a, r: (a, 0, 0)),
                  pl.BlockSpec((1, d, ff), lambda a, r: (a, 0, 0)),
                  pl.BlockSpec((1, ff, d), lambda a, r: (a, 0, 0))],
        out_specs=pl.BlockSpec((1, tr, d), lambda a, r: (a, r, 0)),
        out_shape=jax.ShapeDtypeStruct((e, cap, d), BF16),
        compiler_params=pltpu.CompilerParams(dimension_semantics=("arbitrary", "arbitrary"),
                                             vmem_limit_bytes=EXPERT_VMEM_LIMIT),
        name="experts",
    )(xs, wg, wu, wd)


def _route_kernel(aff_ref, pos_ref, excl_ref, *, cap):
    ne, nblk, lanes = aff_ref.shape
    bits = pltpu.bitcast(aff_ref[...], jnp.int32)

    def count(mask):
        m = jnp.where(mask, 1.0, 0.0)
        return jnp.sum(jnp.sum(m, axis=1, keepdims=True), axis=2, keepdims=True)

    def bit_step(it, thr):
        cand = thr | jnp.left_shift(jnp.int32(1), 30 - it)
        return jnp.where(count(bits >= cand) >= cap, cand, thr)

    thr = lax.fori_loop(0, 31, bit_step, jnp.zeros((ne, 1, 1), jnp.int32))
    gt = bits > thr
    eq = bits == thr
    need = cap - count(gt)

    li = lax.broadcasted_iota(jnp.int32, (lanes, lanes), 0)
    lj = lax.broadcasted_iota(jnp.int32, (lanes, lanes), 1)
    incl_lanes = jnp.where(li <= lj, 1.0, 0.0).astype(BF16)
    all_lanes = jnp.ones((lanes, lanes), BF16)
    bi = lax.broadcasted_iota(jnp.int32, (nblk, nblk), 0)
    bj = lax.broadcasted_iota(jnp.int32, (nblk, nblk), 1)
    earlier_blocks = jnp.where(bj < bi, 1.0, 0.0).astype(BF16)

    def ranks(mask):
        m = jnp.where(mask, 1.0, 0.0).reshape(ne * nblk, lanes).astype(BF16)
        incl = jnp.dot(m, incl_lanes, preferred_element_type=F32).reshape(ne, nblk, lanes)
        tot = jnp.dot(m, all_lanes, preferred_element_type=F32).reshape(ne, nblk, lanes).astype(BF16)
        excl = jnp.stack([jnp.dot(earlier_blocks, tot[i], preferred_element_type=F32) for i in range(ne)])
        return excl, incl

    ex_eq, in_eq = ranks(eq)
    sel = gt | (eq & (ex_eq + in_eq - 1.0 < need))
    ex_sel, in_sel = ranks(sel)
    pos_ref[...] = jnp.where(sel, ex_sel + in_sel - 1.0, -1.0).astype(jnp.int32)
    excl_ref[...] = ex_sel.astype(jnp.int32)


def _route(aff_t, cap):
    ne, t = aff_t.shape
    nblk = t // 128
    shape = (ne, nblk, 128)
    full = pl.BlockSpec(shape, lambda i: (0, 0, 0))
    pos, excl = pl.pallas_call(
        functools.partial(_route_kernel, cap=cap),
        grid=(1,),
        in_specs=[full],
        out_specs=[full, full],
        out_shape=[jax.ShapeDtypeStruct(shape, jnp.int32)] * 2,
        compiler_params=pltpu.CompilerParams(dimension_semantics=("arbitrary",),
                                             vmem_limit_bytes=VMEM_LIMIT),
        name="route",
    )(aff_t.reshape(shape))
    base = jnp.concatenate([excl[:, :, 0], jnp.full((ne, 1), cap, jnp.int32)], axis=1)
    return pos.reshape(ne, t), base


ROUTE_TILE = 256
ROUTE_WINDOW = 64
SLOT_ALIGN = 16
DISPATCH_GROUP = 4
DISPATCH_TILES_PER_STEP = 4


def _window(base_ref, e, blk, w, cap):
    lo = base_ref[e, blk] // SLOT_ALIGN + w * (ROUTE_WINDOW // SLOT_ALIGN)
    return lo * SLOT_ALIGN, jnp.minimum(lo, (cap - ROUTE_WINDOW) // SLOT_ALIGN) * SLOT_ALIGN


def _n_windows(base_ref, e, blk, cap):
    lo, _ = _window(base_ref, e, blk, 0, cap)
    end = base_ref[e, blk + ROUTE_TILE // 128]
    return (end - lo + ROUTE_WINDOW - 1) // ROUTE_WINDOW


def _dispatch_kernel(base_ref, h_ref, pos_ref, xs_ref, *, cap):
    g, j = pl.program_id(0), pl.program_id(1)

    @pl.when(j == 0)
    def _():
        xs_ref[...] = jnp.zeros_like(xs_ref)

    ri = lax.broadcasted_iota(jnp.int32, (ROUTE_WINDOW, ROUTE_TILE), 0)

    def onehot(prow, lo, off, first):
        hit = prow - off == ri
        if not first:
            hit = hit & (prow >= lo)
        return jnp.where(hit, 1.0, 0.0).astype(BF16)

    def add_rows(ge, off, rows):
        sl = pl.ds(pl.multiple_of(off, SLOT_ALIGN), ROUTE_WINDOW)
        xs_ref[ge, sl, :] = xs_ref[ge, sl, :] + rows.astype(BF16)

    for sub in range(DISPATCH_TILES_PER_STEP):
        blk = (j * DISPATCH_TILES_PER_STEP + sub) * (ROUTE_TILE // 128)
        tok = slice(sub * ROUTE_TILE, (sub + 1) * ROUTE_TILE)
        h = h_ref[tok, :]
        prows, offs = [], []
        for ge in range(DISPATCH_GROUP):
            e = g * DISPATCH_GROUP + ge
            prows.append(pos_ref[pl.ds(e, 1), tok])
            offs.append(_window(base_ref, e, blk, 0, cap))
        sel = jnp.concatenate([onehot(prows[ge], *offs[ge], True) for ge in range(DISPATCH_GROUP)], axis=0)
        rows = jnp.dot(sel, h, preferred_element_type=F32)
        for ge in range(DISPATCH_GROUP):
            add_rows(ge, offs[ge][1], rows[ge * ROUTE_WINDOW:(ge + 1) * ROUTE_WINDOW])

        for ge in range(DISPATCH_GROUP):
            e = g * DISPATCH_GROUP + ge

            def extra(w, carry, ge=ge, e=e, blk=blk, h=h, prow=prows[ge]):
                lo, off = _window(base_ref, e, blk, w, cap)
                add_rows(ge, off, jnp.dot(onehot(prow, lo, off, False), h, preferred_element_type=F32))
                return carry

            lax.fori_loop(1, _n_windows(base_ref, e, blk, cap), extra, 0)


def _dispatch(base, h2, pos, cap):
    t, d = h2.shape
    ne = pos.shape[0]
    step = ROUTE_TILE * DISPATCH_TILES_PER_STEP
    return pl.pallas_call(
        functools.partial(_dispatch_kernel, cap=cap),
        grid_spec=pltpu.PrefetchScalarGridSpec(
            num_scalar_prefetch=1,
            grid=(ne // DISPATCH_GROUP, t // step),
            in_specs=[pl.BlockSpec((step, d), lambda g, j, b: (j, 0)),
                      pl.BlockSpec((ne, step), lambda g, j, b: (0, j))],
            out_specs=pl.BlockSpec((DISPATCH_GROUP, cap, d), lambda g, j, b: (g, 0, 0))),
        out_shape=jax.ShapeDtypeStruct((ne, cap, d), BF16),
        compiler_params=pltpu.CompilerParams(dimension_semantics=("arbitrary", "arbitrary"),
                                             vmem_limit_bytes=VMEM_LIMIT),
        name="dispatch",
    )(base, h2, pos)


def _combine_kernel(base_ref, pos_ref, aff_ref, x1_ref, g_ref, g2_ref, eo_ref, o_ref,
                    f_ref, win_buf, win_sem, extra_buf, extra_sem, *, cap, n_tiles):
    j = pl.program_id(0)
    blocks_per_tile = ROUTE_TILE // 128
    blk = j * blocks_per_tile
    slot = lax.rem(j, 2)
    ri = lax.broadcasted_iota(jnp.int32, (ROUTE_WINDOW, ROUTE_TILE), 0)
    tn = (((0,), (0,)), ((), ()))

    def window_copy(e, off, dst, sem):
        return pltpu.make_async_copy(eo_ref.at[e, pl.ds(pl.multiple_of(off, SLOT_ALIGN), ROUTE_WINDOW), :],
                                     dst, sem)

    def first_window_copies(tile, buf_slot):
        return [window_copy(e, _window(base_ref, e, tile * blocks_per_tile, 0, cap)[1],
                            win_buf.at[buf_slot, pl.ds(e * ROUTE_WINDOW, ROUTE_WINDOW), :], win_sem.at[buf_slot])
                for e in range(N_EXPERTS)]

    @pl.when(j == 0)
    def _():
        for cp in first_window_copies(0, 0):
            cp.start()

    @pl.when(j + 1 < n_tiles)
    def _():
        for cp in first_window_copies(j + 1, 1 - slot):
            cp.start()

    def weights(e, lo, off, first):
        prow = pos_ref[e:e + 1, :]
        hit = prow - off == ri
        if not first:
            hit = hit & (prow >= lo)
        return jnp.where(hit, aff_ref[e:e + 1, :], 0.0).astype(BF16)

    q = jnp.concatenate([weights(e, *_window(base_ref, e, blk, 0, cap), True) for e in range(N_EXPERTS)], axis=0)
    for cp in first_window_copies(j, slot):
        cp.wait()
    f_ref[...] = lax.dot_general(q, win_buf[slot], tn, preferred_element_type=F32)

    n_win = [_n_windows(base_ref, e, blk, cap) for e in range(N_EXPERTS)]

    @pl.when(functools.reduce(jnp.maximum, n_win) > 1)
    def _():
        for e in range(N_EXPERTS):
            def extra(w, carry, e=e):
                lo, off = _window(base_ref, e, blk, w, cap)
                cp = window_copy(e, off, extra_buf, extra_sem)
                cp.start()
                cp.wait()
                f_ref[...] += lax.dot_general(weights(e, lo, off, False), extra_buf[...], tn,
                                              preferred_element_type=F32)
                return carry

            lax.fori_loop(1, n_win[e], extra, 0)

    o_ref[...] = x1_ref[...] + g2_ref[0] * _rms(f_ref[...], g_ref[...])


def _combine(base, eo, pos, aff_t, x1, mod3, npost, cond_row_fn, cap):
    t, d = x1.shape
    ne = pos.shape[0]
    n_tiles = t // ROUTE_TILE
    tile = pl.BlockSpec((ROUTE_TILE, d), lambda j, b: (j, 0))
    etile = pl.BlockSpec((ne, ROUTE_TILE), lambda j, b: (0, j))
    return pl.pallas_call(
        functools.partial(_combine_kernel, cap=cap, n_tiles=n_tiles),
        grid_spec=pltpu.PrefetchScalarGridSpec(
            num_scalar_prefetch=1,
            grid=(n_tiles,),
            in_specs=[etile, etile, tile,
                      pl.BlockSpec((1, d), lambda j, b: (0, 0)),
                      pl.BlockSpec((1, 1, d), lambda j, b: (cond_row_fn(j * ROUTE_TILE), 0, 5)),
                      pl.BlockSpec(memory_space=pl.ANY)],
            out_specs=tile,
            scratch_shapes=[pltpu.VMEM((ROUTE_TILE, d), F32),
                            pltpu.VMEM((2, ne * ROUTE_WINDOW, d), BF16), pltpu.SemaphoreType.DMA((2,)),
                            pltpu.VMEM((ROUTE_WINDOW, d), BF16), pltpu.SemaphoreType.DMA(())]),
        out_shape=jax.ShapeDtypeStruct((t, d), F32),
        compiler_params=pltpu.CompilerParams(dimension_semantics=("arbitrary",),
                                             vmem_limit_bytes=VMEM_LIMIT),
        name="combine",
    )(base, pos, aff_t, x1, npost, mod3, eo)


def _trunk(x3, mod3, cond_row_fn, weights, lgs, *, rope_tabs, init, emit_state):
    (n_pre_mix, n_post_mix, n_pre_ffn, n_post_ffn, w_in, gn_g, conv_w, wro, wco, wo, wr_pad, wg, wu, wd) = weights
    b, seq, d = x3.shape
    t = b * seq
    x = x3.reshape(t, d)
    proj = _inproj(x, mod3, n_pre_mix, w_in, cond_row_fn)
    ret = _retention(proj, lgs, gn_g, seq, rope_tabs=rope_tabs, init=init, emit_state=emit_state)
    og = ret[0]
    x1, h2, aff_t = _mix(og, proj, x, mod3, conv_w, wro, wco, wo, wr_pad, n_post_mix, n_pre_ffn, seq, cond_row_fn)
    cap = CAPACITY_FACTOR * t // N_EXPERTS
    pos, base = _route(aff_t, cap)
    eo = _experts(_dispatch(base, h2, pos, cap), wg, wu, wd)
    y = _combine(base, eo, pos, aff_t, x1, mod3, n_post_ffn, cond_row_fn, cap)
    return y.reshape(b, seq, d), ret[1:]


def kernel(x_prompt, x_sample, state_ret_fwd, state_ret_bwd, c, c_ctx, w_ada, b_ada, norm_pre_mix, norm_post_mix,
           norm_pre_ffn, norm_post_ffn, w_in, ret_decay_fwd, ret_decay_bwd, ret_norm_g, conv_w, w_ret_o, w_conv_o,
           w_o, w_router, w_gate, w_up, w_down):
    depth = w_ada.shape[0]
    assert depth == 1
    dec_b, dec_seq = x_sample.shape[0], x_sample.shape[1]
    xp, xs = x_prompt, x_sample
    l = 0
    cond = jnp.zeros((N_COND_ROWS, D_MODEL), F32).at[0:dec_b].set(c).at[CTX_COND_ROW].set(c_ctx)
    mod3 = _ada(cond, w_ada[l], b_ada[l][None, :]).reshape(N_COND_ROWS, 1, 6 * D_MODEL)
    lgs = jnp.stack([-jax.nn.softplus(-ret_decay_fwd[l].astype(F32)),
                     -jax.nn.softplus(-ret_decay_bwd[l].astype(F32))])
    wr_pad = jnp.pad(w_router[l], ((0, 0), (0, ROUTER_LANES - N_EXPERTS)))
    weights = (norm_pre_mix[l][None, :], norm_post_mix[l][None, :], norm_pre_ffn[l][None, :],
               norm_post_ffn[l][None, :], w_in[l].astype(BF16), ret_norm_g[l][None, :], conv_w[l],
               w_ret_o[l].astype(BF16), w_conv_o[l].astype(BF16), w_o[l].astype(BF16), wr_pad,
               w_gate[l], w_up[l], w_down[l])
    yp, (s_f, s_b) = _trunk(xp, mod3, lambda r: CTX_COND_ROW, weights, lgs,
                            rope_tabs=None, init=None, emit_state=True)
    ys, _ = _trunk(xs, mod3, lambda r: r // dec_seq, weights, lgs,
                   rope_tabs=_rope_tables(dec_seq), init=(state_ret_fwd, state_ret_bwd), emit_state=False)
    return (yp, ys, s_f, s_b)
```

```python
import functools

import jax
import jax.numpy as jnp
import numpy as np
from jax import lax
from jax.experimental import pallas as pl
from jax.experimental.pallas import tpu as pltpu

F32 = jnp.float32
BF16 = jnp.bfloat16

D_MODEL = 1024
N_HEADS = 8
DK = 64
DV = 128
CHUNK = 128
GRID_W = 64
N_EXPERTS = 16
CAPACITY_FACTOR = 2
D_IN_TOTAL = 8192
RMS_EPS = 1e-6
GN_EPS = 1e-5
ROPE_BASE = 10000.0
N_COND_ROWS = 16
CTX_COND_ROW = 8
ROUTER_LANES = 128
RET_CHUNKS_PER_STEP = 16
RET_UNROLL = 4
MIX_TILE = 512
MIX_SUB_TILE = 256
VMEM_LIMIT = 48 * 1024 * 1024
EXPERT_ROWS_PER_STEP = 1024
EXPERT_SUB_ROWS = 512
EXPERT_VMEM_LIMIT = 56 * 1024 * 1024


def _sigmoid(x):
    return 1.0 / (1.0 + jnp.exp(-x))


def _rms(x, g):
    return x * lax.rsqrt(jnp.mean(x * x, axis=-1, keepdims=True) + RMS_EPS) * g


def _ada_kernel(c_ref, w_ref, b_ref, o_ref):
    c = c_ref[...]
    s = c * _sigmoid(c)
    o_ref[...] = jnp.dot(s, w_ref[...], preferred_element_type=F32,
                         precision=lax.Precision.HIGHEST) + b_ref[...]


def _ada(cond, w_ada, b_ada):
    n = w_ada.shape[1]
    tn = 1024
    return pl.pallas_call(
        _ada_kernel,
        grid=(n // tn,),
        in_specs=[pl.BlockSpec((N_COND_ROWS, D_MODEL), lambda j: (0, 0)),
                  pl.BlockSpec((D_MODEL, tn), lambda j: (0, j)),
                  pl.BlockSpec((1, tn), lambda j: (0, j))],
        out_specs=pl.BlockSpec((N_COND_ROWS, tn), lambda j: (0, j)),
        out_shape=jax.ShapeDtypeStruct((N_COND_ROWS, n), F32),
        compiler_params=pltpu.CompilerParams(dimension_semantics=("arbitrary",),
                                             vmem_limit_bytes=VMEM_LIMIT),
        name="ada_mod",
    )(cond, w_ada, b_ada)


def _inproj_body(x_ref, g_ref, sh_ref, sc_ref, w_ref, cos_ref, sin_ref, o_ref, h_ref, use_rope):
    j = pl.program_id(1)

    @pl.when(j == 0)
    def _():
        h = _rms(x_ref[...], g_ref[...]) * (1.0 + sc_ref[0]) + sh_ref[0]
        h_ref[...] = h.astype(BF16)

    acc = jnp.dot(h_ref[...], w_ref[...], preferred_element_type=F32)

    @pl.when(j == 0)
    def _():
        lanes = 2 * DK
        n_qk = 2 * N_HEADS * DK
        ci = lax.broadcasted_iota(jnp.int32, (acc.shape[0], lanes), 1)
        for b in range(n_qk // lanes):
            x = acc[:, b * lanes:(b + 1) * lanes]
            if b * lanes >= N_HEADS * DK:
                x = x * (DK ** -0.5)
            if use_rope:
                swapped = jnp.where((ci & 31) < 16, pltpu.roll(x, lanes - 16, 1), pltpu.roll(x, 16, 1))
                x = x * cos_ref[...] + swapped * sin_ref[...]
            o_ref[:, b * lanes:(b + 1) * lanes] = x.astype(BF16)
        o_ref[:, n_qk:] = acc[:, n_qk:].astype(BF16)

    @pl.when(j != 0)
    def _():
        o_ref[...] = acc.astype(BF16)


def _inproj_kernel(x_ref, g_ref, sh_ref, sc_ref, w_ref, *rest, use_rope):
    if use_rope:
        cos_ref, sin_ref, o_ref, h_ref = rest
    else:
        cos_ref = sin_ref = None
        o_ref, h_ref = rest
    _inproj_body(x_ref, g_ref, sh_ref, sc_ref, w_ref, cos_ref, sin_ref, o_ref, h_ref, use_rope)


def _inproj(x, mod3, g, w_bf16, cond_row_fn, seq, rope_tabs):
    t = x.shape[0]
    tm, tn = 1024, 2048
    in_specs = [pl.BlockSpec((tm, D_MODEL), lambda i, j: (i, 0)),
                pl.BlockSpec((1, D_MODEL), lambda i, j: (0, 0)),
                pl.BlockSpec((1, 1, D_MODEL), lambda i, j: (cond_row_fn(i * tm), 0, 0)),
                pl.BlockSpec((1, 1, D_MODEL), lambda i, j: (cond_row_fn(i * tm), 0, 1)),
                pl.BlockSpec((D_MODEL, tn), lambda i, j: (0, j))]
    args = [x, g, mod3, mod3, w_bf16]
    if rope_tabs is not None:
        tiles_per_seq = seq // tm
        in_specs += [pl.BlockSpec((tm, 2 * DK), lambda i, j: (i % tiles_per_seq, 0))] * 2
        args += list(rope_tabs)
    return pl.pallas_call(
        functools.partial(_inproj_kernel, use_rope=rope_tabs is not None),
        grid=(t // tm, D_IN_TOTAL // tn),
        in_specs=in_specs,
        out_specs=pl.BlockSpec((tm, tn), lambda i, j: (i, j)),
        out_shape=jax.ShapeDtypeStruct((t, D_IN_TOTAL), BF16),
        scratch_shapes=[pltpu.VMEM((tm, D_MODEL), BF16)],
        compiler_params=pltpu.CompilerParams(dimension_semantics=("arbitrary", "arbitrary"),
                                             vmem_limit_bytes=VMEM_LIMIT),
        name="inproj",
    )(*args)


def _ret_kernel(*refs, nc, bb, has_init, emit_state):
    it = iter(refs)
    lg_ref = next(it)
    q_ref, k_ref, v_ref, gr_ref, gn_ref = (next(it) for _ in range(5))
    s0f_ref = s0b_ref = sf_ref = sb_ref = None
    if has_init:
        s0f_ref, s0b_ref = next(it), next(it)
    o_ref = next(it)
    if emit_state:
        sf_ref, sb_ref = next(it), next(it)
    kvf_s, kvb_s, rcat_s, p_s, o_s, r0 = (next(it) for _ in range(6))

    p = pl.program_id(1)
    lgf_a, lgf_b = lg_ref[0, 2 * p], lg_ref[0, 2 * p + 1]
    lgb_a, lgb_b = lg_ref[1, 2 * p], lg_ref[1, 2 * p + 1]

    ri = lax.broadcasted_iota(jnp.int32, (CHUNK, 2 * DK), 0)
    ci = lax.broadcasted_iota(jnp.int32, (CHUNK, 2 * DK), 1)
    lane_a = ci < DK
    rowf = ri.astype(F32)
    diff = rowf - ci.astype(F32)
    lgf_lane = jnp.where(lane_a, lgf_a, lgf_b)
    lgb_lane = jnp.where(lane_a, lgb_a, lgb_b)
    xi_f = jnp.exp(lgf_lane * (rowf + 1.0))
    xi_b = jnp.exp(lgb_lane * (CHUNK - rowf))
    zeta_f = jnp.exp(lgf_lane * (CHUNK - 1.0 - rowf))
    zeta_b = jnp.exp(lgb_lane * rowf)

    def decay_matrix(lgf, lgb):
        return jnp.where(diff > 0, jnp.exp(lgf * diff),
                         jnp.where(diff < 0, jnp.exp(lgb * (-diff)), 2.0))

    dm_a = decay_matrix(lgf_a, lgb_a)
    dm_b = decay_matrix(lgf_b, lgb_b)

    r2 = lax.broadcasted_iota(jnp.int32, (2 * DK, 2 * DV), 0)
    c2 = lax.broadcasted_iota(jnp.int32, (2 * DK, 2 * DV), 1)
    top = r2 < DK
    blk = (top == (c2 < DV)).astype(F32)
    cd_f = jnp.exp(jnp.where(top, lgf_a, lgf_b) * float(CHUNK)) * blk
    cd_b = jnp.exp(jnp.where(top, lgb_a, lgb_b) * float(CHUNK)) * blk

    def chunk_rows(c):
        return pl.ds(pl.multiple_of(c * CHUNK, CHUNK), CHUNK)

    n_chunks = bb * nc
    unroll = RET_UNROLL

    def kv_body(c, carry):
        rows = chunk_rows(c)
        k = k_ref[rows, :].astype(F32)
        kz_t = jnp.concatenate([k * zeta_f, k * zeta_b], axis=1).T.astype(BF16)
        kv = jnp.dot(kz_t, v_ref[rows, :], preferred_element_type=F32)
        kvf_s[c] = kv[0:2 * DK] * blk
        kvb_s[c] = kv[2 * DK:4 * DK] * blk
        return carry

    lax.fori_loop(0, n_chunks, kv_body, 0, unroll=unroll)

    def load_state(s_ref, s):
        r0[...] = jnp.zeros_like(r0)
        r0[0:DK, 0:DV] = s_ref[s, 0, 0].astype(F32)
        r0[DK:2 * DK, DV:2 * DV] = s_ref[s, 0, 1].astype(F32)
        return r0[...]

    for s in range(bb):
        def fwd_body(n, rf, s=s):
            c = s * nc + n
            rcat_s[c, 0:2 * DK, :] = rf.astype(BF16)
            return cd_f * rf + kvf_s[c]

        def bwd_body(t, rb, s=s):
            c = s * nc + nc - 1 - t
            rcat_s[c, 2 * DK:4 * DK, :] = rb.astype(BF16)
            return cd_b * rb + kvb_s[c]

        zeros = jnp.zeros((2 * DK, 2 * DV), F32)
        rf_fin = lax.fori_loop(0, nc, fwd_body, load_state(s0f_ref, s) if has_init else zeros)
        rb_fin = lax.fori_loop(0, nc, bwd_body, load_state(s0b_ref, s) if has_init else zeros)
        if emit_state:
            for s_ref, r in ((sf_ref, rf_fin), (sb_ref, rb_fin)):
                s_ref[s, 0, 0] = r[0:DK, 0:DV]
                s_ref[s, 0, 1] = r[DK:2 * DK, DV:2 * DV]

    gn = gn_ref[...]

    def head_norm(o):
        mu = jnp.mean(o, axis=-1, keepdims=True)
        d = o - mu
        var = jnp.mean(d * d, axis=-1, keepdims=True)
        return d * lax.rsqrt(var + GN_EPS)

    def score_body(c, carry):
        rows = chunk_rows(c)
        q16 = q_ref[rows, :]
        zero = jnp.zeros_like(q16)
        q_ab = jnp.concatenate([jnp.where(lane_a, q16, zero), jnp.where(lane_a, zero, q16)], axis=0)
        s = lax.dot_general(q_ab, k_ref[rows, :], (((1,), (1,)), ((), ())), preferred_element_type=F32)
        p_s[c, 0:CHUNK, :] = (s[0:CHUNK] * dm_a).astype(BF16)
        p_s[c, CHUNK:2 * CHUNK, :] = (s[CHUNK:2 * CHUNK] * dm_b).astype(BF16)
        return carry

    lax.fori_loop(0, n_chunks, score_body, 0, unroll=unroll)

    def value_body(c, carry):
        rows = chunk_rows(c)
        q = q_ref[rows, :].astype(F32)
        v = v_ref[rows, :]
        qx = jnp.concatenate([(q * xi_f).astype(BF16), (q * xi_b).astype(BF16)], axis=1)
        oc = jnp.dot(qx, rcat_s[c], preferred_element_type=F32)
        o_a = jnp.dot(p_s[c, 0:CHUNK, :], v[:, 0:DV], preferred_element_type=F32)
        o_b = jnp.dot(p_s[c, CHUNK:2 * CHUNK, :], v[:, DV:2 * DV], preferred_element_type=F32)
        o_s[c, :, 0:DV] = o_a + oc[:, 0:DV]
        o_s[c, :, DV:2 * DV] = o_b + oc[:, DV:2 * DV]
        return carry

    lax.fori_loop(0, n_chunks, value_body, 0, unroll=unroll)

    def gate_body(c, carry):
        rows = chunk_rows(c)
        o = o_s[c]
        y = jnp.concatenate([head_norm(o[:, 0:DV]), head_norm(o[:, DV:2 * DV])], axis=1)
        g = gr_ref[rows, :].astype(F32)
        o_ref[rows, :] = (g * _sigmoid(g) * (y * gn)).astype(BF16)
        return carry

    lax.fori_loop(0, n_chunks, gate_body, 0, unroll=unroll)


def _retention(proj, lgs, gn_g, seq, *, init=None, emit_state):
    t = proj.shape[0]
    b = t // seq
    nc = seq // CHUNK
    bb = max(1, RET_CHUNKS_PER_STEP // nc)
    rows = bb * seq
    has_init = init is not None
    pairs = N_HEADS // 2
    qk_blocks = (N_HEADS * DK) // (2 * DK)
    in_specs = [pl.BlockSpec(memory_space=pltpu.SMEM),
                pl.BlockSpec((rows, 2 * DK), lambda i, p: (i, p)),
                pl.BlockSpec((rows, 2 * DK), lambda i, p: (i, qk_blocks + p)),
                pl.BlockSpec((rows, 2 * DV), lambda i, p: (i, qk_blocks + p)),
                pl.BlockSpec((rows, 2 * DV), lambda i, p: (i, 2 * qk_blocks + p)),
                pl.BlockSpec((1, 2 * DV), lambda i, p: (0, p))]
    args = [lgs, proj, proj, proj, proj, gn_g]
    state_spec = pl.BlockSpec((bb, 1, 2, DK, DV), lambda i, p: (i, 0, p, 0, 0))
    if has_init:
        in_specs += [state_spec, state_spec]
        args += list(init)
    out_specs = [pl.BlockSpec((rows, 2 * DV), lambda i, p: (i, p))]
    out_shape = [jax.ShapeDtypeStruct((t, N_HEADS * DV), BF16)]
    if emit_state:
        out_specs += [state_spec, state_spec]
        out_shape += [jax.ShapeDtypeStruct((b, 1, N_HEADS, DK, DV), F32)] * 2
    return pl.pallas_call(
        functools.partial(_ret_kernel, nc=nc, bb=bb, has_init=has_init, emit_state=emit_state),
        grid=(b // bb, pairs),
        in_specs=in_specs,
        out_specs=out_specs,
        out_shape=out_shape,
        scratch_shapes=[pltpu.VMEM((bb * nc, 2 * DK, 2 * DV), F32),
                        pltpu.VMEM((bb * nc, 2 * DK, 2 * DV), F32),
                        pltpu.VMEM((bb * nc, 4 * DK, 2 * DV), BF16),
                        pltpu.VMEM((bb * nc, 2 * CHUNK, CHUNK), BF16),
                        pltpu.VMEM((bb * nc, CHUNK, 2 * DV), F32),
                        pltpu.VMEM((2 * DK, 2 * DV), F32)],
        compiler_params=pltpu.CompilerParams(dimension_semantics=("arbitrary", "arbitrary"),
                                             vmem_limit_bytes=VMEM_LIMIT),
        name="retention",
    )(*args)


def _rope_tables(seq):
    n_freq = DK // 4
    pos = np.arange(seq)
    inv = jnp.asarray(ROPE_BASE, F32) ** (-jnp.arange(n_freq, dtype=F32) / n_freq)
    ang_r = jnp.asarray(pos // GRID_W, F32)[:, None] * inv
    ang_c = jnp.asarray(pos % GRID_W, F32)[:, None] * inv
    cos = jnp.concatenate([jnp.cos(ang_r)] * 2 + [jnp.cos(ang_c)] * 2, axis=1)
    sin = jnp.concatenate([-jnp.sin(ang_r), jnp.sin(ang_r), -jnp.sin(ang_c), jnp.sin(ang_c)], axis=1)
    return jnp.tile(cos, (1, 2)), jnp.tile(sin, (1, 2))


def _mix_kernel(og_ref, cb_ref, cc_ref, cx_ref, ccp_ref, cxp_ref, ccn_ref, cxn_ref, ga_ref, gb_ref, x_ref,
                cw_ref, wro_ref, wco_ref, wo_ref, wr_ref, npost_ref, npre_ref, g1_ref, sh2_ref, sc2_ref,
                x1_ref, h2_ref, aff_ref, *, tm, sub, seq):
    i = pl.program_id(0)
    halo = ccp_ref.shape[0]
    wr = wr_ref[...]
    w_hi = wr.astype(BF16)
    w_lo = (wr - w_hi.astype(F32)).astype(BF16)
    row = lax.broadcasted_iota(jnp.int32, (sub, 1), 0)
    lane = lax.broadcasted_iota(jnp.int32, (sub, ROUTER_LANES), 1)

    def product_row(cc, cx, r):
        return cc[r:r + 1, :].astype(F32) * cx[r:r + 1, :].astype(F32)

    for s in range(tm // sub):
        r0 = s * sub
        rows = slice(r0, r0 + sub)
        has_prev = jnp.where((i * tm + r0) % seq != 0, 1.0, 0.0)
        has_next = jnp.where((i * tm + r0 + sub) % seq != 0, 1.0, 0.0)
        prev_row = product_row(ccp_ref, cxp_ref, halo - 1) if s == 0 else product_row(cc_ref, cx_ref, r0 - 1)
        next_row = (product_row(ccn_ref, cxn_ref, 0) if r0 + sub == tm
                    else product_row(cc_ref, cx_ref, r0 + sub))
        prod = cc_ref[rows, :].astype(F32) * cx_ref[rows, :].astype(F32)
        up = jnp.where(row == 0, prev_row * has_prev, pltpu.roll(prod, 1, 0))
        dn = jnp.where(row == sub - 1, next_row * has_next, pltpu.roll(prod, sub - 1, 0))
        u = up * cw_ref[0:1, :] + prod * cw_ref[1:2, :] + dn * cw_ref[2:3, :]
        y_conv = jnp.dot((cb_ref[rows, :].astype(F32) * u).astype(BF16), wco_ref[...],
                         preferred_element_type=F32)
        y_ret = jnp.dot(og_ref[rows, :], wro_ref[...], preferred_element_type=F32)
        merged = (_sigmoid(ga_ref[rows, :].astype(F32)) * y_ret
                  + _sigmoid(gb_ref[rows, :].astype(F32)) * y_conv)
        m = jnp.dot(merged.astype(BF16), wo_ref[...], preferred_element_type=F32)
        x1 = x_ref[rows, :] + g1_ref[0] * _rms(m, npost_ref[...])
        x1_ref[rows, :] = x1
        h2 = _rms(x1, npre_ref[...]) * (1.0 + sc2_ref[0]) + sh2_ref[0]
        h_hi = h2.astype(BF16)
        h2_ref[rows, :] = h_hi
        h_lo = (h2 - h_hi.astype(F32)).astype(BF16)
        logits = (jnp.dot(h_hi, w_hi, preferred_element_type=F32)
                  + jnp.dot(h_lo, w_hi, preferred_element_type=F32)
                  + jnp.dot(h_hi, w_lo, preferred_element_type=F32))
        logits = jnp.where(lane < N_EXPERTS, logits, -jnp.inf)
        e = jnp.exp(logits - jnp.max(logits, axis=-1, keepdims=True))
        aff = e / jnp.sum(e, axis=-1, keepdims=True)
        aff_ref[:, rows] = aff.T[0:N_EXPERTS, :]


def _mix(og, proj, x, mod3, conv_w, wro, wco, wo, wr_pad, npost, npre, seq, cond_row_fn):
    t = x.shape[0]
    tm = MIX_TILE
    halo = 16
    hb = tm // halo
    last_halo = t // halo - 1
    col = lambda c: (lambda i: (i, c))
    row_vec = pl.BlockSpec((1, D_MODEL), lambda i: (0, 0))
    wspec = pl.BlockSpec((D_MODEL, D_MODEL), lambda i: (0, 0))
    modspec = lambda c: pl.BlockSpec((1, 1, D_MODEL), lambda i: (cond_row_fn(i * tm), 0, c))
    tile = lambda c: pl.BlockSpec((tm, D_MODEL), col(c))
    prev = lambda c: pl.BlockSpec((halo, D_MODEL), lambda i: (jnp.maximum(i * hb - 1, 0), c))
    nxt = lambda c: pl.BlockSpec((halo, D_MODEL), lambda i: (jnp.minimum((i + 1) * hb, last_halo), c))
    return pl.pallas_call(
        functools.partial(_mix_kernel, tm=tm, sub=MIX_SUB_TILE, seq=seq),
        grid=(t // tm,),
        in_specs=[tile(0), tile(3), tile(4), tile(5), prev(4), prev(5), nxt(4), nxt(5), tile(6), tile(7),
                  tile(0),
                  pl.BlockSpec((3, D_MODEL), lambda i: (0, 0)), wspec, wspec, wspec,
                  pl.BlockSpec((D_MODEL, ROUTER_LANES), lambda i: (0, 0)),
                  row_vec, row_vec, modspec(2), modspec(3), modspec(4)],
        out_specs=[tile(0), tile(0), pl.BlockSpec((N_EXPERTS, tm), lambda i: (0, i))],
        out_shape=[jax.ShapeDtypeStruct((t, D_MODEL), F32),
                   jax.ShapeDtypeStruct((t, D_MODEL), BF16),
                   jax.ShapeDtypeStruct((N_EXPERTS, t), F32)],
        compiler_params=pltpu.CompilerParams(dimension_semantics=("arbitrary",),
                                             vmem_limit_bytes=VMEM_LIMIT),
        name="mix_out",
    )(og, proj, proj, proj, proj, proj, proj, proj, proj, proj, x,
      conv_w, wro, wco, wo, wr_pad, npost, npre, mod3, mod3, mod3)


def _expert_kernel(xs_ref, wg_ref, wu_ref, wd_ref, o_ref, *, sub):
    wg = wg_ref[0].astype(BF16)
    wu = wu_ref[0].astype(BF16)
    wd = wd_ref[0].astype(BF16)
    for r in range(xs_ref.shape[1] // sub):
        rows = slice(r * sub, (r + 1) * sub)
        x = xs_ref[0, rows, :]
        g = jnp.dot(x, wg, preferred_element_type=F32)
        u = jnp.dot(x, wu, preferred_element_type=F32)
        hid = (g * _sigmoid(g) * u).astype(BF16)
        o_ref[0, rows, :] = jnp.dot(hid, wd, preferred_element_type=F32).astype(BF16)


def _experts(xs, wg, wu, wd):
    e, cap, d = xs.shape
    ff = wg.shape[2]
    tr = min(cap, EXPERT_ROWS_PER_STEP)
    return pl.pallas_call(
        functools.partial(_expert_kernel, sub=EXPERT_SUB_ROWS),
        grid=(e, cap // tr),
        in_specs=[pl.BlockSpec((1, tr, d), lambda a, r: (a, r, 0)),
                  pl.BlockSpec((1, d, ff), lambda a, r: (a, 0, 0)),
                  pl.BlockSpec((1, d, ff), lambda a, r: (a, 0, 0)),
                  pl.BlockSpec((1, ff, d), lambda a, r: (a, 0, 0))],
        out_specs=pl.BlockSpec((1, tr, d), lambda a, r: (a, r, 0)),
        out_shape=jax.ShapeDtypeStruct((e, cap, d), BF16),
        compiler_params=pltpu.CompilerParams(dimension_semantics=("arbitrary", "arbitrary"),
                                             vmem_limit_bytes=EXPERT_VMEM_LIMIT),
        name="experts",
    )(xs, wg, wu, wd)


def _route_kernel(aff_ref, pos_ref, excl_ref, *, cap):
    ne, nblk, lanes = aff_ref.shape
    aff = aff_ref[...]

    def count(mask):
        m = jnp.where(mask, 1.0, 0.0)
        return jnp.sum(jnp.sum(m, axis=1, keepdims=True), axis=2, keepdims=True)

    def as_float(bits):
        return lax.bitcast_convert_type(bits, F32)

    def bit_step(it, thr_bits):
        cand = thr_bits | jnp.left_shift(jnp.int32(1), 30 - it)
        return jnp.where(count(aff >= as_float(cand)) >= cap, cand, thr_bits)

    thr = as_float(lax.fori_loop(0, 31, bit_step, jnp.zeros((ne, 1, 1), jnp.int32)))
    gt = aff > thr
    eq = aff == thr
    need = cap - count(gt)

    li = lax.broadcasted_iota(jnp.int32, (lanes, lanes), 0)
    lj = lax.broadcasted_iota(jnp.int32, (lanes, lanes), 1)
    incl_lanes = jnp.where(li <= lj, 1.0, 0.0).astype(BF16)
    all_lanes = jnp.ones((lanes, lanes), BF16)
    bi = lax.broadcasted_iota(jnp.int32, (nblk, nblk), 0)
    bj = lax.broadcasted_iota(jnp.int32, (nblk, nblk), 1)
    earlier_blocks = jnp.where(bj < bi, 1.0, 0.0).astype(BF16)

    def ranks(mask):
        m = jnp.where(mask, 1.0, 0.0).reshape(ne * nblk, lanes).astype(BF16)
        incl = jnp.dot(m, incl_lanes, preferred_element_type=F32).reshape(ne, nblk, lanes)
        tot = jnp.dot(m, all_lanes, preferred_element_type=F32).reshape(ne, nblk, lanes).astype(BF16)
        excl = jnp.stack([jnp.dot(earlier_blocks, tot[i], preferred_element_type=F32) for i in range(ne)])
        return excl, incl

    ex_eq, in_eq = ranks(eq)
    sel = gt | (eq & (ex_eq + in_eq - 1.0 < need))
    ex_sel, in_sel = ranks(sel)
    pos_ref[...] = jnp.where(sel, ex_sel + in_sel - 1.0, -1.0).astype(jnp.int32)
    excl_ref[...] = ex_sel.astype(jnp.int32)


def _route(aff_t, cap):
    ne, t = aff_t.shape
    nblk = t // 128
    shape = (ne, nblk, 128)
    full = pl.BlockSpec(shape, lambda i: (0, 0, 0))
    pos, excl = pl.pallas_call(
        functools.partial(_route_kernel, cap=cap),
        grid=(1,),
        in_specs=[full],
        out_specs=[full, full],
        out_shape=[jax.ShapeDtypeStruct(shape, jnp.int32)] * 2,
        compiler_params=pltpu.CompilerParams(dimension_semantics=("arbitrary",),
                                             vmem_limit_bytes=VMEM_LIMIT),
        name="route",
    )(aff_t.reshape(shape))
    base = jnp.concatenate([excl[:, :, 0], jnp.full((ne, 1), cap, jnp.int32)], axis=1)
    return pos.reshape(ne, t), base


ROUTE_TILE = 256
ROUTE_WINDOW = 64
SLOT_ALIGN = 16
DISPATCH_GROUP = 4
DISPATCH_TILES_PER_STEP = 4


def _window(base_ref, e, blk, w, cap):
    lo = base_ref[e, blk] // SLOT_ALIGN + w * (ROUTE_WINDOW // SLOT_ALIGN)
    return lo * SLOT_ALIGN, jnp.minimum(lo, (cap - ROUTE_WINDOW) // SLOT_ALIGN) * SLOT_ALIGN


def _n_windows(base_ref, e, blk, cap):
    lo, _ = _window(base_ref, e, blk, 0, cap)
    end = base_ref[e, blk + ROUTE_TILE // 128]
    return (end - lo + ROUTE_WINDOW - 1) // ROUTE_WINDOW


def _dispatch_kernel(base_ref, h_ref, pos_ref, xs_ref, *, cap):
    g, j = pl.program_id(0), pl.program_id(1)

    @pl.when(j == 0)
    def _():
        xs_ref[...] = jnp.zeros_like(xs_ref)

    ri = lax.broadcasted_iota(jnp.int32, (ROUTE_WINDOW, ROUTE_TILE), 0)

    def onehot(prow, lo, off, first):
        hit = prow - off == ri
        if not first:
            hit = hit & (prow >= lo)
        return jnp.where(hit, 1.0, 0.0).astype(BF16)

    def add_rows(ge, off, rows):
        sl = pl.ds(pl.multiple_of(off, SLOT_ALIGN), ROUTE_WINDOW)
        xs_ref[ge, sl, :] = xs_ref[ge, sl, :] + rows.astype(BF16)

    for sub in range(DISPATCH_TILES_PER_STEP):
        blk = (j * DISPATCH_TILES_PER_STEP + sub) * (ROUTE_TILE // 128)
        tok = slice(sub * ROUTE_TILE, (sub + 1) * ROUTE_TILE)
        h = h_ref[tok, :]
        prows, offs = [], []
        for ge in range(DISPATCH_GROUP):
            e = g * DISPATCH_GROUP + ge
            prows.append(pos_ref[pl.ds(e, 1), tok])
            offs.append(_window(base_ref, e, blk, 0, cap))
        sel = jnp.concatenate([onehot(prows[ge], *offs[ge], True) for ge in range(DISPATCH_GROUP)], axis=0)
        rows = jnp.dot(sel, h, preferred_element_type=F32)
        for ge in range(DISPATCH_GROUP):
            add_rows(ge, offs[ge][1], rows[ge * ROUTE_WINDOW:(ge + 1) * ROUTE_WINDOW])

        for ge in range(DISPATCH_GROUP):
            e = g * DISPATCH_GROUP + ge

            def extra(w, carry, ge=ge, e=e, blk=blk, h=h, prow=prows[ge]):
                lo, off = _window(base_ref, e, blk, w, cap)
                add_rows(ge, off, jnp.dot(onehot(prow, lo, off, False), h, preferred_element_type=F32))
                return carry

            lax.fori_loop(1, _n_windows(base_ref, e, blk, cap), extra, 0)


def _dispatch(base, h2, pos, cap):
    t, d = h2.shape
    ne = pos.shape[0]
    step = ROUTE_TILE * DISPATCH_TILES_PER_STEP
    return pl.pallas_call(
        functools.partial(_dispatch_kernel, cap=cap),
        grid_spec=pltpu.PrefetchScalarGridSpec(
            num_scalar_prefetch=1,
            grid=(ne // DISPATCH_GROUP, t // step),
            in_specs=[pl.BlockSpec((step, d), lambda g, j, b: (j, 0)),
                      pl.BlockSpec((ne, step), lambda g, j, b: (0, j))],
            out_specs=pl.BlockSpec((DISPATCH_GROUP, cap, d), lambda g, j, b: (g, 0, 0))),
        out_shape=jax.ShapeDtypeStruct((ne, cap, d), BF16),
        compiler_params=pltpu.CompilerParams(dimension_semantics=("arbitrary", "arbitrary"),
                                             vmem_limit_bytes=VMEM_LIMIT),
        name="dispatch",
    )(base, h2, pos)


def _combine_kernel(base_ref, pos_ref, aff_ref, x1_ref, g_ref, g2_ref, eo_ref, o_ref,
                    f_ref, win_buf, win_sem, extra_buf, extra_sem, *, cap, n_tiles):
    j = pl.program_id(0)
    blocks_per_tile = ROUTE_TILE // 128
    blk = j * blocks_per_tile
    slot = lax.rem(j, 2)
    ri = lax.broadcasted_iota(jnp.int32, (ROUTE_WINDOW, ROUTE_TILE), 0)
    tn = (((0,), (0,)), ((), ()))

    def window_copy(e, off, dst, sem):
        return pltpu.make_async_copy(eo_ref.at[e, pl.ds(pl.multiple_of(off, SLOT_ALIGN), ROUTE_WINDOW), :],
                                     dst, sem)

    def first_window_copies(tile, buf_slot):
        return [window_copy(e, _window(base_ref, e, tile * blocks_per_tile, 0, cap)[1],
                            win_buf.at[buf_slot, pl.ds(e * ROUTE_WINDOW, ROUTE_WINDOW), :], win_sem.at[buf_slot])
                for e in range(N_EXPERTS)]

    @pl.when(j == 0)
    def _():
        for cp in first_window_copies(0, 0):
            cp.start()

    @pl.when(j + 1 < n_tiles)
    def _():
        for cp in first_window_copies(j + 1, 1 - slot):
            cp.start()

    def weights(e, lo, off, first):
        prow = pos_ref[e:e + 1, :]
        hit = prow - off == ri
        if not first:
            hit = hit & (prow >= lo)
        return jnp.where(hit, aff_ref[e:e + 1, :], 0.0).astype(BF16)

    q = jnp.concatenate([weights(e, *_window(base_ref, e, blk, 0, cap), True) for e in range(N_EXPERTS)], axis=0)
    for cp in first_window_copies(j, slot):
        cp.wait()
    f_ref[...] = lax.dot_general(q, win_buf[slot], tn, preferred_element_type=F32)

    n_win = [_n_windows(base_ref, e, blk, cap) for e in range(N_EXPERTS)]

    @pl.when(functools.reduce(jnp.maximum, n_win) > 1)
    def _():
        for e in range(N_EXPERTS):
            def extra(w, carry, e=e):
                lo, off = _window(base_ref, e, blk, w, cap)
                cp = window_copy(e, off, extra_buf, extra_sem)
                cp.start()
                cp.wait()
                f_ref[...] += lax.dot_general(weights(e, lo, off, False), extra_buf[...], tn,
                                              preferred_element_type=F32)
                return carry

            lax.fori_loop(1, n_win[e], extra, 0)

    o_ref[...] = x1_ref[...] + g2_ref[0] * _rms(f_ref[...], g_ref[...])


def _combine(base, eo, pos, aff_t, x1, mod3, npost, cond_row_fn, cap):
    t, d = x1.shape
    ne = pos.shape[0]
    n_tiles = t // ROUTE_TILE
    tile = pl.BlockSpec((ROUTE_TILE, d), lambda j, b: (j, 0))
    etile = pl.BlockSpec((ne, ROUTE_TILE), lambda j, b: (0, j))
    return pl.pallas_call(
        functools.partial(_combine_kernel, cap=cap, n_tiles=n_tiles),
        grid_spec=pltpu.PrefetchScalarGridSpec(
            num_scalar_prefetch=1,
            grid=(n_tiles,),
            in_specs=[etile, etile, tile,
                      pl.BlockSpec((1, d), lambda j, b: (0, 0)),
                      pl.BlockSpec((1, 1, d), lambda j, b: (cond_row_fn(j * ROUTE_TILE), 0, 5)),
                      pl.BlockSpec(memory_space=pl.ANY)],
            out_specs=tile,
            scratch_shapes=[pltpu.VMEM((ROUTE_TILE, d), F32),
                            pltpu.VMEM((2, ne * ROUTE_WINDOW, d), BF16), pltpu.SemaphoreType.DMA((2,)),
                            pltpu.VMEM((ROUTE_WINDOW, d), BF16), pltpu.SemaphoreType.DMA(())]),
        out_shape=jax.ShapeDtypeStruct((t, d), F32),
        compiler_params=pltpu.CompilerParams(dimension_semantics=("arbitrary",),
                                             vmem_limit_bytes=VMEM_LIMIT),
        name="combine",
    )(base, pos, aff_t, x1, npost, mod3, eo)


def _trunk(x3, mod3, cond_row_fn, weights, lgs, *, rope_tabs, init, emit_state):
    (n_pre_mix, n_post_mix, n_pre_ffn, n_post_ffn, w_in, gn_g, conv_w, wro, wco, wo, wr_pad, wg, wu, wd) = weights
    b, seq, d = x3.shape
    t = b * seq
    x = x3.reshape(t, d)
    proj = _inproj(x, mod3, n_pre_mix, w_in, cond_row_fn, seq, rope_tabs)
    ret = _retention(proj, lgs, gn_g, seq, init=init, emit_state=emit_state)
    og = ret[0]
    x1, h2, aff_t = _mix(og, proj, x, mod3, conv_w, wro, wco, wo, wr_pad, n_post_mix, n_pre_ffn, seq, cond_row_fn)
    cap = CAPACITY_FACTOR * t // N_EXPERTS
    pos, base = _route(aff_t, cap)
    eo = _experts(_dispatch(base, h2, pos, cap), wg, wu, wd)
    y = _combine(base, eo, pos, aff_t, x1, mod3, n_post_ffn, cond_row_fn, cap)
    return y.reshape(b, seq, d), ret[1:]


def kernel(x_prompt, x_sample, state_ret_fwd, state_ret_bwd, c, c_ctx, w_ada, b_ada, norm_pre_mix, norm_post_mix,
           norm_pre_ffn, norm_post_ffn, w_in, ret_decay_fwd, ret_decay_bwd, ret_norm_g, conv_w, w_ret_o, w_conv_o,
           w_o, w_router, w_gate, w_up, w_down):
    depth = w_ada.shape[0]
    assert depth == 1
    dec_b, dec_seq = x_sample.shape[0], x_sample.shape[1]
    xp, xs = x_prompt, x_sample
    l = 0
    cond = jnp.zeros((N_COND_ROWS, D_MODEL), F32).at[0:dec_b].set(c).at[CTX_COND_ROW].set(c_ctx)
    mod3 = _ada(cond, w_ada[l], b_ada[l][None, :]).reshape(N_COND_ROWS, 1, 6 * D_MODEL)
    lgs = jnp.stack([-jax.nn.softplus(-ret_decay_fwd[l].astype(F32)),
                     -jax.nn.softplus(-ret_decay_bwd[l].astype(F32))])
    wr_pad = jnp.pad(w_router[l], ((0, 0), (0, ROUTER_LANES - N_EXPERTS)))
    weights = (norm_pre_mix[l][None, :], norm_post_mix[l][None, :], norm_pre_ffn[l][None, :],
               norm_post_ffn[l][None, :], w_in[l].astype(BF16), ret_norm_g[l][None, :], conv_w[l],
               w_ret_o[l].astype(BF16), w_conv_o[l].astype(BF16), w_o[l].astype(BF16), wr_pad,
               w_gate[l], w_up[l], w_down[l])
    yp, (s_f, s_b) = _trunk(xp, mod3, lambda r: CTX_COND_ROW, weights, lgs,
                            rope_tabs=None, init=None, emit_state=True)
    ys, _ = _trunk(xs, mod3, lambda r: r // dec_seq, weights, lgs,
                   rope_tabs=_rope_tables(dec_seq), init=(state_ret_fwd, state_ret_bwd), emit_state=False)
    return (yp, ys, s_f, s_b)
```

```python
import functools

import jax
import jax.numpy as jnp
import numpy as np
from jax import lax
from jax.experimental import pallas as pl
from jax.experimental.pallas import tpu as pltpu

F32 = jnp.float32
BF16 = jnp.bfloat16

D_MODEL = 1024
N_HEADS = 8
DK = 64
DV = 128
CHUNK = 128
GRID_W = 64
N_EXPERTS = 16
CAPACITY_FACTOR = 2
D_IN_TOTAL = 8192
RMS_EPS = 1e-6
GN_EPS = 1e-5
ROPE_BASE = 10000.0
N_COND_ROWS = 16
CTX_COND_ROW = 8
ROUTER_LANES = 128
RET_CHUNKS_PER_STEP = 16
RET_UNROLL = 4
MIX_TILE = 512
MIX_SUB_TILE = 256
VMEM_LIMIT = 48 * 1024 * 1024
EXPERT_ROWS_PER_STEP = 1024
EXPERT_SUB_ROWS = 512
EXPERT_VMEM_LIMIT = 56 * 1024 * 1024


def _sigmoid(x):
    return 1.0 / (1.0 + jnp.exp(-x))


def _rms(x, g):
    return x * lax.rsqrt(jnp.mean(x * x, axis=-1, keepdims=True) + RMS_EPS) * g


def _ada_kernel(c_ref, w_ref, b_ref, o_ref):
    c = c_ref[...]
    s = c * _sigmoid(c)
    o_ref[...] = jnp.dot(s, w_ref[...], preferred_element_type=F32,
                         precision=lax.Precision.HIGHEST) + b_ref[...]


def _ada(cond, w_ada, b_ada):
    n = w_ada.shape[1]
    tn = 1024
    return pl.pallas_call(
        _ada_kernel,
        grid=(n // tn,),
        in_specs=[pl.BlockSpec((N_COND_ROWS, D_MODEL), lambda j: (0, 0)),
                  pl.BlockSpec((D_MODEL, tn), lambda j: (0, j)),
                  pl.BlockSpec((1, tn), lambda j: (0, j))],
        out_specs=pl.BlockSpec((N_COND_ROWS, tn), lambda j: (0, j)),
        out_shape=jax.ShapeDtypeStruct((N_COND_ROWS, n), F32),
        compiler_params=pltpu.CompilerParams(dimension_semantics=("arbitrary",),
                                             vmem_limit_bytes=VMEM_LIMIT),
        name="ada_mod",
    )(cond, w_ada, b_ada)


def _inproj_body(x_ref, g_ref, sh_ref, sc_ref, w_ref, cos_ref, sin_ref, o_ref, h_ref, use_rope):
    j = pl.program_id(1)

    @pl.when(j == 0)
    def _():
        h = _rms(x_ref[...], g_ref[...]) * (1.0 + sc_ref[0]) + sh_ref[0]
        h_ref[...] = h.astype(BF16)

    @pl.when(j == 0)
    def _():
        lanes = 2 * DK
        width = 2 * lanes
        n_qk = 2 * N_HEADS * DK
        ci = lax.broadcasted_iota(jnp.int32, (h_ref.shape[0], lanes), 1)
        for b in range(o_ref.shape[1] // width):
            acc = jnp.dot(h_ref[...], w_ref[:, b * width:(b + 1) * width], preferred_element_type=F32)
            if b * width >= n_qk:
                o_ref[:, b * width:(b + 1) * width] = acc.astype(BF16)
                continue
            for half in range(2):
                col = b * width + half * lanes
                x = acc[:, half * lanes:(half + 1) * lanes]
                if col >= N_HEADS * DK:
                    x = x * (DK ** -0.5)
                if use_rope:
                    swapped = jnp.where((ci & 31) < 16, pltpu.roll(x, lanes - 16, 1), pltpu.roll(x, 16, 1))
                    x = x * cos_ref[...] + swapped * sin_ref[...]
                o_ref[:, col:col + lanes] = x.astype(BF16)

    @pl.when(j != 0)
    def _():
        o_ref[...] = jnp.dot(h_ref[...], w_ref[...], preferred_element_type=F32).astype(BF16)


def _inproj_kernel(x_ref, g_ref, sh_ref, sc_ref, w_ref, *rest, use_rope):
    if use_rope:
        cos_ref, sin_ref, o_ref, h_ref = rest
    else:
        cos_ref = sin_ref = None
        o_ref, h_ref = rest
    _inproj_body(x_ref, g_ref, sh_ref, sc_ref, w_ref, cos_ref, sin_ref, o_ref, h_ref, use_rope)


def _inproj(x, mod3, g, w_bf16, cond_row_fn, seq, rope_tabs):
    t = x.shape[0]
    tm, tn = 1024, 2048
    in_specs = [pl.BlockSpec((tm, D_MODEL), lambda i, j: (i, 0)),
                pl.BlockSpec((1, D_MODEL), lambda i, j: (0, 0)),
                pl.BlockSpec((1, 1, D_MODEL), lambda i, j: (cond_row_fn(i * tm), 0, 0)),
                pl.BlockSpec((1, 1, D_MODEL), lambda i, j: (cond_row_fn(i * tm), 0, 1)),
                pl.BlockSpec((D_MODEL, tn), lambda i, j: (0, j))]
    args = [x, g, mod3, mod3, w_bf16]
    if rope_tabs is not None:
        tiles_per_seq = seq // tm
        in_specs += [pl.BlockSpec((tm, 2 * DK), lambda i, j: (i % tiles_per_seq, 0))] * 2
        args += list(rope_tabs)
    return pl.pallas_call(
        functools.partial(_inproj_kernel, use_rope=rope_tabs is not None),
        grid=(t // tm, D_IN_TOTAL // tn),
        in_specs=in_specs,
        out_specs=pl.BlockSpec((tm, tn), lambda i, j: (i, j)),
        out_shape=jax.ShapeDtypeStruct((t, D_IN_TOTAL), BF16),
        scratch_shapes=[pltpu.VMEM((tm, D_MODEL), BF16)],
        compiler_params=pltpu.CompilerParams(dimension_semantics=("arbitrary", "arbitrary"),
                                             vmem_limit_bytes=VMEM_LIMIT),
        name="inproj",
    )(*args)


def _ret_kernel(*refs, nc, bb, has_init, emit_state):
    it = iter(refs)
    lg_ref = next(it)
    q_ref, k_ref, v_ref, gr_ref, gn_ref = (next(it) for _ in range(5))
    s0f_ref = s0b_ref = sf_ref = sb_ref = None
    if has_init:
        s0f_ref, s0b_ref = next(it), next(it)
    o_ref = next(it)
    if emit_state:
        sf_ref, sb_ref = next(it), next(it)
    kvf_s, kvb_s, rcat_s, p_s, o_s, r0 = (next(it) for _ in range(6))

    p = pl.program_id(1)
    lgf_a, lgf_b = lg_ref[0, 2 * p], lg_ref[0, 2 * p + 1]
    lgb_a, lgb_b = lg_ref[1, 2 * p], lg_ref[1, 2 * p + 1]

    ri = lax.broadcasted_iota(jnp.int32, (CHUNK, 2 * DK), 0)
    ci = lax.broadcasted_iota(jnp.int32, (CHUNK, 2 * DK), 1)
    lane_a = ci < DK
    rowf = ri.astype(F32)
    diff = rowf - ci.astype(F32)
    lgf_lane = jnp.where(lane_a, lgf_a, lgf_b)
    lgb_lane = jnp.where(lane_a, lgb_a, lgb_b)
    xi_f = jnp.exp(lgf_lane * (rowf + 1.0))
    xi_b = jnp.exp(lgb_lane * (CHUNK - rowf))
    zeta_f = jnp.exp(lgf_lane * (CHUNK - 1.0 - rowf))
    zeta_b = jnp.exp(lgb_lane * rowf)

    def decay_matrix(lgf, lgb):
        return jnp.where(diff > 0, jnp.exp(lgf * diff),
                         jnp.where(diff < 0, jnp.exp(lgb * (-diff)), 2.0))

    dm_a = decay_matrix(lgf_a, lgb_a)
    dm_b = decay_matrix(lgf_b, lgb_b)

    r2 = lax.broadcasted_iota(jnp.int32, (2 * DK, 2 * DV), 0)
    c2 = lax.broadcasted_iota(jnp.int32, (2 * DK, 2 * DV), 1)
    top = r2 < DK
    blk = (top == (c2 < DV)).astype(F32)
    cd_f = jnp.exp(jnp.where(top, lgf_a, lgf_b) * float(CHUNK)) * blk
    cd_b = jnp.exp(jnp.where(top, lgb_a, lgb_b) * float(CHUNK)) * blk

    def chunk_rows(c):
        return pl.ds(pl.multiple_of(c * CHUNK, CHUNK), CHUNK)

    n_chunks = bb * nc
    unroll = RET_UNROLL

    def kv_body(c, carry):
        rows = chunk_rows(c)
        k = k_ref[rows, :].astype(F32)
        kz_t = jnp.concatenate([k * zeta_f, k * zeta_b], axis=1).T.astype(BF16)
        kv = jnp.dot(kz_t, v_ref[rows, :], preferred_element_type=F32)
        kvf_s[c] = kv[0:2 * DK] * blk
        kvb_s[c] = kv[2 * DK:4 * DK] * blk
        return carry

    lax.fori_loop(0, n_chunks, kv_body, 0, unroll=unroll)

    def load_state(s_ref, s):
        r0[...] = jnp.zeros_like(r0)
        r0[0:DK, 0:DV] = s_ref[s, 0, 0].astype(F32)
        r0[DK:2 * DK, DV:2 * DV] = s_ref[s, 0, 1].astype(F32)
        return r0[...]

    for s in range(bb):
        def fwd_body(n, rf, s=s):
            c = s * nc + n
            rcat_s[c, 0:2 * DK, :] = rf.astype(BF16)
            return cd_f * rf + kvf_s[c]

        def bwd_body(t, rb, s=s):
            c = s * nc + nc - 1 - t
            rcat_s[c, 2 * DK:4 * DK, :] = rb.astype(BF16)
            return cd_b * rb + kvb_s[c]

        zeros = jnp.zeros((2 * DK, 2 * DV), F32)
        rf_fin = lax.fori_loop(0, nc, fwd_body, load_state(s0f_ref, s) if has_init else zeros)
        rb_fin = lax.fori_loop(0, nc, bwd_body, load_state(s0b_ref, s) if has_init else zeros)
        if emit_state:
            for s_ref, r in ((sf_ref, rf_fin), (sb_ref, rb_fin)):
                s_ref[s, 0, 0] = r[0:DK, 0:DV]
                s_ref[s, 0, 1] = r[DK:2 * DK, DV:2 * DV]

    gn = gn_ref[...]

    lane_mean = jnp.full((DV, DV), 1.0 / DV, BF16)

    def head_norm(o):
        mu = jnp.dot(o.astype(BF16), lane_mean, preferred_element_type=F32)
        d = o - mu
        var = jnp.dot((d * d).astype(BF16), lane_mean, preferred_element_type=F32)
        return d * lax.rsqrt(var + GN_EPS)

    def score_body(c, carry):
        rows = chunk_rows(c)
        q16 = q_ref[rows, :]
        zero = jnp.zeros_like(q16)
        q_ab = jnp.concatenate([jnp.where(lane_a, q16, zero), jnp.where(lane_a, zero, q16)], axis=0)
        s = lax.dot_general(q_ab, k_ref[rows, :], (((1,), (1,)), ((), ())), preferred_element_type=F32)
        p_s[c, 0:CHUNK, :] = (s[0:CHUNK] * dm_a).astype(BF16)
        p_s[c, CHUNK:2 * CHUNK, :] = (s[CHUNK:2 * CHUNK] * dm_b).astype(BF16)
        return carry

    lax.fori_loop(0, n_chunks, score_body, 0, unroll=unroll)

    def value_body(c, carry):
        rows = chunk_rows(c)
        q = q_ref[rows, :].astype(F32)
        v = v_ref[rows, :]
        qx = jnp.concatenate([(q * xi_f).astype(BF16), (q * xi_b).astype(BF16)], axis=1)
        oc = jnp.dot(qx, rcat_s[c], preferred_element_type=F32)
        o_a = jnp.dot(p_s[c, 0:CHUNK, :], v[:, 0:DV], preferred_element_type=F32)
        o_b = jnp.dot(p_s[c, CHUNK:2 * CHUNK, :], v[:, DV:2 * DV], preferred_element_type=F32)
        o_s[c, :, 0:DV] = o_a + oc[:, 0:DV]
        o_s[c, :, DV:2 * DV] = o_b + oc[:, DV:2 * DV]
        return carry

    lax.fori_loop(0, n_chunks, value_body, 0, unroll=unroll)

    def gate_body(c, carry):
        rows = chunk_rows(c)
        o = o_s[c]
        y = jnp.concatenate([head_norm(o[:, 0:DV]), head_norm(o[:, DV:2 * DV])], axis=1)
        g = gr_ref[rows, :].astype(F32)
        o_ref[rows, :] = (g * _sigmoid(g) * (y * gn)).astype(BF16)
        return carry

    lax.fori_loop(0, n_chunks, gate_body, 0, unroll=unroll)


def _retention(proj, lgs, gn_g, seq, *, init=None, emit_state):
    t = proj.shape[0]
    b = t // seq
    nc = seq // CHUNK
    bb = max(1, RET_CHUNKS_PER_STEP // nc)
    rows = bb * seq
    has_init = init is not None
    pairs = N_HEADS // 2
    qk_blocks = (N_HEADS * DK) // (2 * DK)
    in_specs = [pl.BlockSpec(memory_space=pltpu.SMEM),
                pl.BlockSpec((rows, 2 * DK), lambda i, p: (i, p)),
                pl.BlockSpec((rows, 2 * DK), lambda i, p: (i, qk_blocks + p)),
                pl.BlockSpec((rows, 2 * DV), lambda i, p: (i, qk_blocks + p)),
                pl.BlockSpec((rows, 2 * DV), lambda i, p: (i, 2 * qk_blocks + p)),
                pl.BlockSpec((1, 2 * DV), lambda i, p: (0, p))]
    args = [lgs, proj, proj, proj, proj, gn_g]
    state_spec = pl.BlockSpec((bb, 1, 2, DK, DV), lambda i, p: (i, 0, p, 0, 0))
    if has_init:
        in_specs += [state_spec, state_spec]
        args += list(init)
    out_specs = [pl.BlockSpec((rows, 2 * DV), lambda i, p: (i, p))]
    out_shape = [jax.ShapeDtypeStruct((t, N_HEADS * DV), BF16)]
    if emit_state:
        out_specs += [state_spec, state_spec]
        out_shape += [jax.ShapeDtypeStruct((b, 1, N_HEADS, DK, DV), F32)] * 2
    return pl.pallas_call(
        functools.partial(_ret_kernel, nc=nc, bb=bb, has_init=has_init, emit_state=emit_state),
        grid=(b // bb, pairs),
        in_specs=in_specs,
        out_specs=out_specs,
        out_shape=out_shape,
        scratch_shapes=[pltpu.VMEM((bb * nc, 2 * DK, 2 * DV), F32),
                        pltpu.VMEM((bb * nc, 2 * DK, 2 * DV), F32),
                        pltpu.VMEM((bb * nc, 4 * DK, 2 * DV), BF16),
                        pltpu.VMEM((bb * nc, 2 * CHUNK, CHUNK), BF16),
                        pltpu.VMEM((bb * nc, CHUNK, 2 * DV), F32),
                        pltpu.VMEM((2 * DK, 2 * DV), F32)],
        compiler_params=pltpu.CompilerParams(dimension_semantics=("arbitrary", "arbitrary"),
                                             vmem_limit_bytes=VMEM_LIMIT),
        name="retention",
    )(*args)


def _rope_tables(seq):
    n_freq = DK // 4
    pos = np.arange(seq)
    inv = jnp.asarray(ROPE_BASE, F32) ** (-jnp.arange(n_freq, dtype=F32) / n_freq)
    ang_r = jnp.asarray(pos // GRID_W, F32)[:, None] * inv
    ang_c = jnp.asarray(pos % GRID_W, F32)[:, None] * inv
    cos = jnp.concatenate([jnp.cos(ang_r)] * 2 + [jnp.cos(ang_c)] * 2, axis=1)
    sin = jnp.concatenate([-jnp.sin(ang_r), jnp.sin(ang_r), -jnp.sin(ang_c), jnp.sin(ang_c)], axis=1)
    return jnp.tile(cos, (1, 2)), jnp.tile(sin, (1, 2))


def _mix_kernel(og_ref, cb_ref, cc_ref, cx_ref, ccp_ref, cxp_ref, ccn_ref, cxn_ref, ga_ref, gb_ref, x_ref,
                cw_ref, wro_ref, wco_ref, wo_ref, wr_ref, npost_ref, npre_ref, g1_ref, sh2_ref, sc2_ref,
                x1_ref, h2_ref, aff_ref, *, tm, sub, seq):
    i = pl.program_id(0)
    halo = ccp_ref.shape[0]
    wr = wr_ref[...]
    w_hi = wr.astype(BF16)
    w_lo = (wr - w_hi.astype(F32)).astype(BF16)
    row = lax.broadcasted_iota(jnp.int32, (sub, 1), 0)
    lane = lax.broadcasted_iota(jnp.int32, (sub, ROUTER_LANES), 1)

    def product_row(cc, cx, r):
        return cc[r:r + 1, :].astype(F32) * cx[r:r + 1, :].astype(F32)

    for s in range(tm // sub):
        r0 = s * sub
        rows = slice(r0, r0 + sub)
        has_prev = jnp.where((i * tm + r0) % seq != 0, 1.0, 0.0)
        has_next = jnp.where((i * tm + r0 + sub) % seq != 0, 1.0, 0.0)
        prev_row = product_row(ccp_ref, cxp_ref, halo - 1) if s == 0 else product_row(cc_ref, cx_ref, r0 - 1)
        next_row = (product_row(ccn_ref, cxn_ref, 0) if r0 + sub == tm
                    else product_row(cc_ref, cx_ref, r0 + sub))
        prod = cc_ref[rows, :].astype(F32) * cx_ref[rows, :].astype(F32)
        up = jnp.where(row == 0, prev_row * has_prev, pltpu.roll(prod, 1, 0))
        dn = jnp.where(row == sub - 1, next_row * has_next, pltpu.roll(prod, sub - 1, 0))
        u = up * cw_ref[0:1, :] + prod * cw_ref[1:2, :] + dn * cw_ref[2:3, :]
        y_conv = jnp.dot((cb_ref[rows, :].astype(F32) * u).astype(BF16), wco_ref[...],
                         preferred_element_type=F32)
        y_ret = jnp.dot(og_ref[rows, :], wro_ref[...], preferred_element_type=F32)
        merged = (_sigmoid(ga_ref[rows, :].astype(F32)) * y_ret
                  + _sigmoid(gb_ref[rows, :].astype(F32)) * y_conv)
        m = jnp.dot(merged.astype(BF16), wo_ref[...], preferred_element_type=F32)
        x1 = x_ref[rows, :] + g1_ref[0] * _rms(m, npost_ref[...])
        x1_ref[rows, :] = x1
        h2 = _rms(x1, npre_ref[...]) * (1.0 + sc2_ref[0]) + sh2_ref[0]
        h_hi = h2.astype(BF16)
        h2_ref[rows, :] = h_hi
        h_lo = (h2 - h_hi.astype(F32)).astype(BF16)
        logits = (jnp.dot(h_hi, w_hi, preferred_element_type=F32)
                  + jnp.dot(h_lo, w_hi, preferred_element_type=F32)
                  + jnp.dot(h_hi, w_lo, preferred_element_type=F32))
        logits = jnp.where(lane < N_EXPERTS, logits, -jnp.inf)
        e = jnp.exp(logits - jnp.max(logits, axis=-1, keepdims=True))
        aff = e / jnp.sum(e, axis=-1, keepdims=True)
        aff_ref[:, rows] = aff.T[0:N_EXPERTS, :]


def _mix(og, proj, x, mod3, conv_w, wro, wco, wo, wr_pad, npost, npre, seq, cond_row_fn):
    t = x.shape[0]
    tm = MIX_TILE
    halo = 16
    hb = tm // halo
    last_halo = t // halo - 1
    col = lambda c: (lambda i: (i, c))
    row_vec = pl.BlockSpec((1, D_MODEL), lambda i: (0, 0))
    wspec = pl.BlockSpec((D_MODEL, D_MODEL), lambda i: (0, 0))
    modspec = lambda c: pl.BlockSpec((1, 1, D_MODEL), lambda i: (cond_row_fn(i * tm), 0, c))
    tile = lambda c: pl.BlockSpec((tm, D_MODEL), col(c))
    prev = lambda c: pl.BlockSpec((halo, D_MODEL), lambda i: (jnp.maximum(i * hb - 1, 0), c))
    nxt = lambda c: pl.BlockSpec((halo, D_MODEL), lambda i: (jnp.minimum((i + 1) * hb, last_halo), c))
    return pl.pallas_call(
        functools.partial(_mix_kernel, tm=tm, sub=MIX_SUB_TILE, seq=seq),
        grid=(t // tm,),
        in_specs=[tile(0), tile(3), tile(4), tile(5), prev(4), prev(5), nxt(4), nxt(5), tile(6), tile(7),
                  tile(0),
                  pl.BlockSpec((3, D_MODEL), lambda i: (0, 0)), wspec, wspec, wspec,
                  pl.BlockSpec((D_MODEL, ROUTER_LANES), lambda i: (0, 0)),
                  row_vec, row_vec, modspec(2), modspec(3), modspec(4)],
        out_specs=[tile(0), tile(0), pl.BlockSpec((N_EXPERTS, tm), lambda i: (0, i))],
        out_shape=[jax.ShapeDtypeStruct((t, D_MODEL), F32),
                   jax.ShapeDtypeStruct((t, D_MODEL), BF16),
                   jax.ShapeDtypeStruct((N_EXPERTS, t), F32)],
        compiler_params=pltpu.CompilerParams(dimension_semantics=("arbitrary",),
                                             vmem_limit_bytes=VMEM_LIMIT),
        name="mix_out",
    )(og, proj, proj, proj, proj, proj, proj, proj, proj, proj, x,
      conv_w, wro, wco, wo, wr_pad, npost, npre, mod3, mod3, mod3)


def _expert_kernel(xs_ref, wg_ref, wu_ref, wd_ref, o_ref, *, sub):
    wg = wg_ref[0].astype(BF16)
    wu = wu_ref[0].astype(BF16)
    wd = wd_ref[0].astype(BF16)
    for r in range(xs_ref.shape[1] // sub):
        rows = slice(r * sub, (r + 1) * sub)
        x = xs_ref[0, rows, :]
        g = jnp.dot(x, wg, preferred_element_type=F32)
        u = jnp.dot(x, wu, preferred_element_type=F32)
        hid = (g * _sigmoid(g) * u).astype(BF16)
        o_ref[0, rows, :] = jnp.dot(hid, wd, preferred_element_type=F32).astype(BF16)


def _experts(xs, wg, wu, wd):
    e, cap, d = xs.shape
    ff = wg.shape[2]
    tr = min(cap, EXPERT_ROWS_PER_STEP)
    return pl.pallas_call(
        functools.partial(_expert_kernel, sub=EXPERT_SUB_ROWS),
        grid=(e, cap // tr),
        in_specs=[pl.BlockSpec((1, tr, d), lambda a, r: (a, r, 0)),
                  pl.BlockSpec((1, d, ff), lambda a, r: (a, 0, 0)),
                  pl.BlockSpec((1, d, ff), lambda a, r: (a, 0, 0)),
                  pl.BlockSpec((1, ff, d), lambda a, r: (a, 0, 0))],
        out_specs=pl.BlockSpec((1, tr, d), lambda a, r: (a, r, 0)),
        out_shape=jax.ShapeDtypeStruct((e, cap, d), BF16),
        compiler_params=pltpu.CompilerParams(dimension_semantics=("arbitrary", "arbitrary"),
                                             vmem_limit_bytes=EXPERT_VMEM_LIMIT),
        name="experts",
    )(xs, wg, wu, wd)


def _route_kernel(aff_ref, pos_ref, excl_ref, *, cap):
    ne, nblk, lanes = aff_ref.shape
    aff = aff_ref[...]

    def count(mask):
        m = jnp.where(mask, 1.0, 0.0)
        return jnp.sum(jnp.sum(m, axis=1, keepdims=True), axis=2, keepdims=True)

    def as_float(bits):
        return lax.bitcast_convert_type(bits, F32)

    def bit_step(it, thr_bits):
        cand = thr_bits | jnp.left_shift(jnp.int32(1), 30 - it)
        return jnp.where(count(aff >= as_float(cand)) >= cap, cand, thr_bits)

    thr = as_float(lax.fori_loop(0, 31, bit_step, jnp.zeros((ne, 1, 1), jnp.int32)))
    gt = aff > thr
    eq = aff == thr
    need = cap - count(gt)

    li = lax.broadcasted_iota(jnp.int32, (lanes, lanes), 0)
    lj = lax.broadcasted_iota(jnp.int32, (lanes, lanes), 1)
    incl_lanes = jnp.where(li <= lj, 1.0, 0.0).astype(BF16)
    all_lanes = jnp.ones((lanes, lanes), BF16)
    bi = lax.broadcasted_iota(jnp.int32, (nblk, nblk), 0)
    bj = lax.broadcasted_iota(jnp.int32, (nblk, nblk), 1)
    earlier_blocks = jnp.where(bj < bi, 1.0, 0.0).astype(BF16)

    def ranks(mask):
        m = jnp.where(mask, 1.0, 0.0).reshape(ne * nblk, lanes).astype(BF16)
        incl = jnp.dot(m, incl_lanes, preferred_element_type=F32).reshape(ne, nblk, lanes)
        tot = jnp.dot(m, all_lanes, preferred_element_type=F32).reshape(ne, nblk, lanes).astype(BF16)
        excl = jnp.stack([jnp.dot(earlier_blocks, tot[i], preferred_element_type=F32) for i in range(ne)])
        return excl, incl

    ex_eq, in_eq = ranks(eq)
    sel = gt | (eq & (ex_eq + in_eq - 1.0 < need))
    ex_sel, in_sel = ranks(sel)
    pos_ref[...] = jnp.where(sel, ex_sel + in_sel - 1.0, -1.0).astype(jnp.int32)
    excl_ref[...] = ex_sel.astype(jnp.int32)


def _route(aff_t, cap):
    ne, t = aff_t.shape
    nblk = t // 128
    shape = (ne, nblk, 128)
    full = pl.BlockSpec(shape, lambda i: (0, 0, 0))
    pos, excl = pl.pallas_call(
        functools.partial(_route_kernel, cap=cap),
        grid=(1,),
        in_specs=[full],
        out_specs=[full, full],
        out_shape=[jax.ShapeDtypeStruct(shape, jnp.int32)] * 2,
        compiler_params=pltpu.CompilerParams(dimension_semantics=("arbitrary",),
                                             vmem_limit_bytes=VMEM_LIMIT),
        name="route",
    )(aff_t.reshape(shape))
    base = jnp.concatenate([excl[:, :, 0], jnp.full((ne, 1), cap, jnp.int32)], axis=1)
    return pos.reshape(ne, t), base


ROUTE_TILE = 256
ROUTE_WINDOW = 64
SLOT_ALIGN = 16
DISPATCH_GROUP = 4
DISPATCH_TILES_PER_STEP = 4


def _window(base_ref, e, blk, w, cap):
    lo = base_ref[e, blk] // SLOT_ALIGN + w * (ROUTE_WINDOW // SLOT_ALIGN)
    return lo * SLOT_ALIGN, jnp.minimum(lo, (cap - ROUTE_WINDOW) // SLOT_ALIGN) * SLOT_ALIGN


def _n_windows(base_ref, e, blk, cap):
    lo, _ = _window(base_ref, e, blk, 0, cap)
    end = base_ref[e, blk + ROUTE_TILE // 128]
    return (end - lo + ROUTE_WINDOW - 1) // ROUTE_WINDOW


def _dispatch_kernel(base_ref, h_ref, pos_ref, xs_ref, *, cap):
    g, j = pl.program_id(0), pl.program_id(1)

    @pl.when(j == 0)
    def _():
        xs_ref[...] = jnp.zeros_like(xs_ref)

    ri = lax.broadcasted_iota(jnp.int32, (ROUTE_WINDOW, ROUTE_TILE), 0)

    def onehot(prow, lo, off, first):
        hit = prow - off == ri
        if not first:
            hit = hit & (prow >= lo)
        return jnp.where(hit, 1.0, 0.0).astype(BF16)

    def add_rows(ge, off, rows):
        sl = pl.ds(pl.multiple_of(off, SLOT_ALIGN), ROUTE_WINDOW)
        xs_ref[ge, sl, :] = xs_ref[ge, sl, :] + rows.astype(BF16)

    for sub in range(DISPATCH_TILES_PER_STEP):
        blk = (j * DISPATCH_TILES_PER_STEP + sub) * (ROUTE_TILE // 128)
        tok = slice(sub * ROUTE_TILE, (sub + 1) * ROUTE_TILE)
        h = h_ref[tok, :]
        prows, offs = [], []
        for ge in range(DISPATCH_GROUP):
            e = g * DISPATCH_GROUP + ge
            prows.append(pos_ref[pl.ds(e, 1), tok])
            offs.append(_window(base_ref, e, blk, 0, cap))
        sel = jnp.concatenate([onehot(prows[ge], *offs[ge], True) for ge in range(DISPATCH_GROUP)], axis=0)
        rows = jnp.dot(sel, h, preferred_element_type=F32)
        for ge in range(DISPATCH_GROUP):
            add_rows(ge, offs[ge][1], rows[ge * ROUTE_WINDOW:(ge + 1) * ROUTE_WINDOW])

        for ge in range(DISPATCH_GROUP):
            e = g * DISPATCH_GROUP + ge

            def extra(w, carry, ge=ge, e=e, blk=blk, h=h, prow=prows[ge]):
                lo, off = _window(base_ref, e, blk, w, cap)
                add_rows(ge, off, jnp.dot(onehot(prow, lo, off, False), h, preferred_element_type=F32))
                return carry

            lax.fori_loop(1, _n_windows(base_ref, e, blk, cap), extra, 0)


def _dispatch(base, h2, pos, cap):
    t, d = h2.shape
    ne = pos.shape[0]
    step = ROUTE_TILE * DISPATCH_TILES_PER_STEP
    return pl.pallas_call(
        functools.partial(_dispatch_kernel, cap=cap),
        grid_spec=pltpu.PrefetchScalarGridSpec(
            num_scalar_prefetch=1,
            grid=(ne // DISPATCH_GROUP, t // step),
            in_specs=[pl.BlockSpec((step, d), lambda g, j, b: (j, 0)),
                      pl.BlockSpec((ne, step), lambda g, j, b: (0, j))],
            out_specs=pl.BlockSpec((DISPATCH_GROUP, cap, d), lambda g, j, b: (g, 0, 0))),
        out_shape=jax.ShapeDtypeStruct((ne, cap, d), BF16),
        compiler_params=pltpu.CompilerParams(dimension_semantics=("arbitrary", "arbitrary"),
                                             vmem_limit_bytes=VMEM_LIMIT),
        name="dispatch",
    )(base, h2, pos)


def _combine_kernel(base_ref, pos_ref, aff_ref, x1_ref, g_ref, g2_ref, eo_ref, o_ref,
                    f_ref, win_buf, win_sem, extra_buf, extra_sem, *, cap, n_tiles):
    j = pl.program_id(0)
    blocks_per_tile = ROUTE_TILE // 128
    blk = j * blocks_per_tile
    slot = lax.rem(j, 2)
    ri = lax.broadcasted_iota(jnp.int32, (ROUTE_WINDOW, ROUTE_TILE), 0)
    tn = (((0,), (0,)), ((), ()))

    def window_copy(e, off, dst, sem):
        return pltpu.make_async_copy(eo_ref.at[e, pl.ds(pl.multiple_of(off, SLOT_ALIGN), ROUTE_WINDOW), :],
                                     dst, sem)

    def first_window_copies(tile, buf_slot):
        return [window_copy(e, _window(base_ref, e, tile * blocks_per_tile, 0, cap)[1],
                            win_buf.at[buf_slot, pl.ds(e * ROUTE_WINDOW, ROUTE_WINDOW), :], win_sem.at[buf_slot])
                for e in range(N_EXPERTS)]

    @pl.when(j == 0)
    def _():
        for cp in first_window_copies(0, 0):
            cp.start()

    @pl.when(j + 1 < n_tiles)
    def _():
        for cp in first_window_copies(j + 1, 1 - slot):
            cp.start()

    def weights(e, lo, off, first):
        prow = pos_ref[e:e + 1, :]
        hit = prow - off == ri
        if not first:
            hit = hit & (prow >= lo)
        return jnp.where(hit, aff_ref[e:e + 1, :], 0.0).astype(BF16)

    q = jnp.concatenate([weights(e, *_window(base_ref, e, blk, 0, cap), True) for e in range(N_EXPERTS)], axis=0)
    for cp in first_window_copies(j, slot):
        cp.wait()
    f_ref[...] = lax.dot_general(q, win_buf[slot], tn, preferred_element_type=F32)

    n_win = [_n_windows(base_ref, e, blk, cap) for e in range(N_EXPERTS)]

    @pl.when(functools.reduce(jnp.maximum, n_win) > 1)
    def _():
        for e in range(N_EXPERTS):
            def extra(w, carry, e=e):
                lo, off = _window(base_ref, e, blk, w, cap)
                cp = window_copy(e, off, extra_buf, extra_sem)
                cp.start()
                cp.wait()
                f_ref[...] += lax.dot_general(weights(e, lo, off, False), extra_buf[...], tn,
                                              preferred_element_type=F32)
                return carry

            lax.fori_loop(1, n_win[e], extra, 0)

    o_ref[...] = x1_ref[...] + g2_ref[0] * _rms(f_ref[...], g_ref[...])


def _combine(base, eo, pos, aff_t, x1, mod3, npost, cond_row_fn, cap):
    t, d = x1.shape
    ne = pos.shape[0]
    n_tiles = t // ROUTE_TILE
    tile = pl.BlockSpec((ROUTE_TILE, d), lambda j, b: (j, 0))
    etile = pl.BlockSpec((ne, ROUTE_TILE), lambda j, b: (0, j))
    return pl.pallas_call(
        functools.partial(_combine_kernel, cap=cap, n_tiles=n_tiles),
        grid_spec=pltpu.PrefetchScalarGridSpec(
            num_scalar_prefetch=1,
            grid=(n_tiles,),
            in_specs=[etile, etile, tile,
                      pl.BlockSpec((1, d), lambda j, b: (0, 0)),
                      pl.BlockSpec((1, 1, d), lambda j, b: (cond_row_fn(j * ROUTE_TILE), 0, 5)),
                      pl.BlockSpec(memory_space=pl.ANY)],
            out_specs=tile,
            scratch_shapes=[pltpu.VMEM((ROUTE_TILE, d), F32),
                            pltpu.VMEM((2, ne * ROUTE_WINDOW, d), BF16), pltpu.SemaphoreType.DMA((2,)),
                            pltpu.VMEM((ROUTE_WINDOW, d), BF16), pltpu.SemaphoreType.DMA(())]),
        out_shape=jax.ShapeDtypeStruct((t, d), F32),
        compiler_params=pltpu.CompilerParams(dimension_semantics=("arbitrary",),
                                             vmem_limit_bytes=VMEM_LIMIT),
        name="combine",
    )(base, pos, aff_t, x1, npost, mod3, eo)


def _trunk(x3, mod3, cond_row_fn, weights, lgs, *, rope_tabs, init, emit_state):
    (n_pre_mix, n_post_mix, n_pre_ffn, n_post_ffn, w_in, gn_g, conv_w, wro, wco, wo, wr_pad, wg, wu, wd) = weights
    b, seq, d = x3.shape
    t = b * seq
    x = x3.reshape(t, d)
    proj = _inproj(x, mod3, n_pre_mix, w_in, cond_row_fn, seq, rope_tabs)
    ret = _retention(proj, lgs, gn_g, seq, init=init, emit_state=emit_state)
    og = ret[0]
    x1, h2, aff_t = _mix(og, proj, x, mod3, conv_w, wro, wco, wo, wr_pad, n_post_mix, n_pre_ffn, seq, cond_row_fn)
    cap = CAPACITY_FACTOR * t // N_EXPERTS
    pos, base = _route(aff_t, cap)
    eo = _experts(_dispatch(base, h2, pos, cap), wg, wu, wd)
    y = _combine(base, eo, pos, aff_t, x1, mod3, n_post_ffn, cond_row_fn, cap)
    return y.reshape(b, seq, d), ret[1:]


def kernel(x_prompt, x_sample, state_ret_fwd, state_ret_bwd, c, c_ctx, w_ada, b_ada, norm_pre_mix, norm_post_mix,
           norm_pre_ffn, norm_post_ffn, w_in, ret_decay_fwd, ret_decay_bwd, ret_norm_g, conv_w, w_ret_o, w_conv_o,
           w_o, w_router, w_gate, w_up, w_down):
    depth = w_ada.shape[0]
    assert depth == 1
    dec_b, dec_seq = x_sample.shape[0], x_sample.shape[1]
    xp, xs = x_prompt, x_sample
    l = 0
    cond = jnp.zeros((N_COND_ROWS, D_MODEL), F32).at[0:dec_b].set(c).at[CTX_COND_ROW].set(c_ctx)
    mod3 = _ada(cond, w_ada[l], b_ada[l][None, :]).reshape(N_COND_ROWS, 1, 6 * D_MODEL)
    lgs = jnp.stack([-jax.nn.softplus(-ret_decay_fwd[l].astype(F32)),
                     -jax.nn.softplus(-ret_decay_bwd[l].astype(F32))])
    wr_pad = jnp.pad(w_router[l], ((0, 0), (0, ROUTER_LANES - N_EXPERTS)))
    weights = (norm_pre_mix[l][None, :], norm_post_mix[l][None, :], norm_pre_ffn[l][None, :],
               norm_post_ffn[l][None, :], w_in[l].astype(BF16), ret_norm_g[l][None, :], conv_w[l],
               w_ret_o[l].astype(BF16), w_conv_o[l].astype(BF16), w_o[l].astype(BF16), wr_pad,
               w_gate[l], w_up[l], w_down[l])
    yp, (s_f, s_b) = _trunk(xp, mod3, lambda r: CTX_COND_ROW, weights, lgs,
                            rope_tabs=None, init=None, emit_state=True)
    ys, _ = _trunk(xs, mod3, lambda r: r // dec_seq, weights, lgs,
                   rope_tabs=_rope_tables(dec_seq), init=(state_ret_fwd, state_ret_bwd), emit_state=False)
    return (yp, ys, s_f, s_b)
```

```python
import functools

import jax
import jax.numpy as jnp
import numpy as np
from jax import lax
from jax.experimental import pallas as pl
from jax.experimental.pallas import tpu as pltpu

F32 = jnp.float32
BF16 = jnp.bfloat16

D_MODEL = 1024
N_HEADS = 8
DK = 64
DV = 128
CHUNK = 128
GRID_W = 64
N_EXPERTS = 16
CAPACITY_FACTOR = 2
D_IN_TOTAL = 8192
RMS_EPS = 1e-6
GN_EPS = 1e-5
ROPE_BASE = 10000.0
N_COND_ROWS = 16
CTX_COND_ROW = 8
ROUTER_LANES = 128
RET_CHUNKS_PER_STEP = 16
RET_UNROLL = 4
MIX_TILE = 512
MIX_SUB_TILE = 256
VMEM_LIMIT = 48 * 1024 * 1024
EXPERT_ROWS_PER_STEP = 1024
EXPERT_SUB_ROWS = 512
EXPERT_VMEM_LIMIT = 56 * 1024 * 1024


def _sigmoid(x):
    return 1.0 / (1.0 + jnp.exp(-x))


def _rms(x, g):
    return x * lax.rsqrt(jnp.mean(x * x, axis=-1, keepdims=True) + RMS_EPS) * g


def _ada_kernel(c_ref, w_ref, b_ref, o_ref):
    c = c_ref[...]
    s = c * _sigmoid(c)
    o_ref[...] = jnp.dot(s, w_ref[...], preferred_element_type=F32,
                         precision=lax.Precision.HIGHEST) + b_ref[...]


def _ada(cond, w_ada, b_ada):
    n = w_ada.shape[1]
    tn = 1024
    return pl.pallas_call(
        _ada_kernel,
        grid=(n // tn,),
        in_specs=[pl.BlockSpec((N_COND_ROWS, D_MODEL), lambda j: (0, 0)),
                  pl.BlockSpec((D_MODEL, tn), lambda j: (0, j)),
                  pl.BlockSpec((1, tn), lambda j: (0, j))],
        out_specs=pl.BlockSpec((N_COND_ROWS, tn), lambda j: (0, j)),
        out_shape=jax.ShapeDtypeStruct((N_COND_ROWS, n), F32),
        compiler_params=pltpu.CompilerParams(dimension_semantics=("arbitrary",),
                                             vmem_limit_bytes=VMEM_LIMIT),
        name="ada_mod",
    )(cond, w_ada, b_ada)


def _inproj_body(x_ref, g_ref, sh_ref, sc_ref, w_ref, cos_ref, sin_ref, a_ref, p_ref, s_ref, h_ref, use_rope):
    j = pl.program_id(1)

    @pl.when(j == 0)
    def _():
        h = _rms(x_ref[...], g_ref[...]) * (1.0 + sc_ref[0]) + sh_ref[0]
        h_ref[...] = h.astype(BF16)

    lanes = 2 * DK
    width = 2 * lanes
    n_slices = w_ref.shape[1] // width
    n_qk = 2 * N_HEADS * DK

    def slice_dot(b):
        return jnp.dot(h_ref[...], w_ref[:, b * width:(b + 1) * width], preferred_element_type=F32)

    @pl.when(j == 0)
    def _():
        ci = lax.broadcasted_iota(jnp.int32, (h_ref.shape[0], lanes), 1)
        for b in range(n_slices):
            acc = slice_dot(b)
            if b * width >= n_qk:
                a_ref[:, b * width:(b + 1) * width] = acc.astype(BF16)
                continue
            for half in range(2):
                col = b * width + half * lanes
                x = acc[:, half * lanes:(half + 1) * lanes]
                if col >= N_HEADS * DK:
                    x = x * (DK ** -0.5)
                if use_rope:
                    swapped = jnp.where((ci & 31) < 16, pltpu.roll(x, lanes - 16, 1), pltpu.roll(x, 16, 1))
                    x = x * cos_ref[...] + swapped * sin_ref[...]
                a_ref[:, col:col + lanes] = x.astype(BF16)

    @pl.when(j == 1)
    def _():
        for b in range(n_slices):
            acc = slice_dot(b)
            if b < n_slices // 2:
                acc = acc * _sigmoid(acc)
            a_ref[:, b * width:(b + 1) * width] = acc.astype(BF16)

    @pl.when(j == 2)
    def _():
        for b in range(n_slices // 2):
            p_ref[:, b * width:(b + 1) * width] = (slice_dot(b) * slice_dot(b + n_slices // 2)).astype(BF16)

    @pl.when(j == 3)
    def _():
        for b in range(n_slices):
            s_ref[:, b * width:(b + 1) * width] = _sigmoid(slice_dot(b)).astype(BF16)


def _inproj_kernel(x_ref, g_ref, sh_ref, sc_ref, w_ref, *rest, use_rope):
    if use_rope:
        cos_ref, sin_ref, a_ref, p_ref, s_ref, h_ref = rest
    else:
        cos_ref = sin_ref = None
        a_ref, p_ref, s_ref, h_ref = rest
    _inproj_body(x_ref, g_ref, sh_ref, sc_ref, w_ref, cos_ref, sin_ref, a_ref, p_ref, s_ref, h_ref, use_rope)


def _inproj(x, mod3, g, w_bf16, cond_row_fn, seq, rope_tabs):
    t = x.shape[0]
    tm, tn = 1024, 2048
    assert D_IN_TOTAL == 4 * tn
    in_specs = [pl.BlockSpec((tm, D_MODEL), lambda i, j: (i, 0)),
                pl.BlockSpec((1, D_MODEL), lambda i, j: (0, 0)),
                pl.BlockSpec((1, 1, D_MODEL), lambda i, j: (cond_row_fn(i * tm), 0, 0)),
                pl.BlockSpec((1, 1, D_MODEL), lambda i, j: (cond_row_fn(i * tm), 0, 1)),
                pl.BlockSpec((D_MODEL, tn), lambda i, j: (0, j))]
    args = [x, g, mod3, mod3, w_bf16]
    if rope_tabs is not None:
        tiles_per_seq = seq // tm
        in_specs += [pl.BlockSpec((tm, 2 * DK), lambda i, j: (i % tiles_per_seq, 0))] * 2
        args += list(rope_tabs)
    return pl.pallas_call(
        functools.partial(_inproj_kernel, use_rope=rope_tabs is not None),
        grid=(t // tm, D_IN_TOTAL // tn),
        in_specs=in_specs,
        out_specs=[pl.BlockSpec((tm, tn), lambda i, j: (i, jnp.minimum(j, 1))),
                   pl.BlockSpec((tm, tn // 2), lambda i, j: (i, 0)),
                   pl.BlockSpec((tm, tn), lambda i, j: (i, 0))],
        out_shape=[jax.ShapeDtypeStruct((t, 2 * tn), BF16),
                   jax.ShapeDtypeStruct((t, tn // 2), BF16),
                   jax.ShapeDtypeStruct((t, tn), BF16)],
        scratch_shapes=[pltpu.VMEM((tm, D_MODEL), BF16)],
        compiler_params=pltpu.CompilerParams(dimension_semantics=("arbitrary", "arbitrary"),
                                             vmem_limit_bytes=VMEM_LIMIT),
        name="inproj",
    )(*args)


def _ret_kernel(*refs, nc, bb, has_init, emit_state):
    it = iter(refs)
    lg_ref = next(it)
    q_ref, k_ref, v_ref, gr_ref, gn_ref = (next(it) for _ in range(5))
    s0f_ref = s0b_ref = sf_ref = sb_ref = None
    if has_init:
        s0f_ref, s0b_ref = next(it), next(it)
    o_ref = next(it)
    if emit_state:
        sf_ref, sb_ref = next(it), next(it)
    kvf_s, kvb_s, rcat_s, p_s, o_s, r0 = (next(it) for _ in range(6))

    p = pl.program_id(1)
    lgf_a, lgf_b = lg_ref[0, 2 * p], lg_ref[0, 2 * p + 1]
    lgb_a, lgb_b = lg_ref[1, 2 * p], lg_ref[1, 2 * p + 1]

    ri = lax.broadcasted_iota(jnp.int32, (CHUNK, 2 * DK), 0)
    ci = lax.broadcasted_iota(jnp.int32, (CHUNK, 2 * DK), 1)
    lane_a = ci < DK
    rowf = ri.astype(F32)
    diff = rowf - ci.astype(F32)
    lgf_lane = jnp.where(lane_a, lgf_a, lgf_b)
    lgb_lane = jnp.where(lane_a, lgb_a, lgb_b)
    xi_f = jnp.exp(lgf_lane * (rowf + 1.0))
    xi_b = jnp.exp(lgb_lane * (CHUNK - rowf))
    zeta_f = jnp.exp(lgf_lane * (CHUNK - 1.0 - rowf))
    zeta_b = jnp.exp(lgb_lane * rowf)

    def decay_matrix(lgf, lgb):
        return jnp.where(diff > 0, jnp.exp(lgf * diff),
                         jnp.where(diff < 0, jnp.exp(lgb * (-diff)), 2.0))

    dm_a = decay_matrix(lgf_a, lgb_a)
    dm_b = decay_matrix(lgf_b, lgb_b)

    r2 = lax.broadcasted_iota(jnp.int32, (2 * DK, 2 * DV), 0)
    c2 = lax.broadcasted_iota(jnp.int32, (2 * DK, 2 * DV), 1)
    top = r2 < DK
    blk = (top == (c2 < DV)).astype(F32)
    cd_f = jnp.exp(jnp.where(top, lgf_a, lgf_b) * float(CHUNK)) * blk
    cd_b = jnp.exp(jnp.where(top, lgb_a, lgb_b) * float(CHUNK)) * blk

    def chunk_rows(c):
        return pl.ds(pl.multiple_of(c * CHUNK, CHUNK), CHUNK)

    n_chunks = bb * nc
    unroll = RET_UNROLL

    def kv_body(c, carry):
        rows = chunk_rows(c)
        k = k_ref[rows, :].astype(F32)
        kz_t = jnp.concatenate([k * zeta_f, k * zeta_b], axis=1).T.astype(BF16)
        kv = jnp.dot(kz_t, v_ref[rows, :], preferred_element_type=F32)
        kvf_s[c] = kv[0:2 * DK] * blk
        kvb_s[c] = kv[2 * DK:4 * DK] * blk
        return carry

    lax.fori_loop(0, n_chunks, kv_body, 0, unroll=unroll)

    def load_state(s_ref, s):
        r0[...] = jnp.zeros_like(r0)
        r0[0:DK, 0:DV] = s_ref[s, 0, 0].astype(F32)
        r0[DK:2 * DK, DV:2 * DV] = s_ref[s, 0, 1].astype(F32)
        return r0[...]

    for s in range(bb):
        def fwd_body(n, rf, s=s):
            c = s * nc + n
            rcat_s[c, 0:2 * DK, :] = rf.astype(BF16)
            return cd_f * rf + kvf_s[c]

        def bwd_body(t, rb, s=s):
            c = s * nc + nc - 1 - t
            rcat_s[c, 2 * DK:4 * DK, :] = rb.astype(BF16)
            return cd_b * rb + kvb_s[c]

        zeros = jnp.zeros((2 * DK, 2 * DV), F32)
        rf_fin = lax.fori_loop(0, nc, fwd_body, load_state(s0f_ref, s) if has_init else zeros)
        rb_fin = lax.fori_loop(0, nc, bwd_body, load_state(s0b_ref, s) if has_init else zeros)
        if emit_state:
            for s_ref, r in ((sf_ref, rf_fin), (sb_ref, rb_fin)):
                s_ref[s, 0, 0] = r[0:DK, 0:DV]
                s_ref[s, 0, 1] = r[DK:2 * DK, DV:2 * DV]

    gn = gn_ref[...]

    lane_mean = jnp.full((DV, DV), 1.0 / DV, BF16)

    def head_norm(o):
        mu = jnp.dot(o.astype(BF16), lane_mean, preferred_element_type=F32)
        d = o - mu
        var = jnp.dot((d * d).astype(BF16), lane_mean, preferred_element_type=F32)
        return d * lax.rsqrt(var + GN_EPS)

    def score_body(c, carry):
        rows = chunk_rows(c)
        q16 = q_ref[rows, :]
        zero = jnp.zeros_like(q16)
        q_ab = jnp.concatenate([jnp.where(lane_a, q16, zero), jnp.where(lane_a, zero, q16)], axis=0)
        s = lax.dot_general(q_ab, k_ref[rows, :], (((1,), (1,)), ((), ())), preferred_element_type=F32)
        p_s[c, 0:CHUNK, :] = (s[0:CHUNK] * dm_a).astype(BF16)
        p_s[c, CHUNK:2 * CHUNK, :] = (s[CHUNK:2 * CHUNK] * dm_b).astype(BF16)
        return carry

    lax.fori_loop(0, n_chunks, score_body, 0, unroll=unroll)

    def value_body(c, carry):
        rows = chunk_rows(c)
        q = q_ref[rows, :].astype(F32)
        v = v_ref[rows, :]
        qx = jnp.concatenate([(q * xi_f).astype(BF16), (q * xi_b).astype(BF16)], axis=1)
        oc = jnp.dot(qx, rcat_s[c], preferred_element_type=F32)
        o_a = jnp.dot(p_s[c, 0:CHUNK, :], v[:, 0:DV], preferred_element_type=F32)
        o_b = jnp.dot(p_s[c, CHUNK:2 * CHUNK, :], v[:, DV:2 * DV], preferred_element_type=F32)
        o_s[c, :, 0:DV] = o_a + oc[:, 0:DV]
        o_s[c, :, DV:2 * DV] = o_b + oc[:, DV:2 * DV]
        return carry

    lax.fori_loop(0, n_chunks, value_body, 0, unroll=unroll)

    def gate_body(c, carry):
        rows = chunk_rows(c)
        o = o_s[c]
        y = jnp.concatenate([head_norm(o[:, 0:DV]), head_norm(o[:, DV:2 * DV])], axis=1)
        o_ref[rows, :] = (gr_ref[rows, :].astype(F32) * (y * gn)).astype(BF16)
        return carry

    lax.fori_loop(0, n_chunks, gate_body, 0, unroll=unroll)


def _retention(proj, lgs, gn_g, seq, *, init=None, emit_state):
    t = proj.shape[0]
    b = t // seq
    nc = seq // CHUNK
    bb = max(1, RET_CHUNKS_PER_STEP // nc)
    rows = bb * seq
    has_init = init is not None
    pairs = N_HEADS // 2
    qk_blocks = (N_HEADS * DK) // (2 * DK)
    in_specs = [pl.BlockSpec(memory_space=pltpu.SMEM),
                pl.BlockSpec((rows, 2 * DK), lambda i, p: (i, p)),
                pl.BlockSpec((rows, 2 * DK), lambda i, p: (i, qk_blocks + p)),
                pl.BlockSpec((rows, 2 * DV), lambda i, p: (i, qk_blocks + p)),
                pl.BlockSpec((rows, 2 * DV), lambda i, p: (i, 2 * qk_blocks + p)),
                pl.BlockSpec((1, 2 * DV), lambda i, p: (0, p))]
    args = [lgs, proj, proj, proj, proj, gn_g]
    state_spec = pl.BlockSpec((bb, 1, 2, DK, DV), lambda i, p: (i, 0, p, 0, 0))
    if has_init:
        in_specs += [state_spec, state_spec]
        args += list(init)
    out_specs = [pl.BlockSpec((rows, 2 * DV), lambda i, p: (i, p))]
    out_shape = [jax.ShapeDtypeStruct((t, N_HEADS * DV), BF16)]
    if emit_state:
        out_specs += [state_spec, state_spec]
        out_shape += [jax.ShapeDtypeStruct((b, 1, N_HEADS, DK, DV), F32)] * 2
    return pl.pallas_call(
        functools.partial(_ret_kernel, nc=nc, bb=bb, has_init=has_init, emit_state=emit_state),
        grid=(b // bb, pairs),
        in_specs=in_specs,
        out_specs=out_specs,
        out_shape=out_shape,
        scratch_shapes=[pltpu.VMEM((bb * nc, 2 * DK, 2 * DV), F32),
                        pltpu.VMEM((bb * nc, 2 * DK, 2 * DV), F32),
                        pltpu.VMEM((bb * nc, 4 * DK, 2 * DV), BF16),
                        pltpu.VMEM((bb * nc, 2 * CHUNK, CHUNK), BF16),
                        pltpu.VMEM((bb * nc, CHUNK, 2 * DV), F32),
                        pltpu.VMEM((2 * DK, 2 * DV), F32)],
        compiler_params=pltpu.CompilerParams(dimension_semantics=("arbitrary", "arbitrary"),
                                             vmem_limit_bytes=VMEM_LIMIT),
        name="retention",
    )(*args)


def _rope_tables(seq):
    n_freq = DK // 4
    pos = np.arange(seq)
    inv = jnp.asarray(ROPE_BASE, F32) ** (-jnp.arange(n_freq, dtype=F32) / n_freq)
    ang_r = jnp.asarray(pos // GRID_W, F32)[:, None] * inv
    ang_c = jnp.asarray(pos % GRID_W, F32)[:, None] * inv
    cos = jnp.concatenate([jnp.cos(ang_r)] * 2 + [jnp.cos(ang_c)] * 2, axis=1)
    sin = jnp.concatenate([-jnp.sin(ang_r), jnp.sin(ang_r), -jnp.sin(ang_c), jnp.sin(ang_c)], axis=1)
    return jnp.tile(cos, (1, 2)), jnp.tile(sin, (1, 2))


def _mix_kernel(og_ref, cb_ref, p_ref, pp_ref, pn_ref, sa_ref, sb_ref, x_ref,
                cw_ref, wro_ref, wco_ref, wo_ref, wr_ref, npost_ref, npre_ref, g1_ref, sh2_ref, sc2_ref,
                x1_ref, h2_ref, aff_ref, *, tm, sub, seq):
    i = pl.program_id(0)
    halo = pp_ref.shape[0]
    wr = wr_ref[...]
    w_hi = wr.astype(BF16)
    w_lo = (wr - w_hi.astype(F32)).astype(BF16)
    w_hi_lo = jnp.concatenate([w_hi, w_lo], axis=1)
    row = lax.broadcasted_iota(jnp.int32, (sub, 1), 0)
    lane = lax.broadcasted_iota(jnp.int32, (sub, ROUTER_LANES), 1)

    for s in range(tm // sub):
        r0 = s * sub
        rows = slice(r0, r0 + sub)
        has_prev = jnp.where((i * tm + r0) % seq != 0, 1.0, 0.0)
        has_next = jnp.where((i * tm + r0 + sub) % seq != 0, 1.0, 0.0)
        prev_row = (pp_ref[halo - 1:halo, :] if s == 0 else p_ref[r0 - 1:r0, :]).astype(F32)
        next_row = (pn_ref[0:1, :] if r0 + sub == tm else p_ref[r0 + sub:r0 + sub + 1, :]).astype(F32)
        prod = p_ref[rows, :].astype(F32)
        up = jnp.where(row == 0, prev_row * has_prev, pltpu.roll(prod, 1, 0))
        dn = jnp.where(row == sub - 1, next_row * has_next, pltpu.roll(prod, sub - 1, 0))
        u = up * cw_ref[0:1, :] + prod * cw_ref[1:2, :] + dn * cw_ref[2:3, :]
        y_conv = jnp.dot((cb_ref[rows, :].astype(F32) * u).astype(BF16), wco_ref[...],
                         preferred_element_type=F32)
        y_ret = jnp.dot(og_ref[rows, :], wro_ref[...], preferred_element_type=F32)
        merged = sa_ref[rows, :].astype(F32) * y_ret + sb_ref[rows, :].astype(F32) * y_conv
        m = jnp.dot(merged.astype(BF16), wo_ref[...], preferred_element_type=F32)
        x1 = x_ref[rows, :] + g1_ref[0] * _rms(m, npost_ref[...])
        x1_ref[rows, :] = x1
        h2 = _rms(x1, npre_ref[...]) * (1.0 + sc2_ref[0]) + sh2_ref[0]
        h_hi = h2.astype(BF16)
        h2_ref[rows, :] = h_hi
        h_lo = (h2 - h_hi.astype(F32)).astype(BF16)
        hi_terms = jnp.dot(h_hi, w_hi_lo, preferred_element_type=F32)
        logits = (hi_terms[:, 0:ROUTER_LANES] + hi_terms[:, ROUTER_LANES:2 * ROUTER_LANES]
                  + jnp.dot(h_lo, w_hi, preferred_element_type=F32))
        logits = jnp.where(lane < N_EXPERTS, logits, -jnp.inf)
        e = jnp.exp(logits - jnp.max(logits, axis=-1, keepdims=True))
        aff = e / jnp.sum(e, axis=-1, keepdims=True)
        aff_ref[:, rows] = aff.T[0:N_EXPERTS, :]


def _mix(og, proj_a, proj_p, proj_s, x, mod3, conv_w, wro, wco, wo, wr_pad, npost, npre, seq, cond_row_fn):
    t = x.shape[0]
    conv_b_block = (2 * N_HEADS * DK + 2 * N_HEADS * DV) // D_MODEL
    tm = MIX_TILE
    halo = 16
    hb = tm // halo
    last_halo = t // halo - 1
    col = lambda c: (lambda i: (i, c))
    row_vec = pl.BlockSpec((1, D_MODEL), lambda i: (0, 0))
    wspec = pl.BlockSpec((D_MODEL, D_MODEL), lambda i: (0, 0))
    modspec = lambda c: pl.BlockSpec((1, 1, D_MODEL), lambda i: (cond_row_fn(i * tm), 0, c))
    tile = lambda c: pl.BlockSpec((tm, D_MODEL), col(c))
    prev = lambda c: pl.BlockSpec((halo, D_MODEL), lambda i: (jnp.maximum(i * hb - 1, 0), c))
    nxt = lambda c: pl.BlockSpec((halo, D_MODEL), lambda i: (jnp.minimum((i + 1) * hb, last_halo), c))
    return pl.pallas_call(
        functools.partial(_mix_kernel, tm=tm, sub=MIX_SUB_TILE, seq=seq),
        grid=(t // tm,),
        in_specs=[tile(0), tile(conv_b_block), tile(0), prev(0), nxt(0), tile(0), tile(1),
                  tile(0),
                  pl.BlockSpec((3, D_MODEL), lambda i: (0, 0)), wspec, wspec, wspec,
                  pl.BlockSpec((D_MODEL, ROUTER_LANES), lambda i: (0, 0)),
                  row_vec, row_vec, modspec(2), modspec(3), modspec(4)],
        out_specs=[tile(0), tile(0), pl.BlockSpec((N_EXPERTS, tm), lambda i: (0, i))],
        out_shape=[jax.ShapeDtypeStruct((t, D_MODEL), F32),
                   jax.ShapeDtypeStruct((t, D_MODEL), BF16),
                   jax.ShapeDtypeStruct((N_EXPERTS, t), F32)],
        compiler_params=pltpu.CompilerParams(dimension_semantics=("arbitrary",),
                                             vmem_limit_bytes=VMEM_LIMIT),
        name="mix_out",
    )(og, proj_a, proj_p, proj_p, proj_p, proj_s, proj_s, x,
      conv_w, wro, wco, wo, wr_pad, npost, npre, mod3, mod3, mod3)


def _expert_kernel(xs_ref, wg_ref, wu_ref, wd_ref, o_ref, *, sub):
    wg = wg_ref[0].astype(BF16)
    wu = wu_ref[0].astype(BF16)
    wd = wd_ref[0].astype(BF16)
    for r in range(xs_ref.shape[1] // sub):
        rows = slice(r * sub, (r + 1) * sub)
        x = xs_ref[0, rows, :]
        g = jnp.dot(x, wg, preferred_element_type=F32)
        u = jnp.dot(x, wu, preferred_element_type=F32)
        hid = (g * _sigmoid(g) * u).astype(BF16)
        o_ref[0, rows, :] = jnp.dot(hid, wd, preferred_element_type=F32).astype(BF16)


def _experts(xs, wg, wu, wd):
    e, cap, d = xs.shape
    ff = wg.shape[2]
    tr = min(cap, EXPERT_ROWS_PER_STEP)
    return pl.pallas_call(
        functools.partial(_expert_kernel, sub=EXPERT_SUB_ROWS),
        grid=(e, cap // tr),
        in_specs=[pl.BlockSpec((1, tr, d), lambda a, r: (a, r, 0)),
                  pl.BlockSpec((1, d, ff), lambda a, r: (a, 0, 0)),
                  pl.BlockSpec((1, d, ff), lambda a, r: (a, 0, 0)),
                  pl.BlockSpec((1, ff, d), lambda a, r: (a, 0, 0))],
        out_specs=pl.BlockSpec((1, tr, d), lambda a, r: (a, r, 0)),
        out_shape=jax.ShapeDtypeStruct((e, cap, d), BF16),
        compiler_params=pltpu.CompilerParams(dimension_semantics=("arbitrary", "arbitrary"),
                                             vmem_limit_bytes=EXPERT_VMEM_LIMIT),
        name="experts",
    )(xs, wg, wu, wd)


def _route_kernel(aff_ref, pos_ref, excl_ref, *, cap):
    ne, nblk, lanes = aff_ref.shape
    aff = aff_ref[...]

    def count(mask):
        m = jnp.where(mask, 1.0, 0.0)
        return jnp.sum(jnp.sum(m, axis=1, keepdims=True), axis=2, keepdims=True)

    def as_float(bits):
        return lax.bitcast_convert_type(bits, F32)

    def bit_step(it, thr_bits):
        cand = thr_bits | jnp.left_shift(jnp.int32(1), 30 - it)
        return jnp.where(count(aff >= as_float(cand)) >= cap, cand, thr_bits)

    thr = as_float(lax.fori_loop(0, 31, bit_step, jnp.zeros((ne, 1, 1), jnp.int32)))
    gt = aff > thr
    eq = aff == thr
    need = cap - count(gt)

    li = lax.broadcasted_iota(jnp.int32, (lanes, lanes), 0)
    lj = lax.broadcasted_iota(jnp.int32, (lanes, lanes), 1)
    incl_lanes = jnp.where(li <= lj, 1.0, 0.0).astype(BF16)
    all_lanes = jnp.ones((lanes, lanes), BF16)
    bi = lax.broadcasted_iota(jnp.int32, (nblk, nblk), 0)
    bj = lax.broadcasted_iota(jnp.int32, (nblk, nblk), 1)
    earlier_blocks = jnp.where(bj < bi, 1.0, 0.0).astype(BF16)

    def ranks(mask):
        m = jnp.where(mask, 1.0, 0.0).reshape(ne * nblk, lanes).astype(BF16)
        incl = jnp.dot(m, incl_lanes, preferred_element_type=F32).reshape(ne, nblk, lanes)
        tot = jnp.dot(m, all_lanes, preferred_element_type=F32).reshape(ne, nblk, lanes).astype(BF16)
        excl = jnp.stack([jnp.dot(earlier_blocks, tot[i], preferred_element_type=F32) for i in range(ne)])
        return excl, incl

    ex_eq, in_eq = ranks(eq)
    sel = gt | (eq & (ex_eq + in_eq - 1.0 < need))
    ex_sel, in_sel = ranks(sel)
    pos_ref[...] = jnp.where(sel, ex_sel + in_sel - 1.0, -1.0).astype(jnp.int32)
    excl_ref[...] = ex_sel.astype(jnp.int32)


def _route(aff_t, cap):
    ne, t = aff_t.shape
    nblk = t // 128
    shape = (ne, nblk, 128)
    full = pl.BlockSpec(shape, lambda i: (0, 0, 0))
    pos, excl = pl.pallas_call(
        functools.partial(_route_kernel, cap=cap),
        grid=(1,),
        in_specs=[full],
        out_specs=[full, full],
        out_shape=[jax.ShapeDtypeStruct(shape, jnp.int32)] * 2,
        compiler_params=pltpu.CompilerParams(dimension_semantics=("arbitrary",),
                                             vmem_limit_bytes=VMEM_LIMIT),
        name="route",
    )(aff_t.reshape(shape))
    base = jnp.concatenate([excl[:, :, 0], jnp.full((ne, 1), cap, jnp.int32)], axis=1)
    return pos.reshape(ne, t), base


ROUTE_TILE = 256
ROUTE_WINDOW = 64
SLOT_ALIGN = 16
DISPATCH_GROUP = 4
DISPATCH_TILES_PER_STEP = 4


def _window(base_ref, e, blk, w, cap):
    lo = base_ref[e, blk] // SLOT_ALIGN + w * (ROUTE_WINDOW // SLOT_ALIGN)
    return lo * SLOT_ALIGN, jnp.minimum(lo, (cap - ROUTE_WINDOW) // SLOT_ALIGN) * SLOT_ALIGN


def _n_windows(base_ref, e, blk, cap):
    lo, _ = _window(base_ref, e, blk, 0, cap)
    end = base_ref[e, blk + ROUTE_TILE // 128]
    return (end - lo + ROUTE_WINDOW - 1) // ROUTE_WINDOW


def _dispatch_kernel(base_ref, h_ref, pos_ref, xs_ref, *, cap):
    g, j = pl.program_id(0), pl.program_id(1)

    @pl.when(j == 0)
    def _():
        xs_ref[...] = jnp.zeros_like(xs_ref)

    ri = lax.broadcasted_iota(jnp.int32, (ROUTE_WINDOW, ROUTE_TILE), 0)

    def onehot(prow, lo, off, first):
        hit = prow - off == ri
        if not first:
            hit = hit & (prow >= lo)
        return jnp.where(hit, 1.0, 0.0).astype(BF16)

    def add_rows(ge, off, rows):
        sl = pl.ds(pl.multiple_of(off, SLOT_ALIGN), ROUTE_WINDOW)
        xs_ref[ge, sl, :] = xs_ref[ge, sl, :] + rows.astype(BF16)

    for sub in range(DISPATCH_TILES_PER_STEP):
        blk = (j * DISPATCH_TILES_PER_STEP + sub) * (ROUTE_TILE // 128)
        tok = slice(sub * ROUTE_TILE, (sub + 1) * ROUTE_TILE)
        h = h_ref[tok, :]
        prows, offs = [], []
        for ge in range(DISPATCH_GROUP):
            e = g * DISPATCH_GROUP + ge
            prows.append(pos_ref[pl.ds(e, 1), tok])
            offs.append(_window(base_ref, e, blk, 0, cap))
        sel = jnp.concatenate([onehot(prows[ge], *offs[ge], True) for ge in range(DISPATCH_GROUP)], axis=0)
        rows = jnp.dot(sel, h, preferred_element_type=F32)
        for ge in range(DISPATCH_GROUP):
            add_rows(ge, offs[ge][1], rows[ge * ROUTE_WINDOW:(ge + 1) * ROUTE_WINDOW])

        for ge in range(DISPATCH_GROUP):
            e = g * DISPATCH_GROUP + ge

            def extra(w, carry, ge=ge, e=e, blk=blk, h=h, prow=prows[ge]):
                lo, off = _window(base_ref, e, blk, w, cap)
                add_rows(ge, off, jnp.dot(onehot(prow, lo, off, False), h, preferred_element_type=F32))
                return carry

            lax.fori_loop(1, _n_windows(base_ref, e, blk, cap), extra, 0)


def _dispatch(base, h2, pos, cap):
    t, d = h2.shape
    ne = pos.shape[0]
    step = ROUTE_TILE * DISPATCH_TILES_PER_STEP
    return pl.pallas_call(
        functools.partial(_dispatch_kernel, cap=cap),
        grid_spec=pltpu.PrefetchScalarGridSpec(
            num_scalar_prefetch=1,
            grid=(ne // DISPATCH_GROUP, t // step),
            in_specs=[pl.BlockSpec((step, d), lambda g, j, b: (j, 0)),
                      pl.BlockSpec((ne, step), lambda g, j, b: (0, j))],
            out_specs=pl.BlockSpec((DISPATCH_GROUP, cap, d), lambda g, j, b: (g, 0, 0))),
        out_shape=jax.ShapeDtypeStruct((ne, cap, d), BF16),
        compiler_params=pltpu.CompilerParams(dimension_semantics=("arbitrary", "arbitrary"),
                                             vmem_limit_bytes=VMEM_LIMIT),
        name="dispatch",
    )(base, h2, pos)


def _combine_kernel(base_ref, pos_ref, aff_ref, x1_ref, g_ref, g2_ref, eo_ref, o_ref,
                    f_ref, win_buf, win_sem, extra_buf, extra_sem, *, cap, n_tiles):
    j = pl.program_id(0)
    blocks_per_tile = ROUTE_TILE // 128
    blk = j * blocks_per_tile
    slot = lax.rem(j, 2)
    ri = lax.broadcasted_iota(jnp.int32, (ROUTE_WINDOW, ROUTE_TILE), 0)
    tn = (((0,), (0,)), ((), ()))

    def window_copy(e, off, dst, sem):
        return pltpu.make_async_copy(eo_ref.at[e, pl.ds(pl.multiple_of(off, SLOT_ALIGN), ROUTE_WINDOW), :],
                                     dst, sem)

    def first_window_copies(tile, buf_slot):
        return [window_copy(e, _window(base_ref, e, tile * blocks_per_tile, 0, cap)[1],
                            win_buf.at[buf_slot, pl.ds(e * ROUTE_WINDOW, ROUTE_WINDOW), :], win_sem.at[buf_slot])
                for e in range(N_EXPERTS)]

    @pl.when(j == 0)
    def _():
        for cp in first_window_copies(0, 0):
            cp.start()

    @pl.when(j + 1 < n_tiles)
    def _():
        for cp in first_window_copies(j + 1, 1 - slot):
            cp.start()

    def weights(e, lo, off, first):
        prow = pos_ref[e:e + 1, :]
        hit = prow - off == ri
        if not first:
            hit = hit & (prow >= lo)
        return jnp.where(hit, aff_ref[e:e + 1, :], 0.0).astype(BF16)

    q = jnp.concatenate([weights(e, *_window(base_ref, e, blk, 0, cap), True) for e in range(N_EXPERTS)], axis=0)
    for cp in first_window_copies(j, slot):
        cp.wait()
    f_ref[...] = lax.dot_general(q, win_buf[slot], tn, preferred_element_type=F32)

    n_win = [_n_windows(base_ref, e, blk, cap) for e in range(N_EXPERTS)]

    @pl.when(functools.reduce(jnp.maximum, n_win) > 1)
    def _():
        for e in range(N_EXPERTS):
            def extra(w, carry, e=e):
                lo, off = _window(base_ref, e, blk, w, cap)
                cp = window_copy(e, off, extra_buf, extra_sem)
                cp.start()
                cp.wait()
                f_ref[...] += lax.dot_general(weights(e, lo, off, False), extra_buf[...], tn,
                                              preferred_element_type=F32)
                return carry

            lax.fori_loop(1, n_win[e], extra, 0)

    o_ref[...] = x1_ref[...] + g2_ref[0] * _rms(f_ref[...], g_ref[...])


def _combine(base, eo, pos, aff_t, x1, mod3, npost, cond_row_fn, cap):
    t, d = x1.shape
    ne = pos.shape[0]
    n_tiles = t // ROUTE_TILE
    tile = pl.BlockSpec((ROUTE_TILE, d), lambda j, b: (j, 0))
    etile = pl.BlockSpec((ne, ROUTE_TILE), lambda j, b: (0, j))
    return pl.pallas_call(
        functools.partial(_combine_kernel, cap=cap, n_tiles=n_tiles),
        grid_spec=pltpu.PrefetchScalarGridSpec(
            num_scalar_prefetch=1,
            grid=(n_tiles,),
            in_specs=[etile, etile, tile,
                      pl.BlockSpec((1, d), lambda j, b: (0, 0)),
                      pl.BlockSpec((1, 1, d), lambda j, b: (cond_row_fn(j * ROUTE_TILE), 0, 5)),
                      pl.BlockSpec(memory_space=pl.ANY)],
            out_specs=tile,
            scratch_shapes=[pltpu.VMEM((ROUTE_TILE, d), F32),
                            pltpu.VMEM((2, ne * ROUTE_WINDOW, d), BF16), pltpu.SemaphoreType.DMA((2,)),
                            pltpu.VMEM((ROUTE_WINDOW, d), BF16), pltpu.SemaphoreType.DMA(())]),
        out_shape=jax.ShapeDtypeStruct((t, d), F32),
        compiler_params=pltpu.CompilerParams(dimension_semantics=("arbitrary",),
                                             vmem_limit_bytes=VMEM_LIMIT),
        name="combine",
    )(base, pos, aff_t, x1, npost, mod3, eo)


def _trunk(x3, mod3, cond_row_fn, weights, lgs, *, rope_tabs, init, emit_state):
    (n_pre_mix, n_post_mix, n_pre_ffn, n_post_ffn, w_in, gn_g, conv_w, wro, wco, wo, wr_pad, wg, wu, wd) = weights
    b, seq, d = x3.shape
    t = b * seq
    x = x3.reshape(t, d)
    proj_a, proj_p, proj_s = _inproj(x, mod3, n_pre_mix, w_in, cond_row_fn, seq, rope_tabs)
    ret = _retention(proj_a, lgs, gn_g, seq, init=init, emit_state=emit_state)
    og = ret[0]
    x1, h2, aff_t = _mix(og, proj_a, proj_p, proj_s, x, mod3, conv_w, wro, wco, wo, wr_pad, n_post_mix,
                         n_pre_ffn, seq, cond_row_fn)
    cap = CAPACITY_FACTOR * t // N_EXPERTS
    pos, base = _route(aff_t, cap)
    eo = _experts(_dispatch(base, h2, pos, cap), wg, wu, wd)
    y = _combine(base, eo, pos, aff_t, x1, mod3, n_post_ffn, cond_row_fn, cap)
    return y.reshape(b, seq, d), ret[1:]


def kernel(x_prompt, x_sample, state_ret_fwd, state_ret_bwd, c, c_ctx, w_ada, b_ada, norm_pre_mix, norm_post_mix,
           norm_pre_ffn, norm_post_ffn, w_in, ret_decay_fwd, ret_decay_bwd, ret_norm_g, conv_w, w_ret_o, w_conv_o,
           w_o, w_router, w_gate, w_up, w_down):
    depth = w_ada.shape[0]
    assert depth == 1
    dec_b, dec_seq = x_sample.shape[0], x_sample.shape[1]
    xp, xs = x_prompt, x_sample
    l = 0
    cond = jnp.zeros((N_COND_ROWS, D_MODEL), F32).at[0:dec_b].set(c).at[CTX_COND_ROW].set(c_ctx)
    mod3 = _ada(cond, w_ada[l], b_ada[l][None, :]).reshape(N_COND_ROWS, 1, 6 * D_MODEL)
    lgs = jnp.stack([-jax.nn.softplus(-ret_decay_fwd[l].astype(F32)),
                     -jax.nn.softplus(-ret_decay_bwd[l].astype(F32))])
    wr_pad = jnp.pad(w_router[l], ((0, 0), (0, ROUTER_LANES - N_EXPERTS)))
    weights = (norm_pre_mix[l][None, :], norm_post_mix[l][None, :], norm_pre_ffn[l][None, :],
               norm_post_ffn[l][None, :], w_in[l].astype(BF16), ret_norm_g[l][None, :], conv_w[l],
               w_ret_o[l].astype(BF16), w_conv_o[l].astype(BF16), w_o[l].astype(BF16), wr_pad,
               w_gate[l], w_up[l], w_down[l])
    yp, (s_f, s_b) = _trunk(xp, mod3, lambda r: CTX_COND_ROW, weights, lgs,
                            rope_tabs=None, init=None, emit_state=True)
    ys, _ = _trunk(xs, mod3, lambda r: r // dec_seq, weights, lgs,
                   rope_tabs=_rope_tables(dec_seq), init=(state_ret_fwd, state_ret_bwd), emit_state=False)
    return (yp, ys, s_f, s_b)
```

```python
import functools

import jax
import jax.numpy as jnp
import numpy as np
from jax import lax
from jax.experimental import pallas as pl
from jax.experimental.pallas import tpu as pltpu

F32 = jnp.float32
BF16 = jnp.bfloat16

D_MODEL = 1024
N_HEADS = 8
DK = 64
DV = 128
CHUNK = 128
GRID_W = 64
N_EXPERTS = 16
CAPACITY_FACTOR = 2
D_IN_TOTAL = 8192
RMS_EPS = 1e-6
GN_EPS = 1e-5
ROPE_BASE = 10000.0
N_COND_ROWS = 16
CTX_COND_ROW = 8
ROUTER_LANES = 128
RET_CHUNKS_PER_STEP = 16
RET_UNROLL = 4
MIX_TILE = 512
MIX_SUB_TILE = 256
VMEM_LIMIT = 48 * 1024 * 1024
EXPERT_ROWS_PER_STEP = 1024
EXPERT_SUB_ROWS = 512
EXPERT_VMEM_LIMIT = 56 * 1024 * 1024


def _sigmoid(x):
    return 1.0 / (1.0 + jnp.exp(-x))


def _rms(x, g):
    return x * lax.rsqrt(jnp.mean(x * x, axis=-1, keepdims=True) + RMS_EPS) * g


def _ada_kernel(c_ref, w_ref, b_ref, o_ref):
    c = c_ref[...]
    s = c * _sigmoid(c)
    o_ref[...] = jnp.dot(s, w_ref[...], preferred_element_type=F32,
                         precision=lax.Precision.HIGHEST) + b_ref[...]


def _ada(cond, w_ada, b_ada):
    n = w_ada.shape[1]
    tn = 1024
    return pl.pallas_call(
        _ada_kernel,
        grid=(n // tn,),
        in_specs=[pl.BlockSpec((N_COND_ROWS, D_MODEL), lambda j: (0, 0)),
                  pl.BlockSpec((D_MODEL, tn), lambda j: (0, j)),
                  pl.BlockSpec((1, tn), lambda j: (0, j))],
        out_specs=pl.BlockSpec((N_COND_ROWS, tn), lambda j: (0, j)),
        out_shape=jax.ShapeDtypeStruct((N_COND_ROWS, n), F32),
        compiler_params=pltpu.CompilerParams(dimension_semantics=("arbitrary",),
                                             vmem_limit_bytes=VMEM_LIMIT),
        name="ada_mod",
    )(cond, w_ada, b_ada)


def _inproj_body(x_ref, g_ref, sh_ref, sc_ref, w_ref, cos_ref, sin_ref, a_ref, p_ref, s_ref, h_ref, use_rope):
    j = pl.program_id(1)

    @pl.when(j == 0)
    def _():
        h = _rms(x_ref[...], g_ref[...]) * (1.0 + sc_ref[0]) + sh_ref[0]
        h_ref[...] = h.astype(BF16)

    lanes = 2 * DK
    width = 2 * lanes
    n_slices = w_ref.shape[1] // width
    n_qk = 2 * N_HEADS * DK

    def slice_dot(b):
        return jnp.dot(h_ref[...], w_ref[:, b * width:(b + 1) * width], preferred_element_type=F32)

    @pl.when(j == 0)
    def _():
        ci = lax.broadcasted_iota(jnp.int32, (h_ref.shape[0], lanes), 1)
        for b in range(n_slices):
            acc = slice_dot(b)
            if b * width >= n_qk:
                a_ref[:, b * width:(b + 1) * width] = acc.astype(BF16)
                continue
            for half in range(2):
                col = b * width + half * lanes
                x = acc[:, half * lanes:(half + 1) * lanes]
                if col >= N_HEADS * DK:
                    x = x * (DK ** -0.5)
                if use_rope:
                    swapped = jnp.where((ci & 31) < 16, pltpu.roll(x, lanes - 16, 1), pltpu.roll(x, 16, 1))
                    x = x * cos_ref[...] + swapped * sin_ref[...]
                a_ref[:, col:col + lanes] = x.astype(BF16)

    @pl.when(j == 1)
    def _():
        for b in range(n_slices):
            acc = slice_dot(b)
            if b < n_slices // 2:
                acc = acc * _sigmoid(acc)
            a_ref[:, b * width:(b + 1) * width] = acc.astype(BF16)

    @pl.when(j == 2)
    def _():
        for b in range(n_slices // 2):
            p_ref[:, b * width:(b + 1) * width] = (slice_dot(b) * slice_dot(b + n_slices // 2)).astype(BF16)

    @pl.when(j == 3)
    def _():
        for b in range(n_slices):
            s_ref[:, b * width:(b + 1) * width] = _sigmoid(slice_dot(b)).astype(BF16)


def _inproj_kernel(x_ref, g_ref, sh_ref, sc_ref, w_ref, *rest, use_rope):
    if use_rope:
        cos_ref, sin_ref, a_ref, p_ref, s_ref, h_ref = rest
    else:
        cos_ref = sin_ref = None
        a_ref, p_ref, s_ref, h_ref = rest
    _inproj_body(x_ref, g_ref, sh_ref, sc_ref, w_ref, cos_ref, sin_ref, a_ref, p_ref, s_ref, h_ref, use_rope)


def _inproj(x, mod3, g, w_bf16, cond_row_fn, seq, rope_tabs):
    t = x.shape[0]
    tm, tn = 1024, 2048
    assert D_IN_TOTAL == 4 * tn
    in_specs = [pl.BlockSpec((tm, D_MODEL), lambda i, j: (i, 0)),
                pl.BlockSpec((1, D_MODEL), lambda i, j: (0, 0)),
                pl.BlockSpec((1, 1, D_MODEL), lambda i, j: (cond_row_fn(i * tm), 0, 0)),
                pl.BlockSpec((1, 1, D_MODEL), lambda i, j: (cond_row_fn(i * tm), 0, 1)),
                pl.BlockSpec((D_MODEL, tn), lambda i, j: (0, j))]
    args = [x, g, mod3, mod3, w_bf16]
    if rope_tabs is not None:
        tiles_per_seq = seq // tm
        in_specs += [pl.BlockSpec((tm, 2 * DK), lambda i, j: (i % tiles_per_seq, 0))] * 2
        args += list(rope_tabs)
    return pl.pallas_call(
        functools.partial(_inproj_kernel, use_rope=rope_tabs is not None),
        grid=(t // tm, D_IN_TOTAL // tn),
        in_specs=in_specs,
        out_specs=[pl.BlockSpec((tm, tn), lambda i, j: (i, jnp.minimum(j, 1))),
                   pl.BlockSpec((tm, tn // 2), lambda i, j: (i, 0)),
                   pl.BlockSpec((tm, tn), lambda i, j: (i, 0))],
        out_shape=[jax.ShapeDtypeStruct((t, 2 * tn), BF16),
                   jax.ShapeDtypeStruct((t, tn // 2), BF16),
                   jax.ShapeDtypeStruct((t, tn), BF16)],
        scratch_shapes=[pltpu.VMEM((tm, D_MODEL), BF16)],
        compiler_params=pltpu.CompilerParams(dimension_semantics=("arbitrary", "arbitrary"),
                                             vmem_limit_bytes=VMEM_LIMIT),
        name="inproj",
    )(*args)


def _ret_kernel(*refs, nc, bb, has_init, emit_state):
    it = iter(refs)
    lg_ref = next(it)
    q_ref, k_ref, v_ref, gr_ref, gn_ref = (next(it) for _ in range(5))
    s0f_ref = s0b_ref = sf_ref = sb_ref = None
    if has_init:
        s0f_ref, s0b_ref = next(it), next(it)
    o_ref = next(it)
    if emit_state:
        sf_ref, sb_ref = next(it), next(it)
    kvf_s, kvb_s, rcat_s, p_s, o_s, r0 = (next(it) for _ in range(6))

    p = pl.program_id(1)
    lgf_a, lgf_b = lg_ref[0, 2 * p], lg_ref[0, 2 * p + 1]
    lgb_a, lgb_b = lg_ref[1, 2 * p], lg_ref[1, 2 * p + 1]

    ri = lax.broadcasted_iota(jnp.int32, (CHUNK, 2 * DK), 0)
    ci = lax.broadcasted_iota(jnp.int32, (CHUNK, 2 * DK), 1)
    lane_a = ci < DK
    rowf = ri.astype(F32)
    diff = rowf - ci.astype(F32)
    lgf_lane = jnp.where(lane_a, lgf_a, lgf_b)
    lgb_lane = jnp.where(lane_a, lgb_a, lgb_b)
    xi_f = jnp.exp(lgf_lane * (rowf + 1.0))
    xi_b = jnp.exp(lgb_lane * (CHUNK - rowf))
    zeta_f = jnp.exp(lgf_lane * (CHUNK - 1.0 - rowf))
    zeta_b = jnp.exp(lgb_lane * rowf)

    def decay_matrix(lgf, lgb):
        return jnp.where(diff > 0, jnp.exp(lgf * diff),
                         jnp.where(diff < 0, jnp.exp(lgb * (-diff)), 2.0))

    dm_a = decay_matrix(lgf_a, lgb_a)
    dm_b = decay_matrix(lgf_b, lgb_b)

    r2 = lax.broadcasted_iota(jnp.int32, (2 * DK, 2 * DV), 0)
    c2 = lax.broadcasted_iota(jnp.int32, (2 * DK, 2 * DV), 1)
    top = r2 < DK
    blk = (top == (c2 < DV)).astype(F32)
    cd_f = jnp.exp(jnp.where(top, lgf_a, lgf_b) * float(CHUNK)) * blk
    cd_b = jnp.exp(jnp.where(top, lgb_a, lgb_b) * float(CHUNK)) * blk

    def chunk_rows(c):
        return pl.ds(pl.multiple_of(c * CHUNK, CHUNK), CHUNK)

    n_chunks = bb * nc
    unroll = RET_UNROLL

    def kv_body(c, carry):
        rows = chunk_rows(c)
        k = k_ref[rows, :].astype(F32)
        kz_t = jnp.concatenate([k * zeta_f, k * zeta_b], axis=1).T.astype(BF16)
        kv = jnp.dot(kz_t, v_ref[rows, :], preferred_element_type=F32)
        kvf_s[c] = kv[0:2 * DK] * blk
        kvb_s[c] = kv[2 * DK:4 * DK] * blk
        return carry

    def score_body(c, carry):
        rows = chunk_rows(c)
        q16 = q_ref[rows, :]
        zero = jnp.zeros_like(q16)
        q_ab = jnp.concatenate([jnp.where(lane_a, q16, zero), jnp.where(lane_a, zero, q16)], axis=0)
        s = lax.dot_general(q_ab, k_ref[rows, :], (((1,), (1,)), ((), ())), preferred_element_type=F32)
        p_s[c, 0:CHUNK, :] = (s[0:CHUNK] * dm_a).astype(BF16)
        p_s[c, CHUNK:2 * CHUNK, :] = (s[CHUNK:2 * CHUNK] * dm_b).astype(BF16)
        return carry

    lax.fori_loop(0, n_chunks, lambda c, carry: score_body(c, kv_body(c, carry)), 0, unroll=unroll)

    def load_state(s_ref, s):
        r0[...] = jnp.zeros_like(r0)
        r0[0:DK, 0:DV] = s_ref[s, 0, 0].astype(F32)
        r0[DK:2 * DK, DV:2 * DV] = s_ref[s, 0, 1].astype(F32)
        return r0[...]

    for s in range(bb):
        def fwd_body(n, rf, s=s):
            c = s * nc + n
            rcat_s[c, 0:2 * DK, :] = rf.astype(BF16)
            return cd_f * rf + kvf_s[c]

        def bwd_body(t, rb, s=s):
            c = s * nc + nc - 1 - t
            rcat_s[c, 2 * DK:4 * DK, :] = rb.astype(BF16)
            return cd_b * rb + kvb_s[c]

        zeros = jnp.zeros((2 * DK, 2 * DV), F32)
        rf_fin = lax.fori_loop(0, nc, fwd_body, load_state(s0f_ref, s) if has_init else zeros)
        rb_fin = lax.fori_loop(0, nc, bwd_body, load_state(s0b_ref, s) if has_init else zeros)
        if emit_state:
            for s_ref, r in ((sf_ref, rf_fin), (sb_ref, rb_fin)):
                s_ref[s, 0, 0] = r[0:DK, 0:DV]
                s_ref[s, 0, 1] = r[DK:2 * DK, DV:2 * DV]

    gn = gn_ref[...]

    lane_mean = jnp.full((DV, DV), 1.0 / DV, BF16)

    def value_body(c, carry):
        rows = chunk_rows(c)
        q = q_ref[rows, :].astype(F32)
        v = v_ref[rows, :]
        qx = jnp.concatenate([(q * xi_f).astype(BF16), (q * xi_b).astype(BF16)], axis=1)
        oc = jnp.dot(qx, rcat_s[c], preferred_element_type=F32)
        o_a = jnp.dot(p_s[c, 0:CHUNK, :], v[:, 0:DV], preferred_element_type=F32)
        o_b = jnp.dot(p_s[c, CHUNK:2 * CHUNK, :], v[:, DV:2 * DV], preferred_element_type=F32)
        o_s[c, :, 0:DV] = o_a + oc[:, 0:DV]
        o_s[c, :, DV:2 * DV] = o_b + oc[:, DV:2 * DV]
        return carry

    lax.fori_loop(0, n_chunks, value_body, 0, unroll=unroll)

    def lane_means(x):
        return jnp.concatenate([jnp.dot(x[:, h * DV:(h + 1) * DV].astype(BF16), lane_mean,
                                        preferred_element_type=F32) for h in range(2)], axis=1)

    def gate_body(c, carry):
        rows = chunk_rows(c)
        o = o_s[c]
        d = o - lane_means(o)
        y = d * lax.rsqrt(lane_means(d * d) + GN_EPS)
        o_ref[rows, :] = (gr_ref[rows, :].astype(F32) * (y * gn)).astype(BF16)
        return carry

    lax.fori_loop(0, n_chunks, gate_body, 0, unroll=unroll)


def _retention(proj, lgs, gn_g, seq, *, init=None, emit_state):
    t = proj.shape[0]
    b = t // seq
    nc = seq // CHUNK
    bb = max(1, RET_CHUNKS_PER_STEP // nc)
    rows = bb * seq
    has_init = init is not None
    pairs = N_HEADS // 2
    qk_blocks = (N_HEADS * DK) // (2 * DK)
    in_specs = [pl.BlockSpec(memory_space=pltpu.SMEM),
                pl.BlockSpec((rows, 2 * DK), lambda i, p: (i, p)),
                pl.BlockSpec((rows, 2 * DK), lambda i, p: (i, qk_blocks + p)),
                pl.BlockSpec((rows, 2 * DV), lambda i, p: (i, qk_blocks + p)),
                pl.BlockSpec((rows, 2 * DV), lambda i, p: (i, 2 * qk_blocks + p)),
                pl.BlockSpec((1, 2 * DV), lambda i, p: (0, p))]
    args = [lgs, proj, proj, proj, proj, gn_g]
    state_spec = pl.BlockSpec((bb, 1, 2, DK, DV), lambda i, p: (i, 0, p, 0, 0))
    if has_init:
        in_specs += [state_spec, state_spec]
        args += list(init)
    out_specs = [pl.BlockSpec((rows, 2 * DV), lambda i, p: (i, p))]
    out_shape = [jax.ShapeDtypeStruct((t, N_HEADS * DV), BF16)]
    if emit_state:
        out_specs += [state_spec, state_spec]
        out_shape += [jax.ShapeDtypeStruct((b, 1, N_HEADS, DK, DV), F32)] * 2
    return pl.pallas_call(
        functools.partial(_ret_kernel, nc=nc, bb=bb, has_init=has_init, emit_state=emit_state),
        grid=(b // bb, pairs),
        in_specs=in_specs,
        out_specs=out_specs,
        out_shape=out_shape,
        scratch_shapes=[pltpu.VMEM((bb * nc, 2 * DK, 2 * DV), F32),
                        pltpu.VMEM((bb * nc, 2 * DK, 2 * DV), F32),
                        pltpu.VMEM((bb * nc, 4 * DK, 2 * DV), BF16),
                        pltpu.VMEM((bb * nc, 2 * CHUNK, CHUNK), BF16),
                        pltpu.VMEM((bb * nc, CHUNK, 2 * DV), F32),
                        pltpu.VMEM((2 * DK, 2 * DV), F32)],
        compiler_params=pltpu.CompilerParams(dimension_semantics=("arbitrary", "arbitrary"),
                                             vmem_limit_bytes=VMEM_LIMIT),
        name="retention",
    )(*args)


def _rope_tables(seq):
    n_freq = DK // 4
    pos = np.arange(seq)
    inv = jnp.asarray(ROPE_BASE, F32) ** (-jnp.arange(n_freq, dtype=F32) / n_freq)
    ang_r = jnp.asarray(pos // GRID_W, F32)[:, None] * inv
    ang_c = jnp.asarray(pos % GRID_W, F32)[:, None] * inv
    cos = jnp.concatenate([jnp.cos(ang_r)] * 2 + [jnp.cos(ang_c)] * 2, axis=1)
    sin = jnp.concatenate([-jnp.sin(ang_r), jnp.sin(ang_r), -jnp.sin(ang_c), jnp.sin(ang_c)], axis=1)
    return jnp.tile(cos, (1, 2)), jnp.tile(sin, (1, 2))


def _mix_kernel(og_ref, cb_ref, p_ref, pp_ref, pn_ref, sa_ref, sb_ref, x_ref,
                cw_ref, wro_ref, wco_ref, wo_ref, wr_ref, npost_ref, npre_ref, g1_ref, sh2_ref, sc2_ref,
                x1_ref, h2_ref, aff_ref, *, tm, sub, seq):
    i = pl.program_id(0)
    halo = pp_ref.shape[0]
    wr = wr_ref[...]
    w_hi = wr.astype(BF16)
    w_lo = (wr - w_hi.astype(F32)).astype(BF16)
    w_hi_lo = jnp.concatenate([w_hi, w_lo], axis=1)
    row = lax.broadcasted_iota(jnp.int32, (sub, 1), 0)
    lane = lax.broadcasted_iota(jnp.int32, (sub, ROUTER_LANES), 1)

    for s in range(tm // sub):
        r0 = s * sub
        rows = slice(r0, r0 + sub)
        has_prev = jnp.where((i * tm + r0) % seq != 0, 1.0, 0.0)
        has_next = jnp.where((i * tm + r0 + sub) % seq != 0, 1.0, 0.0)
        prev_row = (pp_ref[halo - 1:halo, :] if s == 0 else p_ref[r0 - 1:r0, :]).astype(F32)
        next_row = (pn_ref[0:1, :] if r0 + sub == tm else p_ref[r0 + sub:r0 + sub + 1, :]).astype(F32)
        prod = p_ref[rows, :].astype(F32)
        up = jnp.where(row == 0, prev_row * has_prev, pltpu.roll(prod, 1, 0))
        dn = jnp.where(row == sub - 1, next_row * has_next, pltpu.roll(prod, sub - 1, 0))
        u = up * cw_ref[0:1, :] + prod * cw_ref[1:2, :] + dn * cw_ref[2:3, :]
        y_conv = jnp.dot((cb_ref[rows, :].astype(F32) * u).astype(BF16), wco_ref[...],
                         preferred_element_type=F32)
        y_ret = jnp.dot(og_ref[rows, :], wro_ref[...], preferred_element_type=F32)
        merged = sa_ref[rows, :].astype(F32) * y_ret + sb_ref[rows, :].astype(F32) * y_conv
        m = jnp.dot(merged.astype(BF16), wo_ref[...], preferred_element_type=F32)
        x1 = x_ref[rows, :] + g1_ref[0] * _rms(m, npost_ref[...])
        x1_ref[rows, :] = x1
        h2 = _rms(x1, npre_ref[...]) * (1.0 + sc2_ref[0]) + sh2_ref[0]
        h_hi = h2.astype(BF16)
        h2_ref[rows, :] = h_hi
        h_lo = (h2 - h_hi.astype(F32)).astype(BF16)
        hi_terms = jnp.dot(h_hi, w_hi_lo, preferred_element_type=F32)
        logits = (hi_terms[:, 0:ROUTER_LANES] + hi_terms[:, ROUTER_LANES:2 * ROUTER_LANES]
                  + jnp.dot(h_lo, w_hi, preferred_element_type=F32))
        logits = jnp.where(lane < N_EXPERTS, logits, -jnp.inf)
        e = jnp.exp(logits - jnp.max(logits, axis=-1, keepdims=True))
        aff = e / jnp.sum(e, axis=-1, keepdims=True)
        aff_ref[:, rows] = aff.T[0:N_EXPERTS, :]


def _mix(og, proj_a, proj_p, proj_s, x, mod3, conv_w, wro, wco, wo, wr_pad, npost, npre, seq, cond_row_fn):
    t = x.shape[0]
    conv_b_block = (2 * N_HEADS * DK + 2 * N_HEADS * DV) // D_MODEL
    tm = MIX_TILE
    halo = 16
    hb = tm // halo
    last_halo = t // halo - 1
    col = lambda c: (lambda i: (i, c))
    row_vec = pl.BlockSpec((1, D_MODEL), lambda i: (0, 0))
    wspec = pl.BlockSpec((D_MODEL, D_MODEL), lambda i: (0, 0))
    modspec = lambda c: pl.BlockSpec((1, 1, D_MODEL), lambda i: (cond_row_fn(i * tm), 0, c))
    tile = lambda c: pl.BlockSpec((tm, D_MODEL), col(c))
    prev = lambda c: pl.BlockSpec((halo, D_MODEL), lambda i: (jnp.maximum(i * hb - 1, 0), c))
    nxt = lambda c: pl.BlockSpec((halo, D_MODEL), lambda i: (jnp.minimum((i + 1) * hb, last_halo), c))
    return pl.pallas_call(
        functools.partial(_mix_kernel, tm=tm, sub=MIX_SUB_TILE, seq=seq),
        grid=(t // tm,),
        in_specs=[tile(0), tile(conv_b_block), tile(0), prev(0), nxt(0), tile(0), tile(1),
                  tile(0),
                  pl.BlockSpec((3, D_MODEL), lambda i: (0, 0)), wspec, wspec, wspec,
                  pl.BlockSpec((D_MODEL, ROUTER_LANES), lambda i: (0, 0)),
                  row_vec, row_vec, modspec(2), modspec(3), modspec(4)],
        out_specs=[tile(0), tile(0), pl.BlockSpec((N_EXPERTS, tm), lambda i: (0, i))],
        out_shape=[jax.ShapeDtypeStruct((t, D_MODEL), F32),
                   jax.ShapeDtypeStruct((t, D_MODEL), BF16),
                   jax.ShapeDtypeStruct((N_EXPERTS, t), F32)],
        compiler_params=pltpu.CompilerParams(dimension_semantics=("arbitrary",),
                                             vmem_limit_bytes=VMEM_LIMIT),
        name="mix_out",
    )(og, proj_a, proj_p, proj_p, proj_p, proj_s, proj_s, x,
      conv_w, wro, wco, wo, wr_pad, npost, npre, mod3, mod3, mod3)


def _expert_kernel(xs_ref, wg_ref, wu_ref, wd_ref, o_ref, *, sub):
    wg = wg_ref[0].astype(BF16)
    wu = wu_ref[0].astype(BF16)
    wd = wd_ref[0].astype(BF16)
    for r in range(xs_ref.shape[1] // sub):
        rows = slice(r * sub, (r + 1) * sub)
        x = xs_ref[0, rows, :]
        g = jnp.dot(x, wg, preferred_element_type=F32)
        u = jnp.dot(x, wu, preferred_element_type=F32)
        hid = (g * _sigmoid(g) * u).astype(BF16)
        o_ref[0, rows, :] = jnp.dot(hid, wd, preferred_element_type=F32).astype(BF16)


def _experts(xs, wg, wu, wd):
    e, cap, d = xs.shape
    ff = wg.shape[2]
    tr = min(cap, EXPERT_ROWS_PER_STEP)
    return pl.pallas_call(
        functools.partial(_expert_kernel, sub=EXPERT_SUB_ROWS),
        grid=(e, cap // tr),
        in_specs=[pl.BlockSpec((1, tr, d), lambda a, r: (a, r, 0)),
                  pl.BlockSpec((1, d, ff), lambda a, r: (a, 0, 0)),
                  pl.BlockSpec((1, d, ff), lambda a, r: (a, 0, 0)),
                  pl.BlockSpec((1, ff, d), lambda a, r: (a, 0, 0))],
        out_specs=pl.BlockSpec((1, tr, d), lambda a, r: (a, r, 0)),
        out_shape=jax.ShapeDtypeStruct((e, cap, d), BF16),
        compiler_params=pltpu.CompilerParams(dimension_semantics=("arbitrary", "arbitrary"),
                                             vmem_limit_bytes=EXPERT_VMEM_LIMIT),
        name="experts",
    )(xs, wg, wu, wd)


def _block_ranks(mask):
    ne, nblk, lanes = mask.shape
    li = lax.broadcasted_iota(jnp.int32, (lanes, lanes), 0)
    lj = lax.broadcasted_iota(jnp.int32, (lanes, lanes), 1)
    incl_lanes = jnp.where(li <= lj, 1.0, 0.0).astype(BF16)
    all_lanes = jnp.ones((lanes, lanes), BF16)
    bi = lax.broadcasted_iota(jnp.int32, (nblk, nblk), 0)
    bj = lax.broadcasted_iota(jnp.int32, (nblk, nblk), 1)
    earlier_blocks = jnp.where(bj < bi, 1.0, 0.0).astype(BF16)
    m = mask.reshape(ne * nblk, lanes).astype(BF16)
    incl = jnp.dot(m, incl_lanes, preferred_element_type=F32).reshape(ne, nblk, lanes)
    tot = jnp.dot(m, all_lanes, preferred_element_type=F32).reshape(ne, nblk, lanes).astype(BF16)
    excl = jnp.stack([jnp.dot(earlier_blocks, tot[i], preferred_element_type=F32) for i in range(ne)])
    return excl, incl


def _select_kernel(aff_ref, sel_ref, *, cap):
    ne, nblk, lanes = aff_ref.shape
    aff = aff_ref[...]

    def count(mask):
        m = jnp.where(mask, 1.0, 0.0)
        return jnp.sum(jnp.sum(m, axis=1, keepdims=True), axis=2, keepdims=True)

    def as_float(bits):
        return lax.bitcast_convert_type(bits, F32)

    def bit_step(it, thr_bits):
        cand = thr_bits | jnp.left_shift(jnp.int32(1), 30 - it)
        return jnp.where(count(aff >= as_float(cand)) >= cap, cand, thr_bits)

    thr = as_float(lax.fori_loop(0, 31, bit_step, jnp.zeros((ne, 1, 1), jnp.int32)))
    gt = aff > thr
    eq = aff == thr
    need = cap - count(gt)
    ex_eq, in_eq = _block_ranks(jnp.where(eq, 1.0, 0.0))
    sel_ref[...] = jnp.where(gt | (eq & (ex_eq + in_eq - 1.0 < need)), 1.0, 0.0)


def _slot_kernel(sel_ref, pos_ref, excl_ref):
    sel = sel_ref[...]
    ex_sel, in_sel = _block_ranks(sel)
    pos_ref[...] = jnp.where(sel > 0.0, ex_sel + in_sel - 1.0, -1.0).astype(jnp.int32)
    excl_ref[...] = ex_sel.astype(jnp.int32)


def _to_tile_order(a, nb):
    if nb == 1:
        return a
    ne, t = a.shape
    r = ROUTE_TILE // nb
    return a.reshape(ne, nb, t // (nb * r), r).transpose(0, 2, 1, 3).reshape(ne, t)


def _route(aff_t, cap, nb):
    ne, t = aff_t.shape
    nblk = t // 128
    shape = (ne, nblk, 128)
    full = pl.BlockSpec(shape, lambda i: (0, 0, 0))
    params = pltpu.CompilerParams(dimension_semantics=("arbitrary",), vmem_limit_bytes=VMEM_LIMIT)
    sel = pl.pallas_call(
        functools.partial(_select_kernel, cap=cap),
        grid=(1,), in_specs=[full], out_specs=full,
        out_shape=jax.ShapeDtypeStruct(shape, F32),
        compiler_params=params, name="route_select",
    )(aff_t.reshape(shape))
    sel = _to_tile_order(sel.reshape(ne, t), nb)
    pos, excl = pl.pallas_call(
        _slot_kernel,
        grid=(1,), in_specs=[full], out_specs=[full, full],
        out_shape=[jax.ShapeDtypeStruct(shape, jnp.int32)] * 2,
        compiler_params=params, name="route_slots",
    )(sel.reshape(shape))
    base = jnp.concatenate([excl[:, :, 0], jnp.full((ne, 1), cap, jnp.int32)], axis=1)
    return pos.reshape(ne, t), base, _to_tile_order(aff_t, nb)


ROUTE_TILE = 256
ROUTE_WINDOW = 64
SLOT_ALIGN = 16
DISPATCH_GROUP = 4
DISPATCH_TILES_PER_STEP = 8


def _window(base_ref, e, blk, w, cap):
    lo = base_ref[e, blk] // SLOT_ALIGN + w * (ROUTE_WINDOW // SLOT_ALIGN)
    return lo * SLOT_ALIGN, jnp.minimum(lo, (cap - ROUTE_WINDOW) // SLOT_ALIGN) * SLOT_ALIGN


def _n_windows(base_ref, e, blk, cap):
    lo, _ = _window(base_ref, e, blk, 0, cap)
    end = base_ref[e, blk + ROUTE_TILE // 128]
    return (end - lo + ROUTE_WINDOW - 1) // ROUTE_WINDOW


def _dispatch_kernel(base_ref, h_ref, pos_ref, xs_ref, *, cap):
    g, j = pl.program_id(0), pl.program_id(1)

    @pl.when(j == 0)
    def _():
        xs_ref[...] = jnp.zeros_like(xs_ref)

    ri = lax.broadcasted_iota(jnp.int32, (ROUTE_WINDOW, ROUTE_TILE), 0)

    def onehot(prow, lo, off, first):
        hit = prow - off == ri
        if not first:
            hit = hit & (prow >= lo)
        return jnp.where(hit, 1.0, 0.0).astype(BF16)

    def add_rows(ge, off, rows):
        sl = pl.ds(pl.multiple_of(off, SLOT_ALIGN), ROUTE_WINDOW)
        xs_ref[ge, sl, :] = xs_ref[ge, sl, :] + rows.astype(BF16)

    for sub in range(DISPATCH_TILES_PER_STEP):
        blk = (j * DISPATCH_TILES_PER_STEP + sub) * (ROUTE_TILE // 128)
        tok = slice(sub * ROUTE_TILE, (sub + 1) * ROUTE_TILE)
        h = h_ref[:, sub].reshape(ROUTE_TILE, h_ref.shape[-1])
        prows, offs = [], []
        for ge in range(DISPATCH_GROUP):
            e = g * DISPATCH_GROUP + ge
            prows.append(pos_ref[pl.ds(e, 1), tok])
            offs.append(_window(base_ref, e, blk, 0, cap))
        sel = jnp.concatenate([onehot(prows[ge], *offs[ge], True) for ge in range(DISPATCH_GROUP)], axis=0)
        rows = jnp.dot(sel, h, preferred_element_type=F32)
        for ge in range(DISPATCH_GROUP):
            add_rows(ge, offs[ge][1], rows[ge * ROUTE_WINDOW:(ge + 1) * ROUTE_WINDOW])

        for ge in range(DISPATCH_GROUP):
            e = g * DISPATCH_GROUP + ge

            def extra(w, carry, ge=ge, e=e, blk=blk, h=h, prow=prows[ge]):
                lo, off = _window(base_ref, e, blk, w, cap)
                add_rows(ge, off, jnp.dot(onehot(prow, lo, off, False), h, preferred_element_type=F32))
                return carry

            lax.fori_loop(1, _n_windows(base_ref, e, blk, cap), extra, 0)


def _tile_view(a, nb):
    t, d = a.shape
    r = ROUTE_TILE // nb
    return a.reshape(nb, t // (nb * r), r, d)


def _dispatch(base, h2, pos, cap, nb):
    t, d = h2.shape
    ne = pos.shape[0]
    step = ROUTE_TILE * DISPATCH_TILES_PER_STEP
    return pl.pallas_call(
        functools.partial(_dispatch_kernel, cap=cap),
        grid_spec=pltpu.PrefetchScalarGridSpec(
            num_scalar_prefetch=1,
            grid=(ne // DISPATCH_GROUP, t // step),
            in_specs=[pl.BlockSpec((nb, DISPATCH_TILES_PER_STEP, ROUTE_TILE // nb, d),
                                   lambda g, j, b: (0, j, 0, 0)),
                      pl.BlockSpec((ne, step), lambda g, j, b: (0, j))],
            out_specs=pl.BlockSpec((DISPATCH_GROUP, cap, d), lambda g, j, b: (g, 0, 0))),
        out_shape=jax.ShapeDtypeStruct((ne, cap, d), BF16),
        compiler_params=pltpu.CompilerParams(dimension_semantics=("arbitrary", "arbitrary"),
                                             vmem_limit_bytes=VMEM_LIMIT),
        name="dispatch",
    )(base, _tile_view(h2, nb), pos)


def _combine_kernel(base_ref, pos_ref, aff_ref, x1_ref, g_ref, g2_ref, eo_ref, o_ref,
                    f_ref, win_buf, win_sem, extra_buf, extra_sem, *, cap, n_tiles):
    j = pl.program_id(0)
    blocks_per_tile = ROUTE_TILE // 128
    blk = j * blocks_per_tile
    slot = lax.rem(j, 2)
    ri = lax.broadcasted_iota(jnp.int32, (ROUTE_WINDOW, ROUTE_TILE), 0)
    tn = (((0,), (0,)), ((), ()))

    def window_copy(e, off, dst, sem):
        return pltpu.make_async_copy(eo_ref.at[e, pl.ds(pl.multiple_of(off, SLOT_ALIGN), ROUTE_WINDOW), :],
                                     dst, sem)

    def first_window_copies(tile, buf_slot):
        return [window_copy(e, _window(base_ref, e, tile * blocks_per_tile, 0, cap)[1],
                            win_buf.at[buf_slot, pl.ds(e * ROUTE_WINDOW, ROUTE_WINDOW), :], win_sem.at[buf_slot])
                for e in range(N_EXPERTS)]

    def wait_windows(buf_slot):
        for e in range(N_EXPERTS):
            window_copy(e, 0, win_buf.at[buf_slot, pl.ds(e * ROUTE_WINDOW, ROUTE_WINDOW), :],
                        win_sem.at[buf_slot]).wait()

    @pl.when(j == 0)
    def _():
        for cp in first_window_copies(0, 0):
            cp.start()

    for cp in first_window_copies(jnp.minimum(j + 1, n_tiles - 1), 1 - slot):
        cp.start()

    def weights(e, lo, off, first):
        prow = pos_ref[e:e + 1, :]
        hit = prow - off == ri
        if not first:
            hit = hit & (prow >= lo)
        return jnp.where(hit, aff_ref[e:e + 1, :], 0.0).astype(BF16)

    q = jnp.concatenate([weights(e, *_window(base_ref, e, blk, 0, cap), True) for e in range(N_EXPERTS)], axis=0)
    wait_windows(slot)
    f_ref[...] = lax.dot_general(q, win_buf[slot], tn, preferred_element_type=F32)

    n_win = [_n_windows(base_ref, e, blk, cap) for e in range(N_EXPERTS)]

    @pl.when(functools.reduce(jnp.maximum, n_win) > 1)
    def _():
        for e in range(N_EXPERTS):
            def extra(w, carry, e=e):
                lo, off = _window(base_ref, e, blk, w, cap)
                cp = window_copy(e, off, extra_buf, extra_sem)
                cp.start()
                cp.wait()
                f_ref[...] += lax.dot_general(weights(e, lo, off, False), extra_buf[...], tn,
                                              preferred_element_type=F32)
                return carry

            lax.fori_loop(1, n_win[e], extra, 0)

    nb, _, r, d = x1_ref.shape
    f = f_ref[...].reshape(nb, r, d)
    o_ref[:, 0] = x1_ref[:, 0] + g2_ref[...] * _rms(f, g_ref[...])

    @pl.when(j == n_tiles - 1)
    def _():
        wait_windows(1 - slot)


def _combine(base, eo, pos, aff, x1, mod3, npost, cond_block, cap, nb):
    t, d = x1.shape
    ne = pos.shape[0]
    n_tiles = t // ROUTE_TILE
    tile = pl.BlockSpec((nb, 1, ROUTE_TILE // nb, d), lambda j, b: (0, j, 0, 0))
    etile = pl.BlockSpec((ne, ROUTE_TILE), lambda j, b: (0, j))
    return pl.pallas_call(
        functools.partial(_combine_kernel, cap=cap, n_tiles=n_tiles),
        grid_spec=pltpu.PrefetchScalarGridSpec(
            num_scalar_prefetch=1,
            grid=(n_tiles,),
            in_specs=[etile, etile, tile,
                      pl.BlockSpec((1, d), lambda j, b: (0, 0)),
                      pl.BlockSpec((nb, 1, d), lambda j, b: (cond_block, 0, 5)),
                      pl.BlockSpec(memory_space=pl.ANY)],
            out_specs=tile,
            scratch_shapes=[pltpu.VMEM((ROUTE_TILE, d), F32),
                            pltpu.VMEM((2, ne * ROUTE_WINDOW, d), BF16), pltpu.SemaphoreType.DMA((2,)),
                            pltpu.VMEM((ROUTE_WINDOW, d), BF16), pltpu.SemaphoreType.DMA(())]),
        out_shape=jax.ShapeDtypeStruct(_tile_view(x1, nb).shape, F32),
        compiler_params=pltpu.CompilerParams(dimension_semantics=("arbitrary",),
                                             vmem_limit_bytes=VMEM_LIMIT),
        name="combine",
    )(base, pos, aff, _tile_view(x1, nb), npost, mod3, eo).reshape(t, d)


def _trunk(x3, mod3, first_cond_row, shared_cond, weights, lgs, *, rope_tabs, init, emit_state):
    (n_pre_mix, n_post_mix, n_pre_ffn, n_post_ffn, w_in, gn_g, conv_w, wro, wco, wo, wr_pad, wg, wu, wd) = weights
    b, seq, d = x3.shape
    t = b * seq
    x = x3.reshape(t, d)
    cond_row_fn = (lambda r: first_cond_row) if shared_cond else (lambda r: first_cond_row + r // seq)
    nb = 1 if shared_cond else b
    assert ROUTE_TILE % nb == 0 and (ROUTE_TILE // nb) % SLOT_ALIGN == 0 and first_cond_row % nb == 0
    proj_a, proj_p, proj_s = _inproj(x, mod3, n_pre_mix, w_in, cond_row_fn, seq, rope_tabs)
    ret = _retention(proj_a, lgs, gn_g, seq, init=init, emit_state=emit_state)
    og = ret[0]
    x1, h2, aff_t = _mix(og, proj_a, proj_p, proj_s, x, mod3, conv_w, wro, wco, wo, wr_pad, n_post_mix,
                         n_pre_ffn, seq, cond_row_fn)
    cap = CAPACITY_FACTOR * t // N_EXPERTS
    pos, base, aff = _route(aff_t, cap, nb)
    eo = _experts(_dispatch(base, h2, pos, cap, nb), wg, wu, wd)
    y = _combine(base, eo, pos, aff, x1, mod3, n_post_ffn, first_cond_row // nb, cap, nb)
    return y.reshape(b, seq, d), ret[1:]


def kernel(x_prompt, x_sample, state_ret_fwd, state_ret_bwd, c, c_ctx, w_ada, b_ada, norm_pre_mix, norm_post_mix,
           norm_pre_ffn, norm_post_ffn, w_in, ret_decay_fwd, ret_decay_bwd, ret_norm_g, conv_w, w_ret_o, w_conv_o,
           w_o, w_router, w_gate, w_up, w_down):
    depth = w_ada.shape[0]
    assert depth == 1
    dec_b, dec_seq = x_sample.shape[0], x_sample.shape[1]
    xp, xs = x_prompt, x_sample
    l = 0
    cond = jnp.zeros((N_COND_ROWS, D_MODEL), F32).at[0:dec_b].set(c).at[CTX_COND_ROW].set(c_ctx)
    mod3 = _ada(cond, w_ada[l], b_ada[l][None, :]).reshape(N_COND_ROWS, 1, 6 * D_MODEL)
    lgs = jnp.stack([-jax.nn.softplus(-ret_decay_fwd[l].astype(F32)),
                     -jax.nn.softplus(-ret_decay_bwd[l].astype(F32))])
    wr_pad = jnp.pad(w_router[l], ((0, 0), (0, ROUTER_LANES - N_EXPERTS)))
    weights = (norm_pre_mix[l][None, :], norm_post_mix[l][None, :], norm_pre_ffn[l][None, :],
               norm_post_ffn[l][None, :], w_in[l].astype(BF16), ret_norm_g[l][None, :], conv_w[l],
               w_ret_o[l].astype(BF16), w_conv_o[l].astype(BF16), w_o[l].astype(BF16), wr_pad,
               w_gate[l], w_up[l], w_down[l])
    yp, (s_f, s_b) = _trunk(xp, mod3, CTX_COND_ROW, True, weights, lgs,
                            rope_tabs=None, init=None, emit_state=True)
    ys, _ = _trunk(xs, mod3, 0, False, weights, lgs,
                   rope_tabs=_rope_tables(dec_seq), init=(state_ret_fwd, state_ret_bwd), emit_state=False)
    return (yp, ys, s_f, s_b)
```

```python
import functools

import jax
import jax.numpy as jnp
import numpy as np
from jax import lax
from jax.experimental import pallas as pl
from jax.experimental.pallas import tpu as pltpu

F32 = jnp.float32
BF16 = jnp.bfloat16

D_MODEL = 1024
N_HEADS = 8
DK = 64
DV = 128
CHUNK = 128
GRID_W = 64
N_EXPERTS = 16
CAPACITY_FACTOR = 2
D_IN_TOTAL = 8192
RMS_EPS = 1e-6
GN_EPS = 1e-5
ROPE_BASE = 10000.0
N_COND_ROWS = 16
CTX_COND_ROW = 8
ROUTER_LANES = 128
RET_CHUNKS_PER_STEP = 16
RET_UNROLL = 4
MIX_TILE = 1024
MIX_SUB_TILE = 512
VMEM_LIMIT = 48 * 1024 * 1024
EXPERT_ROWS_PER_STEP = 1024
EXPERT_SUB_ROWS = 512
EXPERT_VMEM_LIMIT = 56 * 1024 * 1024


def _sigmoid(x):
    return 1.0 / (1.0 + jnp.exp(-x))


def _rms(x, g):
    return x * lax.rsqrt(jnp.mean(x * x, axis=-1, keepdims=True) + RMS_EPS) * g


def _ada_kernel(c_ref, w_ref, b_ref, o_ref):
    c = c_ref[...]
    s = c * _sigmoid(c)
    o_ref[...] = jnp.dot(s, w_ref[...], preferred_element_type=F32,
                         precision=lax.Precision.HIGHEST) + b_ref[...]


def _ada(cond, w_ada, b_ada):
    n = w_ada.shape[1]
    tn = 1024
    return pl.pallas_call(
        _ada_kernel,
        grid=(n // tn,),
        in_specs=[pl.BlockSpec((N_COND_ROWS, D_MODEL), lambda j: (0, 0)),
                  pl.BlockSpec((D_MODEL, tn), lambda j: (0, j)),
                  pl.BlockSpec((1, tn), lambda j: (0, j))],
        out_specs=pl.BlockSpec((N_COND_ROWS, tn), lambda j: (0, j)),
        out_shape=jax.ShapeDtypeStruct((N_COND_ROWS, n), F32),
        compiler_params=pltpu.CompilerParams(dimension_semantics=("arbitrary",),
                                             vmem_limit_bytes=VMEM_LIMIT),
        name="ada_mod",
    )(cond, w_ada, b_ada)


def _inproj_body(x_ref, g_ref, sh_ref, sc_ref, w_ref, cos_ref, sin_ref, a_ref, p_ref, s_ref, h_ref, use_rope):
    j = pl.program_id(1)

    @pl.when(j == 0)
    def _():
        h = _rms(x_ref[...], g_ref[...]) * (1.0 + sc_ref[0]) + sh_ref[0]
        h_ref[...] = h.astype(BF16)

    lanes = 2 * DK
    width = 2 * lanes
    n_slices = w_ref.shape[1] // width
    n_qk = 2 * N_HEADS * DK

    def slice_dot(b):
        return jnp.dot(h_ref[...], w_ref[:, b * width:(b + 1) * width], preferred_element_type=F32)

    @pl.when(j == 0)
    def _():
        ci = lax.broadcasted_iota(jnp.int32, (h_ref.shape[0], lanes), 1)
        for b in range(n_slices):
            acc = slice_dot(b)
            if b * width >= n_qk:
                a_ref[:, b * width:(b + 1) * width] = acc.astype(BF16)
                continue
            for half in range(2):
                col = b * width + half * lanes
                x = acc[:, half * lanes:(half + 1) * lanes]
                if col >= N_HEADS * DK:
                    x = x * (DK ** -0.5)
                if use_rope:
                    swapped = jnp.where((ci & 31) < 16, pltpu.roll(x, lanes - 16, 1), pltpu.roll(x, 16, 1))
                    x = x * cos_ref[...] + swapped * sin_ref[...]
                a_ref[:, col:col + lanes] = x.astype(BF16)

    @pl.when(j == 1)
    def _():
        for b in range(n_slices):
            acc = slice_dot(b)
            if b < n_slices // 2:
                acc = acc * _sigmoid(acc)
            a_ref[:, b * width:(b + 1) * width] = acc.astype(BF16)

    @pl.when(j == 2)
    def _():
        for b in range(n_slices // 2):
            p_ref[:, b * width:(b + 1) * width] = (slice_dot(b) * slice_dot(b + n_slices // 2)).astype(BF16)

    @pl.when(j == 3)
    def _():
        for b in range(n_slices):
            s_ref[:, b * width:(b + 1) * width] = _sigmoid(slice_dot(b)).astype(BF16)


def _inproj_kernel(x_ref, g_ref, sh_ref, sc_ref, w_ref, *rest, use_rope):
    if use_rope:
        cos_ref, sin_ref, a_ref, p_ref, s_ref, h_ref = rest
    else:
        cos_ref = sin_ref = None
        a_ref, p_ref, s_ref, h_ref = rest
    _inproj_body(x_ref, g_ref, sh_ref, sc_ref, w_ref, cos_ref, sin_ref, a_ref, p_ref, s_ref, h_ref, use_rope)


def _inproj(x, mod3, g, w_bf16, cond_row_fn, seq, rope_tabs):
    t = x.shape[0]
    tm, tn = 1024, 2048
    assert D_IN_TOTAL == 4 * tn
    in_specs = [pl.BlockSpec((tm, D_MODEL), lambda i, j: (i, 0)),
                pl.BlockSpec((1, D_MODEL), lambda i, j: (0, 0)),
                pl.BlockSpec((1, 1, D_MODEL), lambda i, j: (cond_row_fn(i * tm), 0, 0)),
                pl.BlockSpec((1, 1, D_MODEL), lambda i, j: (cond_row_fn(i * tm), 0, 1)),
                pl.BlockSpec((D_MODEL, tn), lambda i, j: (0, j))]
    args = [x, g, mod3, mod3, w_bf16]
    if rope_tabs is not None:
        tiles_per_seq = seq // tm
        in_specs += [pl.BlockSpec((tm, 2 * DK), lambda i, j: (i % tiles_per_seq, 0))] * 2
        args += list(rope_tabs)
    return pl.pallas_call(
        functools.partial(_inproj_kernel, use_rope=rope_tabs is not None),
        grid=(t // tm, D_IN_TOTAL // tn),
        in_specs=in_specs,
        out_specs=[pl.BlockSpec((tm, tn), lambda i, j: (i, jnp.minimum(j, 1))),
                   pl.BlockSpec((tm, tn // 2), lambda i, j: (i, 0)),
                   pl.BlockSpec((tm, tn), lambda i, j: (i, 0))],
        out_shape=[jax.ShapeDtypeStruct((t, 2 * tn), BF16),
                   jax.ShapeDtypeStruct((t, tn // 2), BF16),
                   jax.ShapeDtypeStruct((t, tn), BF16)],
        scratch_shapes=[pltpu.VMEM((tm, D_MODEL), BF16)],
        compiler_params=pltpu.CompilerParams(dimension_semantics=("arbitrary", "arbitrary"),
                                             vmem_limit_bytes=VMEM_LIMIT),
        name="inproj",
    )(*args)


def _ret_kernel(*refs, nc, bb, has_init, emit_state):
    it = iter(refs)
    lg_ref = next(it)
    q_ref, k_ref, v_ref, gr_ref, gn_ref = (next(it) for _ in range(5))
    s0f_ref = s0b_ref = sf_ref = sb_ref = None
    if has_init:
        s0f_ref, s0b_ref = next(it), next(it)
    o_ref = next(it)
    if emit_state:
        sf_ref, sb_ref = next(it), next(it)
    kvf_s, kvb_s, rcat_s, p_s, o_s, r0 = (next(it) for _ in range(6))

    p = pl.program_id(1)
    lgf_a, lgf_b = lg_ref[0, 2 * p], lg_ref[0, 2 * p + 1]
    lgb_a, lgb_b = lg_ref[1, 2 * p], lg_ref[1, 2 * p + 1]

    ri = lax.broadcasted_iota(jnp.int32, (CHUNK, 2 * DK), 0)
    ci = lax.broadcasted_iota(jnp.int32, (CHUNK, 2 * DK), 1)
    lane_a = ci < DK
    rowf = ri.astype(F32)
    diff = rowf - ci.astype(F32)
    lgf_lane = jnp.where(lane_a, lgf_a, lgf_b)
    lgb_lane = jnp.where(lane_a, lgb_a, lgb_b)
    xi_f = jnp.exp(lgf_lane * (rowf + 1.0))
    xi_b = jnp.exp(lgb_lane * (CHUNK - rowf))
    zeta_f = jnp.exp(lgf_lane * (CHUNK - 1.0 - rowf))
    zeta_b = jnp.exp(lgb_lane * rowf)

    def decay_matrix(lgf, lgb):
        return jnp.where(diff > 0, jnp.exp(lgf * diff),
                         jnp.where(diff < 0, jnp.exp(lgb * (-diff)), 2.0))

    dm_a = decay_matrix(lgf_a, lgb_a)
    dm_b = decay_matrix(lgf_b, lgb_b)

    r2 = lax.broadcasted_iota(jnp.int32, (2 * DK, 2 * DV), 0)
    c2 = lax.broadcasted_iota(jnp.int32, (2 * DK, 2 * DV), 1)
    top = r2 < DK
    blk = (top == (c2 < DV)).astype(F32)
    cd_f = jnp.exp(jnp.where(top, lgf_a, lgf_b) * float(CHUNK)) * blk
    cd_b = jnp.exp(jnp.where(top, lgb_a, lgb_b) * float(CHUNK)) * blk

    def chunk_rows(c):
        return pl.ds(pl.multiple_of(c * CHUNK, CHUNK), CHUNK)

    n_chunks = bb * nc
    unroll = RET_UNROLL

    def kv_body(c, carry):
        rows = chunk_rows(c)
        k = k_ref[rows, :].astype(F32)
        kz_t = jnp.concatenate([k * zeta_f, k * zeta_b], axis=1).T.astype(BF16)
        kv = jnp.dot(kz_t, v_ref[rows, :], preferred_element_type=F32)
        kvf_s[c] = kv[0:2 * DK] * blk
        kvb_s[c] = kv[2 * DK:4 * DK] * blk
        return carry

    lax.fori_loop(0, n_chunks, kv_body, 0, unroll=unroll)

    def score_body(c, carry):
        rows = chunk_rows(c)
        q16 = q_ref[rows, :]
        zero = jnp.zeros_like(q16)
        q_ab = jnp.concatenate([jnp.where(lane_a, q16, zero), jnp.where(lane_a, zero, q16)], axis=0)
        s = lax.dot_general(q_ab, k_ref[rows, :], (((1,), (1,)), ((), ())), preferred_element_type=F32)
        p_s[c, 0:CHUNK, :] = (s[0:CHUNK] * dm_a).astype(BF16)
        p_s[c, CHUNK:2 * CHUNK, :] = (s[CHUNK:2 * CHUNK] * dm_b).astype(BF16)
        return carry

    lax.fori_loop(0, n_chunks, score_body, 0, unroll=unroll)

    def load_state(s_ref, s):
        r0[...] = jnp.zeros_like(r0)
        r0[0:DK, 0:DV] = s_ref[s, 0, 0].astype(F32)
        r0[DK:2 * DK, DV:2 * DV] = s_ref[s, 0, 1].astype(F32)
        return r0[...]

    for s in range(bb):
        def fwd_body(n, rf, s=s):
            c = s * nc + n
            rcat_s[c, 0:2 * DK, :] = rf.astype(BF16)
            return cd_f * rf + kvf_s[c]

        def bwd_body(t, rb, s=s):
            c = s * nc + nc - 1 - t
            rcat_s[c, 2 * DK:4 * DK, :] = rb.astype(BF16)
            return cd_b * rb + kvb_s[c]

        zeros = jnp.zeros((2 * DK, 2 * DV), F32)
        rf_fin = lax.fori_loop(0, nc, fwd_body, load_state(s0f_ref, s) if has_init else zeros)
        rb_fin = lax.fori_loop(0, nc, bwd_body, load_state(s0b_ref, s) if has_init else zeros)
        if emit_state:
            for s_ref, r in ((sf_ref, rf_fin), (sb_ref, rb_fin)):
                s_ref[s, 0, 0] = r[0:DK, 0:DV]
                s_ref[s, 0, 1] = r[DK:2 * DK, DV:2 * DV]

    gn = gn_ref[...]

    lane_mean = jnp.full((DV, DV), 1.0 / DV, BF16)

    def value_body(c, carry):
        rows = chunk_rows(c)
        q = q_ref[rows, :].astype(F32)
        v = v_ref[rows, :]
        qx = jnp.concatenate([(q * xi_f).astype(BF16), (q * xi_b).astype(BF16)], axis=1)
        oc = jnp.dot(qx, rcat_s[c], preferred_element_type=F32)
        o_a = jnp.dot(p_s[c, 0:CHUNK, :], v[:, 0:DV], preferred_element_type=F32)
        o_b = jnp.dot(p_s[c, CHUNK:2 * CHUNK, :], v[:, DV:2 * DV], preferred_element_type=F32)
        o_s[c, :, 0:DV] = o_a + oc[:, 0:DV]
        o_s[c, :, DV:2 * DV] = o_b + oc[:, DV:2 * DV]
        return carry

    lax.fori_loop(0, n_chunks, value_body, 0, unroll=unroll)

    def lane_means(x):
        return jnp.concatenate([jnp.dot(x[:, h * DV:(h + 1) * DV].astype(BF16), lane_mean,
                                        preferred_element_type=F32) for h in range(2)], axis=1)

    def gate_body(c, carry):
        rows = chunk_rows(c)
        o = o_s[c]
        d = o - lane_means(o)
        y = d * lax.rsqrt(lane_means(d * d) + GN_EPS)
        o_ref[rows, :] = (gr_ref[rows, :].astype(F32) * (y * gn)).astype(BF16)
        return carry

    lax.fori_loop(0, n_chunks, gate_body, 0, unroll=unroll)


def _retention(proj, lgs, gn_g, seq, *, init=None, emit_state):
    t = proj.shape[0]
    b = t // seq
    nc = seq // CHUNK
    bb = max(1, RET_CHUNKS_PER_STEP // nc)
    rows = bb * seq
    has_init = init is not None
    pairs = N_HEADS // 2
    qk_blocks = (N_HEADS * DK) // (2 * DK)
    in_specs = [pl.BlockSpec(memory_space=pltpu.SMEM),
                pl.BlockSpec((rows, 2 * DK), lambda i, p: (i, p)),
                pl.BlockSpec((rows, 2 * DK), lambda i, p: (i, qk_blocks + p)),
                pl.BlockSpec((rows, 2 * DV), lambda i, p: (i, qk_blocks + p)),
                pl.BlockSpec((rows, 2 * DV), lambda i, p: (i, 2 * qk_blocks + p)),
                pl.BlockSpec((1, 2 * DV), lambda i, p: (0, p))]
    args = [lgs, proj, proj, proj, proj, gn_g]
    state_spec = pl.BlockSpec((bb, 1, 2, DK, DV), lambda i, p: (i, 0, p, 0, 0))
    if has_init:
        in_specs += [state_spec, state_spec]
        args += list(init)
    out_specs = [pl.BlockSpec((rows, 2 * DV), lambda i, p: (i, p))]
    out_shape = [jax.ShapeDtypeStruct((t, N_HEADS * DV), BF16)]
    if emit_state:
        out_specs += [state_spec, state_spec]
        out_shape += [jax.ShapeDtypeStruct((b, 1, N_HEADS, DK, DV), F32)] * 2
    return pl.pallas_call(
        functools.partial(_ret_kernel, nc=nc, bb=bb, has_init=has_init, emit_state=emit_state),
        grid=(b // bb, pairs),
        in_specs=in_specs,
        out_specs=out_specs,
        out_shape=out_shape,
        scratch_shapes=[pltpu.VMEM((bb * nc, 2 * DK, 2 * DV), F32),
                        pltpu.VMEM((bb * nc, 2 * DK, 2 * DV), F32),
                        pltpu.VMEM((bb * nc, 4 * DK, 2 * DV), BF16),
                        pltpu.VMEM((bb * nc, 2 * CHUNK, CHUNK), BF16),
                        pltpu.VMEM((bb * nc, CHUNK, 2 * DV), F32),
                        pltpu.VMEM((2 * DK, 2 * DV), F32)],
        compiler_params=pltpu.CompilerParams(dimension_semantics=("arbitrary", "arbitrary"),
                                             vmem_limit_bytes=VMEM_LIMIT),
        name="retention",
    )(*args)


def _rope_tables(seq):
    n_freq = DK // 4
    pos = np.arange(seq)
    inv = jnp.asarray(ROPE_BASE, F32) ** (-jnp.arange(n_freq, dtype=F32) / n_freq)
    ang_r = jnp.asarray(pos // GRID_W, F32)[:, None] * inv
    ang_c = jnp.asarray(pos % GRID_W, F32)[:, None] * inv
    cos = jnp.concatenate([jnp.cos(ang_r)] * 2 + [jnp.cos(ang_c)] * 2, axis=1)
    sin = jnp.concatenate([-jnp.sin(ang_r), jnp.sin(ang_r), -jnp.sin(ang_c), jnp.sin(ang_c)], axis=1)
    return jnp.tile(cos, (1, 2)), jnp.tile(sin, (1, 2))


def _mix_kernel(og_ref, cb_ref, p_ref, pp_ref, pn_ref, sa_ref, sb_ref, x_ref,
                cw_ref, wro_ref, wco_ref, wo_ref, wr_ref, npost_ref, npre_ref, g1_ref, sh2_ref, sc2_ref,
                x1_ref, h2_ref, aff_ref, *, tm, sub, seq):
    i = pl.program_id(0)
    halo = pp_ref.shape[0]
    wr = wr_ref[...]
    w_hi = wr.astype(BF16)
    w_lo = (wr - w_hi.astype(F32)).astype(BF16)
    w_hi_lo = jnp.concatenate([w_hi, w_lo], axis=1)
    row = lax.broadcasted_iota(jnp.int32, (sub, 1), 0)
    lane = lax.broadcasted_iota(jnp.int32, (sub, ROUTER_LANES), 1)

    for s in range(tm // sub):
        r0 = s * sub
        rows = slice(r0, r0 + sub)
        token = i * tm + r0 + row
        in_seq = token & (seq - 1) if seq & (seq - 1) == 0 else token % seq
        prev_row = (pp_ref[halo - 1:halo, :] if s == 0 else p_ref[r0 - 1:r0, :]).astype(F32)
        next_row = (pn_ref[0:1, :] if r0 + sub == tm else p_ref[r0 + sub:r0 + sub + 1, :]).astype(F32)
        prod = p_ref[rows, :].astype(F32)
        up = jnp.where(in_seq == 0, 0.0, jnp.where(row == 0, prev_row, pltpu.roll(prod, 1, 0)))
        dn = jnp.where(in_seq == seq - 1, 0.0, jnp.where(row == sub - 1, next_row, pltpu.roll(prod, sub - 1, 0)))
        u = up * cw_ref[0:1, :] + prod * cw_ref[1:2, :] + dn * cw_ref[2:3, :]
        y_conv = jnp.dot((cb_ref[rows, :].astype(F32) * u).astype(BF16), wco_ref[...],
                         preferred_element_type=F32)
        y_ret = jnp.dot(og_ref[rows, :], wro_ref[...], preferred_element_type=F32)
        merged = sa_ref[rows, :].astype(F32) * y_ret + sb_ref[rows, :].astype(F32) * y_conv
        m = jnp.dot(merged.astype(BF16), wo_ref[...], preferred_element_type=F32)
        x1 = x_ref[rows, :] + g1_ref[0] * _rms(m, npost_ref[...])
        x1_ref[rows, :] = x1
        h2 = _rms(x1, npre_ref[...]) * (1.0 + sc2_ref[0]) + sh2_ref[0]
        h_hi = h2.astype(BF16)
        h2_ref[rows, :] = h_hi
        h_lo = (h2 - h_hi.astype(F32)).astype(BF16)
        hi_terms = jnp.dot(h_hi, w_hi_lo, preferred_element_type=F32)
        logits = (hi_terms[:, 0:ROUTER_LANES] + hi_terms[:, ROUTER_LANES:2 * ROUTER_LANES]
                  + jnp.dot(h_lo, w_hi, preferred_element_type=F32))
        logits = jnp.where(lane < N_EXPERTS, logits, -jnp.inf)
        e = jnp.exp(logits - jnp.max(logits, axis=-1, keepdims=True))
        aff = e / jnp.sum(e, axis=-1, keepdims=True)
        aff_ref[:, rows] = aff.T[0:N_EXPERTS, :]


def _mix(og, proj_a, proj_p, proj_s, x, mod3, conv_w, wro, wco, wo, wr_pad, npost, npre, seq, cond_row_fn):
    t = x.shape[0]
    conv_b_block = (2 * N_HEADS * DK + 2 * N_HEADS * DV) // D_MODEL
    tm = MIX_TILE
    halo = 16
    hb = tm // halo
    last_halo = t // halo - 1
    col = lambda c: (lambda i: (i, c))
    row_vec = pl.BlockSpec((1, D_MODEL), lambda i: (0, 0))
    wspec = pl.BlockSpec((D_MODEL, D_MODEL), lambda i: (0, 0))
    modspec = lambda c: pl.BlockSpec((1, 1, D_MODEL), lambda i: (cond_row_fn(i * tm), 0, c))
    tile = lambda c: pl.BlockSpec((tm, D_MODEL), col(c))
    prev = lambda c: pl.BlockSpec((halo, D_MODEL), lambda i: (jnp.maximum(i * hb - 1, 0), c))
    nxt = lambda c: pl.BlockSpec((halo, D_MODEL), lambda i: (jnp.minimum((i + 1) * hb, last_halo), c))
    return pl.pallas_call(
        functools.partial(_mix_kernel, tm=tm, sub=MIX_SUB_TILE, seq=seq),
        grid=(t // tm,),
        in_specs=[tile(0), tile(conv_b_block), tile(0), prev(0), nxt(0), tile(0), tile(1),
                  tile(0),
                  pl.BlockSpec((3, D_MODEL), lambda i: (0, 0)), wspec, wspec, wspec,
                  pl.BlockSpec((D_MODEL, ROUTER_LANES), lambda i: (0, 0)),
                  row_vec, row_vec, modspec(2), modspec(3), modspec(4)],
        out_specs=[tile(0), tile(0), pl.BlockSpec((N_EXPERTS, tm), lambda i: (0, i))],
        out_shape=[jax.ShapeDtypeStruct((t, D_MODEL), F32),
                   jax.ShapeDtypeStruct((t, D_MODEL), BF16),
                   jax.ShapeDtypeStruct((N_EXPERTS, t), F32)],
        compiler_params=pltpu.CompilerParams(dimension_semantics=("arbitrary",),
                                             vmem_limit_bytes=VMEM_LIMIT),
        name="mix_out",
    )(og, proj_a, proj_p, proj_p, proj_p, proj_s, proj_s, x,
      conv_w, wro, wco, wo, wr_pad, npost, npre, mod3, mod3, mod3)


def _expert_kernel(xs_ref, wg_ref, wu_ref, wd_ref, o_ref, *, sub):
    wg = wg_ref[0].astype(BF16)
    wu = wu_ref[0].astype(BF16)
    wd = wd_ref[0].astype(BF16)
    for r in range(xs_ref.shape[1] // sub):
        rows = slice(r * sub, (r + 1) * sub)
        x = xs_ref[0, rows, :]
        g = jnp.dot(x, wg, preferred_element_type=F32)
        u = jnp.dot(x, wu, preferred_element_type=F32)
        hid = (g * _sigmoid(g) * u).astype(BF16)
        o_ref[0, rows, :] = jnp.dot(hid, wd, preferred_element_type=F32).astype(BF16)


def _experts(xs, wg, wu, wd):
    e, cap, d = xs.shape
    ff = wg.shape[2]
    tr = min(cap, EXPERT_ROWS_PER_STEP)
    return pl.pallas_call(
        functools.partial(_expert_kernel, sub=EXPERT_SUB_ROWS),
        grid=(e, cap // tr),
        in_specs=[pl.BlockSpec((1, tr, d), lambda a, r: (a, r, 0)),
                  pl.BlockSpec((1, d, ff), lambda a, r: (a, 0, 0)),
                  pl.BlockSpec((1, d, ff), lambda a, r: (a, 0, 0)),
                  pl.BlockSpec((1, ff, d), lambda a, r: (a, 0, 0))],
        out_specs=pl.BlockSpec((1, tr, d), lambda a, r: (a, r, 0)),
        out_shape=jax.ShapeDtypeStruct((e, cap, d), BF16),
        compiler_params=pltpu.CompilerParams(dimension_semantics=("arbitrary", "arbitrary"),
                                             vmem_limit_bytes=EXPERT_VMEM_LIMIT),
        name="experts",
    )(xs, wg, wu, wd)


def _block_ranks(mask):
    ne, nblk, lanes = mask.shape
    li = lax.broadcasted_iota(jnp.int32, (lanes, lanes), 0)
    lj = lax.broadcasted_iota(jnp.int32, (lanes, lanes), 1)
    incl_lanes = jnp.where(li <= lj, 1.0, 0.0).astype(BF16)
    all_lanes = jnp.ones((lanes, lanes), BF16)
    bi = lax.broadcasted_iota(jnp.int32, (nblk, nblk), 0)
    bj = lax.broadcasted_iota(jnp.int32, (nblk, nblk), 1)
    earlier_blocks = jnp.where(bj < bi, 1.0, 0.0).astype(BF16)
    m = mask.reshape(ne * nblk, lanes).astype(BF16)
    incl = jnp.dot(m, incl_lanes, preferred_element_type=F32).reshape(ne, nblk, lanes)
    tot = jnp.dot(m, all_lanes, preferred_element_type=F32).reshape(ne, nblk, lanes).astype(BF16)
    excl = jnp.stack([jnp.dot(earlier_blocks, tot[i], preferred_element_type=F32) for i in range(ne)])
    return excl, incl


def _select_kernel(aff_ref, sel_ref, *, cap):
    ne, nblk, lanes = aff_ref.shape
    aff = aff_ref[...]

    def count(mask):
        m = jnp.where(mask, 1.0, 0.0)
        return jnp.sum(jnp.sum(m, axis=1, keepdims=True), axis=2, keepdims=True)

    def as_float(bits):
        return lax.bitcast_convert_type(bits, F32)

    def bit_step(it, thr_bits):
        cand = thr_bits | jnp.left_shift(jnp.int32(1), 30 - it)
        return jnp.where(count(aff >= as_float(cand)) >= cap, cand, thr_bits)

    thr = as_float(lax.fori_loop(0, 31, bit_step, jnp.zeros((ne, 1, 1), jnp.int32)))
    gt = aff > thr
    eq = aff == thr
    need = cap - count(gt)
    ex_eq, in_eq = _block_ranks(jnp.where(eq, 1.0, 0.0))
    sel_ref[...] = jnp.where(gt | (eq & (ex_eq + in_eq - 1.0 < need)), 1.0, 0.0)


def _slot_kernel(sel_ref, pos_ref, excl_ref):
    sel = sel_ref[...]
    ex_sel, in_sel = _block_ranks(sel)
    pos_ref[...] = jnp.where(sel > 0.0, ex_sel + in_sel - 1.0, -1.0).astype(jnp.int32)
    excl_ref[...] = ex_sel.astype(jnp.int32)


def _to_tile_order(a, nb):
    if nb == 1:
        return a
    ne, t = a.shape
    r = ROUTE_TILE // nb
    return a.reshape(ne, nb, t // (nb * r), r).transpose(0, 2, 1, 3).reshape(ne, t)


def _route(aff_t, cap, nb):
    ne, t = aff_t.shape
    nblk = t // 128
    shape = (ne, nblk, 128)
    full = pl.BlockSpec(shape, lambda i: (0, 0, 0))
    params = pltpu.CompilerParams(dimension_semantics=("arbitrary",), vmem_limit_bytes=VMEM_LIMIT)
    sel = pl.pallas_call(
        functools.partial(_select_kernel, cap=cap),
        grid=(1,), in_specs=[full], out_specs=full,
        out_shape=jax.ShapeDtypeStruct(shape, F32),
        compiler_params=params, name="route_select",
    )(aff_t.reshape(shape))
    sel = _to_tile_order(sel.reshape(ne, t), nb)
    pos, excl = pl.pallas_call(
        _slot_kernel,
        grid=(1,), in_specs=[full], out_specs=[full, full],
        out_shape=[jax.ShapeDtypeStruct(shape, jnp.int32)] * 2,
        compiler_params=params, name="route_slots",
    )(sel.reshape(shape))
    base = jnp.concatenate([excl[:, :, 0], jnp.full((ne, 1), cap, jnp.int32)], axis=1)
    return pos.reshape(ne, t), base, _to_tile_order(aff_t, nb)


ROUTE_TILE = 256
ROUTE_WINDOW = 64
SLOT_ALIGN = 16
DISPATCH_GROUP = 4
DISPATCH_TILES_PER_STEP = 8


def _window(base_ref, e, blk, w, cap):
    lo = base_ref[e, blk] // SLOT_ALIGN + w * (ROUTE_WINDOW // SLOT_ALIGN)
    return lo * SLOT_ALIGN, jnp.minimum(lo, (cap - ROUTE_WINDOW) // SLOT_ALIGN) * SLOT_ALIGN


def _n_windows(base_ref, e, blk, cap):
    lo, _ = _window(base_ref, e, blk, 0, cap)
    end = base_ref[e, blk + ROUTE_TILE // 128]
    return (end - lo + ROUTE_WINDOW - 1) // ROUTE_WINDOW


def _dispatch_kernel(base_ref, h_ref, pos_ref, xs_ref, *, cap):
    g, j = pl.program_id(0), pl.program_id(1)

    @pl.when(j == 0)
    def _():
        xs_ref[...] = jnp.zeros_like(xs_ref)

    ri = lax.broadcasted_iota(jnp.int32, (ROUTE_WINDOW, ROUTE_TILE), 0)

    def onehot(prow, lo, off, first):
        hit = prow - off == ri
        if not first:
            hit = hit & (prow >= lo)
        return jnp.where(hit, 1.0, 0.0).astype(BF16)

    def add_rows(ge, off, rows):
        sl = pl.ds(pl.multiple_of(off, SLOT_ALIGN), ROUTE_WINDOW)
        xs_ref[ge, sl, :] = xs_ref[ge, sl, :] + rows.astype(BF16)

    for sub in range(DISPATCH_TILES_PER_STEP):
        blk = (j * DISPATCH_TILES_PER_STEP + sub) * (ROUTE_TILE // 128)
        tok = slice(sub * ROUTE_TILE, (sub + 1) * ROUTE_TILE)
        h = h_ref[:, sub].reshape(ROUTE_TILE, h_ref.shape[-1])
        prows, offs = [], []
        for ge in range(DISPATCH_GROUP):
            e = g * DISPATCH_GROUP + ge
            prows.append(pos_ref[pl.ds(e, 1), tok])
            offs.append(_window(base_ref, e, blk, 0, cap))
        sel = jnp.concatenate([onehot(prows[ge], *offs[ge], True) for ge in range(DISPATCH_GROUP)], axis=0)
        rows = jnp.dot(sel, h, preferred_element_type=F32)
        for ge in range(DISPATCH_GROUP):
            add_rows(ge, offs[ge][1], rows[ge * ROUTE_WINDOW:(ge + 1) * ROUTE_WINDOW])

        for ge in range(DISPATCH_GROUP):
            e = g * DISPATCH_GROUP + ge

            def extra(w, carry, ge=ge, e=e, blk=blk, h=h, prow=prows[ge]):
                lo, off = _window(base_ref, e, blk, w, cap)
                add_rows(ge, off, jnp.dot(onehot(prow, lo, off, False), h, preferred_element_type=F32))
                return carry

            lax.fori_loop(1, _n_windows(base_ref, e, blk, cap), extra, 0)


def _tile_view(a, nb):
    t, d = a.shape
    r = ROUTE_TILE // nb
    return a.reshape(nb, t // (nb * r), r, d)


def _dispatch(base, h2, pos, cap, nb):
    t, d = h2.shape
    ne = pos.shape[0]
    step = ROUTE_TILE * DISPATCH_TILES_PER_STEP
    return pl.pallas_call(
        functools.partial(_dispatch_kernel, cap=cap),
        grid_spec=pltpu.PrefetchScalarGridSpec(
            num_scalar_prefetch=1,
            grid=(ne // DISPATCH_GROUP, t // step),
            in_specs=[pl.BlockSpec((nb, DISPATCH_TILES_PER_STEP, ROUTE_TILE // nb, d),
                                   lambda g, j, b: (0, j, 0, 0)),
                      pl.BlockSpec((ne, step), lambda g, j, b: (0, j))],
            out_specs=pl.BlockSpec((DISPATCH_GROUP, cap, d), lambda g, j, b: (g, 0, 0))),
        out_shape=jax.ShapeDtypeStruct((ne, cap, d), BF16),
        compiler_params=pltpu.CompilerParams(dimension_semantics=("arbitrary", "arbitrary"),
                                             vmem_limit_bytes=VMEM_LIMIT),
        name="dispatch",
    )(base, _tile_view(h2, nb), pos)


def _combine_kernel(base_ref, pos_ref, aff_ref, x1_ref, g_ref, g2_ref, eo_ref, o_ref,
                    f_ref, win_buf, win_sem, extra_buf, extra_sem, *, cap, n_tiles):
    j = pl.program_id(0)
    blocks_per_tile = ROUTE_TILE // 128
    blk = j * blocks_per_tile
    slot = lax.rem(j, 2)
    ri = lax.broadcasted_iota(jnp.int32, (ROUTE_WINDOW, ROUTE_TILE), 0)
    tn = (((0,), (0,)), ((), ()))

    def window_copy(e, off, dst, sem):
        return pltpu.make_async_copy(eo_ref.at[e, pl.ds(pl.multiple_of(off, SLOT_ALIGN), ROUTE_WINDOW), :],
                                     dst, sem)

    def first_window_copies(tile, buf_slot):
        return [window_copy(e, _window(base_ref, e, tile * blocks_per_tile, 0, cap)[1],
                            win_buf.at[buf_slot, pl.ds(e * ROUTE_WINDOW, ROUTE_WINDOW), :], win_sem.at[buf_slot])
                for e in range(N_EXPERTS)]

    def wait_windows(buf_slot):
        for e in range(N_EXPERTS):
            window_copy(e, 0, win_buf.at[buf_slot, pl.ds(e * ROUTE_WINDOW, ROUTE_WINDOW), :],
                        win_sem.at[buf_slot]).wait()

    @pl.when(j == 0)
    def _():
        for cp in first_window_copies(0, 0):
            cp.start()

    for cp in first_window_copies(jnp.minimum(j + 1, n_tiles - 1), 1 - slot):
        cp.start()

    def weights(e, lo, off, first):
        prow = pos_ref[e:e + 1, :]
        hit = prow - off == ri
        if not first:
            hit = hit & (prow >= lo)
        return jnp.where(hit, aff_ref[e:e + 1, :], 0.0).astype(BF16)

    q = jnp.concatenate([weights(e, *_window(base_ref, e, blk, 0, cap), True) for e in range(N_EXPERTS)], axis=0)
    wait_windows(slot)
    f_ref[...] = lax.dot_general(q, win_buf[slot], tn, preferred_element_type=F32)

    n_win = [_n_windows(base_ref, e, blk, cap) for e in range(N_EXPERTS)]

    @pl.when(functools.reduce(jnp.maximum, n_win) > 1)
    def _():
        for e in range(N_EXPERTS):
            def extra(w, carry, e=e):
                lo, off = _window(base_ref, e, blk, w, cap)
                cp = window_copy(e, off, extra_buf, extra_sem)
                cp.start()
                cp.wait()
                f_ref[...] += lax.dot_general(weights(e, lo, off, False), extra_buf[...], tn,
                                              preferred_element_type=F32)
                return carry

            lax.fori_loop(1, n_win[e], extra, 0)

    nb, _, r, d = x1_ref.shape
    f = f_ref[...].reshape(nb, r, d)
    o_ref[:, 0] = x1_ref[:, 0] + g2_ref[...] * _rms(f, g_ref[...])

    @pl.when(j == n_tiles - 1)
    def _():
        wait_windows(1 - slot)


def _combine(base, eo, pos, aff, x1, mod3, npost, cond_block, cap, nb):
    t, d = x1.shape
    ne = pos.shape[0]
    n_tiles = t // ROUTE_TILE
    tile = pl.BlockSpec((nb, 1, ROUTE_TILE // nb, d), lambda j, b: (0, j, 0, 0))
    etile = pl.BlockSpec((ne, ROUTE_TILE), lambda j, b: (0, j))
    return pl.pallas_call(
        functools.partial(_combine_kernel, cap=cap, n_tiles=n_tiles),
        grid_spec=pltpu.PrefetchScalarGridSpec(
            num_scalar_prefetch=1,
            grid=(n_tiles,),
            in_specs=[etile, etile, tile,
                      pl.BlockSpec((1, d), lambda j, b: (0, 0)),
                      pl.BlockSpec((nb, 1, d), lambda j, b: (cond_block, 0, 5)),
                      pl.BlockSpec(memory_space=pl.ANY)],
            out_specs=tile,
            scratch_shapes=[pltpu.VMEM((ROUTE_TILE, d), F32),
                            pltpu.VMEM((2, ne * ROUTE_WINDOW, d), BF16), pltpu.SemaphoreType.DMA((2,)),
                            pltpu.VMEM((ROUTE_WINDOW, d), BF16), pltpu.SemaphoreType.DMA(())]),
        out_shape=jax.ShapeDtypeStruct(_tile_view(x1, nb).shape, F32),
        compiler_params=pltpu.CompilerParams(dimension_semantics=("arbitrary",),
                                             vmem_limit_bytes=VMEM_LIMIT),
        name="combine",
    )(base, pos, aff, _tile_view(x1, nb), npost, mod3, eo).reshape(t, d)


def _trunk(x3, mod3, first_cond_row, shared_cond, weights, lgs, *, rope_tabs, init, emit_state):
    (n_pre_mix, n_post_mix, n_pre_ffn, n_post_ffn, w_in, gn_g, conv_w, wro, wco, wo, wr_pad, wg, wu, wd) = weights
    b, seq, d = x3.shape
    t = b * seq
    x = x3.reshape(t, d)
    cond_row_fn = (lambda r: first_cond_row) if shared_cond else (lambda r: first_cond_row + r // seq)
    nb = 1 if shared_cond else b
    assert ROUTE_TILE % nb == 0 and (ROUTE_TILE // nb) % SLOT_ALIGN == 0 and first_cond_row % nb == 0
    proj_a, proj_p, proj_s = _inproj(x, mod3, n_pre_mix, w_in, cond_row_fn, seq, rope_tabs)
    ret = _retention(proj_a, lgs, gn_g, seq, init=init, emit_state=emit_state)
    og = ret[0]
    x1, h2, aff_t = _mix(og, proj_a, proj_p, proj_s, x, mod3, conv_w, wro, wco, wo, wr_pad, n_post_mix,
                         n_pre_ffn, seq, cond_row_fn)
    cap = CAPACITY_FACTOR * t // N_EXPERTS
    pos, base, aff = _route(aff_t, cap, nb)
    eo = _experts(_dispatch(base, h2, pos, cap, nb), wg, wu, wd)
    y = _combine(base, eo, pos, aff, x1, mod3, n_post_ffn, first_cond_row // nb, cap, nb)
    return y.reshape(b, seq, d), ret[1:]


def kernel(x_prompt, x_sample, state_ret_fwd, state_ret_bwd, c, c_ctx, w_ada, b_ada, norm_pre_mix, norm_post_mix,
           norm_pre_ffn, norm_post_ffn, w_in, ret_decay_fwd, ret_decay_bwd, ret_norm_g, conv_w, w_ret_o, w_conv_o,
           w_o, w_router, w_gate, w_up, w_down):
    depth = w_ada.shape[0]
    assert depth == 1
    dec_b, dec_seq = x_sample.shape[0], x_sample.shape[1]
    xp, xs = x_prompt, x_sample
    l = 0
    cond = jnp.zeros((N_COND_ROWS, D_MODEL), F32).at[0:dec_b].set(c).at[CTX_COND_ROW].set(c_ctx)
    mod3 = _ada(cond, w_ada[l], b_ada[l][None, :]).reshape(N_COND_ROWS, 1, 6 * D_MODEL)
    lgs = jnp.stack([-jax.nn.softplus(-ret_decay_fwd[l].astype(F32)),
                     -jax.nn.softplus(-ret_decay_bwd[l].astype(F32))])
    wr_pad = jnp.pad(w_router[l], ((0, 0), (0, ROUTER_LANES - N_EXPERTS)))
    weights = (norm_pre_mix[l][None, :], norm_post_mix[l][None, :], norm_pre_ffn[l][None, :],
               norm_post_ffn[l][None, :], w_in[l].astype(BF16), ret_norm_g[l][None, :], conv_w[l],
               w_ret_o[l].astype(BF16), w_conv_o[l].astype(BF16), w_o[l].astype(BF16), wr_pad,
               w_gate[l], w_up[l], w_down[l])
    yp, (s_f, s_b) = _trunk(xp, mod3, CTX_COND_ROW, True, weights, lgs,
                            rope_tabs=None, init=None, emit_state=True)
    ys, _ = _trunk(xs, mod3, 0, False, weights, lgs,
                   rope_tabs=_rope_tables(dec_seq), init=(state_ret_fwd, state_ret_bwd), emit_state=False)
    return (yp, ys, s_f, s_b)
```

```python
import functools

import jax
import jax.numpy as jnp
import numpy as np
from jax import lax
from jax.experimental import pallas as pl
from jax.experimental.pallas import tpu as pltpu

F32 = jnp.float32
BF16 = jnp.bfloat16

D_MODEL = 1024
N_HEADS = 8
DK = 64
DV = 128
CHUNK = 128
GRID_W = 64
N_EXPERTS = 16
CAPACITY_FACTOR = 2
D_IN_TOTAL = 8192
RMS_EPS = 1e-6
GN_EPS = 1e-5
ROPE_BASE = 10000.0
N_COND_ROWS = 16
CTX_COND_ROW = 8
ROUTER_LANES = 128
RET_CHUNKS_PER_STEP = 16
RET_UNROLL = 4
MIX_TILE = 1024
MIX_SUB_TILE = 512
VMEM_LIMIT = 48 * 1024 * 1024
EXPERT_ROWS_PER_STEP = 1024
EXPERT_SUB_ROWS = 512
EXPERT_VMEM_LIMIT = 56 * 1024 * 1024


def _sigmoid(x):
    return 1.0 / (1.0 + jnp.exp(-x))


def _rms(x, g):
    return x * lax.rsqrt(jnp.mean(x * x, axis=-1, keepdims=True) + RMS_EPS) * g


def _ada_kernel(c_ref, w_ref, b_ref, o_ref):
    c = c_ref[...]
    s = c * _sigmoid(c)
    o_ref[...] = jnp.dot(s, w_ref[...], preferred_element_type=F32,
                         precision=lax.Precision.HIGHEST) + b_ref[...]


def _ada(cond, w_ada, b_ada):
    n = w_ada.shape[1]
    tn = 1024
    return pl.pallas_call(
        _ada_kernel,
        grid=(n // tn,),
        in_specs=[pl.BlockSpec((N_COND_ROWS, D_MODEL), lambda j: (0, 0)),
                  pl.BlockSpec((D_MODEL, tn), lambda j: (0, j)),
                  pl.BlockSpec((1, tn), lambda j: (0, j))],
        out_specs=pl.BlockSpec((N_COND_ROWS, tn), lambda j: (0, j)),
        out_shape=jax.ShapeDtypeStruct((N_COND_ROWS, n), F32),
        compiler_params=pltpu.CompilerParams(dimension_semantics=("arbitrary",),
                                             vmem_limit_bytes=VMEM_LIMIT),
        name="ada_mod",
    )(cond, w_ada, b_ada)


def _inproj_body(x_ref, g_ref, sh_ref, sc_ref, w_ref, cos_ref, sin_ref, a_ref, p_ref, s_ref, h_ref, use_rope):
    j = pl.program_id(1)

    @pl.when(j == 0)
    def _():
        h = _rms(x_ref[...], g_ref[...]) * (1.0 + sc_ref[0]) + sh_ref[0]
        h_ref[...] = h.astype(BF16)

    lanes = 2 * DK
    width = 2 * lanes
    n_slices = w_ref.shape[1] // width
    n_qk = 2 * N_HEADS * DK

    def slice_dot(b):
        return jnp.dot(h_ref[...], w_ref[:, b * width:(b + 1) * width], preferred_element_type=F32)

    @pl.when(j == 0)
    def _():
        ci = lax.broadcasted_iota(jnp.int32, (h_ref.shape[0], lanes), 1)
        for b in range(n_slices):
            acc = slice_dot(b)
            if b * width >= n_qk:
                a_ref[:, b * width:(b + 1) * width] = acc.astype(BF16)
                continue
            for half in range(2):
                col = b * width + half * lanes
                x = acc[:, half * lanes:(half + 1) * lanes]
                if col >= N_HEADS * DK:
                    x = x * (DK ** -0.5)
                if use_rope:
                    swapped = jnp.where((ci & 31) < 16, pltpu.roll(x, lanes - 16, 1), pltpu.roll(x, 16, 1))
                    x = x * cos_ref[...] + swapped * sin_ref[...]
                a_ref[:, col:col + lanes] = x.astype(BF16)

    @pl.when(j == 1)
    def _():
        for b in range(n_slices):
            acc = slice_dot(b)
            if b < n_slices // 2:
                acc = acc * _sigmoid(acc)
            a_ref[:, b * width:(b + 1) * width] = acc.astype(BF16)

    @pl.when(j == 2)
    def _():
        for b in range(n_slices // 2):
            p_ref[:, b * width:(b + 1) * width] = (slice_dot(b) * slice_dot(b + n_slices // 2)).astype(BF16)

    @pl.when(j == 3)
    def _():
        for b in range(n_slices):
            s_ref[:, b * width:(b + 1) * width] = _sigmoid(slice_dot(b)).astype(BF16)


def _inproj_kernel(x_ref, g_ref, sh_ref, sc_ref, w_ref, *rest, use_rope):
    if use_rope:
        cos_ref, sin_ref, a_ref, p_ref, s_ref, h_ref = rest
    else:
        cos_ref = sin_ref = None
        a_ref, p_ref, s_ref, h_ref = rest
    _inproj_body(x_ref, g_ref, sh_ref, sc_ref, w_ref, cos_ref, sin_ref, a_ref, p_ref, s_ref, h_ref, use_rope)


def _inproj(x, mod3, g, w_bf16, cond_row_fn, seq, rope_tabs):
    t = x.shape[0]
    tm, tn = 1024, 2048
    assert D_IN_TOTAL == 4 * tn
    in_specs = [pl.BlockSpec((tm, D_MODEL), lambda i, j: (i, 0)),
                pl.BlockSpec((1, D_MODEL), lambda i, j: (0, 0)),
                pl.BlockSpec((1, 1, D_MODEL), lambda i, j: (cond_row_fn(i * tm), 0, 0)),
                pl.BlockSpec((1, 1, D_MODEL), lambda i, j: (cond_row_fn(i * tm), 0, 1)),
                pl.BlockSpec((D_MODEL, tn), lambda i, j: (0, j))]
    args = [x, g, mod3, mod3, w_bf16]
    if rope_tabs is not None:
        tiles_per_seq = seq // tm
        in_specs += [pl.BlockSpec((tm, 2 * DK), lambda i, j: (i % tiles_per_seq, 0))] * 2
        args += list(rope_tabs)
    return pl.pallas_call(
        functools.partial(_inproj_kernel, use_rope=rope_tabs is not None),
        grid=(t // tm, D_IN_TOTAL // tn),
        in_specs=in_specs,
        out_specs=[pl.BlockSpec((tm, tn), lambda i, j: (i, jnp.minimum(j, 1))),
                   pl.BlockSpec((tm, tn // 2), lambda i, j: (i, 0)),
                   pl.BlockSpec((tm, tn), lambda i, j: (i, 0))],
        out_shape=[jax.ShapeDtypeStruct((t, 2 * tn), BF16),
                   jax.ShapeDtypeStruct((t, tn // 2), BF16),
                   jax.ShapeDtypeStruct((t, tn), BF16)],
        scratch_shapes=[pltpu.VMEM((tm, D_MODEL), BF16)],
        compiler_params=pltpu.CompilerParams(dimension_semantics=("arbitrary", "arbitrary"),
                                             vmem_limit_bytes=VMEM_LIMIT),
        name="inproj",
    )(*args)


def _ret_kernel(*refs, nc, bb, has_init, emit_state):
    it = iter(refs)
    lg_ref = next(it)
    q_ref, k_ref, v_ref, gr_ref, gn_ref = (next(it) for _ in range(5))
    s0f_ref = s0b_ref = sf_ref = sb_ref = None
    if has_init:
        s0f_ref, s0b_ref = next(it), next(it)
    o_ref = next(it)
    if emit_state:
        sf_ref, sb_ref = next(it), next(it)
    kvf_s, kvb_s, rcat_s, p_s, o_s, r0 = (next(it) for _ in range(6))

    p = pl.program_id(1)
    lgf_a, lgf_b = lg_ref[0, 2 * p], lg_ref[0, 2 * p + 1]
    lgb_a, lgb_b = lg_ref[1, 2 * p], lg_ref[1, 2 * p + 1]

    ri = lax.broadcasted_iota(jnp.int32, (CHUNK, 2 * DK), 0)
    ci = lax.broadcasted_iota(jnp.int32, (CHUNK, 2 * DK), 1)
    lane_a = ci < DK
    rowf = ri.astype(F32)
    diff = rowf - ci.astype(F32)
    lgf_lane = jnp.where(lane_a, lgf_a, lgf_b)
    lgb_lane = jnp.where(lane_a, lgb_a, lgb_b)
    xi_f = jnp.exp(lgf_lane * (rowf + 1.0))
    xi_b = jnp.exp(lgb_lane * (CHUNK - rowf))
    zeta_f = jnp.exp(lgf_lane * (CHUNK - 1.0 - rowf))
    zeta_b = jnp.exp(lgb_lane * rowf)

    def decay_matrix(lgf, lgb):
        return jnp.where(diff > 0, jnp.exp(lgf * diff),
                         jnp.where(diff < 0, jnp.exp(lgb * (-diff)), 2.0))

    dm_a = decay_matrix(lgf_a, lgb_a)
    dm_b = decay_matrix(lgf_b, lgb_b)

    r2 = lax.broadcasted_iota(jnp.int32, (2 * DK, 2 * DV), 0)
    c2 = lax.broadcasted_iota(jnp.int32, (2 * DK, 2 * DV), 1)
    top = r2 < DK
    blk = (top == (c2 < DV)).astype(F32)
    cd_f = jnp.exp(jnp.where(top, lgf_a, lgf_b) * float(CHUNK)) * blk
    cd_b = jnp.exp(jnp.where(top, lgb_a, lgb_b) * float(CHUNK)) * blk

    def chunk_rows(c):
        return pl.ds(pl.multiple_of(c * CHUNK, CHUNK), CHUNK)

    n_chunks = bb * nc
    unroll = RET_UNROLL

    def kv_body(c, carry):
        rows = chunk_rows(c)
        k = k_ref[rows, :].astype(F32)
        kz_t = jnp.concatenate([k * zeta_f, k * zeta_b], axis=1).T.astype(BF16)
        kv = jnp.dot(kz_t, v_ref[rows, :], preferred_element_type=F32)
        kvf_s[c] = kv[0:2 * DK] * blk
        kvb_s[c] = kv[2 * DK:4 * DK] * blk
        return carry

    def score_body(c, carry):
        rows = chunk_rows(c)
        q16 = q_ref[rows, :]
        zero = jnp.zeros_like(q16)
        q_ab = jnp.concatenate([jnp.where(lane_a, q16, zero), jnp.where(lane_a, zero, q16)], axis=0)
        s = lax.dot_general(q_ab, k_ref[rows, :], (((1,), (1,)), ((), ())), preferred_element_type=F32)
        p_s[c, 0:CHUNK, :] = (s[0:CHUNK] * dm_a).astype(BF16)
        p_s[c, CHUNK:2 * CHUNK, :] = (s[CHUNK:2 * CHUNK] * dm_b).astype(BF16)
        return carry

    lax.fori_loop(0, n_chunks, lambda c, carry: score_body(c, kv_body(c, carry)), 0, unroll=unroll)

    def load_state(s_ref, s):
        r0[...] = jnp.zeros_like(r0)
        r0[0:DK, 0:DV] = s_ref[s, 0, 0].astype(F32)
        r0[DK:2 * DK, DV:2 * DV] = s_ref[s, 0, 1].astype(F32)
        return r0[...]

    for s in range(bb):
        def fwd_body(n, rf, s=s):
            c = s * nc + n
            rcat_s[c, 0:2 * DK, :] = rf.astype(BF16)
            return cd_f * rf + kvf_s[c]

        def bwd_body(t, rb, s=s):
            c = s * nc + nc - 1 - t
            rcat_s[c, 2 * DK:4 * DK, :] = rb.astype(BF16)
            return cd_b * rb + kvb_s[c]

        zeros = jnp.zeros((2 * DK, 2 * DV), F32)
        rf_fin = lax.fori_loop(0, nc, fwd_body, load_state(s0f_ref, s) if has_init else zeros)
        rb_fin = lax.fori_loop(0, nc, bwd_body, load_state(s0b_ref, s) if has_init else zeros)
        if emit_state:
            for s_ref, r in ((sf_ref, rf_fin), (sb_ref, rb_fin)):
                s_ref[s, 0, 0] = r[0:DK, 0:DV]
                s_ref[s, 0, 1] = r[DK:2 * DK, DV:2 * DV]

    gn = gn_ref[...]

    lane_mean = jnp.full((DV, DV), 1.0 / DV, BF16)

    def value_body(c, carry):
        rows = chunk_rows(c)
        q = q_ref[rows, :].astype(F32)
        v = v_ref[rows, :]
        qx = jnp.concatenate([(q * xi_f).astype(BF16), (q * xi_b).astype(BF16)], axis=1)
        oc = jnp.dot(qx, rcat_s[c], preferred_element_type=F32)
        o_a = jnp.dot(p_s[c, 0:CHUNK, :], v[:, 0:DV], preferred_element_type=F32)
        o_b = jnp.dot(p_s[c, CHUNK:2 * CHUNK, :], v[:, DV:2 * DV], preferred_element_type=F32)
        o_s[c, :, 0:DV] = o_a + oc[:, 0:DV]
        o_s[c, :, DV:2 * DV] = o_b + oc[:, DV:2 * DV]
        return carry

    lax.fori_loop(0, n_chunks, value_body, 0, unroll=unroll)

    def head_norm(o):
        mu = jnp.dot(o.astype(BF16), lane_mean, preferred_element_type=F32)
        d = o - mu
        var = jnp.dot((d * d).astype(BF16), lane_mean, preferred_element_type=F32)
        return d * lax.rsqrt(var + GN_EPS)

    def gate_body(c, carry):
        rows = chunk_rows(c)
        o = o_s[c]
        y = jnp.concatenate([head_norm(o[:, 0:DV]), head_norm(o[:, DV:2 * DV])], axis=1)
        o_ref[rows, :] = (gr_ref[rows, :].astype(F32) * (y * gn)).astype(BF16)
        return carry

    lax.fori_loop(0, n_chunks, gate_body, 0, unroll=unroll)


def _retention(proj, lgs, gn_g, seq, *, init=None, emit_state):
    t = proj.shape[0]
    b = t // seq
    nc = seq // CHUNK
    bb = max(1, RET_CHUNKS_PER_STEP // nc)
    rows = bb * seq
    has_init = init is not None
    pairs = N_HEADS // 2
    qk_blocks = (N_HEADS * DK) // (2 * DK)
    in_specs = [pl.BlockSpec(memory_space=pltpu.SMEM),
                pl.BlockSpec((rows, 2 * DK), lambda i, p: (i, p)),
                pl.BlockSpec((rows, 2 * DK), lambda i, p: (i, qk_blocks + p)),
                pl.BlockSpec((rows, 2 * DV), lambda i, p: (i, qk_blocks + p)),
                pl.BlockSpec((rows, 2 * DV), lambda i, p: (i, 2 * qk_blocks + p)),
                pl.BlockSpec((1, 2 * DV), lambda i, p: (0, p))]
    args = [lgs, proj, proj, proj, proj, gn_g]
    state_spec = pl.BlockSpec((bb, 1, 2, DK, DV), lambda i, p: (i, 0, p, 0, 0))
    if has_init:
        in_specs += [state_spec, state_spec]
        args += list(init)
    out_specs = [pl.BlockSpec((rows, 2 * DV), lambda i, p: (i, p))]
    out_shape = [jax.ShapeDtypeStruct((t, N_HEADS * DV), BF16)]
    if emit_state:
        out_specs += [state_spec, state_spec]
        out_shape += [jax.ShapeDtypeStruct((b, 1, N_HEADS, DK, DV), F32)] * 2
    return pl.pallas_call(
        functools.partial(_ret_kernel, nc=nc, bb=bb, has_init=has_init, emit_state=emit_state),
        grid=(b // bb, pairs),
        in_specs=in_specs,
        out_specs=out_specs,
        out_shape=out_shape,
        scratch_shapes=[pltpu.VMEM((bb * nc, 2 * DK, 2 * DV), F32),
                        pltpu.VMEM((bb * nc, 2 * DK, 2 * DV), F32),
                        pltpu.VMEM((bb * nc, 4 * DK, 2 * DV), BF16),
                        pltpu.VMEM((bb * nc, 2 * CHUNK, CHUNK), BF16),
                        pltpu.VMEM((bb * nc, CHUNK, 2 * DV), F32),
                        pltpu.VMEM((2 * DK, 2 * DV), F32)],
        compiler_params=pltpu.CompilerParams(dimension_semantics=("arbitrary", "arbitrary"),
                                             vmem_limit_bytes=VMEM_LIMIT),
        name="retention",
    )(*args)


def _rope_tables(seq):
    n_freq = DK // 4
    pos = np.arange(seq)
    inv = jnp.asarray(ROPE_BASE, F32) ** (-jnp.arange(n_freq, dtype=F32) / n_freq)
    ang_r = jnp.asarray(pos // GRID_W, F32)[:, None] * inv
    ang_c = jnp.asarray(pos % GRID_W, F32)[:, None] * inv
    cos = jnp.concatenate([jnp.cos(ang_r)] * 2 + [jnp.cos(ang_c)] * 2, axis=1)
    sin = jnp.concatenate([-jnp.sin(ang_r), jnp.sin(ang_r), -jnp.sin(ang_c), jnp.sin(ang_c)], axis=1)
    return jnp.tile(cos, (1, 2)), jnp.tile(sin, (1, 2))


def _mix_kernel(og_ref, cb_ref, p_ref, pp_ref, pn_ref, sa_ref, sb_ref, x_ref,
                cw_ref, wro_ref, wco_ref, wo_ref, wr_ref, npost_ref, npre_ref, g1_ref, sh2_ref, sc2_ref,
                x1_ref, h2_ref, aff_ref, *, tm, sub, seq):
    i = pl.program_id(0)
    halo = pp_ref.shape[0]
    wr = wr_ref[...]
    w_hi = wr.astype(BF16)
    w_lo = (wr - w_hi.astype(F32)).astype(BF16)
    w_hi_lo = jnp.concatenate([w_hi, w_lo], axis=1)
    row = lax.broadcasted_iota(jnp.int32, (sub, 1), 0)
    lane = lax.broadcasted_iota(jnp.int32, (sub, ROUTER_LANES), 1)

    for s in range(tm // sub):
        r0 = s * sub
        rows = slice(r0, r0 + sub)
        token = i * tm + r0 + row
        in_seq = token & (seq - 1) if seq & (seq - 1) == 0 else token % seq
        prev_row = (pp_ref[halo - 1:halo, :] if s == 0 else p_ref[r0 - 1:r0, :]).astype(F32)
        next_row = (pn_ref[0:1, :] if r0 + sub == tm else p_ref[r0 + sub:r0 + sub + 1, :]).astype(F32)
        prod = p_ref[rows, :].astype(F32)
        up = jnp.where(in_seq == 0, 0.0, jnp.where(row == 0, prev_row, pltpu.roll(prod, 1, 0)))
        dn = jnp.where(in_seq == seq - 1, 0.0, jnp.where(row == sub - 1, next_row, pltpu.roll(prod, sub - 1, 0)))
        u = up * cw_ref[0:1, :] + prod * cw_ref[1:2, :] + dn * cw_ref[2:3, :]
        y_conv = jnp.dot((cb_ref[rows, :].astype(F32) * u).astype(BF16), wco_ref[...],
                         preferred_element_type=F32)
        y_ret = jnp.dot(og_ref[rows, :], wro_ref[...], preferred_element_type=F32)
        merged = sa_ref[rows, :].astype(F32) * y_ret + sb_ref[rows, :].astype(F32) * y_conv
        m = jnp.dot(merged.astype(BF16), wo_ref[...], preferred_element_type=F32)
        x1 = x_ref[rows, :] + g1_ref[0] * _rms(m, npost_ref[...])
        x1_ref[rows, :] = x1
        h2 = _rms(x1, npre_ref[...]) * (1.0 + sc2_ref[0]) + sh2_ref[0]
        h_hi = h2.astype(BF16)
        h2_ref[rows, :] = h_hi
        h_lo = (h2 - h_hi.astype(F32)).astype(BF16)
        hi_terms = jnp.dot(h_hi, w_hi_lo, preferred_element_type=F32)
        logits = (hi_terms[:, 0:ROUTER_LANES] + hi_terms[:, ROUTER_LANES:2 * ROUTER_LANES]
                  + jnp.dot(h_lo, w_hi, preferred_element_type=F32))
        logits = jnp.where(lane < N_EXPERTS, logits, -jnp.inf)
        e = jnp.exp(logits - jnp.max(logits, axis=-1, keepdims=True))
        aff = e / jnp.sum(e, axis=-1, keepdims=True)
        aff_ref[:, rows] = aff.T[0:N_EXPERTS, :]


def _mix(og, proj_a, proj_p, proj_s, x, mod3, conv_w, wro, wco, wo, wr_pad, npost, npre, seq, cond_row_fn):
    t = x.shape[0]
    conv_b_block = (2 * N_HEADS * DK + 2 * N_HEADS * DV) // D_MODEL
    tm = MIX_TILE
    halo = 16
    hb = tm // halo
    last_halo = t // halo - 1
    col = lambda c: (lambda i: (i, c))
    row_vec = pl.BlockSpec((1, D_MODEL), lambda i: (0, 0))
    wspec = pl.BlockSpec((D_MODEL, D_MODEL), lambda i: (0, 0))
    modspec = lambda c: pl.BlockSpec((1, 1, D_MODEL), lambda i: (cond_row_fn(i * tm), 0, c))
    tile = lambda c: pl.BlockSpec((tm, D_MODEL), col(c))
    prev = lambda c: pl.BlockSpec((halo, D_MODEL), lambda i: (jnp.maximum(i * hb - 1, 0), c))
    nxt = lambda c: pl.BlockSpec((halo, D_MODEL), lambda i: (jnp.minimum((i + 1) * hb, last_halo), c))
    return pl.pallas_call(
        functools.partial(_mix_kernel, tm=tm, sub=MIX_SUB_TILE, seq=seq),
        grid=(t // tm,),
        in_specs=[tile(0), tile(conv_b_block), tile(0), prev(0), nxt(0), tile(0), tile(1),
                  tile(0),
                  pl.BlockSpec((3, D_MODEL), lambda i: (0, 0)), wspec, wspec, wspec,
                  pl.BlockSpec((D_MODEL, ROUTER_LANES), lambda i: (0, 0)),
                  row_vec, row_vec, modspec(2), modspec(3), modspec(4)],
        out_specs=[tile(0), tile(0), pl.BlockSpec((N_EXPERTS, tm), lambda i: (0, i))],
        out_shape=[jax.ShapeDtypeStruct((t, D_MODEL), F32),
                   jax.ShapeDtypeStruct((t, D_MODEL), BF16),
                   jax.ShapeDtypeStruct((N_EXPERTS, t), F32)],
        compiler_params=pltpu.CompilerParams(dimension_semantics=("arbitrary",),
                                             vmem_limit_bytes=VMEM_LIMIT),
        name="mix_out",
    )(og, proj_a, proj_p, proj_p, proj_p, proj_s, proj_s, x,
      conv_w, wro, wco, wo, wr_pad, npost, npre, mod3, mod3, mod3)


def _expert_kernel(xs_ref, wg_ref, wu_ref, wd_ref, o_ref, *, sub):
    wg = wg_ref[0].astype(BF16)
    wu = wu_ref[0].astype(BF16)
    wd = wd_ref[0].astype(BF16)
    for r in range(xs_ref.shape[1] // sub):
        rows = slice(r * sub, (r + 1) * sub)
        x = xs_ref[0, rows, :]
        g = jnp.dot(x, wg, preferred_element_type=F32)
        u = jnp.dot(x, wu, preferred_element_type=F32)
        hid = (g * _sigmoid(g) * u).astype(BF16)
        o_ref[0, rows, :] = jnp.dot(hid, wd, preferred_element_type=F32).astype(BF16)


def _experts(xs, wg, wu, wd):
    e, cap, d = xs.shape
    ff = wg.shape[2]
    tr = min(cap, EXPERT_ROWS_PER_STEP)
    return pl.pallas_call(
        functools.partial(_expert_kernel, sub=EXPERT_SUB_ROWS),
        grid=(e, cap // tr),
        in_specs=[pl.BlockSpec((1, tr, d), lambda a, r: (a, r, 0)),
                  pl.BlockSpec((1, d, ff), lambda a, r: (a, 0, 0)),
                  pl.BlockSpec((1, d, ff), lambda a, r: (a, 0, 0)),
                  pl.BlockSpec((1, ff, d), lambda a, r: (a, 0, 0))],
        out_specs=pl.BlockSpec((1, tr, d), lambda a, r: (a, r, 0)),
        out_shape=jax.ShapeDtypeStruct((e, cap, d), BF16),
        compiler_params=pltpu.CompilerParams(dimension_semantics=("arbitrary", "arbitrary"),
                                             vmem_limit_bytes=EXPERT_VMEM_LIMIT),
        name="experts",
    )(xs, wg, wu, wd)


def _block_ranks(mask):
    ne, nblk, lanes = mask.shape
    li = lax.broadcasted_iota(jnp.int32, (lanes, lanes), 0)
    lj = lax.broadcasted_iota(jnp.int32, (lanes, lanes), 1)
    incl_lanes = jnp.where(li <= lj, 1.0, 0.0).astype(BF16)
    all_lanes = jnp.ones((lanes, lanes), BF16)
    bi = lax.broadcasted_iota(jnp.int32, (nblk, nblk), 0)
    bj = lax.broadcasted_iota(jnp.int32, (nblk, nblk), 1)
    earlier_blocks = jnp.where(bj < bi, 1.0, 0.0).astype(BF16)
    m = mask.reshape(ne * nblk, lanes).astype(BF16)
    incl = jnp.dot(m, incl_lanes, preferred_element_type=F32).reshape(ne, nblk, lanes)
    tot = jnp.dot(m, all_lanes, preferred_element_type=F32).reshape(ne, nblk, lanes).astype(BF16)
    excl = jnp.stack([jnp.dot(earlier_blocks, tot[i], preferred_element_type=F32) for i in range(ne)])
    return excl, incl


def _select_kernel(aff_ref, sel_ref, *, cap):
    ne, nblk, lanes = aff_ref.shape
    aff = aff_ref[...]

    def count(mask):
        m = jnp.where(mask, 1.0, 0.0)
        return jnp.sum(jnp.sum(m, axis=1, keepdims=True), axis=2, keepdims=True)

    def as_float(bits):
        return lax.bitcast_convert_type(bits, F32)

    def bit_step(it, thr_bits):
        cand = thr_bits | jnp.left_shift(jnp.int32(1), 30 - it)
        return jnp.where(count(aff >= as_float(cand)) >= cap, cand, thr_bits)

    thr = as_float(lax.fori_loop(0, 31, bit_step, jnp.zeros((ne, 1, 1), jnp.int32)))
    gt = aff > thr
    eq = aff == thr
    need = cap - count(gt)
    ex_eq, in_eq = _block_ranks(jnp.where(eq, 1.0, 0.0))
    sel_ref[...] = jnp.where(gt | (eq & (ex_eq + in_eq - 1.0 < need)), 1.0, 0.0)


def _slot_kernel(sel_ref, pos_ref, excl_ref):
    sel = sel_ref[...]
    ex_sel, in_sel = _block_ranks(sel)
    pos_ref[...] = jnp.where(sel > 0.0, ex_sel + in_sel - 1.0, -1.0).astype(jnp.int32)
    excl_ref[...] = ex_sel.astype(jnp.int32)


def _to_tile_order(a, nb):
    if nb == 1:
        return a
    ne, t = a.shape
    r = ROUTE_TILE // nb
    return a.reshape(ne, nb, t // (nb * r), r).transpose(0, 2, 1, 3).reshape(ne, t)


def _route(aff_t, cap, nb):
    ne, t = aff_t.shape
    nblk = t // 128
    shape = (ne, nblk, 128)
    full = pl.BlockSpec(shape, lambda i: (0, 0, 0))
    params = pltpu.CompilerParams(dimension_semantics=("arbitrary",), vmem_limit_bytes=VMEM_LIMIT)
    sel = pl.pallas_call(
        functools.partial(_select_kernel, cap=cap),
        grid=(1,), in_specs=[full], out_specs=full,
        out_shape=jax.ShapeDtypeStruct(shape, F32),
        compiler_params=params, name="route_select",
    )(aff_t.reshape(shape))
    sel = _to_tile_order(sel.reshape(ne, t), nb)
    pos, excl = pl.pallas_call(
        _slot_kernel,
        grid=(1,), in_specs=[full], out_specs=[full, full],
        out_shape=[jax.ShapeDtypeStruct(shape, jnp.int32)] * 2,
        compiler_params=params, name="route_slots",
    )(sel.reshape(shape))
    base = jnp.concatenate([excl[:, :, 0], jnp.full((ne, 1), cap, jnp.int32)], axis=1)
    return pos.reshape(ne, t), base, _to_tile_order(aff_t, nb)


ROUTE_TILE = 256
ROUTE_WINDOW = 64
SLOT_ALIGN = 16
DISPATCH_SLOT_ROWS = 8192
DISPATCH_TILES_PER_STEP = 8


def _window(base_ref, e, blk, w, cap):
    lo = base_ref[e, blk] // SLOT_ALIGN + w * (ROUTE_WINDOW // SLOT_ALIGN)
    return lo * SLOT_ALIGN, jnp.minimum(lo, (cap - ROUTE_WINDOW) // SLOT_ALIGN) * SLOT_ALIGN


def _n_windows(base_ref, e, blk, cap):
    lo, _ = _window(base_ref, e, blk, 0, cap)
    end = base_ref[e, blk + ROUTE_TILE // 128]
    return (end - lo + ROUTE_WINDOW - 1) // ROUTE_WINDOW


def _dispatch_kernel(base_ref, h_ref, pos_ref, xs_ref, *, cap):
    g, j = pl.program_id(0), pl.program_id(1)
    group = xs_ref.shape[0]

    @pl.when(j == 0)
    def _():
        xs_ref[...] = jnp.zeros_like(xs_ref)

    ri = lax.broadcasted_iota(jnp.int32, (ROUTE_WINDOW, ROUTE_TILE), 0)

    def onehot(prow, lo, off, first):
        hit = prow - off == ri
        if not first:
            hit = hit & (prow >= lo)
        return jnp.where(hit, 1.0, 0.0).astype(BF16)

    def add_rows(ge, off, rows):
        sl = pl.ds(pl.multiple_of(off, SLOT_ALIGN), ROUTE_WINDOW)
        xs_ref[ge, sl, :] = xs_ref[ge, sl, :] + rows.astype(BF16)

    for sub in range(DISPATCH_TILES_PER_STEP):
        blk = (j * DISPATCH_TILES_PER_STEP + sub) * (ROUTE_TILE // 128)
        tok = slice(sub * ROUTE_TILE, (sub + 1) * ROUTE_TILE)
        h = h_ref[:, sub].reshape(ROUTE_TILE, h_ref.shape[-1])
        prows, offs = [], []
        for ge in range(group):
            e = g * group + ge
            prows.append(pos_ref[pl.ds(e, 1), tok])
            offs.append(_window(base_ref, e, blk, 0, cap))
        sel = jnp.concatenate([onehot(prows[ge], *offs[ge], True) for ge in range(group)], axis=0)
        rows = jnp.dot(sel, h, preferred_element_type=F32)
        for ge in range(group):
            add_rows(ge, offs[ge][1], rows[ge * ROUTE_WINDOW:(ge + 1) * ROUTE_WINDOW])

        for ge in range(group):
            e = g * group + ge

            def extra(w, carry, ge=ge, e=e, blk=blk, h=h, prow=prows[ge]):
                lo, off = _window(base_ref, e, blk, w, cap)
                add_rows(ge, off, jnp.dot(onehot(prow, lo, off, False), h, preferred_element_type=F32))
                return carry

            lax.fori_loop(1, _n_windows(base_ref, e, blk, cap), extra, 0)


def _tile_view(a, nb):
    t, d = a.shape
    r = ROUTE_TILE // nb
    return a.reshape(nb, t // (nb * r), r, d)


def _dispatch(base, h2, pos, cap, nb):
    t, d = h2.shape
    ne = pos.shape[0]
    step = ROUTE_TILE * DISPATCH_TILES_PER_STEP
    group = min(ne, DISPATCH_SLOT_ROWS // cap)
    return pl.pallas_call(
        functools.partial(_dispatch_kernel, cap=cap),
        grid_spec=pltpu.PrefetchScalarGridSpec(
            num_scalar_prefetch=1,
            grid=(ne // group, t // step),
            in_specs=[pl.BlockSpec((nb, DISPATCH_TILES_PER_STEP, ROUTE_TILE // nb, d),
                                   lambda g, j, b: (0, j, 0, 0)),
                      pl.BlockSpec((ne, step), lambda g, j, b: (0, j))],
            out_specs=pl.BlockSpec((group, cap, d), lambda g, j, b: (g, 0, 0))),
        out_shape=jax.ShapeDtypeStruct((ne, cap, d), BF16),
        compiler_params=pltpu.CompilerParams(dimension_semantics=("arbitrary", "arbitrary"),
                                             vmem_limit_bytes=VMEM_LIMIT),
        name="dispatch",
    )(base, _tile_view(h2, nb), pos)


def _combine_kernel(base_ref, pos_ref, aff_ref, x1_ref, g_ref, g2_ref, eo_ref, o_ref,
                    f_ref, win_buf, win_sem, extra_buf, extra_sem, *, cap, n_tiles):
    j = pl.program_id(0)
    blocks_per_tile = ROUTE_TILE // 128
    blk = j * blocks_per_tile
    slot = lax.rem(j, 2)
    ri = lax.broadcasted_iota(jnp.int32, (ROUTE_WINDOW, ROUTE_TILE), 0)
    tn = (((0,), (0,)), ((), ()))

    def window_copy(e, off, dst, sem):
        return pltpu.make_async_copy(eo_ref.at[e, pl.ds(pl.multiple_of(off, SLOT_ALIGN), ROUTE_WINDOW), :],
                                     dst, sem)

    def first_window_copies(tile, buf_slot):
        return [window_copy(e, _window(base_ref, e, tile * blocks_per_tile, 0, cap)[1],
                            win_buf.at[buf_slot, pl.ds(e * ROUTE_WINDOW, ROUTE_WINDOW), :], win_sem.at[buf_slot])
                for e in range(N_EXPERTS)]

    def wait_windows(buf_slot):
        for e in range(N_EXPERTS):
            window_copy(e, 0, win_buf.at[buf_slot, pl.ds(e * ROUTE_WINDOW, ROUTE_WINDOW), :],
                        win_sem.at[buf_slot]).wait()

    @pl.when(j == 0)
    def _():
        for cp in first_window_copies(0, 0):
            cp.start()

    for cp in first_window_copies(jnp.minimum(j + 1, n_tiles - 1), 1 - slot):
        cp.start()

    def weights(e, lo, off, first):
        prow = pos_ref[e:e + 1, :]
        hit = prow - off == ri
        if not first:
            hit = hit & (prow >= lo)
        return jnp.where(hit, aff_ref[e:e + 1, :], 0.0).astype(BF16)

    q = jnp.concatenate([weights(e, *_window(base_ref, e, blk, 0, cap), True) for e in range(N_EXPERTS)], axis=0)
    wait_windows(slot)
    f_ref[...] = lax.dot_general(q, win_buf[slot], tn, preferred_element_type=F32)

    n_win = [_n_windows(base_ref, e, blk, cap) for e in range(N_EXPERTS)]

    @pl.when(functools.reduce(jnp.maximum, n_win) > 1)
    def _():
        for e in range(N_EXPERTS):
            def extra(w, carry, e=e):
                lo, off = _window(base_ref, e, blk, w, cap)
                cp = window_copy(e, off, extra_buf, extra_sem)
                cp.start()
                cp.wait()
                f_ref[...] += lax.dot_general(weights(e, lo, off, False), extra_buf[...], tn,
                                              preferred_element_type=F32)
                return carry

            lax.fori_loop(1, n_win[e], extra, 0)

    nb, _, r, d = x1_ref.shape
    f = f_ref[...].reshape(nb, r, d)
    o_ref[:, 0] = x1_ref[:, 0] + g2_ref[...] * _rms(f, g_ref[...])

    @pl.when(j == n_tiles - 1)
    def _():
        wait_windows(1 - slot)


def _combine(base, eo, pos, aff, x1, mod3, npost, cond_block, cap, nb):
    t, d = x1.shape
    ne = pos.shape[0]
    n_tiles = t // ROUTE_TILE
    tile = pl.BlockSpec((nb, 1, ROUTE_TILE // nb, d), lambda j, b: (0, j, 0, 0))
    etile = pl.BlockSpec((ne, ROUTE_TILE), lambda j, b: (0, j))
    return pl.pallas_call(
        functools.partial(_combine_kernel, cap=cap, n_tiles=n_tiles),
        grid_spec=pltpu.PrefetchScalarGridSpec(
            num_scalar_prefetch=1,
            grid=(n_tiles,),
            in_specs=[etile, etile, tile,
                      pl.BlockSpec((1, d), lambda j, b: (0, 0)),
                      pl.BlockSpec((nb, 1, d), lambda j, b: (cond_block, 0, 5)),
                      pl.BlockSpec(memory_space=pl.ANY)],
            out_specs=tile,
            scratch_shapes=[pltpu.VMEM((ROUTE_TILE, d), F32),
                            pltpu.VMEM((2, ne * ROUTE_WINDOW, d), BF16), pltpu.SemaphoreType.DMA((2,)),
                            pltpu.VMEM((ROUTE_WINDOW, d), BF16), pltpu.SemaphoreType.DMA(())]),
        out_shape=jax.ShapeDtypeStruct(_tile_view(x1, nb).shape, F32),
        compiler_params=pltpu.CompilerParams(dimension_semantics=("arbitrary",),
                                             vmem_limit_bytes=VMEM_LIMIT),
        name="combine",
    )(base, pos, aff, _tile_view(x1, nb), npost, mod3, eo).reshape(t, d)


def _trunk(x3, mod3, first_cond_row, shared_cond, weights, lgs, *, rope_tabs, init, emit_state):
    (n_pre_mix, n_post_mix, n_pre_ffn, n_post_ffn, w_in, gn_g, conv_w, wro, wco, wo, wr_pad, wg, wu, wd) = weights
    b, seq, d = x3.shape
    t = b * seq
    x = x3.reshape(t, d)
    cond_row_fn = (lambda r: first_cond_row) if shared_cond else (lambda r: first_cond_row + r // seq)
    nb = 1 if shared_cond else b
    assert ROUTE_TILE % nb == 0 and (ROUTE_TILE // nb) % SLOT_ALIGN == 0 and first_cond_row % nb == 0
    proj_a, proj_p, proj_s = _inproj(x, mod3, n_pre_mix, w_in, cond_row_fn, seq, rope_tabs)
    ret = _retention(proj_a, lgs, gn_g, seq, init=init, emit_state=emit_state)
    og = ret[0]
    x1, h2, aff_t = _mix(og, proj_a, proj_p, proj_s, x, mod3, conv_w, wro, wco, wo, wr_pad, n_post_mix,
                         n_pre_ffn, seq, cond_row_fn)
    cap = CAPACITY_FACTOR * t // N_EXPERTS
    pos, base, aff = _route(aff_t, cap, nb)
    eo = _experts(_dispatch(base, h2, pos, cap, nb), wg, wu, wd)
    y = _combine(base, eo, pos, aff, x1, mod3, n_post_ffn, first_cond_row // nb, cap, nb)
    return y.reshape(b, seq, d), ret[1:]


def kernel(x_prompt, x_sample, state_ret_fwd, state_ret_bwd, c, c_ctx, w_ada, b_ada, norm_pre_mix, norm_post_mix,
           norm_pre_ffn, norm_post_ffn, w_in, ret_decay_fwd, ret_decay_bwd, ret_norm_g, conv_w, w_ret_o, w_conv_o,
           w_o, w_router, w_gate, w_up, w_down):
    depth = w_ada.shape[0]
    assert depth == 1
    dec_b, dec_seq = x_sample.shape[0], x_sample.shape[1]
    xp, xs = x_prompt, x_sample
    l = 0
    cond = jnp.zeros((N_COND_ROWS, D_MODEL), F32).at[0:dec_b].set(c).at[CTX_COND_ROW].set(c_ctx)
    mod3 = _ada(cond, w_ada[l], b_ada[l][None, :]).reshape(N_COND_ROWS, 1, 6 * D_MODEL)
    lgs = jnp.stack([-jax.nn.softplus(-ret_decay_fwd[l].astype(F32)),
                     -jax.nn.softplus(-ret_decay_bwd[l].astype(F32))])
    wr_pad = jnp.pad(w_router[l], ((0, 0), (0, ROUTER_LANES - N_EXPERTS)))
    weights = (norm_pre_mix[l][None, :], norm_post_mix[l][None, :], norm_pre_ffn[l][None, :],
               norm_post_ffn[l][None, :], w_in[l].astype(BF16), ret_norm_g[l][None, :], conv_w[l],
               w_ret_o[l].astype(BF16), w_conv_o[l].astype(BF16), w_o[l].astype(BF16), wr_pad,
               w_gate[l], w_up[l], w_down[l])
    yp, (s_f, s_b) = _trunk(xp, mod3, CTX_COND_ROW, True, weights, lgs,
                            rope_tabs=None, init=None, emit_state=True)
    ys, _ = _trunk(xs, mod3, 0, False, weights, lgs,
                   rope_tabs=_rope_tables(dec_seq), init=(state_ret_fwd, state_ret_bwd), emit_state=False)
    return (yp, ys, s_f, s_b)
```

```python
import functools

import jax
import jax.numpy as jnp
import numpy as np
from jax import lax
from jax.experimental import pallas as pl
from jax.experimental.pallas import tpu as pltpu

F32 = jnp.float32
BF16 = jnp.bfloat16

D_MODEL = 1024
N_HEADS = 8
DK = 64
DV = 128
CHUNK = 128
GRID_W = 64
N_EXPERTS = 16
CAPACITY_FACTOR = 2
D_IN_TOTAL = 8192
RMS_EPS = 1e-6
GN_EPS = 1e-5
ROPE_BASE = 10000.0
N_COND_ROWS = 16
CTX_COND_ROW = 8
ROUTER_LANES = 128
RET_CHUNKS_PER_STEP = 16
RET_UNROLL = 4
MIX_TILE = 1024
MIX_SUB_TILE = 512
VMEM_LIMIT = 48 * 1024 * 1024
EXPERT_ROWS_PER_STEP = 1024
EXPERT_SUB_ROWS = 512
EXPERT_VMEM_LIMIT = 56 * 1024 * 1024


def _sigmoid(x):
    return 0.5 * jnp.tanh(0.5 * x) + 0.5


def _rms(x, g):
    return x * lax.rsqrt(jnp.mean(x * x, axis=-1, keepdims=True) + RMS_EPS) * g


def _ada_kernel(c_ref, w_ref, b_ref, o_ref):
    c = c_ref[...]
    s = c * _sigmoid(c)
    o_ref[...] = jnp.dot(s, w_ref[...], preferred_element_type=F32,
                         precision=lax.Precision.HIGHEST) + b_ref[...]


def _ada(cond, w_ada, b_ada):
    n = w_ada.shape[1]
    tn = 1024
    return pl.pallas_call(
        _ada_kernel,
        grid=(n // tn,),
        in_specs=[pl.BlockSpec((N_COND_ROWS, D_MODEL), lambda j: (0, 0)),
                  pl.BlockSpec((D_MODEL, tn), lambda j: (0, j)),
                  pl.BlockSpec((1, tn), lambda j: (0, j))],
        out_specs=pl.BlockSpec((N_COND_ROWS, tn), lambda j: (0, j)),
        out_shape=jax.ShapeDtypeStruct((N_COND_ROWS, n), F32),
        compiler_params=pltpu.CompilerParams(dimension_semantics=("arbitrary",),
                                             vmem_limit_bytes=VMEM_LIMIT),
        name="ada_mod",
    )(cond, w_ada, b_ada)


def _inproj_body(x_ref, g_ref, sh_ref, sc_ref, w_ref, cos_ref, sin_ref, a_ref, p_ref, s_ref, h_ref, use_rope):
    j = pl.program_id(1)

    @pl.when(j == 0)
    def _():
        h = _rms(x_ref[...], g_ref[...]) * (1.0 + sc_ref[0]) + sh_ref[0]
        h_ref[...] = h.astype(BF16)

    lanes = 2 * DK
    width = 2 * lanes
    n_slices = w_ref.shape[1] // width
    n_qk = 2 * N_HEADS * DK

    def slice_dot(b):
        return jnp.dot(h_ref[...], w_ref[:, b * width:(b + 1) * width], preferred_element_type=F32)

    @pl.when(j == 0)
    def _():
        ci = lax.broadcasted_iota(jnp.int32, (h_ref.shape[0], lanes), 1)
        for b in range(n_slices):
            acc = slice_dot(b)
            if b * width >= n_qk:
                a_ref[:, b * width:(b + 1) * width] = acc.astype(BF16)
                continue
            for half in range(2):
                col = b * width + half * lanes
                x = acc[:, half * lanes:(half + 1) * lanes]
                if col >= N_HEADS * DK:
                    x = x * (DK ** -0.5)
                if use_rope:
                    swapped = jnp.where((ci & 31) < 16, pltpu.roll(x, lanes - 16, 1), pltpu.roll(x, 16, 1))
                    x = x * cos_ref[...] + swapped * sin_ref[...]
                a_ref[:, col:col + lanes] = x.astype(BF16)

    @pl.when(j == 1)
    def _():
        for b in range(n_slices):
            acc = slice_dot(b)
            if b < n_slices // 2:
                acc = acc * _sigmoid(acc)
            a_ref[:, b * width:(b + 1) * width] = acc.astype(BF16)

    @pl.when(j == 2)
    def _():
        for b in range(n_slices // 2):
            p_ref[:, b * width:(b + 1) * width] = (slice_dot(b) * slice_dot(b + n_slices // 2)).astype(BF16)

    @pl.when(j == 3)
    def _():
        for b in range(n_slices):
            s_ref[:, b * width:(b + 1) * width] = _sigmoid(slice_dot(b)).astype(BF16)


def _inproj_kernel(x_ref, g_ref, sh_ref, sc_ref, w_ref, *rest, use_rope):
    if use_rope:
        cos_ref, sin_ref, a_ref, p_ref, s_ref, h_ref = rest
    else:
        cos_ref = sin_ref = None
        a_ref, p_ref, s_ref, h_ref = rest
    _inproj_body(x_ref, g_ref, sh_ref, sc_ref, w_ref, cos_ref, sin_ref, a_ref, p_ref, s_ref, h_ref, use_rope)


def _inproj(x, mod3, g, w_bf16, cond_row_fn, seq, rope_tabs):
    t = x.shape[0]
    tm, tn = 1024, 2048
    assert D_IN_TOTAL == 4 * tn
    in_specs = [pl.BlockSpec((tm, D_MODEL), lambda i, j: (i, 0)),
                pl.BlockSpec((1, D_MODEL), lambda i, j: (0, 0)),
                pl.BlockSpec((1, 1, D_MODEL), lambda i, j: (cond_row_fn(i * tm), 0, 0)),
                pl.BlockSpec((1, 1, D_MODEL), lambda i, j: (cond_row_fn(i * tm), 0, 1)),
                pl.BlockSpec((D_MODEL, tn), lambda i, j: (0, j))]
    args = [x, g, mod3, mod3, w_bf16]
    if rope_tabs is not None:
        tiles_per_seq = seq // tm
        in_specs += [pl.BlockSpec((tm, 2 * DK), lambda i, j: (i % tiles_per_seq, 0))] * 2
        args += list(rope_tabs)
    return pl.pallas_call(
        functools.partial(_inproj_kernel, use_rope=rope_tabs is not None),
        grid=(t // tm, D_IN_TOTAL // tn),
        in_specs=in_specs,
        out_specs=[pl.BlockSpec((tm, tn), lambda i, j: (i, jnp.minimum(j, 1))),
                   pl.BlockSpec((tm, tn // 2), lambda i, j: (i, 0)),
                   pl.BlockSpec((tm, tn), lambda i, j: (i, 0))],
        out_shape=[jax.ShapeDtypeStruct((t, 2 * tn), BF16),
                   jax.ShapeDtypeStruct((t, tn // 2), BF16),
                   jax.ShapeDtypeStruct((t, tn), BF16)],
        scratch_shapes=[pltpu.VMEM((tm, D_MODEL), BF16)],
        compiler_params=pltpu.CompilerParams(dimension_semantics=("arbitrary", "arbitrary"),
                                             vmem_limit_bytes=VMEM_LIMIT),
        name="inproj",
    )(*args)


def _ret_kernel(*refs, nc, bb, has_init, emit_state):
    it = iter(refs)
    lg_ref = next(it)
    q_ref, k_ref, v_ref, gr_ref, gn_ref = (next(it) for _ in range(5))
    s0f_ref = s0b_ref = sf_ref = sb_ref = None
    if has_init:
        s0f_ref, s0b_ref = next(it), next(it)
    o_ref = next(it)
    if emit_state:
        sf_ref, sb_ref = next(it), next(it)
    kvf_s, kvb_s, rcat_s, p_s, o_s, r0 = (next(it) for _ in range(6))

    p = pl.program_id(1)
    lgf_a, lgf_b = lg_ref[0, 2 * p], lg_ref[0, 2 * p + 1]
    lgb_a, lgb_b = lg_ref[1, 2 * p], lg_ref[1, 2 * p + 1]

    ri = lax.broadcasted_iota(jnp.int32, (CHUNK, 2 * DK), 0)
    ci = lax.broadcasted_iota(jnp.int32, (CHUNK, 2 * DK), 1)
    lane_a = ci < DK
    rowf = ri.astype(F32)
    diff = rowf - ci.astype(F32)
    lgf_lane = jnp.where(lane_a, lgf_a, lgf_b)
    lgb_lane = jnp.where(lane_a, lgb_a, lgb_b)
    xi_f = jnp.exp(lgf_lane * (rowf + 1.0))
    xi_b = jnp.exp(lgb_lane * (CHUNK - rowf))
    zeta_f = jnp.exp(lgf_lane * (CHUNK - 1.0 - rowf))
    zeta_b = jnp.exp(lgb_lane * rowf)

    def decay_matrix(lgf, lgb):
        return jnp.where(diff > 0, jnp.exp(lgf * diff),
                         jnp.where(diff < 0, jnp.exp(lgb * (-diff)), 2.0))

    dm_a = decay_matrix(lgf_a, lgb_a)
    dm_b = decay_matrix(lgf_b, lgb_b)

    r2 = lax.broadcasted_iota(jnp.int32, (2 * DK, 2 * DV), 0)
    c2 = lax.broadcasted_iota(jnp.int32, (2 * DK, 2 * DV), 1)
    top = r2 < DK
    blk = (top == (c2 < DV)).astype(F32)
    cd_f = jnp.exp(jnp.where(top, lgf_a, lgf_b) * float(CHUNK)) * blk
    cd_b = jnp.exp(jnp.where(top, lgb_a, lgb_b) * float(CHUNK)) * blk

    def chunk_rows(c):
        return pl.ds(pl.multiple_of(c * CHUNK, CHUNK), CHUNK)

    n_chunks = bb * nc
    unroll = RET_UNROLL

    def kv_body(c, carry):
        rows = chunk_rows(c)
        k = k_ref[rows, :].astype(F32)
        kz_t = jnp.concatenate([k * zeta_f, k * zeta_b], axis=1).T.astype(BF16)
        kv = jnp.dot(kz_t, v_ref[rows, :], preferred_element_type=F32)
        kvf_s[c] = kv[0:2 * DK] * blk
        kvb_s[c] = kv[2 * DK:4 * DK] * blk
        return carry

    def score_body(c, carry):
        rows = chunk_rows(c)
        q16 = q_ref[rows, :]
        zero = jnp.zeros_like(q16)
        q_ab = jnp.concatenate([jnp.where(lane_a, q16, zero), jnp.where(lane_a, zero, q16)], axis=0)
        s = lax.dot_general(q_ab, k_ref[rows, :], (((1,), (1,)), ((), ())), preferred_element_type=F32)
        p_s[c, 0:CHUNK, :] = (s[0:CHUNK] * dm_a).astype(BF16)
        p_s[c, CHUNK:2 * CHUNK, :] = (s[CHUNK:2 * CHUNK] * dm_b).astype(BF16)
        return carry

    lax.fori_loop(0, n_chunks, lambda c, carry: score_body(c, kv_body(c, carry)), 0, unroll=unroll)

    def load_state(s_ref, s):
        r0[...] = jnp.zeros_like(r0)
        r0[0:DK, 0:DV] = s_ref[s, 0, 0].astype(F32)
        r0[DK:2 * DK, DV:2 * DV] = s_ref[s, 0, 1].astype(F32)
        return r0[...]

    for s in range(bb):
        def fwd_body(n, rf, s=s):
            c = s * nc + n
            rcat_s[c, 0:2 * DK, :] = rf.astype(BF16)
            return cd_f * rf + kvf_s[c]

        def bwd_body(t, rb, s=s):
            c = s * nc + nc - 1 - t
            rcat_s[c, 2 * DK:4 * DK, :] = rb.astype(BF16)
            return cd_b * rb + kvb_s[c]

        zeros = jnp.zeros((2 * DK, 2 * DV), F32)
        rf_fin = lax.fori_loop(0, nc, fwd_body, load_state(s0f_ref, s) if has_init else zeros)
        rb_fin = lax.fori_loop(0, nc, bwd_body, load_state(s0b_ref, s) if has_init else zeros)
        if emit_state:
            for s_ref, r in ((sf_ref, rf_fin), (sb_ref, rb_fin)):
                s_ref[s, 0, 0] = r[0:DK, 0:DV]
                s_ref[s, 0, 1] = r[DK:2 * DK, DV:2 * DV]

    gn = gn_ref[...]

    lane_mean = jnp.full((DV, DV), 1.0 / DV, BF16)

    def value_body(c, carry):
        rows = chunk_rows(c)
        q = q_ref[rows, :].astype(F32)
        v = v_ref[rows, :]
        qx = jnp.concatenate([(q * xi_f).astype(BF16), (q * xi_b).astype(BF16)], axis=1)
        oc = jnp.dot(qx, rcat_s[c], preferred_element_type=F32)
        o_a = jnp.dot(p_s[c, 0:CHUNK, :], v[:, 0:DV], preferred_element_type=F32)
        o_b = jnp.dot(p_s[c, CHUNK:2 * CHUNK, :], v[:, DV:2 * DV], preferred_element_type=F32)
        o_s[c, :, 0:DV] = o_a + oc[:, 0:DV]
        o_s[c, :, DV:2 * DV] = o_b + oc[:, DV:2 * DV]
        return carry

    lax.fori_loop(0, n_chunks, value_body, 0, unroll=unroll)

    def head_norm(o):
        mu = jnp.dot(o.astype(BF16), lane_mean, preferred_element_type=F32)
        d = o - mu
        var = jnp.dot((d * d).astype(BF16), lane_mean, preferred_element_type=F32)
        return d * lax.rsqrt(var + GN_EPS)

    def gate_body(c, carry):
        rows = chunk_rows(c)
        o = o_s[c]
        y = jnp.concatenate([head_norm(o[:, 0:DV]), head_norm(o[:, DV:2 * DV])], axis=1)
        o_ref[rows, :] = (gr_ref[rows, :].astype(F32) * (y * gn)).astype(BF16)
        return carry

    lax.fori_loop(0, n_chunks, gate_body, 0, unroll=unroll)


def _retention(proj, lgs, gn_g, seq, *, init=None, emit_state):
    t = proj.shape[0]
    b = t // seq
    nc = seq // CHUNK
    bb = max(1, RET_CHUNKS_PER_STEP // nc)
    rows = bb * seq
    has_init = init is not None
    pairs = N_HEADS // 2
    qk_blocks = (N_HEADS * DK) // (2 * DK)
    in_specs = [pl.BlockSpec(memory_space=pltpu.SMEM),
                pl.BlockSpec((rows, 2 * DK), lambda i, p: (i, p)),
                pl.BlockSpec((rows, 2 * DK), lambda i, p: (i, qk_blocks + p)),
                pl.BlockSpec((rows, 2 * DV), lambda i, p: (i, qk_blocks + p)),
                pl.BlockSpec((rows, 2 * DV), lambda i, p: (i, 2 * qk_blocks + p)),
                pl.BlockSpec((1, 2 * DV), lambda i, p: (0, p))]
    args = [lgs, proj, proj, proj, proj, gn_g]
    state_spec = pl.BlockSpec((bb, 1, 2, DK, DV), lambda i, p: (i, 0, p, 0, 0))
    if has_init:
        in_specs += [state_spec, state_spec]
        args += list(init)
    out_specs = [pl.BlockSpec((rows, 2 * DV), lambda i, p: (i, p))]
    out_shape = [jax.ShapeDtypeStruct((t, N_HEADS * DV), BF16)]
    if emit_state:
        out_specs += [state_spec, state_spec]
        out_shape += [jax.ShapeDtypeStruct((b, 1, N_HEADS, DK, DV), F32)] * 2
    return pl.pallas_call(
        functools.partial(_ret_kernel, nc=nc, bb=bb, has_init=has_init, emit_state=emit_state),
        grid=(b // bb, pairs),
        in_specs=in_specs,
        out_specs=out_specs,
        out_shape=out_shape,
        scratch_shapes=[pltpu.VMEM((bb * nc, 2 * DK, 2 * DV), F32),
                        pltpu.VMEM((bb * nc, 2 * DK, 2 * DV), F32),
                        pltpu.VMEM((bb * nc, 4 * DK, 2 * DV), BF16),
                        pltpu.VMEM((bb * nc, 2 * CHUNK, CHUNK), BF16),
                        pltpu.VMEM((bb * nc, CHUNK, 2 * DV), F32),
                        pltpu.VMEM((2 * DK, 2 * DV), F32)],
        compiler_params=pltpu.CompilerParams(dimension_semantics=("arbitrary", "arbitrary"),
                                             vmem_limit_bytes=VMEM_LIMIT),
        name="retention",
    )(*args)


def _rope_tables(seq):
    n_freq = DK // 4
    pos = np.arange(seq)
    inv = jnp.asarray(ROPE_BASE, F32) ** (-jnp.arange(n_freq, dtype=F32) / n_freq)
    ang_r = jnp.asarray(pos // GRID_W, F32)[:, None] * inv
    ang_c = jnp.asarray(pos % GRID_W, F32)[:, None] * inv
    cos = jnp.concatenate([jnp.cos(ang_r)] * 2 + [jnp.cos(ang_c)] * 2, axis=1)
    sin = jnp.concatenate([-jnp.sin(ang_r), jnp.sin(ang_r), -jnp.sin(ang_c), jnp.sin(ang_c)], axis=1)
    return jnp.tile(cos, (1, 2)), jnp.tile(sin, (1, 2))


def _mix_kernel(og_ref, cb_ref, p_ref, pp_ref, pn_ref, sa_ref, sb_ref, x_ref,
                cw_ref, wro_ref, wco_ref, wo_ref, wr_ref, npost_ref, npre_ref, g1_ref, sh2_ref, sc2_ref,
                x1_ref, h2_ref, aff_ref, *, tm, sub, seq):
    i = pl.program_id(0)
    halo = pp_ref.shape[0]
    wr = wr_ref[...]
    w_hi = wr.astype(BF16)
    w_lo = (wr - w_hi.astype(F32)).astype(BF16)
    w_hi_lo = jnp.concatenate([w_hi, w_lo], axis=1)
    row = lax.broadcasted_iota(jnp.int32, (sub, 1), 0)
    lane = lax.broadcasted_iota(jnp.int32, (sub, ROUTER_LANES), 1)

    for s in range(tm // sub):
        r0 = s * sub
        rows = slice(r0, r0 + sub)
        token = i * tm + r0 + row
        in_seq = token & (seq - 1) if seq & (seq - 1) == 0 else token % seq
        prev_row = (pp_ref[halo - 1:halo, :] if s == 0 else p_ref[r0 - 1:r0, :]).astype(F32)
        next_row = (pn_ref[0:1, :] if r0 + sub == tm else p_ref[r0 + sub:r0 + sub + 1, :]).astype(F32)
        prod = p_ref[rows, :].astype(F32)
        up = jnp.where(in_seq == 0, 0.0, jnp.where(row == 0, prev_row, pltpu.roll(prod, 1, 0)))
        dn = jnp.where(in_seq == seq - 1, 0.0, jnp.where(row == sub - 1, next_row, pltpu.roll(prod, sub - 1, 0)))
        u = up * cw_ref[0:1, :] + prod * cw_ref[1:2, :] + dn * cw_ref[2:3, :]
        y_conv = jnp.dot((cb_ref[rows, :].astype(F32) * u).astype(BF16), wco_ref[...],
                         preferred_element_type=F32)
        y_ret = jnp.dot(og_ref[rows, :], wro_ref[...], preferred_element_type=F32)
        merged = sa_ref[rows, :].astype(F32) * y_ret + sb_ref[rows, :].astype(F32) * y_conv
        m = jnp.dot(merged.astype(BF16), wo_ref[...], preferred_element_type=F32)
        x1 = x_ref[rows, :] + g1_ref[0] * _rms(m, npost_ref[...])
        x1_ref[rows, :] = x1
        h2 = _rms(x1, npre_ref[...]) * (1.0 + sc2_ref[0]) + sh2_ref[0]
        h_hi = h2.astype(BF16)
        h2_ref[rows, :] = h_hi
        h_lo = (h2 - h_hi.astype(F32)).astype(BF16)
        hi_terms = jnp.dot(h_hi, w_hi_lo, preferred_element_type=F32)
        logits = (hi_terms[:, 0:ROUTER_LANES] + hi_terms[:, ROUTER_LANES:2 * ROUTER_LANES]
                  + jnp.dot(h_lo, w_hi, preferred_element_type=F32))
        logits = jnp.where(lane < N_EXPERTS, logits, -jnp.inf)
        e = jnp.exp(logits - jnp.max(logits, axis=-1, keepdims=True))
        aff = e / jnp.sum(e, axis=-1, keepdims=True)
        aff_ref[:, rows] = aff.T[0:N_EXPERTS, :]


def _mix(og, proj_a, proj_p, proj_s, x, mod3, conv_w, wro, wco, wo, wr_pad, npost, npre, seq, cond_row_fn):
    t = x.shape[0]
    conv_b_block = (2 * N_HEADS * DK + 2 * N_HEADS * DV) // D_MODEL
    tm = MIX_TILE
    halo = 16
    hb = tm // halo
    last_halo = t // halo - 1
    col = lambda c: (lambda i: (i, c))
    row_vec = pl.BlockSpec((1, D_MODEL), lambda i: (0, 0))
    wspec = pl.BlockSpec((D_MODEL, D_MODEL), lambda i: (0, 0))
    modspec = lambda c: pl.BlockSpec((1, 1, D_MODEL), lambda i: (cond_row_fn(i * tm), 0, c))
    tile = lambda c: pl.BlockSpec((tm, D_MODEL), col(c))
    prev = lambda c: pl.BlockSpec((halo, D_MODEL), lambda i: (jnp.maximum(i * hb - 1, 0), c))
    nxt = lambda c: pl.BlockSpec((halo, D_MODEL), lambda i: (jnp.minimum((i + 1) * hb, last_halo), c))
    return pl.pallas_call(
        functools.partial(_mix_kernel, tm=tm, sub=MIX_SUB_TILE, seq=seq),
        grid=(t // tm,),
        in_specs=[tile(0), tile(conv_b_block), tile(0), prev(0), nxt(0), tile(0), tile(1),
                  tile(0),
                  pl.BlockSpec((3, D_MODEL), lambda i: (0, 0)), wspec, wspec, wspec,
                  pl.BlockSpec((D_MODEL, ROUTER_LANES), lambda i: (0, 0)),
                  row_vec, row_vec, modspec(2), modspec(3), modspec(4)],
        out_specs=[tile(0), tile(0), pl.BlockSpec((N_EXPERTS, tm), lambda i: (0, i))],
        out_shape=[jax.ShapeDtypeStruct((t, D_MODEL), F32),
                   jax.ShapeDtypeStruct((t, D_MODEL), BF16),
                   jax.ShapeDtypeStruct((N_EXPERTS, t), F32)],
        compiler_params=pltpu.CompilerParams(dimension_semantics=("arbitrary",),
                                             vmem_limit_bytes=VMEM_LIMIT),
        name="mix_out",
    )(og, proj_a, proj_p, proj_p, proj_p, proj_s, proj_s, x,
      conv_w, wro, wco, wo, wr_pad, npost, npre, mod3, mod3, mod3)


def _expert_kernel(xs_ref, wg_ref, wu_ref, wd_ref, o_ref, *, sub):
    wg = wg_ref[0].astype(BF16)
    wu = wu_ref[0].astype(BF16)
    wd = wd_ref[0].astype(BF16)
    for r in range(xs_ref.shape[1] // sub):
        rows = slice(r * sub, (r + 1) * sub)
        x = xs_ref[0, rows, :]
        g = jnp.dot(x, wg, preferred_element_type=F32)
        u = jnp.dot(x, wu, preferred_element_type=F32)
        hid = (g * _sigmoid(g) * u).astype(BF16)
        o_ref[0, rows, :] = jnp.dot(hid, wd, preferred_element_type=F32).astype(BF16)


def _experts(xs, wg, wu, wd):
    e, cap, d = xs.shape
    ff = wg.shape[2]
    tr = min(cap, EXPERT_ROWS_PER_STEP)
    return pl.pallas_call(
        functools.partial(_expert_kernel, sub=EXPERT_SUB_ROWS),
        grid=(e, cap // tr),
        in_specs=[pl.BlockSpec((1, tr, d), lambda a, r: (a, r, 0)),
                  pl.BlockSpec((1, d, ff), lambda a, r: (a, 0, 0)),
                  pl.BlockSpec((1, d, ff), lambda a, r: (a, 0, 0)),
                  pl.BlockSpec((1, ff, d), lambda a, r: (a, 0, 0))],
        out_specs=pl.BlockSpec((1, tr, d), lambda a, r: (a, r, 0)),
        out_shape=jax.ShapeDtypeStruct((e, cap, d), BF16),
        compiler_params=pltpu.CompilerParams(dimension_semantics=("arbitrary", "arbitrary"),
                                             vmem_limit_bytes=EXPERT_VMEM_LIMIT),
        name="experts",
    )(xs, wg, wu, wd)


def _block_ranks(mask):
    ne, nblk, lanes = mask.shape
    li = lax.broadcasted_iota(jnp.int32, (lanes, lanes), 0)
    lj = lax.broadcasted_iota(jnp.int32, (lanes, lanes), 1)
    incl_lanes = jnp.where(li <= lj, 1.0, 0.0).astype(BF16)
    all_lanes = jnp.ones((lanes, lanes), BF16)
    bi = lax.broadcasted_iota(jnp.int32, (nblk, nblk), 0)
    bj = lax.broadcasted_iota(jnp.int32, (nblk, nblk), 1)
    earlier_blocks = jnp.where(bj < bi, 1.0, 0.0).astype(BF16)
    m = mask.reshape(ne * nblk, lanes).astype(BF16)
    incl = jnp.dot(m, incl_lanes, preferred_element_type=F32).reshape(ne, nblk, lanes)
    tot = jnp.dot(m, all_lanes, preferred_element_type=F32).reshape(ne, nblk, lanes).astype(BF16)
    excl = jnp.stack([jnp.dot(earlier_blocks, tot[i], preferred_element_type=F32) for i in range(ne)])
    return excl, incl


def _select_kernel(aff_ref, sel_ref, *, cap):
    ne, nblk, lanes = aff_ref.shape
    aff = aff_ref[...]

    def count(mask):
        m = jnp.where(mask, 1.0, 0.0)
        return jnp.sum(jnp.sum(m, axis=1, keepdims=True), axis=2, keepdims=True)

    def as_float(bits):
        return lax.bitcast_convert_type(bits, F32)

    def bit_step(it, thr_bits):
        cand = thr_bits | jnp.left_shift(jnp.int32(1), 30 - it)
        return jnp.where(count(aff >= as_float(cand)) >= cap, cand, thr_bits)

    thr = as_float(lax.fori_loop(0, 31, bit_step, jnp.zeros((ne, 1, 1), jnp.int32)))
    gt = aff > thr
    eq = aff == thr
    need = cap - count(gt)
    ex_eq, in_eq = _block_ranks(jnp.where(eq, 1.0, 0.0))
    sel_ref[...] = jnp.where(gt | (eq & (ex_eq + in_eq - 1.0 < need)), 1.0, 0.0)


def _slot_kernel(sel_ref, pos_ref, excl_ref):
    sel = sel_ref[...]
    ex_sel, in_sel = _block_ranks(sel)
    pos_ref[...] = jnp.where(sel > 0.0, ex_sel + in_sel - 1.0, -1.0).astype(jnp.int32)
    excl_ref[...] = ex_sel.astype(jnp.int32)


def _to_tile_order(a, nb):
    if nb == 1:
        return a
    ne, t = a.shape
    r = ROUTE_TILE // nb
    return a.reshape(ne, nb, t // (nb * r), r).transpose(0, 2, 1, 3).reshape(ne, t)


def _route(aff_t, cap, nb):
    ne, t = aff_t.shape
    nblk = t // 128
    shape = (ne, nblk, 128)
    full = pl.BlockSpec(shape, lambda i: (0, 0, 0))
    params = pltpu.CompilerParams(dimension_semantics=("arbitrary",), vmem_limit_bytes=VMEM_LIMIT)
    sel = pl.pallas_call(
        functools.partial(_select_kernel, cap=cap),
        grid=(1,), in_specs=[full], out_specs=full,
        out_shape=jax.ShapeDtypeStruct(shape, F32),
        compiler_params=params, name="route_select",
    )(aff_t.reshape(shape))
    sel = _to_tile_order(sel.reshape(ne, t), nb)
    pos, excl = pl.pallas_call(
        _slot_kernel,
        grid=(1,), in_specs=[full], out_specs=[full, full],
        out_shape=[jax.ShapeDtypeStruct(shape, jnp.int32)] * 2,
        compiler_params=params, name="route_slots",
    )(sel.reshape(shape))
    base = jnp.concatenate([excl[:, :, 0], jnp.full((ne, 1), cap, jnp.int32)], axis=1)
    return pos.reshape(ne, t), base, _to_tile_order(aff_t, nb)


ROUTE_TILE = 256
ROUTE_WINDOW = 64
SLOT_ALIGN = 16
DISPATCH_SLOT_ROWS = 16384
DISPATCH_TILES_PER_STEP = 8


def _window(base_ref, e, blk, w, cap):
    lo = base_ref[e, blk] // SLOT_ALIGN + w * (ROUTE_WINDOW // SLOT_ALIGN)
    return lo * SLOT_ALIGN, jnp.minimum(lo, (cap - ROUTE_WINDOW) // SLOT_ALIGN) * SLOT_ALIGN


def _n_windows(base_ref, e, blk, cap):
    lo, _ = _window(base_ref, e, blk, 0, cap)
    end = base_ref[e, blk + ROUTE_TILE // 128]
    return (end - lo + ROUTE_WINDOW - 1) // ROUTE_WINDOW


def _dispatch_kernel(base_ref, h_ref, pos_ref, xs_ref, *, cap):
    g, j = pl.program_id(0), pl.program_id(1)
    group = xs_ref.shape[0]

    @pl.when(j == 0)
    def _():
        xs_ref[...] = jnp.zeros_like(xs_ref)

    ri = lax.broadcasted_iota(jnp.int32, (ROUTE_WINDOW, ROUTE_TILE), 0)

    def onehot(prow, lo, off, first):
        hit = prow - off == ri
        if not first:
            hit = hit & (prow >= lo)
        return jnp.where(hit, 1.0, 0.0).astype(BF16)

    def add_rows(ge, off, rows):
        sl = pl.ds(pl.multiple_of(off, SLOT_ALIGN), ROUTE_WINDOW)
        xs_ref[ge, sl, :] = xs_ref[ge, sl, :] + rows.astype(BF16)

    for sub in range(DISPATCH_TILES_PER_STEP):
        blk = (j * DISPATCH_TILES_PER_STEP + sub) * (ROUTE_TILE // 128)
        tok = slice(sub * ROUTE_TILE, (sub + 1) * ROUTE_TILE)
        h = h_ref[:, sub].reshape(ROUTE_TILE, h_ref.shape[-1])
        prows, offs = [], []
        for ge in range(group):
            e = g * group + ge
            prows.append(pos_ref[pl.ds(e, 1), tok])
            offs.append(_window(base_ref, e, blk, 0, cap))
        sel = jnp.concatenate([onehot(prows[ge], *offs[ge], True) for ge in range(group)], axis=0)
        rows = jnp.dot(sel, h, preferred_element_type=F32)
        for ge in range(group):
            add_rows(ge, offs[ge][1], rows[ge * ROUTE_WINDOW:(ge + 1) * ROUTE_WINDOW])

        for ge in range(group):
            e = g * group + ge

            def extra(w, carry, ge=ge, e=e, blk=blk, h=h, prow=prows[ge]):
                lo, off = _window(base_ref, e, blk, w, cap)
                add_rows(ge, off, jnp.dot(onehot(prow, lo, off, False), h, preferred_element_type=F32))
                return carry

            lax.fori_loop(1, _n_windows(base_ref, e, blk, cap), extra, 0)


def _tile_view(a, nb):
    t, d = a.shape
    r = ROUTE_TILE // nb
    return a.reshape(nb, t // (nb * r), r, d)


def _dispatch(base, h2, pos, cap, nb):
    t, d = h2.shape
    ne = pos.shape[0]
    step = ROUTE_TILE * DISPATCH_TILES_PER_STEP
    group = min(ne, DISPATCH_SLOT_ROWS // cap)
    out_spec = pl.BlockSpec((group, cap, d), lambda g, j, b: (g, 0, 0), pipeline_mode=pl.Buffered(1))
    return pl.pallas_call(
        functools.partial(_dispatch_kernel, cap=cap),
        grid_spec=pltpu.PrefetchScalarGridSpec(
            num_scalar_prefetch=1,
            grid=(ne // group, t // step),
            in_specs=[pl.BlockSpec((nb, DISPATCH_TILES_PER_STEP, ROUTE_TILE // nb, d),
                                   lambda g, j, b: (0, j, 0, 0)),
                      pl.BlockSpec((ne, step), lambda g, j, b: (0, j))],
            out_specs=out_spec),
        out_shape=jax.ShapeDtypeStruct((ne, cap, d), BF16),
        compiler_params=pltpu.CompilerParams(dimension_semantics=("arbitrary", "arbitrary"),
                                             vmem_limit_bytes=VMEM_LIMIT),
        name="dispatch",
    )(base, _tile_view(h2, nb), pos)


def _combine_kernel(base_ref, pos_ref, aff_ref, x1_ref, g_ref, g2_ref, eo_ref, o_ref,
                    f_ref, win_buf, win_sem, extra_buf, extra_sem, *, cap, n_tiles):
    j = pl.program_id(0)
    blocks_per_tile = ROUTE_TILE // 128
    blk = j * blocks_per_tile
    slot = lax.rem(j, 2)
    ri = lax.broadcasted_iota(jnp.int32, (ROUTE_WINDOW, ROUTE_TILE), 0)
    tn = (((0,), (0,)), ((), ()))

    def window_copy(e, off, dst, sem):
        return pltpu.make_async_copy(eo_ref.at[e, pl.ds(pl.multiple_of(off, SLOT_ALIGN), ROUTE_WINDOW), :],
                                     dst, sem)

    def first_window_copies(tile, buf_slot):
        return [window_copy(e, _window(base_ref, e, tile * blocks_per_tile, 0, cap)[1],
                            win_buf.at[buf_slot, pl.ds(e * ROUTE_WINDOW, ROUTE_WINDOW), :], win_sem.at[buf_slot])
                for e in range(N_EXPERTS)]

    def wait_windows(buf_slot):
        for e in range(N_EXPERTS):
            window_copy(e, 0, win_buf.at[buf_slot, pl.ds(e * ROUTE_WINDOW, ROUTE_WINDOW), :],
                        win_sem.at[buf_slot]).wait()

    @pl.when(j == 0)
    def _():
        for cp in first_window_copies(0, 0):
            cp.start()

    for cp in first_window_copies(jnp.minimum(j + 1, n_tiles - 1), 1 - slot):
        cp.start()

    def weights(e, lo, off, first):
        prow = pos_ref[e:e + 1, :]
        hit = prow - off == ri
        if not first:
            hit = hit & (prow >= lo)
        return jnp.where(hit, aff_ref[e:e + 1, :], 0.0).astype(BF16)

    q = jnp.concatenate([weights(e, *_window(base_ref, e, blk, 0, cap), True) for e in range(N_EXPERTS)], axis=0)
    wait_windows(slot)
    f_ref[...] = lax.dot_general(q, win_buf[slot], tn, preferred_element_type=F32)

    n_win = [_n_windows(base_ref, e, blk, cap) for e in range(N_EXPERTS)]

    @pl.when(functools.reduce(jnp.maximum, n_win) > 1)
    def _():
        for e in range(N_EXPERTS):
            def extra(w, carry, e=e):
                lo, off = _window(base_ref, e, blk, w, cap)
                cp = window_copy(e, off, extra_buf, extra_sem)
                cp.start()
                cp.wait()
                f_ref[...] += lax.dot_general(weights(e, lo, off, False), extra_buf[...], tn,
                                              preferred_element_type=F32)
                return carry

            lax.fori_loop(1, n_win[e], extra, 0)

    nb, _, r, d = x1_ref.shape
    f = f_ref[...].reshape(nb, r, d)
    o_ref[:, 0] = x1_ref[:, 0] + g2_ref[...] * _rms(f, g_ref[...])

    @pl.when(j == n_tiles - 1)
    def _():
        wait_windows(1 - slot)


def _combine(base, eo, pos, aff, x1, mod3, npost, cond_block, cap, nb):
    t, d = x1.shape
    ne = pos.shape[0]
    n_tiles = t // ROUTE_TILE
    tile = pl.BlockSpec((nb, 1, ROUTE_TILE // nb, d), lambda j, b: (0, j, 0, 0))
    etile = pl.BlockSpec((ne, ROUTE_TILE), lambda j, b: (0, j))
    return pl.pallas_call(
        functools.partial(_combine_kernel, cap=cap, n_tiles=n_tiles),
        grid_spec=pltpu.PrefetchScalarGridSpec(
            num_scalar_prefetch=1,
            grid=(n_tiles,),
            in_specs=[etile, etile, tile,
                      pl.BlockSpec((1, d), lambda j, b: (0, 0)),
                      pl.BlockSpec((nb, 1, d), lambda j, b: (cond_block, 0, 5)),
                      pl.BlockSpec(memory_space=pl.ANY)],
            out_specs=tile,
            scratch_shapes=[pltpu.VMEM((ROUTE_TILE, d), F32),
                            pltpu.VMEM((2, ne * ROUTE_WINDOW, d), BF16), pltpu.SemaphoreType.DMA((2,)),
                            pltpu.VMEM((ROUTE_WINDOW, d), BF16), pltpu.SemaphoreType.DMA(())]),
        out_shape=jax.ShapeDtypeStruct(_tile_view(x1, nb).shape, F32),
        compiler_params=pltpu.CompilerParams(dimension_semantics=("arbitrary",),
                                             vmem_limit_bytes=VMEM_LIMIT),
        name="combine",
    )(base, pos, aff, _tile_view(x1, nb), npost, mod3, eo).reshape(t, d)


def _trunk(x3, mod3, first_cond_row, shared_cond, weights, lgs, *, rope_tabs, init, emit_state):
    (n_pre_mix, n_post_mix, n_pre_ffn, n_post_ffn, w_in, gn_g, conv_w, wro, wco, wo, wr_pad, wg, wu, wd) = weights
    b, seq, d = x3.shape
    t = b * seq
    x = x3.reshape(t, d)
    cond_row_fn = (lambda r: first_cond_row) if shared_cond else (lambda r: first_cond_row + r // seq)
    nb = 1 if shared_cond else b
    assert ROUTE_TILE % nb == 0 and (ROUTE_TILE // nb) % SLOT_ALIGN == 0 and first_cond_row % nb == 0
    proj_a, proj_p, proj_s = _inproj(x, mod3, n_pre_mix, w_in, cond_row_fn, seq, rope_tabs)
    ret = _retention(proj_a, lgs, gn_g, seq, init=init, emit_state=emit_state)
    og = ret[0]
    x1, h2, aff_t = _mix(og, proj_a, proj_p, proj_s, x, mod3, conv_w, wro, wco, wo, wr_pad, n_post_mix,
                         n_pre_ffn, seq, cond_row_fn)
    cap = CAPACITY_FACTOR * t // N_EXPERTS
    pos, base, aff = _route(aff_t, cap, nb)
    eo = _experts(_dispatch(base, h2, pos, cap, nb), wg, wu, wd)
    y = _combine(base, eo, pos, aff, x1, mod3, n_post_ffn, first_cond_row // nb, cap, nb)
    return y.reshape(b, seq, d), ret[1:]


def kernel(x_prompt, x_sample, state_ret_fwd, state_ret_bwd, c, c_ctx, w_ada, b_ada, norm_pre_mix, norm_post_mix,
           norm_pre_ffn, norm_post_ffn, w_in, ret_decay_fwd, ret_decay_bwd, ret_norm_g, conv_w, w_ret_o, w_conv_o,
           w_o, w_router, w_gate, w_up, w_down):
    depth = w_ada.shape[0]
    assert depth == 1
    dec_b, dec_seq = x_sample.shape[0], x_sample.shape[1]
    xp, xs = x_prompt, x_sample
    l = 0
    cond = jnp.zeros((N_COND_ROWS, D_MODEL), F32).at[0:dec_b].set(c).at[CTX_COND_ROW].set(c_ctx)
    mod3 = _ada(cond, w_ada[l], b_ada[l][None, :]).reshape(N_COND_ROWS, 1, 6 * D_MODEL)
    lgs = jnp.stack([-jax.nn.softplus(-ret_decay_fwd[l].astype(F32)),
                     -jax.nn.softplus(-ret_decay_bwd[l].astype(F32))])
    wr_pad = jnp.pad(w_router[l], ((0, 0), (0, ROUTER_LANES - N_EXPERTS)))
    weights = (norm_pre_mix[l][None, :], norm_post_mix[l][None, :], norm_pre_ffn[l][None, :],
               norm_post_ffn[l][None, :], w_in[l].astype(BF16), ret_norm_g[l][None, :], conv_w[l],
               w_ret_o[l].astype(BF16), w_conv_o[l].astype(BF16), w_o[l].astype(BF16), wr_pad,
               w_gate[l], w_up[l], w_down[l])
    yp, (s_f, s_b) = _trunk(xp, mod3, CTX_COND_ROW, True, weights, lgs,
                            rope_tabs=None, init=None, emit_state=True)
    ys, _ = _trunk(xs, mod3, 0, False, weights, lgs,
                   rope_tabs=_rope_tables(dec_seq), init=(state_ret_fwd, state_ret_bwd), emit_state=False)
    return (yp, ys, s_f, s_b)
```

```python
import functools

import jax
import jax.numpy as jnp
import numpy as np
from jax import lax
from jax.experimental import pallas as pl
from jax.experimental.pallas import tpu as pltpu

F32 = jnp.float32
BF16 = jnp.bfloat16

D_MODEL = 1024
N_HEADS = 8
DK = 64
DV = 128
CHUNK = 128
GRID_W = 64
N_EXPERTS = 16
CAPACITY_FACTOR = 2
D_IN_TOTAL = 8192
RMS_EPS = 1e-6
GN_EPS = 1e-5
ROPE_BASE = 10000.0
N_COND_ROWS = 16
CTX_COND_ROW = 8
ROUTER_LANES = 128
RET_CHUNKS_PER_STEP = 16
RET_UNROLL = 16
MIX_TILE = 1024
MIX_SUB_TILE = 512
VMEM_LIMIT = 48 * 1024 * 1024
EXPERT_ROWS_PER_STEP = 1024
EXPERT_SUB_ROWS = 512
EXPERT_VMEM_LIMIT = 56 * 1024 * 1024


def _sigmoid(x):
    return 0.5 * jnp.tanh(0.5 * x) + 0.5


def _rms(x, g):
    return x * lax.rsqrt(jnp.mean(x * x, axis=-1, keepdims=True) + RMS_EPS) * g


def _ada_kernel(c_ref, w_ref, b_ref, o_ref):
    c = c_ref[...]
    s = c * _sigmoid(c)
    o_ref[...] = jnp.dot(s, w_ref[...], preferred_element_type=F32,
                         precision=lax.Precision.HIGHEST) + b_ref[...]


def _ada(cond, w_ada, b_ada):
    n = w_ada.shape[1]
    tn = 1024
    return pl.pallas_call(
        _ada_kernel,
        grid=(n // tn,),
        in_specs=[pl.BlockSpec((N_COND_ROWS, D_MODEL), lambda j: (0, 0)),
                  pl.BlockSpec((D_MODEL, tn), lambda j: (0, j)),
                  pl.BlockSpec((1, tn), lambda j: (0, j))],
        out_specs=pl.BlockSpec((N_COND_ROWS, tn), lambda j: (0, j)),
        out_shape=jax.ShapeDtypeStruct((N_COND_ROWS, n), F32),
        compiler_params=pltpu.CompilerParams(dimension_semantics=("arbitrary",),
                                             vmem_limit_bytes=VMEM_LIMIT),
        name="ada_mod",
    )(cond, w_ada, b_ada)


def _inproj_body(x_ref, g_ref, sh_ref, sc_ref, w_ref, cos_ref, sin_ref, a_ref, p_ref, s_ref, h_ref, use_rope):
    j = pl.program_id(1)

    @pl.when(j == 0)
    def _():
        h = _rms(x_ref[...], g_ref[...]) * (1.0 + sc_ref[0]) + sh_ref[0]
        h_ref[...] = h.astype(BF16)

    lanes = 2 * DK
    width = 2 * lanes
    n_slices = w_ref.shape[1] // width
    n_qk = 2 * N_HEADS * DK

    def slice_dot(b):
        return jnp.dot(h_ref[...], w_ref[:, b * width:(b + 1) * width], preferred_element_type=F32)

    @pl.when(j == 0)
    def _():
        ci = lax.broadcasted_iota(jnp.int32, (h_ref.shape[0], lanes), 1)
        for b in range(n_slices):
            acc = slice_dot(b)
            if b * width >= n_qk:
                a_ref[:, b * width:(b + 1) * width] = acc.astype(BF16)
                continue
            for half in range(2):
                col = b * width + half * lanes
                x = acc[:, half * lanes:(half + 1) * lanes]
                if col >= N_HEADS * DK:
                    x = x * (DK ** -0.5)
                if use_rope:
                    swapped = jnp.where((ci & 31) < 16, pltpu.roll(x, lanes - 16, 1), pltpu.roll(x, 16, 1))
                    x = x * cos_ref[...] + swapped * sin_ref[...]
                a_ref[:, col:col + lanes] = x.astype(BF16)

    @pl.when(j == 1)
    def _():
        for b in range(n_slices):
            acc = slice_dot(b)
            if b < n_slices // 2:
                acc = acc * _sigmoid(acc)
            a_ref[:, b * width:(b + 1) * width] = acc.astype(BF16)

    @pl.when(j == 2)
    def _():
        for b in range(n_slices // 2):
            p_ref[:, b * width:(b + 1) * width] = (slice_dot(b) * slice_dot(b + n_slices // 2)).astype(BF16)

    @pl.when(j == 3)
    def _():
        for b in range(n_slices):
            s_ref[:, b * width:(b + 1) * width] = _sigmoid(slice_dot(b)).astype(BF16)


def _inproj_kernel(x_ref, g_ref, sh_ref, sc_ref, w_ref, *rest, use_rope):
    if use_rope:
        cos_ref, sin_ref, a_ref, p_ref, s_ref, h_ref = rest
    else:
        cos_ref = sin_ref = None
        a_ref, p_ref, s_ref, h_ref = rest
    _inproj_body(x_ref, g_ref, sh_ref, sc_ref, w_ref, cos_ref, sin_ref, a_ref, p_ref, s_ref, h_ref, use_rope)


def _inproj(x, mod3, g, w_bf16, cond_row_fn, seq, rope_tabs):
    t = x.shape[0]
    tm, tn = 1024, 2048
    assert D_IN_TOTAL == 4 * tn
    in_specs = [pl.BlockSpec((tm, D_MODEL), lambda i, j: (i, 0)),
                pl.BlockSpec((1, D_MODEL), lambda i, j: (0, 0)),
                pl.BlockSpec((1, 1, D_MODEL), lambda i, j: (cond_row_fn(i * tm), 0, 0)),
                pl.BlockSpec((1, 1, D_MODEL), lambda i, j: (cond_row_fn(i * tm), 0, 1)),
                pl.BlockSpec((D_MODEL, tn), lambda i, j: (0, j))]
    args = [x, g, mod3, mod3, w_bf16]
    if rope_tabs is not None:
        tiles_per_seq = seq // tm
        in_specs += [pl.BlockSpec((tm, 2 * DK), lambda i, j: (i % tiles_per_seq, 0))] * 2
        args += list(rope_tabs)
    return pl.pallas_call(
        functools.partial(_inproj_kernel, use_rope=rope_tabs is not None),
        grid=(t // tm, D_IN_TOTAL // tn),
        in_specs=in_specs,
        out_specs=[pl.BlockSpec((tm, tn), lambda i, j: (i, jnp.minimum(j, 1))),
                   pl.BlockSpec((tm, tn // 2), lambda i, j: (i, 0)),
                   pl.BlockSpec((tm, tn), lambda i, j: (i, 0))],
        out_shape=[jax.ShapeDtypeStruct((t, 2 * tn), BF16),
                   jax.ShapeDtypeStruct((t, tn // 2), BF16),
                   jax.ShapeDtypeStruct((t, tn), BF16)],
        scratch_shapes=[pltpu.VMEM((tm, D_MODEL), BF16)],
        compiler_params=pltpu.CompilerParams(dimension_semantics=("arbitrary", "arbitrary"),
                                             vmem_limit_bytes=VMEM_LIMIT),
        name="inproj",
    )(*args)


def _ret_kernel(*refs, nc, bb, has_init, emit_state):
    it = iter(refs)
    lg_ref = next(it)
    q_ref, k_ref, v_ref, gr_ref, gn_ref = (next(it) for _ in range(5))
    s0f_ref = s0b_ref = sf_ref = sb_ref = None
    if has_init:
        s0f_ref, s0b_ref = next(it), next(it)
    o_ref = next(it)
    if emit_state:
        sf_ref, sb_ref = next(it), next(it)
    kvf_s, kvb_s, rcat_s, p_s, o_s, r0 = (next(it) for _ in range(6))

    p = pl.program_id(1)
    lgf_a, lgf_b = lg_ref[0, 2 * p], lg_ref[0, 2 * p + 1]
    lgb_a, lgb_b = lg_ref[1, 2 * p], lg_ref[1, 2 * p + 1]

    ri = lax.broadcasted_iota(jnp.int32, (CHUNK, 2 * DK), 0)
    ci = lax.broadcasted_iota(jnp.int32, (CHUNK, 2 * DK), 1)
    lane_a = ci < DK
    rowf = ri.astype(F32)
    diff = rowf - ci.astype(F32)
    lgf_lane = jnp.where(lane_a, lgf_a, lgf_b)
    lgb_lane = jnp.where(lane_a, lgb_a, lgb_b)
    xi_f = jnp.exp(lgf_lane * (rowf + 1.0))
    xi_b = jnp.exp(lgb_lane * (CHUNK - rowf))
    zeta_f = jnp.exp(lgf_lane * (CHUNK - 1.0 - rowf))
    zeta_b = jnp.exp(lgb_lane * rowf)

    def decay_matrix(lgf, lgb):
        return jnp.where(diff > 0, jnp.exp(lgf * diff),
                         jnp.where(diff < 0, jnp.exp(lgb * (-diff)), 2.0))

    dm_a = decay_matrix(lgf_a, lgb_a)
    dm_b = decay_matrix(lgf_b, lgb_b)

    r2 = lax.broadcasted_iota(jnp.int32, (2 * DK, 2 * DV), 0)
    c2 = lax.broadcasted_iota(jnp.int32, (2 * DK, 2 * DV), 1)
    top = r2 < DK
    blk = (top == (c2 < DV)).astype(F32)
    cd_f = jnp.exp(jnp.where(top, lgf_a, lgf_b) * float(CHUNK)) * blk
    cd_b = jnp.exp(jnp.where(top, lgb_a, lgb_b) * float(CHUNK)) * blk

    def chunk_rows(c):
        return pl.ds(pl.multiple_of(c * CHUNK, CHUNK), CHUNK)

    n_chunks = bb * nc
    unroll = RET_UNROLL

    def kv_body(c, carry):
        rows = chunk_rows(c)
        k = k_ref[rows, :].astype(F32)
        kz_t = jnp.concatenate([k * zeta_f, k * zeta_b], axis=1).T.astype(BF16)
        kv = jnp.dot(kz_t, v_ref[rows, :], preferred_element_type=F32)
        kvf_s[c] = kv[0:2 * DK] * blk
        kvb_s[c] = kv[2 * DK:4 * DK] * blk
        return carry

    def score_body(c, carry):
        rows = chunk_rows(c)
        q16 = q_ref[rows, :]
        zero = jnp.zeros_like(q16)
        q_ab = jnp.concatenate([jnp.where(lane_a, q16, zero), jnp.where(lane_a, zero, q16)], axis=0)
        s = lax.dot_general(q_ab, k_ref[rows, :], (((1,), (1,)), ((), ())), preferred_element_type=F32)
        p_s[c, :, 0:CHUNK] = (s[0:CHUNK] * dm_a).astype(BF16)
        p_s[c, :, CHUNK:2 * CHUNK] = (s[CHUNK:2 * CHUNK] * dm_b).astype(BF16)
        return carry

    lax.fori_loop(0, n_chunks, lambda c, carry: score_body(c, kv_body(c, carry)), 0, unroll=unroll)

    def load_state(s_ref, s):
        r0[...] = jnp.zeros_like(r0)
        r0[0:DK, 0:DV] = s_ref[s, 0, 0].astype(F32)
        r0[DK:2 * DK, DV:2 * DV] = s_ref[s, 0, 1].astype(F32)
        return r0[...]

    for s in range(bb):
        def fwd_body(n, rf, s=s):
            c = s * nc + n
            rcat_s[c, 0:2 * DK, :] = rf.astype(BF16)
            return cd_f * rf + kvf_s[c]

        def bwd_body(t, rb, s=s):
            c = s * nc + nc - 1 - t
            rcat_s[c, 2 * DK:4 * DK, :] = rb.astype(BF16)
            return cd_b * rb + kvb_s[c]

        zeros = jnp.zeros((2 * DK, 2 * DV), F32)
        rf_fin = lax.fori_loop(0, nc, fwd_body, load_state(s0f_ref, s) if has_init else zeros)
        rb_fin = lax.fori_loop(0, nc, bwd_body, load_state(s0b_ref, s) if has_init else zeros)
        if emit_state:
            for s_ref, r in ((sf_ref, rf_fin), (sb_ref, rb_fin)):
                s_ref[s, 0, 0] = r[0:DK, 0:DV]
                s_ref[s, 0, 1] = r[DK:2 * DK, DV:2 * DV]

    gn = gn_ref[...]

    hr = lax.broadcasted_iota(jnp.int32, (2 * DV, 2 * DV), 0)
    hc = lax.broadcasted_iota(jnp.int32, (2 * DV, 2 * DV), 1)
    head_mean = jnp.where((hr < DV) == (hc < DV), 1.0 / DV, 0.0).astype(BF16)
    v_lane_a = lax.broadcasted_iota(jnp.int32, (CHUNK, 2 * DV), 1) < DV

    def value_body(c, carry):
        rows = chunk_rows(c)
        q = q_ref[rows, :].astype(F32)
        v = v_ref[rows, :]
        zero = jnp.zeros_like(v)
        v_bd = jnp.concatenate([jnp.where(v_lane_a, v, zero), jnp.where(v_lane_a, zero, v)], axis=0)
        qx = jnp.concatenate([(q * xi_f).astype(BF16), (q * xi_b).astype(BF16)], axis=1)
        o_s[c] = (jnp.dot(p_s[c], v_bd, preferred_element_type=F32)
                  + jnp.dot(qx, rcat_s[c], preferred_element_type=F32))
        return carry

    lax.fori_loop(0, n_chunks, value_body, 0, unroll=unroll)

    def gate_body(c, carry):
        rows = chunk_rows(c)
        o = o_s[c]
        d = o - jnp.dot(o.astype(BF16), head_mean, preferred_element_type=F32)
        var = jnp.dot((d * d).astype(BF16), head_mean, preferred_element_type=F32)
        y = d * lax.rsqrt(var + GN_EPS)
        o_ref[rows, :] = (gr_ref[rows, :].astype(F32) * (y * gn)).astype(BF16)
        return carry

    lax.fori_loop(0, n_chunks, gate_body, 0, unroll=unroll)


def _retention(proj, lgs, gn_g, seq, *, init=None, emit_state):
    t = proj.shape[0]
    b = t // seq
    nc = seq // CHUNK
    bb = max(1, RET_CHUNKS_PER_STEP // nc)
    rows = bb * seq
    has_init = init is not None
    pairs = N_HEADS // 2
    qk_blocks = (N_HEADS * DK) // (2 * DK)
    in_specs = [pl.BlockSpec(memory_space=pltpu.SMEM),
                pl.BlockSpec((rows, 2 * DK), lambda i, p: (i, p)),
                pl.BlockSpec((rows, 2 * DK), lambda i, p: (i, qk_blocks + p)),
                pl.BlockSpec((rows, 2 * DV), lambda i, p: (i, qk_blocks + p)),
                pl.BlockSpec((rows, 2 * DV), lambda i, p: (i, 2 * qk_blocks + p)),
                pl.BlockSpec((1, 2 * DV), lambda i, p: (0, p))]
    args = [lgs, proj, proj, proj, proj, gn_g]
    state_spec = pl.BlockSpec((bb, 1, 2, DK, DV), lambda i, p: (i, 0, p, 0, 0))
    if has_init:
        in_specs += [state_spec, state_spec]
        args += list(init)
    out_specs = [pl.BlockSpec((rows, 2 * DV), lambda i, p: (i, p))]
    out_shape = [jax.ShapeDtypeStruct((t, N_HEADS * DV), BF16)]
    if emit_state:
        out_specs += [state_spec, state_spec]
        out_shape += [jax.ShapeDtypeStruct((b, 1, N_HEADS, DK, DV), F32)] * 2
    return pl.pallas_call(
        functools.partial(_ret_kernel, nc=nc, bb=bb, has_init=has_init, emit_state=emit_state),
        grid=(b // bb, pairs),
        in_specs=in_specs,
        out_specs=out_specs,
        out_shape=out_shape,
        scratch_shapes=[pltpu.VMEM((bb * nc, 2 * DK, 2 * DV), F32),
                        pltpu.VMEM((bb * nc, 2 * DK, 2 * DV), F32),
                        pltpu.VMEM((bb * nc, 4 * DK, 2 * DV), BF16),
                        pltpu.VMEM((bb * nc, CHUNK, 2 * CHUNK), BF16),
                        pltpu.VMEM((bb * nc, CHUNK, 2 * DV), F32),
                        pltpu.VMEM((2 * DK, 2 * DV), F32)],
        compiler_params=pltpu.CompilerParams(dimension_semantics=("arbitrary", "arbitrary"),
                                             vmem_limit_bytes=VMEM_LIMIT),
        name="retention",
    )(*args)


def _rope_tables(seq):
    n_freq = DK // 4
    pos = np.arange(seq)
    inv = jnp.asarray(ROPE_BASE, F32) ** (-jnp.arange(n_freq, dtype=F32) / n_freq)
    ang_r = jnp.asarray(pos // GRID_W, F32)[:, None] * inv
    ang_c = jnp.asarray(pos % GRID_W, F32)[:, None] * inv
    cos = jnp.concatenate([jnp.cos(ang_r)] * 2 + [jnp.cos(ang_c)] * 2, axis=1)
    sin = jnp.concatenate([-jnp.sin(ang_r), jnp.sin(ang_r), -jnp.sin(ang_c), jnp.sin(ang_c)], axis=1)
    return jnp.tile(cos, (1, 2)), jnp.tile(sin, (1, 2))


def _mix_kernel(og_ref, cb_ref, p_ref, pp_ref, pn_ref, sa_ref, sb_ref, x_ref,
                cw_ref, wro_ref, wco_ref, wo_ref, wr_ref, npost_ref, npre_ref, g1_ref, sh2_ref, sc2_ref,
                x1_ref, h2_ref, aff_ref, *, tm, sub, seq):
    i = pl.program_id(0)
    halo = pp_ref.shape[0]
    wr = wr_ref[...]
    w_hi = wr.astype(BF16)
    w_lo = (wr - w_hi.astype(F32)).astype(BF16)
    w_hi_lo = jnp.concatenate([w_hi, w_lo], axis=1)
    row = lax.broadcasted_iota(jnp.int32, (sub, 1), 0)
    lane = lax.broadcasted_iota(jnp.int32, (sub, ROUTER_LANES), 1)

    for s in range(tm // sub):
        r0 = s * sub
        rows = slice(r0, r0 + sub)
        token = i * tm + r0 + row
        in_seq = token & (seq - 1) if seq & (seq - 1) == 0 else token % seq
        prev_row = (pp_ref[halo - 1:halo, :] if s == 0 else p_ref[r0 - 1:r0, :]).astype(F32)
        next_row = (pn_ref[0:1, :] if r0 + sub == tm else p_ref[r0 + sub:r0 + sub + 1, :]).astype(F32)
        prod = p_ref[rows, :].astype(F32)
        up = jnp.where(in_seq == 0, 0.0, jnp.where(row == 0, prev_row, pltpu.roll(prod, 1, 0)))
        dn = jnp.where(in_seq == seq - 1, 0.0, jnp.where(row == sub - 1, next_row, pltpu.roll(prod, sub - 1, 0)))
        u = up * cw_ref[0:1, :] + prod * cw_ref[1:2, :] + dn * cw_ref[2:3, :]
        y_conv = jnp.dot((cb_ref[rows, :].astype(F32) * u).astype(BF16), wco_ref[...],
                         preferred_element_type=F32)
        y_ret = jnp.dot(og_ref[rows, :], wro_ref[...], preferred_element_type=F32)
        merged = sa_ref[rows, :].astype(F32) * y_ret + sb_ref[rows, :].astype(F32) * y_conv
        m = jnp.dot(merged.astype(BF16), wo_ref[...], preferred_element_type=F32)
        x1 = x_ref[rows, :] + g1_ref[0] * _rms(m, npost_ref[...])
        x1_ref[rows, :] = x1
        h2 = _rms(x1, npre_ref[...]) * (1.0 + sc2_ref[0]) + sh2_ref[0]
        h_hi = h2.astype(BF16)
        h2_ref[rows, :] = h_hi
        h_lo = (h2 - h_hi.astype(F32)).astype(BF16)
        hi_terms = jnp.dot(h_hi, w_hi_lo, preferred_element_type=F32)
        logits = (hi_terms[:, 0:ROUTER_LANES] + hi_terms[:, ROUTER_LANES:2 * ROUTER_LANES]
                  + jnp.dot(h_lo, w_hi, preferred_element_type=F32))
        logits = jnp.where(lane < N_EXPERTS, logits, -jnp.inf)
        e = jnp.exp(logits - jnp.max(logits, axis=-1, keepdims=True))
        aff = e / jnp.sum(e, axis=-1, keepdims=True)
        aff_ref[:, rows] = aff.T[0:N_EXPERTS, :]


def _mix(og, proj_a, proj_p, proj_s, x, mod3, conv_w, wro, wco, wo, wr_pad, npost, npre, seq, cond_row_fn):
    t = x.shape[0]
    conv_b_block = (2 * N_HEADS * DK + 2 * N_HEADS * DV) // D_MODEL
    tm = MIX_TILE
    halo = 16
    hb = tm // halo
    last_halo = t // halo - 1
    col = lambda c: (lambda i: (i, c))
    row_vec = pl.BlockSpec((1, D_MODEL), lambda i: (0, 0))
    wspec = pl.BlockSpec((D_MODEL, D_MODEL), lambda i: (0, 0))
    modspec = lambda c: pl.BlockSpec((1, 1, D_MODEL), lambda i: (cond_row_fn(i * tm), 0, c))
    tile = lambda c: pl.BlockSpec((tm, D_MODEL), col(c))
    prev = lambda c: pl.BlockSpec((halo, D_MODEL), lambda i: (jnp.maximum(i * hb - 1, 0), c))
    nxt = lambda c: pl.BlockSpec((halo, D_MODEL), lambda i: (jnp.minimum((i + 1) * hb, last_halo), c))
    return pl.pallas_call(
        functools.partial(_mix_kernel, tm=tm, sub=MIX_SUB_TILE, seq=seq),
        grid=(t // tm,),
        in_specs=[tile(0), tile(conv_b_block), tile(0), prev(0), nxt(0), tile(0), tile(1),
                  tile(0),
                  pl.BlockSpec((3, D_MODEL), lambda i: (0, 0)), wspec, wspec, wspec,
                  pl.BlockSpec((D_MODEL, ROUTER_LANES), lambda i: (0, 0)),
                  row_vec, row_vec, modspec(2), modspec(3), modspec(4)],
        out_specs=[tile(0), tile(0), pl.BlockSpec((N_EXPERTS, tm), lambda i: (0, i))],
        out_shape=[jax.ShapeDtypeStruct((t, D_MODEL), F32),
                   jax.ShapeDtypeStruct((t, D_MODEL), BF16),
                   jax.ShapeDtypeStruct((N_EXPERTS, t), F32)],
        compiler_params=pltpu.CompilerParams(dimension_semantics=("arbitrary",),
                                             vmem_limit_bytes=VMEM_LIMIT),
        name="mix_out",
    )(og, proj_a, proj_p, proj_p, proj_p, proj_s, proj_s, x,
      conv_w, wro, wco, wo, wr_pad, npost, npre, mod3, mod3, mod3)


def _expert_kernel(xs_ref, wg_ref, wu_ref, wd_ref, o_ref, *, sub):
    wg = wg_ref[0].astype(BF16)
    wu = wu_ref[0].astype(BF16)
    wd = wd_ref[0].astype(BF16)
    for r in range(xs_ref.shape[1] // sub):
        rows = slice(r * sub, (r + 1) * sub)
        x = xs_ref[0, rows, :]
        g = jnp.dot(x, wg, preferred_element_type=F32)
        u = jnp.dot(x, wu, preferred_element_type=F32)
        hid = (g * _sigmoid(g) * u).astype(BF16)
        o_ref[0, rows, :] = jnp.dot(hid, wd, preferred_element_type=F32).astype(BF16)


def _experts(xs, wg, wu, wd):
    e, cap, d = xs.shape
    ff = wg.shape[2]
    tr = min(cap, EXPERT_ROWS_PER_STEP)
    return pl.pallas_call(
        functools.partial(_expert_kernel, sub=EXPERT_SUB_ROWS),
        grid=(e, cap // tr),
        in_specs=[pl.BlockSpec((1, tr, d), lambda a, r: (a, r, 0)),
                  pl.BlockSpec((1, d, ff), lambda a, r: (a, 0, 0)),
                  pl.BlockSpec((1, d, ff), lambda a, r: (a, 0, 0)),
                  pl.BlockSpec((1, ff, d), lambda a, r: (a, 0, 0))],
        out_specs=pl.BlockSpec((1, tr, d), lambda a, r: (a, r, 0)),
        out_shape=jax.ShapeDtypeStruct((e, cap, d), BF16),
        compiler_params=pltpu.CompilerParams(dimension_semantics=("arbitrary", "arbitrary"),
                                             vmem_limit_bytes=EXPERT_VMEM_LIMIT),
        name="experts",
    )(xs, wg, wu, wd)


def _block_ranks(mask):
    ne, nblk, lanes = mask.shape
    li = lax.broadcasted_iota(jnp.int32, (lanes, lanes), 0)
    lj = lax.broadcasted_iota(jnp.int32, (lanes, lanes), 1)
    incl_lanes = jnp.where(li <= lj, 1.0, 0.0).astype(BF16)
    all_lanes = jnp.ones((lanes, lanes), BF16)
    bi = lax.broadcasted_iota(jnp.int32, (nblk, nblk), 0)
    bj = lax.broadcasted_iota(jnp.int32, (nblk, nblk), 1)
    earlier_blocks = jnp.where(bj < bi, 1.0, 0.0).astype(BF16)
    m = mask.reshape(ne * nblk, lanes).astype(BF16)
    incl = jnp.dot(m, incl_lanes, preferred_element_type=F32).reshape(ne, nblk, lanes)
    tot = jnp.dot(m, all_lanes, preferred_element_type=F32).reshape(ne, nblk, lanes).astype(BF16)
    excl = jnp.stack([jnp.dot(earlier_blocks, tot[i], preferred_element_type=F32) for i in range(ne)])
    return excl, incl


def _select_kernel(aff_ref, sel_ref, *, cap):
    ne, nblk, lanes = aff_ref.shape
    aff = aff_ref[...]

    def count(mask):
        m = jnp.where(mask, 1.0, 0.0)
        return jnp.sum(jnp.sum(m, axis=1, keepdims=True), axis=2, keepdims=True)

    def as_float(bits):
        return lax.bitcast_convert_type(bits, F32)

    def bit_step(it, thr_bits):
        cand = thr_bits | jnp.left_shift(jnp.int32(1), 30 - it)
        return jnp.where(count(aff >= as_float(cand)) >= cap, cand, thr_bits)

    thr = as_float(lax.fori_loop(0, 31, bit_step, jnp.zeros((ne, 1, 1), jnp.int32)))
    gt = aff > thr
    eq = aff == thr
    need = cap - count(gt)
    ex_eq, in_eq = _block_ranks(jnp.where(eq, 1.0, 0.0))
    sel_ref[...] = jnp.where(gt | (eq & (ex_eq + in_eq - 1.0 < need)), 1.0, 0.0)


def _slot_kernel(sel_ref, pos_ref, excl_ref):
    sel = sel_ref[...]
    ex_sel, in_sel = _block_ranks(sel)
    pos_ref[...] = jnp.where(sel > 0.0, ex_sel + in_sel - 1.0, -1.0).astype(jnp.int32)
    excl_ref[...] = ex_sel.astype(jnp.int32)


def _to_tile_order(a, nb):
    if nb == 1:
        return a
    ne, t = a.shape
    r = ROUTE_TILE // nb
    return a.reshape(ne, nb, t // (nb * r), r).transpose(0, 2, 1, 3).reshape(ne, t)


def _route(aff_t, cap, nb):
    ne, t = aff_t.shape
    nblk = t // 128
    shape = (ne, nblk, 128)
    full = pl.BlockSpec(shape, lambda i: (0, 0, 0))
    params = pltpu.CompilerParams(dimension_semantics=("arbitrary",), vmem_limit_bytes=VMEM_LIMIT)
    sel = pl.pallas_call(
        functools.partial(_select_kernel, cap=cap),
        grid=(1,), in_specs=[full], out_specs=full,
        out_shape=jax.ShapeDtypeStruct(shape, F32),
        compiler_params=params, name="route_select",
    )(aff_t.reshape(shape))
    sel = _to_tile_order(sel.reshape(ne, t), nb)
    pos, excl = pl.pallas_call(
        _slot_kernel,
        grid=(1,), in_specs=[full], out_specs=[full, full],
        out_shape=[jax.ShapeDtypeStruct(shape, jnp.int32)] * 2,
        compiler_params=params, name="route_slots",
    )(sel.reshape(shape))
    base = jnp.concatenate([excl[:, :, 0], jnp.full((ne, 1), cap, jnp.int32)], axis=1)
    return pos.reshape(ne, t), base, _to_tile_order(aff_t, nb)


ROUTE_TILE = 256
ROUTE_WINDOW = 64
SLOT_ALIGN = 16
DISPATCH_SLOT_ROWS = 16384
DISPATCH_TILES_PER_STEP = 8


def _window(base_ref, e, blk, w, cap):
    lo = base_ref[e, blk] // SLOT_ALIGN + w * (ROUTE_WINDOW // SLOT_ALIGN)
    return lo * SLOT_ALIGN, jnp.minimum(lo, (cap - ROUTE_WINDOW) // SLOT_ALIGN) * SLOT_ALIGN


def _n_windows(base_ref, e, blk, cap):
    lo, _ = _window(base_ref, e, blk, 0, cap)
    end = base_ref[e, blk + ROUTE_TILE // 128]
    return (end - lo + ROUTE_WINDOW - 1) // ROUTE_WINDOW


def _dispatch_kernel(base_ref, h_ref, pos_ref, xs_ref, *, cap):
    g, j = pl.program_id(0), pl.program_id(1)
    group = xs_ref.shape[0]

    @pl.when(j == 0)
    def _():
        xs_ref[...] = jnp.zeros_like(xs_ref)

    ri = lax.broadcasted_iota(jnp.int32, (ROUTE_WINDOW, ROUTE_TILE), 0)

    def onehot(prow, lo, off, first):
        hit = prow - off == ri
        if not first:
            hit = hit & (prow >= lo)
        return jnp.where(hit, 1.0, 0.0).astype(BF16)

    def add_rows(ge, off, rows):
        sl = pl.ds(pl.multiple_of(off, SLOT_ALIGN), ROUTE_WINDOW)
        xs_ref[ge, sl, :] = xs_ref[ge, sl, :] + rows.astype(BF16)

    for sub in range(DISPATCH_TILES_PER_STEP):
        blk = (j * DISPATCH_TILES_PER_STEP + sub) * (ROUTE_TILE // 128)
        tok = slice(sub * ROUTE_TILE, (sub + 1) * ROUTE_TILE)
        h = h_ref[:, sub].reshape(ROUTE_TILE, h_ref.shape[-1])
        prows, offs = [], []
        for ge in range(group):
            e = g * group + ge
            prows.append(pos_ref[pl.ds(e, 1), tok])
            offs.append(_window(base_ref, e, blk, 0, cap))
        sel = jnp.concatenate([onehot(prows[ge], *offs[ge], True) for ge in range(group)], axis=0)
        rows = jnp.dot(sel, h, preferred_element_type=F32)
        for ge in range(group):
            add_rows(ge, offs[ge][1], rows[ge * ROUTE_WINDOW:(ge + 1) * ROUTE_WINDOW])

        for ge in range(group):
            e = g * group + ge

            def extra(w, carry, ge=ge, e=e, blk=blk, h=h, prow=prows[ge]):
                lo, off = _window(base_ref, e, blk, w, cap)
                add_rows(ge, off, jnp.dot(onehot(prow, lo, off, False), h, preferred_element_type=F32))
                return carry

            lax.fori_loop(1, _n_windows(base_ref, e, blk, cap), extra, 0)


def _tile_view(a, nb):
    t, d = a.shape
    r = ROUTE_TILE // nb
    return a.reshape(nb, t // (nb * r), r, d)


def _dispatch(base, h2, pos, cap, nb):
    t, d = h2.shape
    ne = pos.shape[0]
    step = ROUTE_TILE * DISPATCH_TILES_PER_STEP
    group = min(ne, DISPATCH_SLOT_ROWS // cap)
    out_spec = pl.BlockSpec((group, cap, d), lambda g, j, b: (g, 0, 0), pipeline_mode=pl.Buffered(1))
    return pl.pallas_call(
        functools.partial(_dispatch_kernel, cap=cap),
        grid_spec=pltpu.PrefetchScalarGridSpec(
            num_scalar_prefetch=1,
            grid=(ne // group, t // step),
            in_specs=[pl.BlockSpec((nb, DISPATCH_TILES_PER_STEP, ROUTE_TILE // nb, d),
                                   lambda g, j, b: (0, j, 0, 0)),
                      pl.BlockSpec((ne, step), lambda g, j, b: (0, j))],
            out_specs=out_spec),
        out_shape=jax.ShapeDtypeStruct((ne, cap, d), BF16),
        compiler_params=pltpu.CompilerParams(dimension_semantics=("arbitrary", "arbitrary"),
                                             vmem_limit_bytes=VMEM_LIMIT),
        name="dispatch",
    )(base, _tile_view(h2, nb), pos)


def _combine_kernel(base_ref, pos_ref, aff_ref, x1_ref, g_ref, g2_ref, eo_ref, o_ref,
                    f_ref, win_buf, win_sem, extra_buf, extra_sem, *, cap, n_tiles):
    j = pl.program_id(0)
    blocks_per_tile = ROUTE_TILE // 128
    blk = j * blocks_per_tile
    slot = lax.rem(j, 2)
    ri = lax.broadcasted_iota(jnp.int32, (ROUTE_WINDOW, ROUTE_TILE), 0)
    tn = (((0,), (0,)), ((), ()))

    def window_copy(e, off, dst, sem):
        return pltpu.make_async_copy(eo_ref.at[e, pl.ds(pl.multiple_of(off, SLOT_ALIGN), ROUTE_WINDOW), :],
                                     dst, sem)

    def first_window_copies(tile, buf_slot):
        return [window_copy(e, _window(base_ref, e, tile * blocks_per_tile, 0, cap)[1],
                            win_buf.at[buf_slot, pl.ds(e * ROUTE_WINDOW, ROUTE_WINDOW), :], win_sem.at[buf_slot])
                for e in range(N_EXPERTS)]

    def wait_windows(buf_slot):
        for e in range(N_EXPERTS):
            window_copy(e, 0, win_buf.at[buf_slot, pl.ds(e * ROUTE_WINDOW, ROUTE_WINDOW), :],
                        win_sem.at[buf_slot]).wait()

    @pl.when(j == 0)
    def _():
        for cp in first_window_copies(0, 0):
            cp.start()

    for cp in first_window_copies(jnp.minimum(j + 1, n_tiles - 1), 1 - slot):
        cp.start()

    def weights(e, lo, off, first):
        prow = pos_ref[e:e + 1, :]
        hit = prow - off == ri
        if not first:
            hit = hit & (prow >= lo)
        return jnp.where(hit, aff_ref[e:e + 1, :], 0.0).astype(BF16)

    q = jnp.concatenate([weights(e, *_window(base_ref, e, blk, 0, cap), True) for e in range(N_EXPERTS)], axis=0)
    wait_windows(slot)
    f_ref[...] = lax.dot_general(q, win_buf[slot], tn, preferred_element_type=F32)

    n_win = [_n_windows(base_ref, e, blk, cap) for e in range(N_EXPERTS)]

    @pl.when(functools.reduce(jnp.maximum, n_win) > 1)
    def _():
        for e in range(N_EXPERTS):
            def extra(w, carry, e=e):
                lo, off = _window(base_ref, e, blk, w, cap)
                cp = window_copy(e, off, extra_buf, extra_sem)
                cp.start()
                cp.wait()
                f_ref[...] += lax.dot_general(weights(e, lo, off, False), extra_buf[...], tn,
                                              preferred_element_type=F32)
                return carry

            lax.fori_loop(1, n_win[e], extra, 0)

    nb, _, r, d = x1_ref.shape
    f = f_ref[...].reshape(nb, r, d)
    o_ref[:, 0] = x1_ref[:, 0] + g2_ref[...] * _rms(f, g_ref[...])

    @pl.when(j == n_tiles - 1)
    def _():
        wait_windows(1 - slot)


def _combine(base, eo, pos, aff, x1, mod3, npost, cond_block, cap, nb):
    t, d = x1.shape
    ne = pos.shape[0]
    n_tiles = t // ROUTE_TILE
    tile = pl.BlockSpec((nb, 1, ROUTE_TILE // nb, d), lambda j, b: (0, j, 0, 0))
    etile = pl.BlockSpec((ne, ROUTE_TILE), lambda j, b: (0, j))
    return pl.pallas_call(
        functools.partial(_combine_kernel, cap=cap, n_tiles=n_tiles),
        grid_spec=pltpu.PrefetchScalarGridSpec(
            num_scalar_prefetch=1,
            grid=(n_tiles,),
            in_specs=[etile, etile, tile,
                      pl.BlockSpec((1, d), lambda j, b: (0, 0)),
                      pl.BlockSpec((nb, 1, d), lambda j, b: (cond_block, 0, 5)),
                      pl.BlockSpec(memory_space=pl.ANY)],
            out_specs=tile,
            scratch_shapes=[pltpu.VMEM((ROUTE_TILE, d), F32),
                            pltpu.VMEM((2, ne * ROUTE_WINDOW, d), BF16), pltpu.SemaphoreType.DMA((2,)),
                            pltpu.VMEM((ROUTE_WINDOW, d), BF16), pltpu.SemaphoreType.DMA(())]),
        out_shape=jax.ShapeDtypeStruct(_tile_view(x1, nb).shape, F32),
        compiler_params=pltpu.CompilerParams(dimension_semantics=("arbitrary",),
                                             vmem_limit_bytes=VMEM_LIMIT),
        name="combine",
    )(base, pos, aff, _tile_view(x1, nb), npost, mod3, eo).reshape(t, d)


def _trunk(x3, mod3, first_cond_row, shared_cond, weights, lgs, *, rope_tabs, init, emit_state):
    (n_pre_mix, n_post_mix, n_pre_ffn, n_post_ffn, w_in, gn_g, conv_w, wro, wco, wo, wr_pad, wg, wu, wd) = weights
    b, seq, d = x3.shape
    t = b * seq
    x = x3.reshape(t, d)
    cond_row_fn = (lambda r: first_cond_row) if shared_cond else (lambda r: first_cond_row + r // seq)
    nb = 1 if shared_cond else b
    assert ROUTE_TILE % nb == 0 and (ROUTE_TILE // nb) % SLOT_ALIGN == 0 and first_cond_row % nb == 0
    proj_a, proj_p, proj_s = _inproj(x, mod3, n_pre_mix, w_in, cond_row_fn, seq, rope_tabs)
    ret = _retention(proj_a, lgs, gn_g, seq, init=init, emit_state=emit_state)
    og = ret[0]
    x1, h2, aff_t = _mix(og, proj_a, proj_p, proj_s, x, mod3, conv_w, wro, wco, wo, wr_pad, n_post_mix,
                         n_pre_ffn, seq, cond_row_fn)
    cap = CAPACITY_FACTOR * t // N_EXPERTS
    pos, base, aff = _route(aff_t, cap, nb)
    eo = _experts(_dispatch(base, h2, pos, cap, nb), wg, wu, wd)
    y = _combine(base, eo, pos, aff, x1, mod3, n_post_ffn, first_cond_row // nb, cap, nb)
    return y.reshape(b, seq, d), ret[1:]


def kernel(x_prompt, x_sample, state_ret_fwd, state_ret_bwd, c, c_ctx, w_ada, b_ada, norm_pre_mix, norm_post_mix,
           norm_pre_ffn, norm_post_ffn, w_in, ret_decay_fwd, ret_decay_bwd, ret_norm_g, conv_w, w_ret_o, w_conv_o,
           w_o, w_router, w_gate, w_up, w_down):
    depth = w_ada.shape[0]
    assert depth == 1
    dec_b, dec_seq = x_sample.shape[0], x_sample.shape[1]
    xp, xs = x_prompt, x_sample
    l = 0
    cond = jnp.zeros((N_COND_ROWS, D_MODEL), F32).at[0:dec_b].set(c).at[CTX_COND_ROW].set(c_ctx)
    mod3 = _ada(cond, w_ada[l], b_ada[l][None, :]).reshape(N_COND_ROWS, 1, 6 * D_MODEL)
    lgs = jnp.stack([-jax.nn.softplus(-ret_decay_fwd[l].astype(F32)),
                     -jax.nn.softplus(-ret_decay_bwd[l].astype(F32))])
    wr_pad = jnp.pad(w_router[l], ((0, 0), (0, ROUTER_LANES - N_EXPERTS)))
    weights = (norm_pre_mix[l][None, :], norm_post_mix[l][None, :], norm_pre_ffn[l][None, :],
               norm_post_ffn[l][None, :], w_in[l].astype(BF16), ret_norm_g[l][None, :], conv_w[l],
               w_ret_o[l].astype(BF16), w_conv_o[l].astype(BF16), w_o[l].astype(BF16), wr_pad,
               w_gate[l], w_up[l], w_down[l])
    yp, (s_f, s_b) = _trunk(xp, mod3, CTX_COND_ROW, True, weights, lgs,
                            rope_tabs=None, init=None, emit_state=True)
    ys, _ = _trunk(xs, mod3, 0, False, weights, lgs,
                   rope_tabs=_rope_tables(dec_seq), init=(state_ret_fwd, state_ret_bwd), emit_state=False)
    return (yp, ys, s_f, s_b)
```

```python
import functools

import jax
import jax.numpy as jnp
import numpy as np
from jax import lax
from jax.experimental import pallas as pl
from jax.experimental.pallas import tpu as pltpu

F32 = jnp.float32
BF16 = jnp.bfloat16

D_MODEL = 1024
N_HEADS = 8
DK = 64
DV = 128
CHUNK = 128
GRID_W = 64
N_EXPERTS = 16
CAPACITY_FACTOR = 2
D_IN_TOTAL = 8192
RMS_EPS = 1e-6
GN_EPS = 1e-5
ROPE_BASE = 10000.0
N_COND_ROWS = 16
CTX_COND_ROW = 8
ROUTER_LANES = 128
RET_CHUNKS_PER_STEP = 16
RET_UNROLL = 16
MIX_TILE = 1024
MIX_SUB_TILE = 512
VMEM_LIMIT = 48 * 1024 * 1024
EXPERT_ROWS_PER_STEP = 1024
EXPERT_SUB_ROWS = 512
EXPERT_VMEM_LIMIT = 56 * 1024 * 1024


def _sigmoid(x):
    return 0.5 * jnp.tanh(0.5 * x) + 0.5


def _rms(x, g):
    return x * lax.rsqrt(jnp.mean(x * x, axis=-1, keepdims=True) + RMS_EPS) * g


def _ada_kernel(c_ref, w_ref, b_ref, o_ref):
    c = c_ref[...]
    s = c * _sigmoid(c)
    o_ref[...] = jnp.dot(s, w_ref[...], preferred_element_type=F32,
                         precision=lax.Precision.HIGHEST) + b_ref[...]


def _ada(cond, w_ada, b_ada):
    n = w_ada.shape[1]
    tn = 1024
    return pl.pallas_call(
        _ada_kernel,
        grid=(n // tn,),
        in_specs=[pl.BlockSpec((N_COND_ROWS, D_MODEL), lambda j: (0, 0)),
                  pl.BlockSpec((D_MODEL, tn), lambda j: (0, j)),
                  pl.BlockSpec((1, tn), lambda j: (0, j))],
        out_specs=pl.BlockSpec((N_COND_ROWS, tn), lambda j: (0, j)),
        out_shape=jax.ShapeDtypeStruct((N_COND_ROWS, n), F32),
        compiler_params=pltpu.CompilerParams(dimension_semantics=("arbitrary",),
                                             vmem_limit_bytes=VMEM_LIMIT),
        name="ada_mod",
    )(cond, w_ada, b_ada)


def _inproj_body(x_ref, g_ref, sh_ref, sc_ref, w_ref, cos_ref, sin_ref, a_ref, p_ref, s_ref, h_ref, use_rope):
    j = pl.program_id(1)

    @pl.when(j == 0)
    def _():
        h = _rms(x_ref[...], g_ref[...]) * (1.0 + sc_ref[0]) + sh_ref[0]
        h_ref[...] = h.astype(BF16)

    lanes = 2 * DK
    width = 2 * lanes
    n_slices = w_ref.shape[1] // width
    n_qk = 2 * N_HEADS * DK

    def slice_dot(b):
        return jnp.dot(h_ref[...], w_ref[:, b * width:(b + 1) * width], preferred_element_type=F32)

    @pl.when(j == 0)
    def _():
        ci = lax.broadcasted_iota(jnp.int32, (h_ref.shape[0], lanes), 1)
        for b in range(n_slices):
            acc = slice_dot(b)
            if b * width >= n_qk:
                a_ref[:, b * width:(b + 1) * width] = acc.astype(BF16)
                continue
            for half in range(2):
                col = b * width + half * lanes
                x = acc[:, half * lanes:(half + 1) * lanes]
                if col >= N_HEADS * DK:
                    x = x * (DK ** -0.5)
                if use_rope:
                    swapped = jnp.where((ci & 31) < 16, pltpu.roll(x, lanes - 16, 1), pltpu.roll(x, 16, 1))
                    x = x * cos_ref[...] + swapped * sin_ref[...]
                a_ref[:, col:col + lanes] = x.astype(BF16)

    @pl.when(j == 1)
    def _():
        for b in range(n_slices):
            acc = slice_dot(b)
            if b < n_slices // 2:
                acc = acc * _sigmoid(acc)
            a_ref[:, b * width:(b + 1) * width] = acc.astype(BF16)

    @pl.when(j == 2)
    def _():
        for b in range(n_slices // 2):
            p_ref[:, b * width:(b + 1) * width] = (slice_dot(b) * slice_dot(b + n_slices // 2)).astype(BF16)

    @pl.when(j == 3)
    def _():
        for b in range(n_slices):
            s_ref[:, b * width:(b + 1) * width] = _sigmoid(slice_dot(b)).astype(BF16)


def _inproj_kernel(x_ref, g_ref, sh_ref, sc_ref, w_ref, *rest, use_rope):
    if use_rope:
        cos_ref, sin_ref, a_ref, p_ref, s_ref, h_ref = rest
    else:
        cos_ref = sin_ref = None
        a_ref, p_ref, s_ref, h_ref = rest
    _inproj_body(x_ref, g_ref, sh_ref, sc_ref, w_ref, cos_ref, sin_ref, a_ref, p_ref, s_ref, h_ref, use_rope)


def _inproj(x, mod3, g, w_bf16, cond_row_fn, seq, rope_tabs):
    t = x.shape[0]
    tm, tn = 1024, 2048
    assert D_IN_TOTAL == 4 * tn
    in_specs = [pl.BlockSpec((tm, D_MODEL), lambda i, j: (i, 0)),
                pl.BlockSpec((1, D_MODEL), lambda i, j: (0, 0)),
                pl.BlockSpec((1, 1, D_MODEL), lambda i, j: (cond_row_fn(i * tm), 0, 0)),
                pl.BlockSpec((1, 1, D_MODEL), lambda i, j: (cond_row_fn(i * tm), 0, 1)),
                pl.BlockSpec((D_MODEL, tn), lambda i, j: (0, j))]
    args = [x, g, mod3, mod3, w_bf16]
    if rope_tabs is not None:
        tiles_per_seq = seq // tm
        in_specs += [pl.BlockSpec((tm, 2 * DK), lambda i, j: (i % tiles_per_seq, 0))] * 2
        args += list(rope_tabs)
    return pl.pallas_call(
        functools.partial(_inproj_kernel, use_rope=rope_tabs is not None),
        grid=(t // tm, D_IN_TOTAL // tn),
        in_specs=in_specs,
        out_specs=[pl.BlockSpec((tm, tn), lambda i, j: (i, jnp.minimum(j, 1))),
                   pl.BlockSpec((tm, tn // 2), lambda i, j: (i, 0)),
                   pl.BlockSpec((tm, tn), lambda i, j: (i, 0))],
        out_shape=[jax.ShapeDtypeStruct((t, 2 * tn), BF16),
                   jax.ShapeDtypeStruct((t, tn // 2), BF16),
                   jax.ShapeDtypeStruct((t, tn), BF16)],
        scratch_shapes=[pltpu.VMEM((tm, D_MODEL), BF16)],
        compiler_params=pltpu.CompilerParams(dimension_semantics=("arbitrary", "arbitrary"),
                                             vmem_limit_bytes=VMEM_LIMIT),
        name="inproj",
    )(*args)


def _ret_kernel(*refs, nc, bb, has_init, emit_state):
    it = iter(refs)
    lg_ref = next(it)
    q_ref, k_ref, v_ref, gr_ref, gn_ref = (next(it) for _ in range(5))
    s0f_ref = s0b_ref = sf_ref = sb_ref = None
    if has_init:
        s0f_ref, s0b_ref = next(it), next(it)
    o_ref = next(it)
    if emit_state:
        sf_ref, sb_ref = next(it), next(it)
    kvf_s, kvb_s, rcat_s, p_s, o_s, r0 = (next(it) for _ in range(6))

    p = pl.program_id(1)
    lgf_a, lgf_b = lg_ref[0, 2 * p], lg_ref[0, 2 * p + 1]
    lgb_a, lgb_b = lg_ref[1, 2 * p], lg_ref[1, 2 * p + 1]

    ri = lax.broadcasted_iota(jnp.int32, (CHUNK, 2 * DK), 0)
    ci = lax.broadcasted_iota(jnp.int32, (CHUNK, 2 * DK), 1)
    lane_a = ci < DK
    rowf = ri.astype(F32)
    diff = rowf - ci.astype(F32)
    lgf_lane = jnp.where(lane_a, lgf_a, lgf_b)
    lgb_lane = jnp.where(lane_a, lgb_a, lgb_b)
    xi_f = jnp.exp(lgf_lane * (rowf + 1.0))
    xi_b = jnp.exp(lgb_lane * (CHUNK - rowf))
    zeta_f = jnp.exp(lgf_lane * (CHUNK - 1.0 - rowf))
    zeta_b = jnp.exp(lgb_lane * rowf)

    def decay_matrix(lgf, lgb):
        return jnp.where(diff > 0, jnp.exp(lgf * diff),
                         jnp.where(diff < 0, jnp.exp(lgb * (-diff)), 2.0))

    dm_a = decay_matrix(lgf_a, lgb_a)
    dm_b = decay_matrix(lgf_b, lgb_b)

    r2 = lax.broadcasted_iota(jnp.int32, (2 * DK, 2 * DV), 0)
    c2 = lax.broadcasted_iota(jnp.int32, (2 * DK, 2 * DV), 1)
    top = r2 < DK
    blk = (top == (c2 < DV)).astype(F32)
    cd_f = jnp.exp(jnp.where(top, lgf_a, lgf_b) * float(CHUNK)) * blk
    cd_b = jnp.exp(jnp.where(top, lgb_a, lgb_b) * float(CHUNK)) * blk

    def chunk_rows(c):
        return pl.ds(pl.multiple_of(c * CHUNK, CHUNK), CHUNK)

    n_chunks = bb * nc
    unroll = RET_UNROLL

    def kv_body(c, carry):
        rows = chunk_rows(c)
        k = k_ref[rows, :].astype(F32)
        kz_t = jnp.concatenate([k * zeta_f, k * zeta_b], axis=1).T.astype(BF16)
        kv = jnp.dot(kz_t, v_ref[rows, :], preferred_element_type=F32)
        kvf_s[c] = kv[0:2 * DK] * blk
        kvb_s[c] = kv[2 * DK:4 * DK] * blk
        return carry

    def score_body(c, carry):
        rows = chunk_rows(c)
        q16 = q_ref[rows, :]
        zero = jnp.zeros_like(q16)
        q_ab = jnp.concatenate([jnp.where(lane_a, q16, zero), jnp.where(lane_a, zero, q16)], axis=0)
        s = lax.dot_general(q_ab, k_ref[rows, :], (((1,), (1,)), ((), ())), preferred_element_type=F32)
        p_s[c, :, 0:CHUNK] = (s[0:CHUNK] * dm_a).astype(BF16)
        p_s[c, :, CHUNK:2 * CHUNK] = (s[CHUNK:2 * CHUNK] * dm_b).astype(BF16)
        return carry

    lax.fori_loop(0, n_chunks, lambda c, carry: score_body(c, kv_body(c, carry)), 0, unroll=unroll)

    def load_state(s_ref, s):
        r0[...] = jnp.zeros_like(r0)
        r0[0:DK, 0:DV] = s_ref[s, 0, 0].astype(F32)
        r0[DK:2 * DK, DV:2 * DV] = s_ref[s, 0, 1].astype(F32)
        return r0[...]

    for s in range(bb):
        def fwd_body(n, rf, s=s):
            c = s * nc + n
            rcat_s[c, 0:2 * DK, :] = rf.astype(BF16)
            return cd_f * rf + kvf_s[c]

        def bwd_body(t, rb, s=s):
            c = s * nc + nc - 1 - t
            rcat_s[c, 2 * DK:4 * DK, :] = rb.astype(BF16)
            return cd_b * rb + kvb_s[c]

        zeros = jnp.zeros((2 * DK, 2 * DV), F32)
        rf_fin = lax.fori_loop(0, nc, fwd_body, load_state(s0f_ref, s) if has_init else zeros)
        rb_fin = lax.fori_loop(0, nc, bwd_body, load_state(s0b_ref, s) if has_init else zeros)
        if emit_state:
            for s_ref, r in ((sf_ref, rf_fin), (sb_ref, rb_fin)):
                s_ref[s, 0, 0] = r[0:DK, 0:DV]
                s_ref[s, 0, 1] = r[DK:2 * DK, DV:2 * DV]

    gn = gn_ref[...]

    hr = lax.broadcasted_iota(jnp.int32, (2 * DV, 2 * DV), 0)
    hc = lax.broadcasted_iota(jnp.int32, (2 * DV, 2 * DV), 1)
    head_mean = jnp.where((hr < DV) == (hc < DV), 1.0 / DV, 0.0).astype(BF16)
    v_lane_a = lax.broadcasted_iota(jnp.int32, (CHUNK, 2 * DV), 1) < DV

    def value_body(c, carry):
        rows = chunk_rows(c)
        q = q_ref[rows, :].astype(F32)
        v = v_ref[rows, :]
        zero = jnp.zeros_like(v)
        v_bd = jnp.concatenate([jnp.where(v_lane_a, v, zero), jnp.where(v_lane_a, zero, v)], axis=0)
        qx = jnp.concatenate([(q * xi_f).astype(BF16), (q * xi_b).astype(BF16)], axis=1)
        o_s[c] = (jnp.dot(p_s[c], v_bd, preferred_element_type=F32)
                  + jnp.dot(qx, rcat_s[c], preferred_element_type=F32))
        return carry

    lax.fori_loop(0, n_chunks, value_body, 0, unroll=unroll)

    def gate_body(c, carry):
        rows = chunk_rows(c)
        o = o_s[c]
        d = o - jnp.dot(o.astype(BF16), head_mean, preferred_element_type=F32)
        var = jnp.dot((d * d).astype(BF16), head_mean, preferred_element_type=F32)
        y = d * lax.rsqrt(var + GN_EPS)
        o_ref[rows, :] = (gr_ref[rows, :].astype(F32) * (y * gn)).astype(BF16)
        return carry

    lax.fori_loop(0, n_chunks, gate_body, 0, unroll=unroll)


def _retention(proj, lgs, gn_g, seq, *, init=None, emit_state):
    t = proj.shape[0]
    b = t // seq
    nc = seq // CHUNK
    bb = max(1, RET_CHUNKS_PER_STEP // nc)
    rows = bb * seq
    has_init = init is not None
    pairs = N_HEADS // 2
    qk_blocks = (N_HEADS * DK) // (2 * DK)
    in_specs = [pl.BlockSpec(memory_space=pltpu.SMEM),
                pl.BlockSpec((rows, 2 * DK), lambda i, p: (i, p)),
                pl.BlockSpec((rows, 2 * DK), lambda i, p: (i, qk_blocks + p)),
                pl.BlockSpec((rows, 2 * DV), lambda i, p: (i, qk_blocks + p)),
                pl.BlockSpec((rows, 2 * DV), lambda i, p: (i, 2 * qk_blocks + p)),
                pl.BlockSpec((1, 2 * DV), lambda i, p: (0, p))]
    args = [lgs, proj, proj, proj, proj, gn_g]
    state_spec = pl.BlockSpec((bb, 1, 2, DK, DV), lambda i, p: (i, 0, p, 0, 0))
    if has_init:
        in_specs += [state_spec, state_spec]
        args += list(init)
    out_specs = [pl.BlockSpec((rows, 2 * DV), lambda i, p: (i, p))]
    out_shape = [jax.ShapeDtypeStruct((t, N_HEADS * DV), BF16)]
    if emit_state:
        out_specs += [state_spec, state_spec]
        out_shape += [jax.ShapeDtypeStruct((b, 1, N_HEADS, DK, DV), F32)] * 2
    return pl.pallas_call(
        functools.partial(_ret_kernel, nc=nc, bb=bb, has_init=has_init, emit_state=emit_state),
        grid=(b // bb, pairs),
        in_specs=in_specs,
        out_specs=out_specs,
        out_shape=out_shape,
        scratch_shapes=[pltpu.VMEM((bb * nc, 2 * DK, 2 * DV), F32),
                        pltpu.VMEM((bb * nc, 2 * DK, 2 * DV), F32),
                        pltpu.VMEM((bb * nc, 4 * DK, 2 * DV), BF16),
                        pltpu.VMEM((bb * nc, CHUNK, 2 * CHUNK), BF16),
                        pltpu.VMEM((bb * nc, CHUNK, 2 * DV), F32),
                        pltpu.VMEM((2 * DK, 2 * DV), F32)],
        compiler_params=pltpu.CompilerParams(dimension_semantics=("arbitrary", "arbitrary"),
                                             vmem_limit_bytes=VMEM_LIMIT),
        name="retention",
    )(*args)


def _rope_tables(seq):
    n_freq = DK // 4
    pos = np.arange(seq)
    inv = jnp.asarray(ROPE_BASE, F32) ** (-jnp.arange(n_freq, dtype=F32) / n_freq)
    ang_r = jnp.asarray(pos // GRID_W, F32)[:, None] * inv
    ang_c = jnp.asarray(pos % GRID_W, F32)[:, None] * inv
    cos = jnp.concatenate([jnp.cos(ang_r)] * 2 + [jnp.cos(ang_c)] * 2, axis=1)
    sin = jnp.concatenate([-jnp.sin(ang_r), jnp.sin(ang_r), -jnp.sin(ang_c), jnp.sin(ang_c)], axis=1)
    return jnp.tile(cos, (1, 2)), jnp.tile(sin, (1, 2))


def _mix_kernel(og_ref, cb_ref, p_ref, pp_ref, pn_ref, sa_ref, sb_ref, x_ref,
                cw_ref, wro_ref, wco_ref, wo_ref, wr_ref, npost_ref, npre_ref, g1_ref, sh2_ref, sc2_ref,
                x1_ref, h2_ref, aff_ref, *, tm, sub, seq):
    i = pl.program_id(0)
    halo = pp_ref.shape[0]
    wr = wr_ref[...]
    w_hi = wr.astype(BF16)
    w_lo = (wr - w_hi.astype(F32)).astype(BF16)
    w_hi_lo = jnp.concatenate([w_hi, w_lo], axis=1)
    row = lax.broadcasted_iota(jnp.int32, (sub, 1), 0)
    lane = lax.broadcasted_iota(jnp.int32, (sub, ROUTER_LANES), 1)

    for s in range(tm // sub):
        r0 = s * sub
        rows = slice(r0, r0 + sub)
        token = i * tm + r0 + row
        in_seq = token & (seq - 1) if seq & (seq - 1) == 0 else token % seq
        prev_row = (pp_ref[halo - 1:halo, :] if s == 0 else p_ref[r0 - 1:r0, :]).astype(F32)
        next_row = (pn_ref[0:1, :] if r0 + sub == tm else p_ref[r0 + sub:r0 + sub + 1, :]).astype(F32)
        prod = p_ref[rows, :].astype(F32)
        up = jnp.where(in_seq == 0, 0.0, jnp.where(row == 0, prev_row, pltpu.roll(prod, 1, 0)))
        dn = jnp.where(in_seq == seq - 1, 0.0, jnp.where(row == sub - 1, next_row, pltpu.roll(prod, sub - 1, 0)))
        u = up * cw_ref[0:1, :] + prod * cw_ref[1:2, :] + dn * cw_ref[2:3, :]
        y_conv = jnp.dot((cb_ref[rows, :].astype(F32) * u).astype(BF16), wco_ref[...],
                         preferred_element_type=F32)
        y_ret = jnp.dot(og_ref[rows, :], wro_ref[...], preferred_element_type=F32)
        merged = sa_ref[rows, :].astype(F32) * y_ret + sb_ref[rows, :].astype(F32) * y_conv
        m = jnp.dot(merged.astype(BF16), wo_ref[...], preferred_element_type=F32)
        x1 = x_ref[rows, :] + g1_ref[0] * _rms(m, npost_ref[...])
        x1_ref[rows, :] = x1
        h2 = _rms(x1, npre_ref[...]) * (1.0 + sc2_ref[0]) + sh2_ref[0]
        h_hi = h2.astype(BF16)
        h2_ref[rows, :] = h_hi
        h_lo = (h2 - h_hi.astype(F32)).astype(BF16)
        hi_terms = jnp.dot(h_hi, w_hi_lo, preferred_element_type=F32)
        logits = (hi_terms[:, 0:ROUTER_LANES] + hi_terms[:, ROUTER_LANES:2 * ROUTER_LANES]
                  + jnp.dot(h_lo, w_hi, preferred_element_type=F32))
        logits = jnp.where(lane < N_EXPERTS, logits, -jnp.inf)
        e = jnp.exp(logits - jnp.max(logits, axis=-1, keepdims=True))
        aff = e / jnp.sum(e, axis=-1, keepdims=True)
        aff_ref[:, rows] = aff.T[0:N_EXPERTS, :]


def _mix(og, proj_a, proj_p, proj_s, x, mod3, conv_w, wro, wco, wo, wr_pad, npost, npre, seq, cond_row_fn):
    t = x.shape[0]
    conv_b_block = (2 * N_HEADS * DK + 2 * N_HEADS * DV) // D_MODEL
    tm = MIX_TILE
    halo = 16
    hb = tm // halo
    last_halo = t // halo - 1
    col = lambda c: (lambda i: (i, c))
    row_vec = pl.BlockSpec((1, D_MODEL), lambda i: (0, 0))
    wspec = pl.BlockSpec((D_MODEL, D_MODEL), lambda i: (0, 0))
    modspec = lambda c: pl.BlockSpec((1, 1, D_MODEL), lambda i: (cond_row_fn(i * tm), 0, c))
    tile = lambda c: pl.BlockSpec((tm, D_MODEL), col(c))
    prev = lambda c: pl.BlockSpec((halo, D_MODEL), lambda i: (jnp.maximum(i * hb - 1, 0), c))
    nxt = lambda c: pl.BlockSpec((halo, D_MODEL), lambda i: (jnp.minimum((i + 1) * hb, last_halo), c))
    return pl.pallas_call(
        functools.partial(_mix_kernel, tm=tm, sub=MIX_SUB_TILE, seq=seq),
        grid=(t // tm,),
        in_specs=[tile(0), tile(conv_b_block), tile(0), prev(0), nxt(0), tile(0), tile(1),
                  tile(0),
                  pl.BlockSpec((3, D_MODEL), lambda i: (0, 0)), wspec, wspec, wspec,
                  pl.BlockSpec((D_MODEL, ROUTER_LANES), lambda i: (0, 0)),
                  row_vec, row_vec, modspec(2), modspec(3), modspec(4)],
        out_specs=[tile(0), tile(0), pl.BlockSpec((N_EXPERTS, tm), lambda i: (0, i))],
        out_shape=[jax.ShapeDtypeStruct((t, D_MODEL), F32),
                   jax.ShapeDtypeStruct((t, D_MODEL), BF16),
                   jax.ShapeDtypeStruct((N_EXPERTS, t), F32)],
        compiler_params=pltpu.CompilerParams(dimension_semantics=("arbitrary",),
                                             vmem_limit_bytes=VMEM_LIMIT),
        name="mix_out",
    )(og, proj_a, proj_p, proj_p, proj_p, proj_s, proj_s, x,
      conv_w, wro, wco, wo, wr_pad, npost, npre, mod3, mod3, mod3)


def _expert_kernel(xs_ref, wg_ref, wu_ref, wd_ref, o_ref, *, sub):
    wg = wg_ref[0].astype(BF16)
    wu = wu_ref[0].astype(BF16)
    wd = wd_ref[0].astype(BF16)
    for r in range(xs_ref.shape[1] // sub):
        rows = slice(r * sub, (r + 1) * sub)
        x = xs_ref[0, rows, :]
        g = jnp.dot(x, wg, preferred_element_type=F32)
        u = jnp.dot(x, wu, preferred_element_type=F32)
        hid = (g * _sigmoid(g) * u).astype(BF16)
        o_ref[0, rows, :] = jnp.dot(hid, wd, preferred_element_type=F32).astype(BF16)


def _experts(xs, wg, wu, wd):
    e, cap, d = xs.shape
    ff = wg.shape[2]
    tr = min(cap, EXPERT_ROWS_PER_STEP)
    return pl.pallas_call(
        functools.partial(_expert_kernel, sub=EXPERT_SUB_ROWS),
        grid=(e, cap // tr),
        in_specs=[pl.BlockSpec((1, tr, d), lambda a, r: (a, r, 0)),
                  pl.BlockSpec((1, d, ff), lambda a, r: (a, 0, 0)),
                  pl.BlockSpec((1, d, ff), lambda a, r: (a, 0, 0)),
                  pl.BlockSpec((1, ff, d), lambda a, r: (a, 0, 0))],
        out_specs=pl.BlockSpec((1, tr, d), lambda a, r: (a, r, 0)),
        out_shape=jax.ShapeDtypeStruct((e, cap, d), BF16),
        compiler_params=pltpu.CompilerParams(dimension_semantics=("arbitrary", "arbitrary"),
                                             vmem_limit_bytes=EXPERT_VMEM_LIMIT),
        name="experts",
    )(xs, wg, wu, wd)


def _block_ranks(mask):
    ne, nblk, lanes = mask.shape
    li = lax.broadcasted_iota(jnp.int32, (lanes, lanes), 0)
    lj = lax.broadcasted_iota(jnp.int32, (lanes, lanes), 1)
    incl_lanes = jnp.where(li <= lj, 1.0, 0.0).astype(BF16)
    all_lanes = jnp.ones((lanes, lanes), BF16)
    bi = lax.broadcasted_iota(jnp.int32, (nblk, nblk), 0)
    bj = lax.broadcasted_iota(jnp.int32, (nblk, nblk), 1)
    earlier_blocks = jnp.where(bj < bi, 1.0, 0.0).astype(BF16)
    m = mask.reshape(ne * nblk, lanes).astype(BF16)
    incl = jnp.dot(m, incl_lanes, preferred_element_type=F32).reshape(ne, nblk, lanes)
    tot = jnp.dot(m, all_lanes, preferred_element_type=F32).reshape(ne, nblk, lanes).astype(BF16)
    excl = jnp.stack([jnp.dot(earlier_blocks, tot[i], preferred_element_type=F32) for i in range(ne)])
    return excl, incl


def _select_kernel(aff_ref, sel_ref, *, cap):
    ne, nblk, lanes = aff_ref.shape
    aff = aff_ref[...]

    def count(mask):
        m = jnp.where(mask, 1.0, 0.0)
        return jnp.sum(jnp.sum(m, axis=1, keepdims=True), axis=2, keepdims=True)

    def as_float(bits):
        return lax.bitcast_convert_type(bits, F32)

    def bit_step(it, thr_bits):
        cand = thr_bits | jnp.left_shift(jnp.int32(1), 30 - it)
        return jnp.where(count(aff >= as_float(cand)) >= cap, cand, thr_bits)

    thr = as_float(lax.fori_loop(0, 31, bit_step, jnp.zeros((ne, 1, 1), jnp.int32)))
    gt = aff > thr
    eq = aff == thr
    need = cap - count(gt)
    ex_eq, in_eq = _block_ranks(jnp.where(eq, 1.0, 0.0))
    sel_ref[...] = jnp.where(gt | (eq & (ex_eq + in_eq - 1.0 < need)), 1.0, 0.0)


def _slot_kernel(sel_ref, pos_ref, excl_ref):
    sel = sel_ref[...]
    ex_sel, in_sel = _block_ranks(sel)
    pos_ref[...] = jnp.where(sel > 0.0, ex_sel + in_sel - 1.0, -1.0).astype(jnp.int32)
    excl_ref[...] = ex_sel.astype(jnp.int32)


def _to_tile_order(a, nb):
    if nb == 1:
        return a
    ne, t = a.shape
    r = ROUTE_TILE // nb
    return a.reshape(ne, nb, t // (nb * r), r).transpose(0, 2, 1, 3).reshape(ne, t)


def _route(aff_t, cap, nb):
    ne, t = aff_t.shape
    nblk = t // 128
    shape = (ne, nblk, 128)
    full = pl.BlockSpec(shape, lambda i: (0, 0, 0))
    params = pltpu.CompilerParams(dimension_semantics=("arbitrary",), vmem_limit_bytes=VMEM_LIMIT)
    sel = pl.pallas_call(
        functools.partial(_select_kernel, cap=cap),
        grid=(1,), in_specs=[full], out_specs=full,
        out_shape=jax.ShapeDtypeStruct(shape, F32),
        compiler_params=params, name="route_select",
    )(aff_t.reshape(shape))
    sel = _to_tile_order(sel.reshape(ne, t), nb)
    pos, excl = pl.pallas_call(
        _slot_kernel,
        grid=(1,), in_specs=[full], out_specs=[full, full],
        out_shape=[jax.ShapeDtypeStruct(shape, jnp.int32)] * 2,
        compiler_params=params, name="route_slots",
    )(sel.reshape(shape))
    base = jnp.concatenate([excl[:, :, 0], jnp.full((ne, 1), cap, jnp.int32)], axis=1)
    return pos.reshape(ne, t), base, _to_tile_order(aff_t, nb)


ROUTE_TILE = 256
ROUTE_WINDOW = 64
SLOT_ALIGN = 16
DISPATCH_SLOT_ROWS = 16384
DISPATCH_TILES_PER_STEP = 8
COMBINE_TILES_PER_STEP = 4


def _window(base_ref, e, blk, w, cap):
    lo = base_ref[e, blk] // SLOT_ALIGN + w * (ROUTE_WINDOW // SLOT_ALIGN)
    return lo * SLOT_ALIGN, jnp.minimum(lo, (cap - ROUTE_WINDOW) // SLOT_ALIGN) * SLOT_ALIGN


def _n_windows(base_ref, e, blk, cap):
    lo, _ = _window(base_ref, e, blk, 0, cap)
    end = base_ref[e, blk + ROUTE_TILE // 128]
    return (end - lo + ROUTE_WINDOW - 1) // ROUTE_WINDOW


def _dispatch_kernel(base_ref, h_ref, pos_ref, xs_ref, *, cap):
    g, j = pl.program_id(0), pl.program_id(1)
    group = xs_ref.shape[0]

    @pl.when(j == 0)
    def _():
        xs_ref[...] = jnp.zeros_like(xs_ref)

    ri = lax.broadcasted_iota(jnp.int32, (ROUTE_WINDOW, ROUTE_TILE), 0)

    def onehot(prow, lo, off, first):
        hit = prow - off == ri
        if not first:
            hit = hit & (prow >= lo)
        return jnp.where(hit, 1.0, 0.0).astype(BF16)

    def add_rows(ge, off, rows):
        sl = pl.ds(pl.multiple_of(off, SLOT_ALIGN), ROUTE_WINDOW)
        xs_ref[ge, sl, :] = xs_ref[ge, sl, :] + rows.astype(BF16)

    for sub in range(DISPATCH_TILES_PER_STEP):
        blk = (j * DISPATCH_TILES_PER_STEP + sub) * (ROUTE_TILE // 128)
        tok = slice(sub * ROUTE_TILE, (sub + 1) * ROUTE_TILE)
        h = h_ref[:, sub].reshape(ROUTE_TILE, h_ref.shape[-1])
        prows, offs = [], []
        for ge in range(group):
            e = g * group + ge
            prows.append(pos_ref[pl.ds(e, 1), tok])
            offs.append(_window(base_ref, e, blk, 0, cap))
        sel = jnp.concatenate([onehot(prows[ge], *offs[ge], True) for ge in range(group)], axis=0)
        rows = jnp.dot(sel, h, preferred_element_type=F32)
        for ge in range(group):
            add_rows(ge, offs[ge][1], rows[ge * ROUTE_WINDOW:(ge + 1) * ROUTE_WINDOW])

        for ge in range(group):
            e = g * group + ge

            def extra(w, carry, ge=ge, e=e, blk=blk, h=h, prow=prows[ge]):
                lo, off = _window(base_ref, e, blk, w, cap)
                add_rows(ge, off, jnp.dot(onehot(prow, lo, off, False), h, preferred_element_type=F32))
                return carry

            lax.fori_loop(1, _n_windows(base_ref, e, blk, cap), extra, 0)


def _tile_view(a, nb):
    t, d = a.shape
    r = ROUTE_TILE // nb
    return a.reshape(nb, t // (nb * r), r, d)


def _dispatch(base, h2, pos, cap, nb):
    t, d = h2.shape
    ne = pos.shape[0]
    step = ROUTE_TILE * DISPATCH_TILES_PER_STEP
    group = min(ne, DISPATCH_SLOT_ROWS // cap)
    out_spec = pl.BlockSpec((group, cap, d), lambda g, j, b: (g, 0, 0), pipeline_mode=pl.Buffered(1))
    return pl.pallas_call(
        functools.partial(_dispatch_kernel, cap=cap),
        grid_spec=pltpu.PrefetchScalarGridSpec(
            num_scalar_prefetch=1,
            grid=(ne // group, t // step),
            in_specs=[pl.BlockSpec((nb, DISPATCH_TILES_PER_STEP, ROUTE_TILE // nb, d),
                                   lambda g, j, b: (0, j, 0, 0)),
                      pl.BlockSpec((ne, step), lambda g, j, b: (0, j))],
            out_specs=out_spec),
        out_shape=jax.ShapeDtypeStruct((ne, cap, d), BF16),
        compiler_params=pltpu.CompilerParams(dimension_semantics=("arbitrary", "arbitrary"),
                                             vmem_limit_bytes=VMEM_LIMIT),
        name="dispatch",
    )(base, _tile_view(h2, nb), pos)


def _combine_kernel(base_ref, pos_ref, aff_ref, x1_ref, g_ref, g2_ref, eo_ref, o_ref,
                    f_ref, win_buf, win_sem, extra_buf, extra_sem, *, cap, n_steps):
    j = pl.program_id(0)
    blocks_per_tile = ROUTE_TILE // 128
    slot = lax.rem(j, 2)
    ri = lax.broadcasted_iota(jnp.int32, (ROUTE_WINDOW, ROUTE_TILE), 0)
    tn = (((0,), (0,)), ((), ()))

    def window_copy(e, off, dst, sem):
        return pltpu.make_async_copy(eo_ref.at[e, pl.ds(pl.multiple_of(off, SLOT_ALIGN), ROUTE_WINDOW), :],
                                     dst, sem)

    def window_slot(buf_slot, s, e):
        return win_buf.at[buf_slot, s, pl.ds(e * ROUTE_WINDOW, ROUTE_WINDOW), :]

    def first_window_copies(step, buf_slot):
        return [window_copy(e, _window(base_ref, e, (step * COMBINE_TILES_PER_STEP + s) * blocks_per_tile, 0, cap)[1],
                            window_slot(buf_slot, s, e), win_sem.at[buf_slot])
                for s in range(COMBINE_TILES_PER_STEP) for e in range(N_EXPERTS)]

    def wait_windows(buf_slot):
        for s in range(COMBINE_TILES_PER_STEP):
            for e in range(N_EXPERTS):
                window_copy(e, 0, window_slot(buf_slot, s, e), win_sem.at[buf_slot]).wait()

    @pl.when(j == 0)
    def _():
        for cp in first_window_copies(0, 0):
            cp.start()

    for cp in first_window_copies(jnp.minimum(j + 1, n_steps - 1), 1 - slot):
        cp.start()

    def weights(tok, e, lo, off, first):
        prow = pos_ref[e:e + 1, tok]
        hit = prow - off == ri
        if not first:
            hit = hit & (prow >= lo)
        return jnp.where(hit, aff_ref[e:e + 1, tok], 0.0).astype(BF16)

    qs = []
    for s in range(COMBINE_TILES_PER_STEP):
        blk = (j * COMBINE_TILES_PER_STEP + s) * blocks_per_tile
        tok = slice(s * ROUTE_TILE, (s + 1) * ROUTE_TILE)
        qs.append(jnp.concatenate([weights(tok, e, *_window(base_ref, e, blk, 0, cap), True)
                                   for e in range(N_EXPERTS)], axis=0))
    wait_windows(slot)

    nb, _, r, d = x1_ref.shape
    for s in range(COMBINE_TILES_PER_STEP):
        blk = (j * COMBINE_TILES_PER_STEP + s) * blocks_per_tile
        tok = slice(s * ROUTE_TILE, (s + 1) * ROUTE_TILE)
        f_ref[s] = lax.dot_general(qs[s], win_buf[slot, s], tn, preferred_element_type=F32)

        n_win = [_n_windows(base_ref, e, blk, cap) for e in range(N_EXPERTS)]

        @pl.when(functools.reduce(jnp.maximum, n_win) > 1)
        def _(s=s, blk=blk, tok=tok, n_win=n_win):
            for e in range(N_EXPERTS):
                def extra(w, carry, e=e):
                    lo, off = _window(base_ref, e, blk, w, cap)
                    cp = window_copy(e, off, extra_buf, extra_sem)
                    cp.start()
                    cp.wait()
                    f_ref[s] += lax.dot_general(weights(tok, e, lo, off, False), extra_buf[...], tn,
                                                preferred_element_type=F32)
                    return carry

                lax.fori_loop(1, n_win[e], extra, 0)

        f = f_ref[s].reshape(nb, r, d)
        o_ref[:, s] = x1_ref[:, s] + g2_ref[...] * _rms(f, g_ref[...])

    @pl.when(j == n_steps - 1)
    def _():
        wait_windows(1 - slot)


def _combine(base, eo, pos, aff, x1, mod3, npost, cond_block, cap, nb):
    t, d = x1.shape
    ne = pos.shape[0]
    tps = COMBINE_TILES_PER_STEP
    n_steps = t // (ROUTE_TILE * tps)
    tile = pl.BlockSpec((nb, tps, ROUTE_TILE // nb, d), lambda j, b: (0, j, 0, 0))
    etile = pl.BlockSpec((ne, tps * ROUTE_TILE), lambda j, b: (0, j))
    return pl.pallas_call(
        functools.partial(_combine_kernel, cap=cap, n_steps=n_steps),
        grid_spec=pltpu.PrefetchScalarGridSpec(
            num_scalar_prefetch=1,
            grid=(n_steps,),
            in_specs=[etile, etile, tile,
                      pl.BlockSpec((1, d), lambda j, b: (0, 0)),
                      pl.BlockSpec((nb, 1, d), lambda j, b: (cond_block, 0, 5)),
                      pl.BlockSpec(memory_space=pl.ANY)],
            out_specs=tile,
            scratch_shapes=[pltpu.VMEM((tps, ROUTE_TILE, d), F32),
                            pltpu.VMEM((2, tps, ne * ROUTE_WINDOW, d), BF16), pltpu.SemaphoreType.DMA((2,)),
                            pltpu.VMEM((ROUTE_WINDOW, d), BF16), pltpu.SemaphoreType.DMA(())]),
        out_shape=jax.ShapeDtypeStruct(_tile_view(x1, nb).shape, F32),
        compiler_params=pltpu.CompilerParams(dimension_semantics=("arbitrary",),
                                             vmem_limit_bytes=VMEM_LIMIT),
        name="combine",
    )(base, pos, aff, _tile_view(x1, nb), npost, mod3, eo).reshape(t, d)


def _trunk(x3, mod3, first_cond_row, shared_cond, weights, lgs, *, rope_tabs, init, emit_state):
    (n_pre_mix, n_post_mix, n_pre_ffn, n_post_ffn, w_in, gn_g, conv_w, wro, wco, wo, wr_pad, wg, wu, wd) = weights
    b, seq, d = x3.shape
    t = b * seq
    x = x3.reshape(t, d)
    cond_row_fn = (lambda r: first_cond_row) if shared_cond else (lambda r: first_cond_row + r // seq)
    nb = 1 if shared_cond else b
    assert ROUTE_TILE % nb == 0 and (ROUTE_TILE // nb) % SLOT_ALIGN == 0 and first_cond_row % nb == 0
    proj_a, proj_p, proj_s = _inproj(x, mod3, n_pre_mix, w_in, cond_row_fn, seq, rope_tabs)
    ret = _retention(proj_a, lgs, gn_g, seq, init=init, emit_state=emit_state)
    og = ret[0]
    x1, h2, aff_t = _mix(og, proj_a, proj_p, proj_s, x, mod3, conv_w, wro, wco, wo, wr_pad, n_post_mix,
                         n_pre_ffn, seq, cond_row_fn)
    cap = CAPACITY_FACTOR * t // N_EXPERTS
    pos, base, aff = _route(aff_t, cap, nb)
    eo = _experts(_dispatch(base, h2, pos, cap, nb), wg, wu, wd)
    y = _combine(base, eo, pos, aff, x1, mod3, n_post_ffn, first_cond_row // nb, cap, nb)
    return y.reshape(b, seq, d), ret[1:]


def kernel(x_prompt, x_sample, state_ret_fwd, state_ret_bwd, c, c_ctx, w_ada, b_ada, norm_pre_mix, norm_post_mix,
           norm_pre_ffn, norm_post_ffn, w_in, ret_decay_fwd, ret_decay_bwd, ret_norm_g, conv_w, w_ret_o, w_conv_o,
           w_o, w_router, w_gate, w_up, w_down):
    depth = w_ada.shape[0]
    assert depth == 1
    dec_b, dec_seq = x_sample.shape[0], x_sample.shape[1]
    xp, xs = x_prompt, x_sample
    l = 0
    cond = jnp.zeros((N_COND_ROWS, D_MODEL), F32).at[0:dec_b].set(c).at[CTX_COND_ROW].set(c_ctx)
    mod3 = _ada(cond, w_ada[l], b_ada[l][None, :]).reshape(N_COND_ROWS, 1, 6 * D_MODEL)
    lgs = jnp.stack([-jax.nn.softplus(-ret_decay_fwd[l].astype(F32)),
                     -jax.nn.softplus(-ret_decay_bwd[l].astype(F32))])
    wr_pad = jnp.pad(w_router[l], ((0, 0), (0, ROUTER_LANES - N_EXPERTS)))
    weights = (norm_pre_mix[l][None, :], norm_post_mix[l][None, :], norm_pre_ffn[l][None, :],
               norm_post_ffn[l][None, :], w_in[l].astype(BF16), ret_norm_g[l][None, :], conv_w[l],
               w_ret_o[l].astype(BF16), w_conv_o[l].astype(BF16), w_o[l].astype(BF16), wr_pad,
               w_gate[l], w_up[l], w_down[l])
    yp, (s_f, s_b) = _trunk(xp, mod3, CTX_COND_ROW, True, weights, lgs,
                            rope_tabs=None, init=None, emit_state=True)
    ys, _ = _trunk(xs, mod3, 0, False, weights, lgs,
                   rope_tabs=_rope_tables(dec_seq), init=(state_ret_fwd, state_ret_bwd), emit_state=False)
    return (yp, ys, s_f, s_b)
```

```python
import functools

import jax
import jax.numpy as jnp
import numpy as np
from jax import lax
from jax.experimental import pallas as pl
from jax.experimental.pallas import tpu as pltpu

F32 = jnp.float32
BF16 = jnp.bfloat16

D_MODEL = 1024
N_HEADS = 8
DK = 64
DV = 128
CHUNK = 128
GRID_W = 64
N_EXPERTS = 16
CAPACITY_FACTOR = 2
D_IN_TOTAL = 8192
RMS_EPS = 1e-6
GN_EPS = 1e-5
ROPE_BASE = 10000.0
N_COND_ROWS = 16
CTX_COND_ROW = 8
ROUTER_LANES = 128
RET_CHUNKS_PER_STEP = 32
RET_UNROLL = 16
MIX_TILE = 1024
MIX_SUB_TILE = 512
VMEM_LIMIT = 48 * 1024 * 1024
EXPERT_ROWS_PER_STEP = 1024
EXPERT_SUB_ROWS = 512
EXPERT_VMEM_LIMIT = 56 * 1024 * 1024


def _sigmoid(x):
    return 0.5 * jnp.tanh(0.5 * x) + 0.5


def _rms(x, g):
    return x * lax.rsqrt(jnp.mean(x * x, axis=-1, keepdims=True) + RMS_EPS) * g


def _ada_kernel(c_ref, w_ref, b_ref, o_ref):
    c = c_ref[...]
    s = c * _sigmoid(c)
    o_ref[...] = jnp.dot(s, w_ref[...], preferred_element_type=F32,
                         precision=lax.Precision.HIGHEST) + b_ref[...]


def _ada(cond, w_ada, b_ada):
    n = w_ada.shape[1]
    tn = 1024
    return pl.pallas_call(
        _ada_kernel,
        grid=(n // tn,),
        in_specs=[pl.BlockSpec((N_COND_ROWS, D_MODEL), lambda j: (0, 0)),
                  pl.BlockSpec((D_MODEL, tn), lambda j: (0, j)),
                  pl.BlockSpec((1, tn), lambda j: (0, j))],
        out_specs=pl.BlockSpec((N_COND_ROWS, tn), lambda j: (0, j)),
        out_shape=jax.ShapeDtypeStruct((N_COND_ROWS, n), F32),
        compiler_params=pltpu.CompilerParams(dimension_semantics=("arbitrary",),
                                             vmem_limit_bytes=VMEM_LIMIT),
        name="ada_mod",
    )(cond, w_ada, b_ada)


def _inproj_body(x_ref, g_ref, sh_ref, sc_ref, w_ref, cos_ref, sin_ref, a_ref, p_ref, s_ref, h_ref, use_rope):
    j = pl.program_id(1)

    @pl.when(j == 0)
    def _():
        h = _rms(x_ref[...], g_ref[...]) * (1.0 + sc_ref[0]) + sh_ref[0]
        h_ref[...] = h.astype(BF16)

    lanes = 2 * DK
    width = 2 * lanes
    n_slices = w_ref.shape[1] // width
    n_qk = 2 * N_HEADS * DK

    def slice_dot(b):
        return jnp.dot(h_ref[...], w_ref[:, b * width:(b + 1) * width], preferred_element_type=F32)

    @pl.when(j == 0)
    def _():
        ci = lax.broadcasted_iota(jnp.int32, (h_ref.shape[0], lanes), 1)
        for b in range(n_slices):
            acc = slice_dot(b)
            if b * width >= n_qk:
                a_ref[:, b * width:(b + 1) * width] = acc.astype(BF16)
                continue
            for half in range(2):
                col = b * width + half * lanes
                x = acc[:, half * lanes:(half + 1) * lanes]
                if col >= N_HEADS * DK:
                    x = x * (DK ** -0.5)
                if use_rope:
                    swapped = jnp.where((ci & 31) < 16, pltpu.roll(x, lanes - 16, 1), pltpu.roll(x, 16, 1))
                    x = x * cos_ref[...] + swapped * sin_ref[...]
                a_ref[:, col:col + lanes] = x.astype(BF16)

    @pl.when(j == 1)
    def _():
        for b in range(n_slices):
            acc = slice_dot(b)
            if b < n_slices // 2:
                acc = acc * _sigmoid(acc)
            a_ref[:, b * width:(b + 1) * width] = acc.astype(BF16)

    @pl.when(j == 2)
    def _():
        for b in range(n_slices // 2):
            p_ref[:, b * width:(b + 1) * width] = (slice_dot(b) * slice_dot(b + n_slices // 2)).astype(BF16)

    @pl.when(j == 3)
    def _():
        for b in range(n_slices):
            s_ref[:, b * width:(b + 1) * width] = _sigmoid(slice_dot(b)).astype(BF16)


def _inproj_kernel(x_ref, g_ref, sh_ref, sc_ref, w_ref, *rest, use_rope):
    if use_rope:
        cos_ref, sin_ref, a_ref, p_ref, s_ref, h_ref = rest
    else:
        cos_ref = sin_ref = None
        a_ref, p_ref, s_ref, h_ref = rest
    _inproj_body(x_ref, g_ref, sh_ref, sc_ref, w_ref, cos_ref, sin_ref, a_ref, p_ref, s_ref, h_ref, use_rope)


def _inproj(x, mod3, g, w_bf16, cond_row_fn, seq, rope_tabs):
    t = x.shape[0]
    tm, tn = 1024, 2048
    assert D_IN_TOTAL == 4 * tn
    in_specs = [pl.BlockSpec((tm, D_MODEL), lambda i, j: (i, 0)),
                pl.BlockSpec((1, D_MODEL), lambda i, j: (0, 0)),
                pl.BlockSpec((1, 1, D_MODEL), lambda i, j: (cond_row_fn(i * tm), 0, 0)),
                pl.BlockSpec((1, 1, D_MODEL), lambda i, j: (cond_row_fn(i * tm), 0, 1)),
                pl.BlockSpec((D_MODEL, tn), lambda i, j: (0, j))]
    args = [x, g, mod3, mod3, w_bf16]
    if rope_tabs is not None:
        tiles_per_seq = seq // tm
        in_specs += [pl.BlockSpec((tm, 2 * DK), lambda i, j: (i % tiles_per_seq, 0))] * 2
        args += list(rope_tabs)
    return pl.pallas_call(
        functools.partial(_inproj_kernel, use_rope=rope_tabs is not None),
        grid=(t // tm, D_IN_TOTAL // tn),
        in_specs=in_specs,
        out_specs=[pl.BlockSpec((tm, tn), lambda i, j: (i, jnp.minimum(j, 1))),
                   pl.BlockSpec((tm, tn // 2), lambda i, j: (i, 0)),
                   pl.BlockSpec((tm, tn), lambda i, j: (i, 0))],
        out_shape=[jax.ShapeDtypeStruct((t, 2 * tn), BF16),
                   jax.ShapeDtypeStruct((t, tn // 2), BF16),
                   jax.ShapeDtypeStruct((t, tn), BF16)],
        scratch_shapes=[pltpu.VMEM((tm, D_MODEL), BF16)],
        compiler_params=pltpu.CompilerParams(dimension_semantics=("arbitrary", "arbitrary"),
                                             vmem_limit_bytes=VMEM_LIMIT),
        name="inproj",
    )(*args)


def _ret_kernel(*refs, nc, bb, has_init, emit_state):
    it = iter(refs)
    lg_ref = next(it)
    q_ref, k_ref, v_ref, gr_ref, gn_ref = (next(it) for _ in range(5))
    s0f_ref = s0b_ref = sf_ref = sb_ref = None
    if has_init:
        s0f_ref, s0b_ref = next(it), next(it)
    o_ref = next(it)
    if emit_state:
        sf_ref, sb_ref = next(it), next(it)
    kvf_s, kvb_s, rcat_s, p_s, o_s, r0 = (next(it) for _ in range(6))

    p = pl.program_id(1)
    lgf_a, lgf_b = lg_ref[0, 2 * p], lg_ref[0, 2 * p + 1]
    lgb_a, lgb_b = lg_ref[1, 2 * p], lg_ref[1, 2 * p + 1]

    ri = lax.broadcasted_iota(jnp.int32, (CHUNK, 2 * DK), 0)
    ci = lax.broadcasted_iota(jnp.int32, (CHUNK, 2 * DK), 1)
    lane_a = ci < DK
    rowf = ri.astype(F32)
    diff = rowf - ci.astype(F32)
    lgf_lane = jnp.where(lane_a, lgf_a, lgf_b)
    lgb_lane = jnp.where(lane_a, lgb_a, lgb_b)
    xi_f = jnp.exp(lgf_lane * (rowf + 1.0))
    xi_b = jnp.exp(lgb_lane * (CHUNK - rowf))
    zeta_f = jnp.exp(lgf_lane * (CHUNK - 1.0 - rowf))
    zeta_b = jnp.exp(lgb_lane * rowf)

    def decay_matrix(lgf, lgb):
        return jnp.where(diff > 0, jnp.exp(lgf * diff),
                         jnp.where(diff < 0, jnp.exp(lgb * (-diff)), 2.0))

    dm_a = decay_matrix(lgf_a, lgb_a)
    dm_b = decay_matrix(lgf_b, lgb_b)

    r2 = lax.broadcasted_iota(jnp.int32, (2 * DK, 2 * DV), 0)
    c2 = lax.broadcasted_iota(jnp.int32, (2 * DK, 2 * DV), 1)
    top = r2 < DK
    blk = (top == (c2 < DV)).astype(F32)
    cd_f = jnp.exp(jnp.where(top, lgf_a, lgf_b) * float(CHUNK)) * blk
    cd_b = jnp.exp(jnp.where(top, lgb_a, lgb_b) * float(CHUNK)) * blk

    def chunk_rows(c):
        return pl.ds(pl.multiple_of(c * CHUNK, CHUNK), CHUNK)

    n_chunks = bb * nc
    unroll = RET_UNROLL

    def kv_body(c, carry):
        rows = chunk_rows(c)
        k = k_ref[rows, :].astype(F32)
        kz_t = jnp.concatenate([k * zeta_f, k * zeta_b], axis=1).T.astype(BF16)
        kv = jnp.dot(kz_t, v_ref[rows, :], preferred_element_type=F32)
        kvf_s[c] = kv[0:2 * DK] * blk
        kvb_s[c] = kv[2 * DK:4 * DK] * blk
        return carry

    def score_body(c, carry):
        rows = chunk_rows(c)
        q16 = q_ref[rows, :]
        zero = jnp.zeros_like(q16)
        q_ab = jnp.concatenate([jnp.where(lane_a, q16, zero), jnp.where(lane_a, zero, q16)], axis=0)
        s = lax.dot_general(q_ab, k_ref[rows, :], (((1,), (1,)), ((), ())), preferred_element_type=F32)
        p_s[c, :, 0:CHUNK] = (s[0:CHUNK] * dm_a).astype(BF16)
        p_s[c, :, CHUNK:2 * CHUNK] = (s[CHUNK:2 * CHUNK] * dm_b).astype(BF16)
        return carry

    lax.fori_loop(0, n_chunks, lambda c, carry: score_body(c, kv_body(c, carry)), 0, unroll=unroll)

    def load_state(s_ref, s):
        r0[...] = jnp.zeros_like(r0)
        r0[0:DK, 0:DV] = s_ref[s, 0, 0].astype(F32)
        r0[DK:2 * DK, DV:2 * DV] = s_ref[s, 0, 1].astype(F32)
        return r0[...]

    for s in range(bb):
        def fwd_body(n, rf, s=s):
            c = s * nc + n
            rcat_s[c, 0:2 * DK, :] = rf.astype(BF16)
            return cd_f * rf + kvf_s[c]

        def bwd_body(t, rb, s=s):
            c = s * nc + nc - 1 - t
            rcat_s[c, 2 * DK:4 * DK, :] = rb.astype(BF16)
            return cd_b * rb + kvb_s[c]

        zeros = jnp.zeros((2 * DK, 2 * DV), F32)
        rf_fin = lax.fori_loop(0, nc, fwd_body, load_state(s0f_ref, s) if has_init else zeros)
        rb_fin = lax.fori_loop(0, nc, bwd_body, load_state(s0b_ref, s) if has_init else zeros)
        if emit_state:
            for s_ref, r in ((sf_ref, rf_fin), (sb_ref, rb_fin)):
                s_ref[s, 0, 0] = r[0:DK, 0:DV]
                s_ref[s, 0, 1] = r[DK:2 * DK, DV:2 * DV]

    gn = gn_ref[...]

    hr = lax.broadcasted_iota(jnp.int32, (2 * DV, 2 * DV), 0)
    hc = lax.broadcasted_iota(jnp.int32, (2 * DV, 2 * DV), 1)
    head_mean = jnp.where((hr < DV) == (hc < DV), 1.0 / DV, 0.0).astype(BF16)
    v_lane_a = lax.broadcasted_iota(jnp.int32, (CHUNK, 2 * DV), 1) < DV

    def value_body(c, carry):
        rows = chunk_rows(c)
        q = q_ref[rows, :].astype(F32)
        v = v_ref[rows, :]
        zero = jnp.zeros_like(v)
        v_bd = jnp.concatenate([jnp.where(v_lane_a, v, zero), jnp.where(v_lane_a, zero, v)], axis=0)
        qx = jnp.concatenate([(q * xi_f).astype(BF16), (q * xi_b).astype(BF16)], axis=1)
        o_s[c] = (jnp.dot(p_s[c], v_bd, preferred_element_type=F32)
                  + jnp.dot(qx, rcat_s[c], preferred_element_type=F32))
        return carry

    lax.fori_loop(0, n_chunks, value_body, 0, unroll=unroll)

    def gate_body(c, carry):
        rows = chunk_rows(c)
        o = o_s[c]
        d = o - jnp.dot(o.astype(BF16), head_mean, preferred_element_type=F32)
        var = jnp.dot((d * d).astype(BF16), head_mean, preferred_element_type=F32)
        y = d * lax.rsqrt(var + GN_EPS)
        o_ref[rows, :] = (gr_ref[rows, :].astype(F32) * (y * gn)).astype(BF16)
        return carry

    lax.fori_loop(0, n_chunks, gate_body, 0, unroll=unroll)


def _retention(proj, lgs, gn_g, seq, *, init=None, emit_state):
    t = proj.shape[0]
    b = t // seq
    nc = seq // CHUNK
    bb = max(1, RET_CHUNKS_PER_STEP // nc)
    rows = bb * seq
    has_init = init is not None
    pairs = N_HEADS // 2
    qk_blocks = (N_HEADS * DK) // (2 * DK)
    in_specs = [pl.BlockSpec(memory_space=pltpu.SMEM),
                pl.BlockSpec((rows, 2 * DK), lambda i, p: (i, p)),
                pl.BlockSpec((rows, 2 * DK), lambda i, p: (i, qk_blocks + p)),
                pl.BlockSpec((rows, 2 * DV), lambda i, p: (i, qk_blocks + p)),
                pl.BlockSpec((rows, 2 * DV), lambda i, p: (i, 2 * qk_blocks + p)),
                pl.BlockSpec((1, 2 * DV), lambda i, p: (0, p))]
    args = [lgs, proj, proj, proj, proj, gn_g]
    state_spec = pl.BlockSpec((bb, 1, 2, DK, DV), lambda i, p: (i, 0, p, 0, 0))
    if has_init:
        in_specs += [state_spec, state_spec]
        args += list(init)
    out_specs = [pl.BlockSpec((rows, 2 * DV), lambda i, p: (i, p))]
    out_shape = [jax.ShapeDtypeStruct((t, N_HEADS * DV), BF16)]
    if emit_state:
        out_specs += [state_spec, state_spec]
        out_shape += [jax.ShapeDtypeStruct((b, 1, N_HEADS, DK, DV), F32)] * 2
    return pl.pallas_call(
        functools.partial(_ret_kernel, nc=nc, bb=bb, has_init=has_init, emit_state=emit_state),
        grid=(b // bb, pairs),
        in_specs=in_specs,
        out_specs=out_specs,
        out_shape=out_shape,
        scratch_shapes=[pltpu.VMEM((bb * nc, 2 * DK, 2 * DV), F32),
                        pltpu.VMEM((bb * nc, 2 * DK, 2 * DV), F32),
                        pltpu.VMEM((bb * nc, 4 * DK, 2 * DV), BF16),
                        pltpu.VMEM((bb * nc, CHUNK, 2 * CHUNK), BF16),
                        pltpu.VMEM((bb * nc, CHUNK, 2 * DV), F32),
                        pltpu.VMEM((2 * DK, 2 * DV), F32)],
        compiler_params=pltpu.CompilerParams(dimension_semantics=("arbitrary", "arbitrary"),
                                             vmem_limit_bytes=VMEM_LIMIT),
        name="retention",
    )(*args)


def _rope_tables(seq):
    n_freq = DK // 4
    pos = np.arange(seq)
    inv = jnp.asarray(ROPE_BASE, F32) ** (-jnp.arange(n_freq, dtype=F32) / n_freq)
    ang_r = jnp.asarray(pos // GRID_W, F32)[:, None] * inv
    ang_c = jnp.asarray(pos % GRID_W, F32)[:, None] * inv
    cos = jnp.concatenate([jnp.cos(ang_r)] * 2 + [jnp.cos(ang_c)] * 2, axis=1)
    sin = jnp.concatenate([-jnp.sin(ang_r), jnp.sin(ang_r), -jnp.sin(ang_c), jnp.sin(ang_c)], axis=1)
    return jnp.tile(cos, (1, 2)), jnp.tile(sin, (1, 2))


def _mix_kernel(og_ref, cb_ref, p_ref, pp_ref, pn_ref, sa_ref, sb_ref, x_ref,
                cw_ref, wro_ref, wco_ref, wo_ref, wr_ref, npost_ref, npre_ref, g1_ref, sh2_ref, sc2_ref,
                x1_ref, h2_ref, aff_ref, *, tm, sub, seq):
    i = pl.program_id(0)
    halo = pp_ref.shape[0]
    wr = wr_ref[...]
    w_hi = wr.astype(BF16)
    w_lo = (wr - w_hi.astype(F32)).astype(BF16)
    w_hi_lo = jnp.concatenate([w_hi, w_lo], axis=1)
    row = lax.broadcasted_iota(jnp.int32, (sub, 1), 0)
    lane = lax.broadcasted_iota(jnp.int32, (sub, ROUTER_LANES), 1)

    for s in range(tm // sub):
        r0 = s * sub
        rows = slice(r0, r0 + sub)
        token = i * tm + r0 + row
        in_seq = token & (seq - 1) if seq & (seq - 1) == 0 else token % seq
        prev_row = (pp_ref[halo - 1:halo, :] if s == 0 else p_ref[r0 - 1:r0, :]).astype(F32)
        next_row = (pn_ref[0:1, :] if r0 + sub == tm else p_ref[r0 + sub:r0 + sub + 1, :]).astype(F32)
        prod = p_ref[rows, :].astype(F32)
        up = jnp.where(in_seq == 0, 0.0, jnp.where(row == 0, prev_row, pltpu.roll(prod, 1, 0)))
        dn = jnp.where(in_seq == seq - 1, 0.0, jnp.where(row == sub - 1, next_row, pltpu.roll(prod, sub - 1, 0)))
        u = up * cw_ref[0:1, :] + prod * cw_ref[1:2, :] + dn * cw_ref[2:3, :]
        y_conv = jnp.dot((cb_ref[rows, :].astype(F32) * u).astype(BF16), wco_ref[...],
                         preferred_element_type=F32)
        y_ret = jnp.dot(og_ref[rows, :], wro_ref[...], preferred_element_type=F32)
        merged = sa_ref[rows, :].astype(F32) * y_ret + sb_ref[rows, :].astype(F32) * y_conv
        m = jnp.dot(merged.astype(BF16), wo_ref[...], preferred_element_type=F32)
        x1 = x_ref[rows, :] + g1_ref[0] * _rms(m, npost_ref[...])
        x1_ref[rows, :] = x1
        h2 = _rms(x1, npre_ref[...]) * (1.0 + sc2_ref[0]) + sh2_ref[0]
        h_hi = h2.astype(BF16)
        h2_ref[rows, :] = h_hi
        h_lo = (h2 - h_hi.astype(F32)).astype(BF16)
        hi_terms = jnp.dot(h_hi, w_hi_lo, preferred_element_type=F32)
        logits = (hi_terms[:, 0:ROUTER_LANES] + hi_terms[:, ROUTER_LANES:2 * ROUTER_LANES]
                  + jnp.dot(h_lo, w_hi, preferred_element_type=F32))
        logits = jnp.where(lane < N_EXPERTS, logits, -jnp.inf)
        e = jnp.exp(logits - jnp.max(logits, axis=-1, keepdims=True))
        aff = e / jnp.sum(e, axis=-1, keepdims=True)
        aff_ref[:, rows] = aff.T[0:N_EXPERTS, :]


def _mix(og, proj_a, proj_p, proj_s, x, mod3, conv_w, wro, wco, wo, wr_pad, npost, npre, seq, cond_row_fn):
    t = x.shape[0]
    conv_b_block = (2 * N_HEADS * DK + 2 * N_HEADS * DV) // D_MODEL
    tm = MIX_TILE
    halo = 16
    hb = tm // halo
    last_halo = t // halo - 1
    col = lambda c: (lambda i: (i, c))
    row_vec = pl.BlockSpec((1, D_MODEL), lambda i: (0, 0))
    wspec = pl.BlockSpec((D_MODEL, D_MODEL), lambda i: (0, 0))
    modspec = lambda c: pl.BlockSpec((1, 1, D_MODEL), lambda i: (cond_row_fn(i * tm), 0, c))
    tile = lambda c: pl.BlockSpec((tm, D_MODEL), col(c))
    prev = lambda c: pl.BlockSpec((halo, D_MODEL), lambda i: (jnp.maximum(i * hb - 1, 0), c))
    nxt = lambda c: pl.BlockSpec((halo, D_MODEL), lambda i: (jnp.minimum((i + 1) * hb, last_halo), c))
    return pl.pallas_call(
        functools.partial(_mix_kernel, tm=tm, sub=MIX_SUB_TILE, seq=seq),
        grid=(t // tm,),
        in_specs=[tile(0), tile(conv_b_block), tile(0), prev(0), nxt(0), tile(0), tile(1),
                  tile(0),
                  pl.BlockSpec((3, D_MODEL), lambda i: (0, 0)), wspec, wspec, wspec,
                  pl.BlockSpec((D_MODEL, ROUTER_LANES), lambda i: (0, 0)),
                  row_vec, row_vec, modspec(2), modspec(3), modspec(4)],
        out_specs=[tile(0), tile(0), pl.BlockSpec((N_EXPERTS, tm), lambda i: (0, i))],
        out_shape=[jax.ShapeDtypeStruct((t, D_MODEL), F32),
                   jax.ShapeDtypeStruct((t, D_MODEL), BF16),
                   jax.ShapeDtypeStruct((N_EXPERTS, t), F32)],
        compiler_params=pltpu.CompilerParams(dimension_semantics=("arbitrary",),
                                             vmem_limit_bytes=VMEM_LIMIT),
        name="mix_out",
    )(og, proj_a, proj_p, proj_p, proj_p, proj_s, proj_s, x,
      conv_w, wro, wco, wo, wr_pad, npost, npre, mod3, mod3, mod3)


def _expert_kernel(xs_ref, wg_ref, wu_ref, wd_ref, o_ref, *, sub):
    wg = wg_ref[0].astype(BF16)
    wu = wu_ref[0].astype(BF16)
    wd = wd_ref[0].astype(BF16)
    for r in range(xs_ref.shape[1] // sub):
        rows = slice(r * sub, (r + 1) * sub)
        x = xs_ref[0, rows, :]
        g = jnp.dot(x, wg, preferred_element_type=F32)
        u = jnp.dot(x, wu, preferred_element_type=F32)
        hid = (g * _sigmoid(g) * u).astype(BF16)
        o_ref[0, rows, :] = jnp.dot(hid, wd, preferred_element_type=F32).astype(BF16)


def _experts(xs, wg, wu, wd):
    e, cap, d = xs.shape
    ff = wg.shape[2]
    tr = min(cap, EXPERT_ROWS_PER_STEP)
    return pl.pallas_call(
        functools.partial(_expert_kernel, sub=EXPERT_SUB_ROWS),
        grid=(e, cap // tr),
        in_specs=[pl.BlockSpec((1, tr, d), lambda a, r: (a, r, 0)),
                  pl.BlockSpec((1, d, ff), lambda a, r: (a, 0, 0)),
                  pl.BlockSpec((1, d, ff), lambda a, r: (a, 0, 0)),
                  pl.BlockSpec((1, ff, d), lambda a, r: (a, 0, 0))],
        out_specs=pl.BlockSpec((1, tr, d), lambda a, r: (a, r, 0)),
        out_shape=jax.ShapeDtypeStruct((e, cap, d), BF16),
        compiler_params=pltpu.CompilerParams(dimension_semantics=("arbitrary", "arbitrary"),
                                             vmem_limit_bytes=EXPERT_VMEM_LIMIT),
        name="experts",
    )(xs, wg, wu, wd)


def _block_ranks(mask):
    ne, nblk, lanes = mask.shape
    li = lax.broadcasted_iota(jnp.int32, (lanes, lanes), 0)
    lj = lax.broadcasted_iota(jnp.int32, (lanes, lanes), 1)
    incl_lanes = jnp.where(li <= lj, 1.0, 0.0).astype(BF16)
    all_lanes = jnp.ones((lanes, lanes), BF16)
    bi = lax.broadcasted_iota(jnp.int32, (nblk, nblk), 0)
    bj = lax.broadcasted_iota(jnp.int32, (nblk, nblk), 1)
    earlier_blocks = jnp.where(bj < bi, 1.0, 0.0).astype(BF16)
    m = mask.reshape(ne * nblk, lanes).astype(BF16)
    incl = jnp.dot(m, incl_lanes, preferred_element_type=F32).reshape(ne, nblk, lanes)
    tot = jnp.dot(m, all_lanes, preferred_element_type=F32).reshape(ne, nblk, lanes).astype(BF16)
    excl = jnp.stack([jnp.dot(earlier_blocks, tot[i], preferred_element_type=F32) for i in range(ne)])
    return excl, incl


def _select_kernel(aff_ref, sel_ref, *, cap):
    ne, nblk, lanes = aff_ref.shape
    aff = aff_ref[...]

    def count(mask):
        m = jnp.where(mask, 1.0, 0.0)
        return jnp.sum(jnp.sum(m, axis=1, keepdims=True), axis=2, keepdims=True)

    def as_float(bits):
        return lax.bitcast_convert_type(bits, F32)

    def bit_step(it, thr_bits):
        cand = thr_bits | jnp.left_shift(jnp.int32(1), 30 - it)
        return jnp.where(count(aff >= as_float(cand)) >= cap, cand, thr_bits)

    thr = as_float(lax.fori_loop(0, 31, bit_step, jnp.zeros((ne, 1, 1), jnp.int32)))
    gt = aff > thr
    eq = aff == thr
    need = cap - count(gt)
    ex_eq, in_eq = _block_ranks(jnp.where(eq, 1.0, 0.0))
    sel_ref[...] = jnp.where(gt | (eq & (ex_eq + in_eq - 1.0 < need)), 1.0, 0.0)


def _slot_kernel(sel_ref, pos_ref, excl_ref):
    sel = sel_ref[...]
    ex_sel, in_sel = _block_ranks(sel)
    pos_ref[...] = jnp.where(sel > 0.0, ex_sel + in_sel - 1.0, -1.0).astype(jnp.int32)
    excl_ref[...] = ex_sel.astype(jnp.int32)


def _to_tile_order(a, nb):
    if nb == 1:
        return a
    ne, t = a.shape
    r = ROUTE_TILE // nb
    return a.reshape(ne, nb, t // (nb * r), r).transpose(0, 2, 1, 3).reshape(ne, t)


def _route(aff_t, cap, nb):
    ne, t = aff_t.shape
    nblk = t // 128
    shape = (ne, nblk, 128)
    full = pl.BlockSpec(shape, lambda i: (0, 0, 0))
    params = pltpu.CompilerParams(dimension_semantics=("arbitrary",), vmem_limit_bytes=VMEM_LIMIT)
    sel = pl.pallas_call(
        functools.partial(_select_kernel, cap=cap),
        grid=(1,), in_specs=[full], out_specs=full,
        out_shape=jax.ShapeDtypeStruct(shape, F32),
        compiler_params=params, name="route_select",
    )(aff_t.reshape(shape))
    sel = _to_tile_order(sel.reshape(ne, t), nb)
    pos, excl = pl.pallas_call(
        _slot_kernel,
        grid=(1,), in_specs=[full], out_specs=[full, full],
        out_shape=[jax.ShapeDtypeStruct(shape, jnp.int32)] * 2,
        compiler_params=params, name="route_slots",
    )(sel.reshape(shape))
    base = jnp.concatenate([excl[:, :, 0], jnp.full((ne, 1), cap, jnp.int32)], axis=1)
    return pos.reshape(ne, t), base, _to_tile_order(aff_t, nb)


ROUTE_TILE = 256
ROUTE_WINDOW = 64
SLOT_ALIGN = 16
DISPATCH_SLOT_ROWS = 16384
DISPATCH_TILES_PER_STEP = 8
COMBINE_TILES_PER_STEP = 4


def _window(base_ref, e, blk, w, cap):
    lo = base_ref[e, blk] // SLOT_ALIGN + w * (ROUTE_WINDOW // SLOT_ALIGN)
    return lo * SLOT_ALIGN, jnp.minimum(lo, (cap - ROUTE_WINDOW) // SLOT_ALIGN) * SLOT_ALIGN


def _n_windows(base_ref, e, blk, cap):
    lo, _ = _window(base_ref, e, blk, 0, cap)
    end = base_ref[e, blk + ROUTE_TILE // 128]
    return (end - lo + ROUTE_WINDOW - 1) // ROUTE_WINDOW


def _dispatch_kernel(base_ref, h_ref, pos_ref, xs_ref, *, cap):
    g, j = pl.program_id(0), pl.program_id(1)
    group = xs_ref.shape[0]

    @pl.when(j == 0)
    def _():
        xs_ref[...] = jnp.zeros_like(xs_ref)

    ri = lax.broadcasted_iota(jnp.int32, (ROUTE_WINDOW, ROUTE_TILE), 0)

    def onehot(prow, lo, off, first):
        hit = prow - off == ri
        if not first:
            hit = hit & (prow >= lo)
        return jnp.where(hit, 1.0, 0.0).astype(BF16)

    def add_rows(ge, off, rows):
        sl = pl.ds(pl.multiple_of(off, SLOT_ALIGN), ROUTE_WINDOW)
        xs_ref[ge, sl, :] = xs_ref[ge, sl, :] + rows.astype(BF16)

    for sub in range(DISPATCH_TILES_PER_STEP):
        blk = (j * DISPATCH_TILES_PER_STEP + sub) * (ROUTE_TILE // 128)
        tok = slice(sub * ROUTE_TILE, (sub + 1) * ROUTE_TILE)
        h = h_ref[:, sub].reshape(ROUTE_TILE, h_ref.shape[-1])
        prows, offs = [], []
        for ge in range(group):
            e = g * group + ge
            prows.append(pos_ref[pl.ds(e, 1), tok])
            offs.append(_window(base_ref, e, blk, 0, cap))
        sel = jnp.concatenate([onehot(prows[ge], *offs[ge], True) for ge in range(group)], axis=0)
        rows = jnp.dot(sel, h, preferred_element_type=F32)
        for ge in range(group):
            add_rows(ge, offs[ge][1], rows[ge * ROUTE_WINDOW:(ge + 1) * ROUTE_WINDOW])

        for ge in range(group):
            e = g * group + ge

            def extra(w, carry, ge=ge, e=e, blk=blk, h=h, prow=prows[ge]):
                lo, off = _window(base_ref, e, blk, w, cap)
                add_rows(ge, off, jnp.dot(onehot(prow, lo, off, False), h, preferred_element_type=F32))
                return carry

            lax.fori_loop(1, _n_windows(base_ref, e, blk, cap), extra, 0)


def _tile_view(a, nb):
    t, d = a.shape
    r = ROUTE_TILE // nb
    return a.reshape(nb, t // (nb * r), r, d)


def _dispatch(base, h2, pos, cap, nb):
    t, d = h2.shape
    ne = pos.shape[0]
    step = ROUTE_TILE * DISPATCH_TILES_PER_STEP
    group = min(ne, DISPATCH_SLOT_ROWS // cap)
    out_spec = pl.BlockSpec((group, cap, d), lambda g, j, b: (g, 0, 0), pipeline_mode=pl.Buffered(1))
    return pl.pallas_call(
        functools.partial(_dispatch_kernel, cap=cap),
        grid_spec=pltpu.PrefetchScalarGridSpec(
            num_scalar_prefetch=1,
            grid=(ne // group, t // step),
            in_specs=[pl.BlockSpec((nb, DISPATCH_TILES_PER_STEP, ROUTE_TILE // nb, d),
                                   lambda g, j, b: (0, j, 0, 0)),
                      pl.BlockSpec((ne, step), lambda g, j, b: (0, j))],
            out_specs=out_spec),
        out_shape=jax.ShapeDtypeStruct((ne, cap, d), BF16),
        compiler_params=pltpu.CompilerParams(dimension_semantics=("arbitrary", "arbitrary"),
                                             vmem_limit_bytes=VMEM_LIMIT),
        name="dispatch",
    )(base, _tile_view(h2, nb), pos)


def _combine_kernel(base_ref, pos_ref, aff_ref, x1_ref, g_ref, g2_ref, eo_ref, o_ref,
                    f_ref, win_buf, win_sem, extra_buf, extra_sem, *, cap, n_steps):
    j = pl.program_id(0)
    blocks_per_tile = ROUTE_TILE // 128
    slot = lax.rem(j, 2)
    ri = lax.broadcasted_iota(jnp.int32, (ROUTE_WINDOW, ROUTE_TILE), 0)
    tn = (((0,), (0,)), ((), ()))

    def window_copy(e, off, dst, sem):
        return pltpu.make_async_copy(eo_ref.at[e, pl.ds(pl.multiple_of(off, SLOT_ALIGN), ROUTE_WINDOW), :],
                                     dst, sem)

    def window_slot(buf_slot, s, e):
        return win_buf.at[buf_slot, s, pl.ds(e * ROUTE_WINDOW, ROUTE_WINDOW), :]

    def first_window_copies(step, buf_slot):
        return [window_copy(e, _window(base_ref, e, (step * COMBINE_TILES_PER_STEP + s) * blocks_per_tile, 0, cap)[1],
                            window_slot(buf_slot, s, e), win_sem.at[buf_slot])
                for s in range(COMBINE_TILES_PER_STEP) for e in range(N_EXPERTS)]

    def wait_windows(buf_slot):
        for s in range(COMBINE_TILES_PER_STEP):
            for e in range(N_EXPERTS):
                window_copy(e, 0, window_slot(buf_slot, s, e), win_sem.at[buf_slot]).wait()

    @pl.when(j == 0)
    def _():
        for cp in first_window_copies(0, 0):
            cp.start()

    for cp in first_window_copies(jnp.minimum(j + 1, n_steps - 1), 1 - slot):
        cp.start()

    def weights(tok, e, lo, off, first):
        prow = pos_ref[e:e + 1, tok]
        hit = prow - off == ri
        if not first:
            hit = hit & (prow >= lo)
        return jnp.where(hit, aff_ref[e:e + 1, tok], 0.0).astype(BF16)

    qs = []
    for s in range(COMBINE_TILES_PER_STEP):
        blk = (j * COMBINE_TILES_PER_STEP + s) * blocks_per_tile
        tok = slice(s * ROUTE_TILE, (s + 1) * ROUTE_TILE)
        qs.append(jnp.concatenate([weights(tok, e, *_window(base_ref, e, blk, 0, cap), True)
                                   for e in range(N_EXPERTS)], axis=0))
    wait_windows(slot)

    nb, _, r, d = x1_ref.shape
    for s in range(COMBINE_TILES_PER_STEP):
        blk = (j * COMBINE_TILES_PER_STEP + s) * blocks_per_tile
        tok = slice(s * ROUTE_TILE, (s + 1) * ROUTE_TILE)
        f_ref[s] = lax.dot_general(qs[s], win_buf[slot, s], tn, preferred_element_type=F32)

        n_win = [_n_windows(base_ref, e, blk, cap) for e in range(N_EXPERTS)]

        @pl.when(functools.reduce(jnp.maximum, n_win) > 1)
        def _(s=s, blk=blk, tok=tok, n_win=n_win):
            for e in range(N_EXPERTS):
                def extra(w, carry, e=e):
                    lo, off = _window(base_ref, e, blk, w, cap)
                    cp = window_copy(e, off, extra_buf, extra_sem)
                    cp.start()
                    cp.wait()
                    f_ref[s] += lax.dot_general(weights(tok, e, lo, off, False), extra_buf[...], tn,
                                                preferred_element_type=F32)
                    return carry

                lax.fori_loop(1, n_win[e], extra, 0)

        f = f_ref[s].reshape(nb, r, d)
        o_ref[:, s] = x1_ref[:, s] + g2_ref[...] * _rms(f, g_ref[...])

    @pl.when(j == n_steps - 1)
    def _():
        wait_windows(1 - slot)


def _combine(base, eo, pos, aff, x1, mod3, npost, cond_block, cap, nb):
    t, d = x1.shape
    ne = pos.shape[0]
    tps = COMBINE_TILES_PER_STEP
    n_steps = t // (ROUTE_TILE * tps)
    tile = pl.BlockSpec((nb, tps, ROUTE_TILE // nb, d), lambda j, b: (0, j, 0, 0))
    etile = pl.BlockSpec((ne, tps * ROUTE_TILE), lambda j, b: (0, j))
    return pl.pallas_call(
        functools.partial(_combine_kernel, cap=cap, n_steps=n_steps),
        grid_spec=pltpu.PrefetchScalarGridSpec(
            num_scalar_prefetch=1,
            grid=(n_steps,),
            in_specs=[etile, etile, tile,
                      pl.BlockSpec((1, d), lambda j, b: (0, 0)),
                      pl.BlockSpec((nb, 1, d), lambda j, b: (cond_block, 0, 5)),
                      pl.BlockSpec(memory_space=pl.ANY)],
            out_specs=tile,
            scratch_shapes=[pltpu.VMEM((tps, ROUTE_TILE, d), F32),
                            pltpu.VMEM((2, tps, ne * ROUTE_WINDOW, d), BF16), pltpu.SemaphoreType.DMA((2,)),
                            pltpu.VMEM((ROUTE_WINDOW, d), BF16), pltpu.SemaphoreType.DMA(())]),
        out_shape=jax.ShapeDtypeStruct(_tile_view(x1, nb).shape, F32),
        compiler_params=pltpu.CompilerParams(dimension_semantics=("arbitrary",),
                                             vmem_limit_bytes=VMEM_LIMIT),
        name="combine",
    )(base, pos, aff, _tile_view(x1, nb), npost, mod3, eo).reshape(t, d)


def _trunk(x3, mod3, first_cond_row, shared_cond, weights, lgs, *, rope_tabs, init, emit_state):
    (n_pre_mix, n_post_mix, n_pre_ffn, n_post_ffn, w_in, gn_g, conv_w, wro, wco, wo, wr_pad, wg, wu, wd) = weights
    b, seq, d = x3.shape
    t = b * seq
    x = x3.reshape(t, d)
    cond_row_fn = (lambda r: first_cond_row) if shared_cond else (lambda r: first_cond_row + r // seq)
    nb = 1 if shared_cond else b
    assert ROUTE_TILE % nb == 0 and (ROUTE_TILE // nb) % SLOT_ALIGN == 0 and first_cond_row % nb == 0
    proj_a, proj_p, proj_s = _inproj(x, mod3, n_pre_mix, w_in, cond_row_fn, seq, rope_tabs)
    ret = _retention(proj_a, lgs, gn_g, seq, init=init, emit_state=emit_state)
    og = ret[0]
    x1, h2, aff_t = _mix(og, proj_a, proj_p, proj_s, x, mod3, conv_w, wro, wco, wo, wr_pad, n_post_mix,
                         n_pre_ffn, seq, cond_row_fn)
    cap = CAPACITY_FACTOR * t // N_EXPERTS
    pos, base, aff = _route(aff_t, cap, nb)
    eo = _experts(_dispatch(base, h2, pos, cap, nb), wg, wu, wd)
    y = _combine(base, eo, pos, aff, x1, mod3, n_post_ffn, first_cond_row // nb, cap, nb)
    return y.reshape(b, seq, d), ret[1:]


def kernel(x_prompt, x_sample, state_ret_fwd, state_ret_bwd, c, c_ctx, w_ada, b_ada, norm_pre_mix, norm_post_mix,
           norm_pre_ffn, norm_post_ffn, w_in, ret_decay_fwd, ret_decay_bwd, ret_norm_g, conv_w, w_ret_o, w_conv_o,
           w_o, w_router, w_gate, w_up, w_down):
    depth = w_ada.shape[0]
    assert depth == 1
    dec_b, dec_seq = x_sample.shape[0], x_sample.shape[1]
    xp, xs = x_prompt, x_sample
    l = 0
    cond = jnp.zeros((N_COND_ROWS, D_MODEL), F32).at[0:dec_b].set(c).at[CTX_COND_ROW].set(c_ctx)
    mod3 = _ada(cond, w_ada[l], b_ada[l][None, :]).reshape(N_COND_ROWS, 1, 6 * D_MODEL)
    lgs = jnp.stack([-jax.nn.softplus(-ret_decay_fwd[l].astype(F32)),
                     -jax.nn.softplus(-ret_decay_bwd[l].astype(F32))])
    wr_pad = jnp.pad(w_router[l], ((0, 0), (0, ROUTER_LANES - N_EXPERTS)))
    weights = (norm_pre_mix[l][None, :], norm_post_mix[l][None, :], norm_pre_ffn[l][None, :],
               norm_post_ffn[l][None, :], w_in[l].astype(BF16), ret_norm_g[l][None, :], conv_w[l],
               w_ret_o[l].astype(BF16), w_conv_o[l].astype(BF16), w_o[l].astype(BF16), wr_pad,
               w_gate[l], w_up[l], w_down[l])
    yp, (s_f, s_b) = _trunk(xp, mod3, CTX_COND_ROW, True, weights, lgs,
                            rope_tabs=None, init=None, emit_state=True)
    ys, _ = _trunk(xs, mod3, 0, False, weights, lgs,
                   rope_tabs=_rope_tables(dec_seq), init=(state_ret_fwd, state_ret_bwd), emit_state=False)
    return (yp, ys, s_f, s_b)
```

```python
import functools

import jax
import jax.numpy as jnp
import numpy as np
from jax import lax
from jax.experimental import pallas as pl
from jax.experimental.pallas import tpu as pltpu

F32 = jnp.float32
BF16 = jnp.bfloat16

D_MODEL = 1024
N_HEADS = 8
DK = 64
DV = 128
CHUNK = 128
GRID_W = 64
N_EXPERTS = 16
CAPACITY_FACTOR = 2
D_IN_TOTAL = 8192
RMS_EPS = 1e-6
GN_EPS = 1e-5
ROPE_BASE = 10000.0
LANES = 128
BF16_SUBLANES = 16
N_COND_ROWS = 16
CTX_COND_ROW = 8
ROUTER_LANES = LANES
RET_CHUNKS_PER_STEP = 16
RET_UNROLL = 16
MIX_TILE = 1024
MIX_SUB_TILE = 512
EXPERT_ROWS_PER_STEP = 1024
EXPERT_SUB_ROWS = 512
ROUTE_TILE = 256
ROUTE_WINDOW = 64
SLOT_ALIGN = BF16_SUBLANES
DISPATCH_SLOT_ROWS = 16384
DISPATCH_TILES_PER_STEP = 8
COMBINE_TILES_PER_STEP = 4
VMEM_LIMIT = 48 * 1024 * 1024
EXPERT_VMEM_LIMIT = 56 * 1024 * 1024


def _sigmoid(x):
    return 0.5 * jnp.tanh(0.5 * x) + 0.5


def _rms(x, g):
    return x * lax.rsqrt(jnp.mean(x * x, axis=-1, keepdims=True) + RMS_EPS) * g


def _ada_kernel(c_ref, w_ref, b_ref, o_ref):
    c = c_ref[...]
    s = c * _sigmoid(c)
    o_ref[...] = jnp.dot(s, w_ref[...], preferred_element_type=F32,
                         precision=lax.Precision.HIGHEST) + b_ref[...]


def _ada(cond, w_ada, b_ada):
    n = w_ada.shape[1]
    tn = 1024
    return pl.pallas_call(
        _ada_kernel,
        grid=(n // tn,),
        in_specs=[pl.BlockSpec((N_COND_ROWS, D_MODEL), lambda j: (0, 0)),
                  pl.BlockSpec((D_MODEL, tn), lambda j: (0, j)),
                  pl.BlockSpec((1, tn), lambda j: (0, j))],
        out_specs=pl.BlockSpec((N_COND_ROWS, tn), lambda j: (0, j)),
        out_shape=jax.ShapeDtypeStruct((N_COND_ROWS, n), F32),
        compiler_params=pltpu.CompilerParams(dimension_semantics=("arbitrary",),
                                             vmem_limit_bytes=VMEM_LIMIT),
        name="ada_mod",
    )(cond, w_ada, b_ada)


def _inproj_body(x_ref, g_ref, sh_ref, sc_ref, w_ref, cos_ref, sin_ref, a_ref, p_ref, s_ref, h_ref, use_rope):
    j = pl.program_id(1)

    @pl.when(j == 0)
    def _():
        h = _rms(x_ref[...], g_ref[...]) * (1.0 + sc_ref[0]) + sh_ref[0]
        h_ref[...] = h.astype(BF16)

    lanes = 2 * DK
    width = 2 * lanes
    n_slices = w_ref.shape[1] // width
    n_qk = 2 * N_HEADS * DK

    def slice_dot(b):
        return jnp.dot(h_ref[...], w_ref[:, b * width:(b + 1) * width], preferred_element_type=F32)

    @pl.when(j == 0)
    def _():
        ci = lax.broadcasted_iota(jnp.int32, (h_ref.shape[0], lanes), 1)
        for b in range(n_slices):
            acc = slice_dot(b)
            if b * width >= n_qk:
                a_ref[:, b * width:(b + 1) * width] = acc.astype(BF16)
                continue
            for half in range(2):
                col = b * width + half * lanes
                x = acc[:, half * lanes:(half + 1) * lanes]
                if col >= N_HEADS * DK:
                    x = x * (DK ** -0.5)
                if use_rope:
                    swapped = jnp.where((ci & 31) < 16, pltpu.roll(x, lanes - 16, 1), pltpu.roll(x, 16, 1))
                    x = x * cos_ref[...] + swapped * sin_ref[...]
                a_ref[:, col:col + lanes] = x.astype(BF16)

    @pl.when(j == 1)
    def _():
        for b in range(n_slices):
            acc = slice_dot(b)
            if b < n_slices // 2:
                acc = acc * _sigmoid(acc)
            a_ref[:, b * width:(b + 1) * width] = acc.astype(BF16)

    @pl.when(j == 2)
    def _():
        for b in range(n_slices // 2):
            p_ref[:, b * width:(b + 1) * width] = (slice_dot(b) * slice_dot(b + n_slices // 2)).astype(BF16)

    @pl.when(j == 3)
    def _():
        for b in range(n_slices):
            s_ref[:, b * width:(b + 1) * width] = _sigmoid(slice_dot(b)).astype(BF16)


def _inproj_kernel(x_ref, g_ref, sh_ref, sc_ref, w_ref, *rest, use_rope):
    if use_rope:
        cos_ref, sin_ref, a_ref, p_ref, s_ref, h_ref = rest
    else:
        cos_ref = sin_ref = None
        a_ref, p_ref, s_ref, h_ref = rest
    _inproj_body(x_ref, g_ref, sh_ref, sc_ref, w_ref, cos_ref, sin_ref, a_ref, p_ref, s_ref, h_ref, use_rope)


def _inproj(x, mod3, g, w_bf16, cond_row_fn, seq, rope_tabs):
    t = x.shape[0]
    tm, tn = 1024, 2048
    assert D_IN_TOTAL == 4 * tn
    in_specs = [pl.BlockSpec((tm, D_MODEL), lambda i, j: (i, 0)),
                pl.BlockSpec((1, D_MODEL), lambda i, j: (0, 0)),
                pl.BlockSpec((1, 1, D_MODEL), lambda i, j: (cond_row_fn(i * tm), 0, 0)),
                pl.BlockSpec((1, 1, D_MODEL), lambda i, j: (cond_row_fn(i * tm), 0, 1)),
                pl.BlockSpec((D_MODEL, tn), lambda i, j: (0, j))]
    args = [x, g, mod3, mod3, w_bf16]
    if rope_tabs is not None:
        tiles_per_seq = seq // tm
        in_specs += [pl.BlockSpec((tm, 2 * DK), lambda i, j: (i % tiles_per_seq, 0))] * 2
        args += list(rope_tabs)
    return pl.pallas_call(
        functools.partial(_inproj_kernel, use_rope=rope_tabs is not None),
        grid=(t // tm, D_IN_TOTAL // tn),
        in_specs=in_specs,
        out_specs=[pl.BlockSpec((tm, tn), lambda i, j: (i, jnp.minimum(j, 1))),
                   pl.BlockSpec((tm, tn // 2), lambda i, j: (i, 0)),
                   pl.BlockSpec((tm, tn), lambda i, j: (i, 0))],
        out_shape=[jax.ShapeDtypeStruct((t, 2 * tn), BF16),
                   jax.ShapeDtypeStruct((t, tn // 2), BF16),
                   jax.ShapeDtypeStruct((t, tn), BF16)],
        scratch_shapes=[pltpu.VMEM((tm, D_MODEL), BF16)],
        compiler_params=pltpu.CompilerParams(dimension_semantics=("arbitrary", "arbitrary"),
                                             vmem_limit_bytes=VMEM_LIMIT),
        name="inproj",
    )(*args)


def _ret_kernel(*refs, nc, bb, has_init, emit_state):
    it = iter(refs)
    lg_ref = next(it)
    q_ref, k_ref, v_ref, gr_ref, gn_ref = (next(it) for _ in range(5))
    s0f_ref = s0b_ref = sf_ref = sb_ref = None
    if has_init:
        s0f_ref, s0b_ref = next(it), next(it)
    o_ref = next(it)
    if emit_state:
        sf_ref, sb_ref = next(it), next(it)
    kvf_s, kvb_s, rcat_s, p_s, o_s, r0 = (next(it) for _ in range(6))

    p = pl.program_id(1)
    lgf_a, lgf_b = lg_ref[0, 2 * p], lg_ref[0, 2 * p + 1]
    lgb_a, lgb_b = lg_ref[1, 2 * p], lg_ref[1, 2 * p + 1]

    ri = lax.broadcasted_iota(jnp.int32, (CHUNK, 2 * DK), 0)
    ci = lax.broadcasted_iota(jnp.int32, (CHUNK, 2 * DK), 1)
    lane_a = ci < DK
    rowf = ri.astype(F32)
    diff = rowf - ci.astype(F32)
    lgf_lane = jnp.where(lane_a, lgf_a, lgf_b)
    lgb_lane = jnp.where(lane_a, lgb_a, lgb_b)
    xi_f = jnp.exp(lgf_lane * (rowf + 1.0))
    xi_b = jnp.exp(lgb_lane * (CHUNK - rowf))
    zeta_f = jnp.exp(lgf_lane * (CHUNK - 1.0 - rowf))
    zeta_b = jnp.exp(lgb_lane * rowf)

    def decay_matrix(lgf, lgb):
        return jnp.where(diff > 0, jnp.exp(lgf * diff),
                         jnp.where(diff < 0, jnp.exp(lgb * (-diff)), 2.0))

    dm_a = decay_matrix(lgf_a, lgb_a)
    dm_b = decay_matrix(lgf_b, lgb_b)

    r2 = lax.broadcasted_iota(jnp.int32, (2 * DK, 2 * DV), 0)
    c2 = lax.broadcasted_iota(jnp.int32, (2 * DK, 2 * DV), 1)
    top = r2 < DK
    blk = (top == (c2 < DV)).astype(F32)
    cd_f = jnp.exp(jnp.where(top, lgf_a, lgf_b) * float(CHUNK)) * blk
    cd_b = jnp.exp(jnp.where(top, lgb_a, lgb_b) * float(CHUNK)) * blk

    def chunk_rows(c):
        return pl.ds(pl.multiple_of(c * CHUNK, CHUNK), CHUNK)

    n_chunks = bb * nc
    unroll = RET_UNROLL

    def kv_body(c, carry):
        rows = chunk_rows(c)
        k = k_ref[rows, :].astype(F32)
        kz_t = jnp.concatenate([k * zeta_f, k * zeta_b], axis=1).T.astype(BF16)
        kv = jnp.dot(kz_t, v_ref[rows, :], preferred_element_type=F32)
        kvf_s[c] = kv[0:2 * DK] * blk
        kvb_s[c] = kv[2 * DK:4 * DK] * blk
        return carry

    def score_body(c, carry):
        rows = chunk_rows(c)
        q16 = q_ref[rows, :]
        zero = jnp.zeros_like(q16)
        q_ab = jnp.concatenate([jnp.where(lane_a, q16, zero), jnp.where(lane_a, zero, q16)], axis=0)
        s = lax.dot_general(q_ab, k_ref[rows, :], (((1,), (1,)), ((), ())), preferred_element_type=F32)
        p_s[c, :, 0:CHUNK] = (s[0:CHUNK] * dm_a).astype(BF16)
        p_s[c, :, CHUNK:2 * CHUNK] = (s[CHUNK:2 * CHUNK] * dm_b).astype(BF16)
        return carry

    lax.fori_loop(0, n_chunks, lambda c, carry: score_body(c, kv_body(c, carry)), 0, unroll=unroll)

    def load_state(s_ref, s):
        r0[...] = jnp.zeros_like(r0)
        r0[0:DK, 0:DV] = s_ref[s, 0, 0].astype(F32)
        r0[DK:2 * DK, DV:2 * DV] = s_ref[s, 0, 1].astype(F32)
        return r0[...]

    for s in range(bb):
        def fwd_body(n, rf, s=s):
            c = s * nc + n
            rcat_s[c, 0:2 * DK, :] = rf.astype(BF16)
            return cd_f * rf + kvf_s[c]

        def bwd_body(t, rb, s=s):
            c = s * nc + nc - 1 - t
            rcat_s[c, 2 * DK:4 * DK, :] = rb.astype(BF16)
            return cd_b * rb + kvb_s[c]

        zeros = jnp.zeros((2 * DK, 2 * DV), F32)
        rf_fin = lax.fori_loop(0, nc, fwd_body, load_state(s0f_ref, s) if has_init else zeros)
        rb_fin = lax.fori_loop(0, nc, bwd_body, load_state(s0b_ref, s) if has_init else zeros)
        if emit_state:
            for s_ref, r in ((sf_ref, rf_fin), (sb_ref, rb_fin)):
                s_ref[s, 0, 0] = r[0:DK, 0:DV]
                s_ref[s, 0, 1] = r[DK:2 * DK, DV:2 * DV]

    gn = gn_ref[...]

    hr = lax.broadcasted_iota(jnp.int32, (2 * DV, 2 * DV), 0)
    hc = lax.broadcasted_iota(jnp.int32, (2 * DV, 2 * DV), 1)
    head_mean = jnp.where((hr < DV) == (hc < DV), 1.0 / DV, 0.0).astype(BF16)
    v_lane_a = lax.broadcasted_iota(jnp.int32, (CHUNK, 2 * DV), 1) < DV

    def value_body(c, carry):
        rows = chunk_rows(c)
        q = q_ref[rows, :].astype(F32)
        v = v_ref[rows, :]
        zero = jnp.zeros_like(v)
        v_bd = jnp.concatenate([jnp.where(v_lane_a, v, zero), jnp.where(v_lane_a, zero, v)], axis=0)
        qx = jnp.concatenate([(q * xi_f).astype(BF16), (q * xi_b).astype(BF16)], axis=1)
        o_s[c] = (jnp.dot(p_s[c], v_bd, preferred_element_type=F32)
                  + jnp.dot(qx, rcat_s[c], preferred_element_type=F32))
        return carry

    lax.fori_loop(0, n_chunks, value_body, 0, unroll=unroll)

    def gate_body(c, carry):
        rows = chunk_rows(c)
        o = o_s[c]
        d = o - jnp.dot(o.astype(BF16), head_mean, preferred_element_type=F32)
        var = jnp.dot((d * d).astype(BF16), head_mean, preferred_element_type=F32)
        y = d * lax.rsqrt(var + GN_EPS)
        o_ref[rows, :] = (gr_ref[rows, :].astype(F32) * (y * gn)).astype(BF16)
        return carry

    lax.fori_loop(0, n_chunks, gate_body, 0, unroll=unroll)


def _retention(proj, lgs, gn_g, seq, *, init=None, emit_state):
    t = proj.shape[0]
    b = t // seq
    nc = seq // CHUNK
    bb = max(1, RET_CHUNKS_PER_STEP // nc)
    rows = bb * seq
    has_init = init is not None
    pairs = N_HEADS // 2
    qk_blocks = (N_HEADS * DK) // (2 * DK)
    in_specs = [pl.BlockSpec(memory_space=pltpu.SMEM),
                pl.BlockSpec((rows, 2 * DK), lambda i, p: (i, p)),
                pl.BlockSpec((rows, 2 * DK), lambda i, p: (i, qk_blocks + p)),
                pl.BlockSpec((rows, 2 * DV), lambda i, p: (i, qk_blocks + p)),
                pl.BlockSpec((rows, 2 * DV), lambda i, p: (i, 2 * qk_blocks + p)),
                pl.BlockSpec((1, 2 * DV), lambda i, p: (0, p))]
    args = [lgs, proj, proj, proj, proj, gn_g]
    state_spec = pl.BlockSpec((bb, 1, 2, DK, DV), lambda i, p: (i, 0, p, 0, 0))
    if has_init:
        in_specs += [state_spec, state_spec]
        args += list(init)
    out_specs = [pl.BlockSpec((rows, 2 * DV), lambda i, p: (i, p))]
    out_shape = [jax.ShapeDtypeStruct((t, N_HEADS * DV), BF16)]
    if emit_state:
        out_specs += [state_spec, state_spec]
        out_shape += [jax.ShapeDtypeStruct((b, 1, N_HEADS, DK, DV), F32)] * 2
    return pl.pallas_call(
        functools.partial(_ret_kernel, nc=nc, bb=bb, has_init=has_init, emit_state=emit_state),
        grid=(b // bb, pairs),
        in_specs=in_specs,
        out_specs=out_specs,
        out_shape=out_shape,
        scratch_shapes=[pltpu.VMEM((bb * nc, 2 * DK, 2 * DV), F32),
                        pltpu.VMEM((bb * nc, 2 * DK, 2 * DV), F32),
                        pltpu.VMEM((bb * nc, 4 * DK, 2 * DV), BF16),
                        pltpu.VMEM((bb * nc, CHUNK, 2 * CHUNK), BF16),
                        pltpu.VMEM((bb * nc, CHUNK, 2 * DV), F32),
                        pltpu.VMEM((2 * DK, 2 * DV), F32)],
        compiler_params=pltpu.CompilerParams(dimension_semantics=("arbitrary", "arbitrary"),
                                             vmem_limit_bytes=VMEM_LIMIT),
        name="retention",
    )(*args)


def _rope_tables(seq):
    n_freq = DK // 4
    pos = np.arange(seq)
    inv = jnp.asarray(ROPE_BASE, F32) ** (-jnp.arange(n_freq, dtype=F32) / n_freq)
    ang_r = jnp.asarray(pos // GRID_W, F32)[:, None] * inv
    ang_c = jnp.asarray(pos % GRID_W, F32)[:, None] * inv
    cos = jnp.concatenate([jnp.cos(ang_r)] * 2 + [jnp.cos(ang_c)] * 2, axis=1)
    sin = jnp.concatenate([-jnp.sin(ang_r), jnp.sin(ang_r), -jnp.sin(ang_c), jnp.sin(ang_c)], axis=1)
    return jnp.tile(cos, (1, 2)), jnp.tile(sin, (1, 2))


def _mix_kernel(og_ref, cb_ref, p_ref, pp_ref, pn_ref, sa_ref, sb_ref, x_ref,
                cw_ref, wro_ref, wco_ref, wo_ref, wr_ref, npost_ref, npre_ref, g1_ref, sh2_ref, sc2_ref,
                x1_ref, h2_ref, aff_ref, *, tm, sub, seq):
    i = pl.program_id(0)
    halo = pp_ref.shape[0]
    wr = wr_ref[...]
    w_hi = wr.astype(BF16)
    w_lo = (wr - w_hi.astype(F32)).astype(BF16)
    w_hi_lo = jnp.concatenate([w_hi, w_lo], axis=1)
    row = lax.broadcasted_iota(jnp.int32, (sub, 1), 0)
    lane = lax.broadcasted_iota(jnp.int32, (sub, ROUTER_LANES), 1)

    for s in range(tm // sub):
        r0 = s * sub
        rows = slice(r0, r0 + sub)
        prev_row = (pp_ref[halo - 1:halo, :] if s == 0 else p_ref[r0 - 1:r0, :]).astype(F32)
        next_row = (pn_ref[0:1, :] if r0 + sub == tm else p_ref[r0 + sub:r0 + sub + 1, :]).astype(F32)
        prod = p_ref[rows, :].astype(F32)
        if seq % sub == 0:
            prev_row = prev_row * jnp.where((i * tm + r0) % seq != 0, 1.0, 0.0)
            next_row = next_row * jnp.where((i * tm + r0 + sub) % seq != 0, 1.0, 0.0)
        up = jnp.where(row == 0, prev_row, pltpu.roll(prod, 1, 0))
        dn = jnp.where(row == sub - 1, next_row, pltpu.roll(prod, sub - 1, 0))
        if seq % sub != 0:
            token = i * tm + r0 + row
            in_seq = token & (seq - 1) if seq & (seq - 1) == 0 else token % seq
            up = jnp.where(in_seq == 0, 0.0, up)
            dn = jnp.where(in_seq == seq - 1, 0.0, dn)
        u = up * cw_ref[0:1, :] + prod * cw_ref[1:2, :] + dn * cw_ref[2:3, :]
        y_conv = jnp.dot((cb_ref[rows, :].astype(F32) * u).astype(BF16), wco_ref[...],
                         preferred_element_type=F32)
        y_ret = jnp.dot(og_ref[rows, :], wro_ref[...], preferred_element_type=F32)
        merged = sa_ref[rows, :].astype(F32) * y_ret + sb_ref[rows, :].astype(F32) * y_conv
        m = jnp.dot(merged.astype(BF16), wo_ref[...], preferred_element_type=F32)
        x1 = x_ref[rows, :] + g1_ref[0] * _rms(m, npost_ref[...])
        x1_ref[rows, :] = x1
        h2 = _rms(x1, npre_ref[...]) * (1.0 + sc2_ref[0]) + sh2_ref[0]
        h_hi = h2.astype(BF16)
        h2_ref[rows, :] = h_hi
        h_lo = (h2 - h_hi.astype(F32)).astype(BF16)
        hi_terms = jnp.dot(h_hi, w_hi_lo, preferred_element_type=F32)
        logits = (hi_terms[:, 0:ROUTER_LANES] + hi_terms[:, ROUTER_LANES:2 * ROUTER_LANES]
                  + jnp.dot(h_lo, w_hi, preferred_element_type=F32))
        logits = jnp.where(lane < N_EXPERTS, logits, -jnp.inf)
        e = jnp.exp(logits - jnp.max(logits, axis=-1, keepdims=True))
        aff = e / jnp.sum(e, axis=-1, keepdims=True)
        aff_ref[:, rows] = aff.T[0:N_EXPERTS, :]


def _mix(og, proj_a, proj_p, proj_s, x, mod3, conv_w, wro, wco, wo, wr_pad, npost, npre, seq, cond_row_fn):
    t = x.shape[0]
    conv_b_block = (2 * N_HEADS * DK + 2 * N_HEADS * DV) // D_MODEL
    tm = MIX_TILE
    halo = BF16_SUBLANES
    hb = tm // halo
    last_halo = t // halo - 1
    col = lambda c: (lambda i: (i, c))
    row_vec = pl.BlockSpec((1, D_MODEL), lambda i: (0, 0))
    wspec = pl.BlockSpec((D_MODEL, D_MODEL), lambda i: (0, 0))
    modspec = lambda c: pl.BlockSpec((1, 1, D_MODEL), lambda i: (cond_row_fn(i * tm), 0, c))
    tile = lambda c: pl.BlockSpec((tm, D_MODEL), col(c))
    prev = lambda c: pl.BlockSpec((halo, D_MODEL), lambda i: (jnp.maximum(i * hb - 1, 0), c))
    nxt = lambda c: pl.BlockSpec((halo, D_MODEL), lambda i: (jnp.minimum((i + 1) * hb, last_halo), c))
    return pl.pallas_call(
        functools.partial(_mix_kernel, tm=tm, sub=MIX_SUB_TILE, seq=seq),
        grid=(t // tm,),
        in_specs=[tile(0), tile(conv_b_block), tile(0), prev(0), nxt(0), tile(0), tile(1),
                  tile(0),
                  pl.BlockSpec((3, D_MODEL), lambda i: (0, 0)), wspec, wspec, wspec,
                  pl.BlockSpec((D_MODEL, ROUTER_LANES), lambda i: (0, 0)),
                  row_vec, row_vec, modspec(2), modspec(3), modspec(4)],
        out_specs=[tile(0), tile(0), pl.BlockSpec((N_EXPERTS, tm), lambda i: (0, i))],
        out_shape=[jax.ShapeDtypeStruct((t, D_MODEL), F32),
                   jax.ShapeDtypeStruct((t, D_MODEL), BF16),
                   jax.ShapeDtypeStruct((N_EXPERTS, t), F32)],
        compiler_params=pltpu.CompilerParams(dimension_semantics=("arbitrary",),
                                             vmem_limit_bytes=VMEM_LIMIT),
        name="mix_out",
    )(og, proj_a, proj_p, proj_p, proj_p, proj_s, proj_s, x,
      conv_w, wro, wco, wo, wr_pad, npost, npre, mod3, mod3, mod3)


def _expert_kernel(xs_ref, wg_ref, wu_ref, wd_ref, o_ref, *, sub):
    wg = wg_ref[0].astype(BF16)
    wu = wu_ref[0].astype(BF16)
    wd = wd_ref[0].astype(BF16)
    for r in range(xs_ref.shape[1] // sub):
        rows = slice(r * sub, (r + 1) * sub)
        x = xs_ref[0, rows, :]
        g = jnp.dot(x, wg, preferred_element_type=F32)
        u = jnp.dot(x, wu, preferred_element_type=F32)
        hid = (g * _sigmoid(g) * u).astype(BF16)
        o_ref[0, rows, :] = jnp.dot(hid, wd, preferred_element_type=F32).astype(BF16)


def _experts(xs, wg, wu, wd):
    e, cap, d = xs.shape
    ff = wg.shape[2]
    tr = min(cap, EXPERT_ROWS_PER_STEP)
    return pl.pallas_call(
        functools.partial(_expert_kernel, sub=EXPERT_SUB_ROWS),
        grid=(e, cap // tr),
        in_specs=[pl.BlockSpec((1, tr, d), lambda a, r: (a, r, 0)),
                  pl.BlockSpec((1, d, ff), lambda a, r: (a, 0, 0)),
                  pl.BlockSpec((1, d, ff), lambda a, r: (a, 0, 0)),
                  pl.BlockSpec((1, ff, d), lambda a, r: (a, 0, 0))],
        out_specs=pl.BlockSpec((1, tr, d), lambda a, r: (a, r, 0)),
        out_shape=jax.ShapeDtypeStruct((e, cap, d), BF16),
        compiler_params=pltpu.CompilerParams(dimension_semantics=("arbitrary", "arbitrary"),
                                             vmem_limit_bytes=EXPERT_VMEM_LIMIT),
        name="experts",
    )(xs, wg, wu, wd)


def _block_ranks(mask):
    ne, nblk, lanes = mask.shape
    li = lax.broadcasted_iota(jnp.int32, (lanes, lanes), 0)
    lj = lax.broadcasted_iota(jnp.int32, (lanes, lanes), 1)
    incl_lanes = jnp.where(li <= lj, 1.0, 0.0).astype(BF16)
    all_lanes = jnp.ones((lanes, lanes), BF16)
    bi = lax.broadcasted_iota(jnp.int32, (nblk, nblk), 0)
    bj = lax.broadcasted_iota(jnp.int32, (nblk, nblk), 1)
    earlier_blocks = jnp.where(bj < bi, 1.0, 0.0).astype(BF16)
    m = mask.reshape(ne * nblk, lanes).astype(BF16)
    incl = jnp.dot(m, incl_lanes, preferred_element_type=F32).reshape(ne, nblk, lanes)
    tot = jnp.dot(m, all_lanes, preferred_element_type=F32).reshape(ne, nblk, lanes).astype(BF16)
    excl = jnp.stack([jnp.dot(earlier_blocks, tot[i], preferred_element_type=F32) for i in range(ne)])
    return excl, incl


def _select_kernel(aff_ref, sel_ref, *, cap):
    ne, nblk, lanes = aff_ref.shape
    aff = aff_ref[...]

    def count(mask):
        m = jnp.where(mask, 1.0, 0.0)
        return jnp.sum(jnp.sum(m, axis=1, keepdims=True), axis=2, keepdims=True)

    def as_float(bits):
        return lax.bitcast_convert_type(bits, F32)

    def bit_step(it, thr_bits):
        cand = thr_bits | jnp.left_shift(jnp.int32(1), 30 - it)
        return jnp.where(count(aff >= as_float(cand)) >= cap, cand, thr_bits)

    thr = as_float(lax.fori_loop(0, 31, bit_step, jnp.zeros((ne, 1, 1), jnp.int32)))
    gt = aff > thr
    eq = aff == thr
    need = cap - count(gt)
    ex_eq, in_eq = _block_ranks(jnp.where(eq, 1.0, 0.0))
    sel_ref[...] = jnp.where(gt | (eq & (ex_eq + in_eq - 1.0 < need)), 1.0, 0.0)


def _slot_kernel(sel_ref, pos_ref, excl_ref):
    sel = sel_ref[...]
    ex_sel, in_sel = _block_ranks(sel)
    pos_ref[...] = jnp.where(sel > 0.0, ex_sel + in_sel - 1.0, -1.0).astype(jnp.int32)
    excl_ref[...] = ex_sel.astype(jnp.int32)


def _to_tile_order(a, nb):
    if nb == 1:
        return a
    ne, t = a.shape
    r = ROUTE_TILE // nb
    return a.reshape(ne, nb, t // (nb * r), r).transpose(0, 2, 1, 3).reshape(ne, t)


def _route(aff_t, cap, nb):
    ne, t = aff_t.shape
    nblk = t // LANES
    shape = (ne, nblk, LANES)
    full = pl.BlockSpec(shape, lambda i: (0, 0, 0))
    params = pltpu.CompilerParams(dimension_semantics=("arbitrary",), vmem_limit_bytes=VMEM_LIMIT)
    sel = pl.pallas_call(
        functools.partial(_select_kernel, cap=cap),
        grid=(1,), in_specs=[full], out_specs=full,
        out_shape=jax.ShapeDtypeStruct(shape, F32),
        compiler_params=params, name="route_select",
    )(aff_t.reshape(shape))
    sel = _to_tile_order(sel.reshape(ne, t), nb)
    pos, excl = pl.pallas_call(
        _slot_kernel,
        grid=(1,), in_specs=[full], out_specs=[full, full],
        out_shape=[jax.ShapeDtypeStruct(shape, jnp.int32)] * 2,
        compiler_params=params, name="route_slots",
    )(sel.reshape(shape))
    base = jnp.concatenate([excl[:, :, 0], jnp.full((ne, 1), cap, jnp.int32)], axis=1)
    return pos.reshape(ne, t), base, _to_tile_order(aff_t, nb)


def _window(base_ref, e, blk, w, cap):
    lo = base_ref[e, blk] // SLOT_ALIGN + w * (ROUTE_WINDOW // SLOT_ALIGN)
    return lo * SLOT_ALIGN, jnp.minimum(lo, (cap - ROUTE_WINDOW) // SLOT_ALIGN) * SLOT_ALIGN


def _n_windows(base_ref, e, blk, cap):
    lo, _ = _window(base_ref, e, blk, 0, cap)
    end = base_ref[e, blk + ROUTE_TILE // LANES]
    return (end - lo + ROUTE_WINDOW - 1) // ROUTE_WINDOW


def _dispatch_kernel(base_ref, h_ref, pos_ref, xs_ref, *, cap):
    g, j = pl.program_id(0), pl.program_id(1)
    group = xs_ref.shape[0]

    @pl.when(j == 0)
    def _():
        xs_ref[...] = jnp.zeros_like(xs_ref)

    ri = lax.broadcasted_iota(jnp.int32, (ROUTE_WINDOW, ROUTE_TILE), 0)

    def onehot(prow, lo, off, first):
        hit = prow - off == ri
        if not first:
            hit = hit & (prow >= lo)
        return jnp.where(hit, 1.0, 0.0).astype(BF16)

    def add_rows(ge, off, rows):
        sl = pl.ds(pl.multiple_of(off, SLOT_ALIGN), ROUTE_WINDOW)
        xs_ref[ge, sl, :] = xs_ref[ge, sl, :] + rows.astype(BF16)

    for sub in range(DISPATCH_TILES_PER_STEP):
        blk = (j * DISPATCH_TILES_PER_STEP + sub) * (ROUTE_TILE // LANES)
        tok = slice(sub * ROUTE_TILE, (sub + 1) * ROUTE_TILE)
        h = h_ref[:, sub].reshape(ROUTE_TILE, h_ref.shape[-1])
        prows, offs = [], []
        for ge in range(group):
            e = g * group + ge
            prows.append(pos_ref[pl.ds(e, 1), tok])
            offs.append(_window(base_ref, e, blk, 0, cap))
        sel = jnp.concatenate([onehot(prows[ge], *offs[ge], True) for ge in range(group)], axis=0)
        rows = jnp.dot(sel, h, preferred_element_type=F32)
        for ge in range(group):
            add_rows(ge, offs[ge][1], rows[ge * ROUTE_WINDOW:(ge + 1) * ROUTE_WINDOW])

        for ge in range(group):
            e = g * group + ge

            def extra(w, carry, ge=ge, e=e, blk=blk, h=h, prow=prows[ge]):
                lo, off = _window(base_ref, e, blk, w, cap)
                add_rows(ge, off, jnp.dot(onehot(prow, lo, off, False), h, preferred_element_type=F32))
                return carry

            lax.fori_loop(1, _n_windows(base_ref, e, blk, cap), extra, 0)


def _tile_view(a, nb):
    t, d = a.shape
    r = ROUTE_TILE // nb
    return a.reshape(nb, t // (nb * r), r, d)


def _dispatch(base, h2, pos, cap, nb):
    t, d = h2.shape
    ne = pos.shape[0]
    step = ROUTE_TILE * DISPATCH_TILES_PER_STEP
    group = min(ne, DISPATCH_SLOT_ROWS // cap)
    out_spec = pl.BlockSpec((group, cap, d), lambda g, j, b: (g, 0, 0), pipeline_mode=pl.Buffered(1))
    return pl.pallas_call(
        functools.partial(_dispatch_kernel, cap=cap),
        grid_spec=pltpu.PrefetchScalarGridSpec(
            num_scalar_prefetch=1,
            grid=(ne // group, t // step),
            in_specs=[pl.BlockSpec((nb, DISPATCH_TILES_PER_STEP, ROUTE_TILE // nb, d),
                                   lambda g, j, b: (0, j, 0, 0)),
                      pl.BlockSpec((ne, step), lambda g, j, b: (0, j))],
            out_specs=out_spec),
        out_shape=jax.ShapeDtypeStruct((ne, cap, d), BF16),
        compiler_params=pltpu.CompilerParams(dimension_semantics=("arbitrary", "arbitrary"),
                                             vmem_limit_bytes=VMEM_LIMIT),
        name="dispatch",
    )(base, _tile_view(h2, nb), pos)


def _combine_kernel(base_ref, pos_ref, aff_ref, x1_ref, g_ref, g2_ref, eo_ref, o_ref,
                    f_ref, win_buf, win_sem, extra_buf, extra_sem, *, cap, n_steps):
    j = pl.program_id(0)
    blocks_per_tile = ROUTE_TILE // LANES
    slot = lax.rem(j, 2)
    ri = lax.broadcasted_iota(jnp.int32, (ROUTE_WINDOW, ROUTE_TILE), 0)
    tn = (((0,), (0,)), ((), ()))

    def window_copy(e, off, dst, sem):
        return pltpu.make_async_copy(eo_ref.at[e, pl.ds(pl.multiple_of(off, SLOT_ALIGN), ROUTE_WINDOW), :],
                                     dst, sem)

    def window_slot(buf_slot, s, e):
        return win_buf.at[buf_slot, s, pl.ds(e * ROUTE_WINDOW, ROUTE_WINDOW), :]

    def first_window_copies(step, buf_slot):
        return [window_copy(e, _window(base_ref, e, (step * COMBINE_TILES_PER_STEP + s) * blocks_per_tile, 0, cap)[1],
                            window_slot(buf_slot, s, e), win_sem.at[buf_slot])
                for s in range(COMBINE_TILES_PER_STEP) for e in range(N_EXPERTS)]

    def wait_windows(buf_slot):
        for s in range(COMBINE_TILES_PER_STEP):
            for e in range(N_EXPERTS):
                window_copy(e, 0, window_slot(buf_slot, s, e), win_sem.at[buf_slot]).wait()

    @pl.when(j == 0)
    def _():
        for cp in first_window_copies(0, 0):
            cp.start()

    for cp in first_window_copies(jnp.minimum(j + 1, n_steps - 1), 1 - slot):
        cp.start()

    def weights(tok, e, lo, off, first):
        prow = pos_ref[e:e + 1, tok]
        hit = prow - off == ri
        if not first:
            hit = hit & (prow >= lo)
        return jnp.where(hit, aff_ref[e:e + 1, tok], 0.0).astype(BF16)

    qs = []
    for s in range(COMBINE_TILES_PER_STEP):
        blk = (j * COMBINE_TILES_PER_STEP + s) * blocks_per_tile
        tok = slice(s * ROUTE_TILE, (s + 1) * ROUTE_TILE)
        qs.append(jnp.concatenate([weights(tok, e, *_window(base_ref, e, blk, 0, cap), True)
                                   for e in range(N_EXPERTS)], axis=0))
    wait_windows(slot)

    nb, _, r, d = x1_ref.shape
    for s in range(COMBINE_TILES_PER_STEP):
        blk = (j * COMBINE_TILES_PER_STEP + s) * blocks_per_tile
        tok = slice(s * ROUTE_TILE, (s + 1) * ROUTE_TILE)
        f_ref[s] = lax.dot_general(qs[s], win_buf[slot, s], tn, preferred_element_type=F32)

        n_win = [_n_windows(base_ref, e, blk, cap) for e in range(N_EXPERTS)]

        @pl.when(functools.reduce(jnp.maximum, n_win) > 1)
        def _(s=s, blk=blk, tok=tok, n_win=n_win):
            for e in range(N_EXPERTS):
                def extra(w, carry, e=e):
                    lo, off = _window(base_ref, e, blk, w, cap)
                    cp = window_copy(e, off, extra_buf, extra_sem)
                    cp.start()
                    cp.wait()
                    f_ref[s] += lax.dot_general(weights(tok, e, lo, off, False), extra_buf[...], tn,
                                                preferred_element_type=F32)
                    return carry

                lax.fori_loop(1, n_win[e], extra, 0)

        f = f_ref[s].reshape(nb, r, d)
        o_ref[:, s] = x1_ref[:, s] + g2_ref[...] * _rms(f, g_ref[...])

    @pl.when(j == n_steps - 1)
    def _():
        wait_windows(1 - slot)


def _combine(base, eo, pos, aff, x1, mod3, npost, cond_block, cap, nb):
    t, d = x1.shape
    ne = pos.shape[0]
    tps = COMBINE_TILES_PER_STEP
    n_steps = t // (ROUTE_TILE * tps)
    tile = pl.BlockSpec((nb, tps, ROUTE_TILE // nb, d), lambda j, b: (0, j, 0, 0))
    etile = pl.BlockSpec((ne, tps * ROUTE_TILE), lambda j, b: (0, j))
    return pl.pallas_call(
        functools.partial(_combine_kernel, cap=cap, n_steps=n_steps),
        grid_spec=pltpu.PrefetchScalarGridSpec(
            num_scalar_prefetch=1,
            grid=(n_steps,),
            in_specs=[etile, etile, tile,
                      pl.BlockSpec((1, d), lambda j, b: (0, 0)),
                      pl.BlockSpec((nb, 1, d), lambda j, b: (cond_block, 0, 5)),
                      pl.BlockSpec(memory_space=pl.ANY)],
            out_specs=tile,
            scratch_shapes=[pltpu.VMEM((tps, ROUTE_TILE, d), F32),
                            pltpu.VMEM((2, tps, ne * ROUTE_WINDOW, d), BF16), pltpu.SemaphoreType.DMA((2,)),
                            pltpu.VMEM((ROUTE_WINDOW, d), BF16), pltpu.SemaphoreType.DMA(())]),
        out_shape=jax.ShapeDtypeStruct(_tile_view(x1, nb).shape, F32),
        compiler_params=pltpu.CompilerParams(dimension_semantics=("arbitrary",),
                                             vmem_limit_bytes=VMEM_LIMIT),
        name="combine",
    )(base, pos, aff, _tile_view(x1, nb), npost, mod3, eo).reshape(t, d)


def _trunk(x3, mod3, first_cond_row, shared_cond, weights, lgs, *, rope_tabs, init, emit_state):
    (n_pre_mix, n_post_mix, n_pre_ffn, n_post_ffn, w_in, gn_g, conv_w, wro, wco, wo, wr_pad, wg, wu, wd) = weights
    b, seq, d = x3.shape
    t = b * seq
    x = x3.reshape(t, d)
    cond_row_fn = (lambda r: first_cond_row) if shared_cond else (lambda r: first_cond_row + r // seq)
    nb = 1 if shared_cond else b
    assert ROUTE_TILE % nb == 0 and (ROUTE_TILE // nb) % SLOT_ALIGN == 0 and first_cond_row % nb == 0
    proj_a, proj_p, proj_s = _inproj(x, mod3, n_pre_mix, w_in, cond_row_fn, seq, rope_tabs)
    ret = _retention(proj_a, lgs, gn_g, seq, init=init, emit_state=emit_state)
    og = ret[0]
    x1, h2, aff_t = _mix(og, proj_a, proj_p, proj_s, x, mod3, conv_w, wro, wco, wo, wr_pad, n_post_mix,
                         n_pre_ffn, seq, cond_row_fn)
    cap = CAPACITY_FACTOR * t // N_EXPERTS
    pos, base, aff = _route(aff_t, cap, nb)
    eo = _experts(_dispatch(base, h2, pos, cap, nb), wg, wu, wd)
    y = _combine(base, eo, pos, aff, x1, mod3, n_post_ffn, first_cond_row // nb, cap, nb)
    return y.reshape(b, seq, d), ret[1:]


def kernel(x_prompt, x_sample, state_ret_fwd, state_ret_bwd, c, c_ctx, w_ada, b_ada, norm_pre_mix, norm_post_mix,
           norm_pre_ffn, norm_post_ffn, w_in, ret_decay_fwd, ret_decay_bwd, ret_norm_g, conv_w, w_ret_o, w_conv_o,
           w_o, w_router, w_gate, w_up, w_down):
    depth = w_ada.shape[0]
    assert depth == 1
    dec_b, dec_seq = x_sample.shape[0], x_sample.shape[1]
    xp, xs = x_prompt, x_sample
    l = 0
    cond = jnp.zeros((N_COND_ROWS, D_MODEL), F32).at[0:dec_b].set(c).at[CTX_COND_ROW].set(c_ctx)
    mod3 = _ada(cond, w_ada[l], b_ada[l][None, :]).reshape(N_COND_ROWS, 1, 6 * D_MODEL)
    lgs = jnp.stack([-jax.nn.softplus(-ret_decay_fwd[l].astype(F32)),
                     -jax.nn.softplus(-ret_decay_bwd[l].astype(F32))])
    wr_pad = jnp.pad(w_router[l], ((0, 0), (0, ROUTER_LANES - N_EXPERTS)))
    weights = (norm_pre_mix[l][None, :], norm_post_mix[l][None, :], norm_pre_ffn[l][None, :],
               norm_post_ffn[l][None, :], w_in[l].astype(BF16), ret_norm_g[l][None, :], conv_w[l],
               w_ret_o[l].astype(BF16), w_conv_o[l].astype(BF16), w_o[l].astype(BF16), wr_pad,
               w_gate[l], w_up[l], w_down[l])
    yp, (s_f, s_b) = _trunk(xp, mod3, CTX_COND_ROW, True, weights, lgs,
                            rope_tabs=None, init=None, emit_state=True)
    ys, _ = _trunk(xs, mod3, 0, False, weights, lgs,
                   rope_tabs=_rope_tables(dec_seq), init=(state_ret_fwd, state_ret_bwd), emit_state=False)
    return (yp, ys, s_f, s_b)
```

```python
import functools

import jax
import jax.numpy as jnp
import numpy as np
from jax import lax
from jax.experimental import pallas as pl
from jax.experimental.pallas import tpu as pltpu

F32 = jnp.float32
BF16 = jnp.bfloat16

D_MODEL = 1024
N_HEADS = 8
DK = 64
DV = 128
CHUNK = 128
GRID_W = 64
N_EXPERTS = 16
CAPACITY_FACTOR = 2
D_IN_TOTAL = 8192
RMS_EPS = 1e-6
GN_EPS = 1e-5
ROPE_BASE = 10000.0
LANES = 128
BF16_SUBLANES = 16
N_COND_ROWS = 16
CTX_COND_ROW = 8
ROUTER_LANES = LANES
RET_CHUNKS_PER_STEP = 16
RET_UNROLL = 16
MIX_TILE = 1024
MIX_SUB_TILE = 512
EXPERT_ROWS_PER_STEP = 1024
EXPERT_SUB_ROWS = 512
ROUTE_TILE = 256
ROUTE_WINDOW = 64
SLOT_ALIGN = BF16_SUBLANES
DISPATCH_SLOT_ROWS = 16384
DISPATCH_TILES_PER_STEP = 8
COMBINE_TILES_PER_STEP = 4
VMEM_LIMIT = 48 * 1024 * 1024
EXPERT_VMEM_LIMIT = 56 * 1024 * 1024


def _sigmoid(x):
    return 0.5 * jnp.tanh(0.5 * x) + 0.5


def _rms(x, g):
    return x * lax.rsqrt(jnp.mean(x * x, axis=-1, keepdims=True) + RMS_EPS) * g


def _ada_kernel(c_ref, w_ref, b_ref, o_ref):
    c = c_ref[...]
    s = c * _sigmoid(c)
    o_ref[...] = jnp.dot(s, w_ref[...], preferred_element_type=F32,
                         precision=lax.Precision.HIGHEST) + b_ref[...]


def _ada(cond, w_ada, b_ada):
    n = w_ada.shape[1]
    tn = 1024
    return pl.pallas_call(
        _ada_kernel,
        grid=(n // tn,),
        in_specs=[pl.BlockSpec((N_COND_ROWS, D_MODEL), lambda j: (0, 0)),
                  pl.BlockSpec((D_MODEL, tn), lambda j: (0, j)),
                  pl.BlockSpec((1, tn), lambda j: (0, j))],
        out_specs=pl.BlockSpec((N_COND_ROWS, tn), lambda j: (0, j)),
        out_shape=jax.ShapeDtypeStruct((N_COND_ROWS, n), F32),
        compiler_params=pltpu.CompilerParams(dimension_semantics=("arbitrary",),
                                             vmem_limit_bytes=VMEM_LIMIT),
        name="ada_mod",
    )(cond, w_ada, b_ada)


def _inproj_body(x_ref, g_ref, sh_ref, sc_ref, w_ref, cos_ref, sin_ref, a_ref, p_ref, s_ref, h_ref, use_rope):
    j = pl.program_id(1)

    @pl.when(j == 0)
    def _():
        h = _rms(x_ref[...], g_ref[...]) * (1.0 + sc_ref[0]) + sh_ref[0]
        h_ref[...] = h.astype(BF16)

    lanes = 2 * DK
    width = 2 * lanes
    n_slices = w_ref.shape[1] // width
    n_qk = 2 * N_HEADS * DK

    def slice_dot(b):
        return jnp.dot(h_ref[...], w_ref[:, b * width:(b + 1) * width], preferred_element_type=F32)

    @pl.when(j == 0)
    def _():
        ci = lax.broadcasted_iota(jnp.int32, (h_ref.shape[0], lanes), 1)
        for b in range(n_slices):
            acc = slice_dot(b)
            if b * width >= n_qk:
                a_ref[:, b * width:(b + 1) * width] = acc.astype(BF16)
                continue
            for half in range(2):
                col = b * width + half * lanes
                x = acc[:, half * lanes:(half + 1) * lanes]
                if col >= N_HEADS * DK:
                    x = x * (DK ** -0.5)
                if use_rope:
                    swapped = jnp.where((ci & 31) < 16, pltpu.roll(x, lanes - 16, 1), pltpu.roll(x, 16, 1))
                    x = x * cos_ref[...] + swapped * sin_ref[...]
                a_ref[:, col:col + lanes] = x.astype(BF16)

    @pl.when(j == 1)
    def _():
        for b in range(n_slices):
            acc = slice_dot(b)
            if b < n_slices // 2:
                acc = acc * _sigmoid(acc)
            a_ref[:, b * width:(b + 1) * width] = acc.astype(BF16)

    @pl.when(j == 2)
    def _():
        for b in range(n_slices // 2):
            p_ref[:, b * width:(b + 1) * width] = (slice_dot(b) * slice_dot(b + n_slices // 2)).astype(BF16)

    @pl.when(j == 3)
    def _():
        for b in range(n_slices):
            s_ref[:, b * width:(b + 1) * width] = _sigmoid(slice_dot(b)).astype(BF16)


def _inproj_kernel(x_ref, g_ref, sh_ref, sc_ref, w_ref, *rest, use_rope):
    if use_rope:
        cos_ref, sin_ref, a_ref, p_ref, s_ref, h_ref = rest
    else:
        cos_ref = sin_ref = None
        a_ref, p_ref, s_ref, h_ref = rest
    _inproj_body(x_ref, g_ref, sh_ref, sc_ref, w_ref, cos_ref, sin_ref, a_ref, p_ref, s_ref, h_ref, use_rope)


def _inproj(x, mod3, g, w_bf16, cond_row_fn, seq, rope_tabs):
    t = x.shape[0]
    tm, tn = 1024, 2048
    assert D_IN_TOTAL == 4 * tn
    in_specs = [pl.BlockSpec((tm, D_MODEL), lambda i, j: (i, 0)),
                pl.BlockSpec((1, D_MODEL), lambda i, j: (0, 0)),
                pl.BlockSpec((1, 1, D_MODEL), lambda i, j: (cond_row_fn(i * tm), 0, 0)),
                pl.BlockSpec((1, 1, D_MODEL), lambda i, j: (cond_row_fn(i * tm), 0, 1)),
                pl.BlockSpec((D_MODEL, tn), lambda i, j: (0, j))]
    args = [x, g, mod3, mod3, w_bf16]
    if rope_tabs is not None:
        tiles_per_seq = seq // tm
        in_specs += [pl.BlockSpec((tm, 2 * DK), lambda i, j: (i % tiles_per_seq, 0))] * 2
        args += list(rope_tabs)
    return pl.pallas_call(
        functools.partial(_inproj_kernel, use_rope=rope_tabs is not None),
        grid=(t // tm, D_IN_TOTAL // tn),
        in_specs=in_specs,
        out_specs=[pl.BlockSpec((tm, tn), lambda i, j: (i, jnp.minimum(j, 1))),
                   pl.BlockSpec((tm, tn // 2), lambda i, j: (i, 0)),
                   pl.BlockSpec((tm, tn), lambda i, j: (i, 0))],
        out_shape=[jax.ShapeDtypeStruct((t, 2 * tn), BF16),
                   jax.ShapeDtypeStruct((t, tn // 2), BF16),
                   jax.ShapeDtypeStruct((t, tn), BF16)],
        scratch_shapes=[pltpu.VMEM((tm, D_MODEL), BF16)],
        compiler_params=pltpu.CompilerParams(dimension_semantics=("arbitrary", "arbitrary"),
                                             vmem_limit_bytes=VMEM_LIMIT),
        name="inproj",
    )(*args)


def _ret_kernel(*refs, nc, bb, has_init, emit_state):
    it = iter(refs)
    lg_ref = next(it)
    q_ref, k_ref, v_ref, gr_ref, gn_ref = (next(it) for _ in range(5))
    s0f_ref = s0b_ref = sf_ref = sb_ref = None
    if has_init:
        s0f_ref, s0b_ref = next(it), next(it)
    o_ref = next(it)
    if emit_state:
        sf_ref, sb_ref = next(it), next(it)
    kvf_s, kvb_s, rcat_s, p_s, o_s, r0 = (next(it) for _ in range(6))

    p = pl.program_id(1)
    lgf_a, lgf_b = lg_ref[0, 2 * p], lg_ref[0, 2 * p + 1]
    lgb_a, lgb_b = lg_ref[1, 2 * p], lg_ref[1, 2 * p + 1]

    ri = lax.broadcasted_iota(jnp.int32, (CHUNK, 2 * DK), 0)
    ci = lax.broadcasted_iota(jnp.int32, (CHUNK, 2 * DK), 1)
    lane_a = ci < DK
    rowf = ri.astype(F32)
    diff = rowf - ci.astype(F32)
    lgf_lane = jnp.where(lane_a, lgf_a, lgf_b)
    lgb_lane = jnp.where(lane_a, lgb_a, lgb_b)
    xi_f = jnp.exp(lgf_lane * (rowf + 1.0))
    xi_b = jnp.exp(lgb_lane * (CHUNK - rowf))
    zeta_f = jnp.exp(lgf_lane * (CHUNK - 1.0 - rowf))
    zeta_b = jnp.exp(lgb_lane * rowf)

    def decay_matrix(lgf, lgb):
        return jnp.where(diff > 0, jnp.exp(lgf * diff),
                         jnp.where(diff < 0, jnp.exp(lgb * (-diff)), 2.0))

    dm_a = decay_matrix(lgf_a, lgb_a)
    dm_b = decay_matrix(lgf_b, lgb_b)

    r2 = lax.broadcasted_iota(jnp.int32, (2 * DK, 2 * DV), 0)
    c2 = lax.broadcasted_iota(jnp.int32, (2 * DK, 2 * DV), 1)
    top = r2 < DK
    blk = (top == (c2 < DV)).astype(F32)
    cd_f = jnp.exp(jnp.where(top, lgf_a, lgf_b) * float(CHUNK)) * blk
    cd_b = jnp.exp(jnp.where(top, lgb_a, lgb_b) * float(CHUNK)) * blk

    def chunk_rows(c):
        return pl.ds(pl.multiple_of(c * CHUNK, CHUNK), CHUNK)

    n_chunks = bb * nc
    unroll = RET_UNROLL

    def kv_body(c, carry):
        rows = chunk_rows(c)
        k = k_ref[rows, :].astype(F32)
        kz_t = jnp.concatenate([k * zeta_f, k * zeta_b], axis=1).T.astype(BF16)
        kv = jnp.dot(kz_t, v_ref[rows, :], preferred_element_type=F32)
        kvf_s[c] = kv[0:2 * DK] * blk
        kvb_s[c] = kv[2 * DK:4 * DK] * blk
        return carry

    def score_body(c, carry):
        rows = chunk_rows(c)
        q16 = q_ref[rows, :]
        zero = jnp.zeros_like(q16)
        q_ab = jnp.concatenate([jnp.where(lane_a, q16, zero), jnp.where(lane_a, zero, q16)], axis=0)
        s = lax.dot_general(q_ab, k_ref[rows, :], (((1,), (1,)), ((), ())), preferred_element_type=F32)
        p_s[c, :, 0:CHUNK] = (s[0:CHUNK] * dm_a).astype(BF16)
        p_s[c, :, CHUNK:2 * CHUNK] = (s[CHUNK:2 * CHUNK] * dm_b).astype(BF16)
        return carry

    lax.fori_loop(0, n_chunks, lambda c, carry: score_body(c, kv_body(c, carry)), 0, unroll=unroll)

    def load_state(s_ref, s):
        r0[...] = jnp.zeros_like(r0)
        r0[0:DK, 0:DV] = s_ref[s, 0, 0].astype(F32)
        r0[DK:2 * DK, DV:2 * DV] = s_ref[s, 0, 1].astype(F32)
        return r0[...]

    for s in range(bb):
        def fwd_body(n, rf, s=s):
            c = s * nc + n
            rcat_s[c, 0:2 * DK, :] = rf.astype(BF16)
            return cd_f * rf + kvf_s[c]

        def bwd_body(t, rb, s=s):
            c = s * nc + nc - 1 - t
            rcat_s[c, 2 * DK:4 * DK, :] = rb.astype(BF16)
            return cd_b * rb + kvb_s[c]

        zeros = jnp.zeros((2 * DK, 2 * DV), F32)
        rf_fin = lax.fori_loop(0, nc, fwd_body, load_state(s0f_ref, s) if has_init else zeros)
        rb_fin = lax.fori_loop(0, nc, bwd_body, load_state(s0b_ref, s) if has_init else zeros)
        if emit_state:
            for s_ref, r in ((sf_ref, rf_fin), (sb_ref, rb_fin)):
                s_ref[s, 0, 0] = r[0:DK, 0:DV]
                s_ref[s, 0, 1] = r[DK:2 * DK, DV:2 * DV]

    gn = gn_ref[...]

    hr = lax.broadcasted_iota(jnp.int32, (2 * DV, 2 * DV), 0)
    hc = lax.broadcasted_iota(jnp.int32, (2 * DV, 2 * DV), 1)
    head_mean = jnp.where((hr < DV) == (hc < DV), 1.0 / DV, 0.0).astype(BF16)
    v_lane_a = lax.broadcasted_iota(jnp.int32, (CHUNK, 2 * DV), 1) < DV

    def value_body(c, carry):
        rows = chunk_rows(c)
        q = q_ref[rows, :].astype(F32)
        v = v_ref[rows, :]
        zero = jnp.zeros_like(v)
        v_bd = jnp.concatenate([jnp.where(v_lane_a, v, zero), jnp.where(v_lane_a, zero, v)], axis=0)
        qx = jnp.concatenate([(q * xi_f).astype(BF16), (q * xi_b).astype(BF16)], axis=1)
        o_s[c] = (jnp.dot(p_s[c], v_bd, preferred_element_type=F32)
                  + jnp.dot(qx, rcat_s[c], preferred_element_type=F32))
        return carry

    lax.fori_loop(0, n_chunks, value_body, 0, unroll=unroll)

    def gate_body(c, carry):
        rows = chunk_rows(c)
        o = o_s[c]
        d = o - jnp.dot(o.astype(BF16), head_mean, preferred_element_type=F32)
        var = jnp.dot((d * d).astype(BF16), head_mean, preferred_element_type=F32)
        y = d * lax.rsqrt(var + GN_EPS)
        o_ref[rows, :] = (gr_ref[rows, :].astype(F32) * (y * gn)).astype(BF16)
        return carry

    lax.fori_loop(0, n_chunks, gate_body, 0, unroll=unroll)


def _retention(proj, lgs, gn_g, seq, *, init=None, emit_state):
    t = proj.shape[0]
    b = t // seq
    nc = seq // CHUNK
    bb = max(1, RET_CHUNKS_PER_STEP // nc)
    rows = bb * seq
    has_init = init is not None
    pairs = N_HEADS // 2
    qk_blocks = (N_HEADS * DK) // (2 * DK)
    in_specs = [pl.BlockSpec(memory_space=pltpu.SMEM),
                pl.BlockSpec((rows, 2 * DK), lambda i, p: (i, p)),
                pl.BlockSpec((rows, 2 * DK), lambda i, p: (i, qk_blocks + p)),
                pl.BlockSpec((rows, 2 * DV), lambda i, p: (i, qk_blocks + p)),
                pl.BlockSpec((rows, 2 * DV), lambda i, p: (i, 2 * qk_blocks + p)),
                pl.BlockSpec((1, 2 * DV), lambda i, p: (0, p))]
    args = [lgs, proj, proj, proj, proj, gn_g]
    state_spec = pl.BlockSpec((bb, 1, 2, DK, DV), lambda i, p: (i, 0, p, 0, 0))
    if has_init:
        in_specs += [state_spec, state_spec]
        args += list(init)
    out_specs = [pl.BlockSpec((rows, 2 * DV), lambda i, p: (i, p))]
    out_shape = [jax.ShapeDtypeStruct((t, N_HEADS * DV), BF16)]
    if emit_state:
        out_specs += [state_spec, state_spec]
        out_shape += [jax.ShapeDtypeStruct((b, 1, N_HEADS, DK, DV), F32)] * 2
    return pl.pallas_call(
        functools.partial(_ret_kernel, nc=nc, bb=bb, has_init=has_init, emit_state=emit_state),
        grid=(b // bb, pairs),
        in_specs=in_specs,
        out_specs=out_specs,
        out_shape=out_shape,
        scratch_shapes=[pltpu.VMEM((bb * nc, 2 * DK, 2 * DV), F32),
                        pltpu.VMEM((bb * nc, 2 * DK, 2 * DV), F32),
                        pltpu.VMEM((bb * nc, 4 * DK, 2 * DV), BF16),
                        pltpu.VMEM((bb * nc, CHUNK, 2 * CHUNK), BF16),
                        pltpu.VMEM((bb * nc, CHUNK, 2 * DV), F32),
                        pltpu.VMEM((2 * DK, 2 * DV), F32)],
        compiler_params=pltpu.CompilerParams(dimension_semantics=("arbitrary", "arbitrary"),
                                             vmem_limit_bytes=VMEM_LIMIT),
        name="retention",
    )(*args)


def _rope_tables(seq):
    n_freq = DK // 4
    pos = np.arange(seq)
    inv = jnp.asarray(ROPE_BASE, F32) ** (-jnp.arange(n_freq, dtype=F32) / n_freq)
    ang_r = jnp.asarray(pos // GRID_W, F32)[:, None] * inv
    ang_c = jnp.asarray(pos % GRID_W, F32)[:, None] * inv
    cos = jnp.concatenate([jnp.cos(ang_r)] * 2 + [jnp.cos(ang_c)] * 2, axis=1)
    sin = jnp.concatenate([-jnp.sin(ang_r), jnp.sin(ang_r), -jnp.sin(ang_c), jnp.sin(ang_c)], axis=1)
    return jnp.tile(cos, (1, 2)), jnp.tile(sin, (1, 2))


def _mix_kernel(og_ref, cb_ref, p_ref, pp_ref, pn_ref, sa_ref, sb_ref, x_ref,
                cw_ref, wro_ref, wco_ref, wo_ref, wr_ref, npost_ref, npre_ref, g1_ref, sh2_ref, sc2_ref,
                x1_ref, h2_ref, aff_ref, *, tm, sub, seq):
    i = pl.program_id(0)
    halo = pp_ref.shape[0]
    wr = wr_ref[...]
    w_hi = wr.astype(BF16)
    w_lo = (wr - w_hi.astype(F32)).astype(BF16)
    w_hi_lo = jnp.concatenate([w_hi, w_lo], axis=1)
    row = lax.broadcasted_iota(jnp.int32, (sub, 1), 0)
    lane = lax.broadcasted_iota(jnp.int32, (sub, ROUTER_LANES), 1)

    for s in range(tm // sub):
        r0 = s * sub
        rows = slice(r0, r0 + sub)
        prev_row = (pp_ref[halo - 1:halo, :] if s == 0 else p_ref[r0 - 1:r0, :]).astype(F32)
        next_row = (pn_ref[0:1, :] if r0 + sub == tm else p_ref[r0 + sub:r0 + sub + 1, :]).astype(F32)
        prod = p_ref[rows, :].astype(F32)
        if seq % sub == 0:
            prev_row = prev_row * jnp.where((i * tm + r0) % seq != 0, 1.0, 0.0)
            next_row = next_row * jnp.where((i * tm + r0 + sub) % seq != 0, 1.0, 0.0)
        up = jnp.where(row == 0, prev_row, pltpu.roll(prod, 1, 0))
        dn = jnp.where(row == sub - 1, next_row, pltpu.roll(prod, sub - 1, 0))
        if seq % sub != 0:
            token = i * tm + r0 + row
            in_seq = token & (seq - 1) if seq & (seq - 1) == 0 else token % seq
            up = jnp.where(in_seq == 0, 0.0, up)
            dn = jnp.where(in_seq == seq - 1, 0.0, dn)
        u = up * cw_ref[0:1, :] + prod * cw_ref[1:2, :] + dn * cw_ref[2:3, :]
        y_conv = jnp.dot((cb_ref[rows, :].astype(F32) * u).astype(BF16), wco_ref[...],
                         preferred_element_type=F32)
        y_ret = jnp.dot(og_ref[rows, :], wro_ref[...], preferred_element_type=F32)
        merged = sa_ref[rows, :].astype(F32) * y_ret + sb_ref[rows, :].astype(F32) * y_conv
        m = jnp.dot(merged.astype(BF16), wo_ref[...], preferred_element_type=F32)
        x1 = x_ref[rows, :] + g1_ref[0] * _rms(m, npost_ref[...])
        x1_ref[rows, :] = x1
        h2 = _rms(x1, npre_ref[...]) * (1.0 + sc2_ref[0]) + sh2_ref[0]
        h_hi = h2.astype(BF16)
        h2_ref[rows, :] = h_hi
        h_lo = (h2 - h_hi.astype(F32)).astype(BF16)
        hi_terms = jnp.dot(h_hi, w_hi_lo, preferred_element_type=F32)
        logits = (hi_terms[:, 0:ROUTER_LANES] + hi_terms[:, ROUTER_LANES:2 * ROUTER_LANES]
                  + jnp.dot(h_lo, w_hi, preferred_element_type=F32))
        logits = jnp.where(lane < N_EXPERTS, logits, -jnp.inf)
        e = jnp.exp(logits - jnp.max(logits, axis=-1, keepdims=True))
        aff = e / jnp.sum(e, axis=-1, keepdims=True)
        aff_ref[:, rows] = aff.T[0:N_EXPERTS, :]


def _mix(og, proj_a, proj_p, proj_s, x, mod3, conv_w, wro, wco, wo, wr_pad, npost, npre, seq, cond_row_fn):
    t = x.shape[0]
    conv_b_block = (2 * N_HEADS * DK + 2 * N_HEADS * DV) // D_MODEL
    tm = MIX_TILE
    halo = BF16_SUBLANES
    hb = tm // halo
    last_halo = t // halo - 1
    col = lambda c: (lambda i: (i, c))
    row_vec = pl.BlockSpec((1, D_MODEL), lambda i: (0, 0))
    wspec = pl.BlockSpec((D_MODEL, D_MODEL), lambda i: (0, 0))
    modspec = lambda c: pl.BlockSpec((1, 1, D_MODEL), lambda i: (cond_row_fn(i * tm), 0, c))
    tile = lambda c: pl.BlockSpec((tm, D_MODEL), col(c))
    prev = lambda c: pl.BlockSpec((halo, D_MODEL), lambda i: (jnp.maximum(i * hb - 1, 0), c))
    nxt = lambda c: pl.BlockSpec((halo, D_MODEL), lambda i: (jnp.minimum((i + 1) * hb, last_halo), c))
    return pl.pallas_call(
        functools.partial(_mix_kernel, tm=tm, sub=MIX_SUB_TILE, seq=seq),
        grid=(t // tm,),
        in_specs=[tile(0), tile(conv_b_block), tile(0), prev(0), nxt(0), tile(0), tile(1),
                  tile(0),
                  pl.BlockSpec((3, D_MODEL), lambda i: (0, 0)), wspec, wspec, wspec,
                  pl.BlockSpec((D_MODEL, ROUTER_LANES), lambda i: (0, 0)),
                  row_vec, row_vec, modspec(2), modspec(3), modspec(4)],
        out_specs=[tile(0), tile(0), pl.BlockSpec((N_EXPERTS, tm), lambda i: (0, i))],
        out_shape=[jax.ShapeDtypeStruct((t, D_MODEL), F32),
                   jax.ShapeDtypeStruct((t, D_MODEL), BF16),
                   jax.ShapeDtypeStruct((N_EXPERTS, t), F32)],
        compiler_params=pltpu.CompilerParams(dimension_semantics=("arbitrary",),
                                             vmem_limit_bytes=VMEM_LIMIT),
        name="mix_out",
    )(og, proj_a, proj_p, proj_p, proj_p, proj_s, proj_s, x,
      conv_w, wro, wco, wo, wr_pad, npost, npre, mod3, mod3, mod3)


def _expert_kernel(xs_ref, wg_ref, wu_ref, wd_ref, o_ref, *, sub):
    wg = wg_ref[0].astype(BF16)
    wu = wu_ref[0].astype(BF16)
    wd = wd_ref[0].astype(BF16)
    for r in range(xs_ref.shape[1] // sub):
        rows = slice(r * sub, (r + 1) * sub)
        x = xs_ref[0, rows, :]
        g = jnp.dot(x, wg, preferred_element_type=F32)
        u = jnp.dot(x, wu, preferred_element_type=F32)
        hid = (g * _sigmoid(g) * u).astype(BF16)
        o_ref[0, rows, :] = jnp.dot(hid, wd, preferred_element_type=F32).astype(BF16)


def _experts(xs, wg, wu, wd):
    e, cap, d = xs.shape
    ff = wg.shape[2]
    tr = min(cap, EXPERT_ROWS_PER_STEP)
    return pl.pallas_call(
        functools.partial(_expert_kernel, sub=EXPERT_SUB_ROWS),
        grid=(e, cap // tr),
        in_specs=[pl.BlockSpec((1, tr, d), lambda a, r: (a, r, 0)),
                  pl.BlockSpec((1, d, ff), lambda a, r: (a, 0, 0)),
                  pl.BlockSpec((1, d, ff), lambda a, r: (a, 0, 0)),
                  pl.BlockSpec((1, ff, d), lambda a, r: (a, 0, 0))],
        out_specs=pl.BlockSpec((1, tr, d), lambda a, r: (a, r, 0)),
        out_shape=jax.ShapeDtypeStruct((e, cap, d), BF16),
        compiler_params=pltpu.CompilerParams(dimension_semantics=("arbitrary", "arbitrary"),
                                             vmem_limit_bytes=EXPERT_VMEM_LIMIT),
        name="experts",
    )(xs, wg, wu, wd)


def _block_ranks(mask):
    ne, nblk, lanes = mask.shape
    li = lax.broadcasted_iota(jnp.int32, (lanes, lanes), 0)
    lj = lax.broadcasted_iota(jnp.int32, (lanes, lanes), 1)
    incl_lanes = jnp.where(li <= lj, 1.0, 0.0).astype(BF16)
    all_lanes = jnp.ones((lanes, lanes), BF16)
    bi = lax.broadcasted_iota(jnp.int32, (nblk, nblk), 0)
    bj = lax.broadcasted_iota(jnp.int32, (nblk, nblk), 1)
    earlier_blocks = jnp.where(bj < bi, 1.0, 0.0).astype(BF16)
    m = mask.reshape(ne * nblk, lanes).astype(BF16)
    incl = jnp.dot(m, incl_lanes, preferred_element_type=F32).reshape(ne, nblk, lanes)
    tot = jnp.dot(m, all_lanes, preferred_element_type=F32).reshape(ne, nblk, lanes).astype(BF16)
    excl = jnp.stack([jnp.dot(earlier_blocks, tot[i], preferred_element_type=F32) for i in range(ne)])
    return excl, incl


def _select_kernel(aff_ref, *out_refs, cap):
    sel_ref, slot_refs = (out_refs[0], ()) if len(out_refs) == 1 else (None, out_refs)
    ne, nblk, lanes = aff_ref.shape
    aff = aff_ref[...]

    def count(mask):
        m = jnp.where(mask, 1.0, 0.0)
        return jnp.sum(jnp.sum(m, axis=1, keepdims=True), axis=2, keepdims=True)

    def as_float(bits):
        return lax.bitcast_convert_type(bits, F32)

    def bit_step(it, thr_bits):
        cand = thr_bits | jnp.left_shift(jnp.int32(1), 30 - it)
        return jnp.where(count(aff >= as_float(cand)) >= cap, cand, thr_bits)

    thr = as_float(lax.fori_loop(0, 31, bit_step, jnp.zeros((ne, 1, 1), jnp.int32)))
    gt = aff > thr
    eq = aff == thr
    need = cap - count(gt)
    ex_eq, in_eq = _block_ranks(jnp.where(eq, 1.0, 0.0))
    sel = jnp.where(gt | (eq & (ex_eq + in_eq - 1.0 < need)), 1.0, 0.0)
    if slot_refs:
        _write_slots(sel, *slot_refs)
    else:
        sel_ref[...] = sel


def _write_slots(sel, pos_ref, excl_ref):
    ex_sel, in_sel = _block_ranks(sel)
    pos_ref[...] = jnp.where(sel > 0.0, ex_sel + in_sel - 1.0, -1.0).astype(jnp.int32)
    excl_ref[...] = ex_sel.astype(jnp.int32)


def _slot_kernel(sel_ref, pos_ref, excl_ref):
    _write_slots(sel_ref[...], pos_ref, excl_ref)


def _to_tile_order(a, nb):
    if nb == 1:
        return a
    ne, t = a.shape
    r = ROUTE_TILE // nb
    return a.reshape(ne, nb, t // (nb * r), r).transpose(0, 2, 1, 3).reshape(ne, t)


def _route(aff_t, cap, nb):
    ne, t = aff_t.shape
    nblk = t // LANES
    shape = (ne, nblk, LANES)
    full = pl.BlockSpec(shape, lambda i: (0, 0, 0))
    params = pltpu.CompilerParams(dimension_semantics=("arbitrary",), vmem_limit_bytes=VMEM_LIMIT)
    slots = dict(out_specs=[full, full], out_shape=[jax.ShapeDtypeStruct(shape, jnp.int32)] * 2)
    if nb == 1:
        pos, excl = pl.pallas_call(
            functools.partial(_select_kernel, cap=cap),
            grid=(1,), in_specs=[full], compiler_params=params, name="route", **slots,
        )(aff_t.reshape(shape))
    else:
        sel = pl.pallas_call(
            functools.partial(_select_kernel, cap=cap),
            grid=(1,), in_specs=[full], out_specs=full,
            out_shape=jax.ShapeDtypeStruct(shape, F32),
            compiler_params=params, name="route_select",
        )(aff_t.reshape(shape))
        sel = _to_tile_order(sel.reshape(ne, t), nb)
        pos, excl = pl.pallas_call(
            _slot_kernel,
            grid=(1,), in_specs=[full], compiler_params=params, name="route_slots", **slots,
        )(sel.reshape(shape))
    base = jnp.concatenate([excl[:, :, 0], jnp.full((ne, 1), cap, jnp.int32)], axis=1)
    return pos.reshape(ne, t), base, _to_tile_order(aff_t, nb)


def _window(base_ref, e, blk, w, cap):
    lo = base_ref[e, blk] // SLOT_ALIGN + w * (ROUTE_WINDOW // SLOT_ALIGN)
    return lo * SLOT_ALIGN, jnp.minimum(lo, (cap - ROUTE_WINDOW) // SLOT_ALIGN) * SLOT_ALIGN


def _n_windows(base_ref, e, blk, cap):
    lo, _ = _window(base_ref, e, blk, 0, cap)
    end = base_ref[e, blk + ROUTE_TILE // LANES]
    return (end - lo + ROUTE_WINDOW - 1) // ROUTE_WINDOW


def _dispatch_kernel(base_ref, h_ref, pos_ref, xs_ref, *, cap):
    g, j = pl.program_id(0), pl.program_id(1)
    group = xs_ref.shape[0]

    @pl.when(j == 0)
    def _():
        xs_ref[...] = jnp.zeros_like(xs_ref)

    ri = lax.broadcasted_iota(jnp.int32, (ROUTE_WINDOW, ROUTE_TILE), 0)

    def onehot(prow, lo, off, first):
        hit = prow - off == ri
        if not first:
            hit = hit & (prow >= lo)
        return jnp.where(hit, 1.0, 0.0).astype(BF16)

    def add_rows(ge, off, rows):
        sl = pl.ds(pl.multiple_of(off, SLOT_ALIGN), ROUTE_WINDOW)
        xs_ref[ge, sl, :] = xs_ref[ge, sl, :] + rows.astype(BF16)

    for sub in range(DISPATCH_TILES_PER_STEP):
        blk = (j * DISPATCH_TILES_PER_STEP + sub) * (ROUTE_TILE // LANES)
        tok = slice(sub * ROUTE_TILE, (sub + 1) * ROUTE_TILE)
        h = h_ref[:, sub].reshape(ROUTE_TILE, h_ref.shape[-1])
        prows, offs = [], []
        for ge in range(group):
            e = g * group + ge
            prows.append(pos_ref[pl.ds(e, 1), tok])
            offs.append(_window(base_ref, e, blk, 0, cap))
        sel = jnp.concatenate([onehot(prows[ge], *offs[ge], True) for ge in range(group)], axis=0)
        rows = jnp.dot(sel, h, preferred_element_type=F32)
        for ge in range(group):
            add_rows(ge, offs[ge][1], rows[ge * ROUTE_WINDOW:(ge + 1) * ROUTE_WINDOW])

        for ge in range(group):
            e = g * group + ge

            def extra(w, carry, ge=ge, e=e, blk=blk, h=h, prow=prows[ge]):
                lo, off = _window(base_ref, e, blk, w, cap)
                add_rows(ge, off, jnp.dot(onehot(prow, lo, off, False), h, preferred_element_type=F32))
                return carry

            lax.fori_loop(1, _n_windows(base_ref, e, blk, cap), extra, 0)


def _tile_view(a, nb):
    t, d = a.shape
    r = ROUTE_TILE // nb
    return a.reshape(nb, t // (nb * r), r, d)


def _dispatch(base, h2, pos, cap, nb):
    t, d = h2.shape
    ne = pos.shape[0]
    step = ROUTE_TILE * DISPATCH_TILES_PER_STEP
    group = min(ne, DISPATCH_SLOT_ROWS // cap)
    out_spec = pl.BlockSpec((group, cap, d), lambda g, j, b: (g, 0, 0), pipeline_mode=pl.Buffered(1))
    return pl.pallas_call(
        functools.partial(_dispatch_kernel, cap=cap),
        grid_spec=pltpu.PrefetchScalarGridSpec(
            num_scalar_prefetch=1,
            grid=(ne // group, t // step),
            in_specs=[pl.BlockSpec((nb, DISPATCH_TILES_PER_STEP, ROUTE_TILE // nb, d),
                                   lambda g, j, b: (0, j, 0, 0)),
                      pl.BlockSpec((ne, step), lambda g, j, b: (0, j))],
            out_specs=out_spec),
        out_shape=jax.ShapeDtypeStruct((ne, cap, d), BF16),
        compiler_params=pltpu.CompilerParams(dimension_semantics=("arbitrary", "arbitrary"),
                                             vmem_limit_bytes=VMEM_LIMIT),
        name="dispatch",
    )(base, _tile_view(h2, nb), pos)


def _combine_kernel(base_ref, pos_ref, aff_ref, x1_ref, g_ref, g2_ref, eo_ref, o_ref,
                    f_ref, win_buf, win_sem, extra_buf, extra_sem, *, cap, n_steps):
    j = pl.program_id(0)
    blocks_per_tile = ROUTE_TILE // LANES
    slot = lax.rem(j, 2)
    ri = lax.broadcasted_iota(jnp.int32, (ROUTE_WINDOW, ROUTE_TILE), 0)
    tn = (((0,), (0,)), ((), ()))

    def window_copy(e, off, dst, sem):
        return pltpu.make_async_copy(eo_ref.at[e, pl.ds(pl.multiple_of(off, SLOT_ALIGN), ROUTE_WINDOW), :],
                                     dst, sem)

    def window_slot(buf_slot, s, e):
        return win_buf.at[buf_slot, s, pl.ds(e * ROUTE_WINDOW, ROUTE_WINDOW), :]

    def first_window_copies(step, buf_slot):
        return [window_copy(e, _window(base_ref, e, (step * COMBINE_TILES_PER_STEP + s) * blocks_per_tile, 0, cap)[1],
                            window_slot(buf_slot, s, e), win_sem.at[buf_slot])
                for s in range(COMBINE_TILES_PER_STEP) for e in range(N_EXPERTS)]

    def wait_windows(buf_slot):
        for s in range(COMBINE_TILES_PER_STEP):
            for e in range(N_EXPERTS):
                window_copy(e, 0, window_slot(buf_slot, s, e), win_sem.at[buf_slot]).wait()

    @pl.when(j == 0)
    def _():
        for cp in first_window_copies(0, 0):
            cp.start()

    for cp in first_window_copies(jnp.minimum(j + 1, n_steps - 1), 1 - slot):
        cp.start()

    def weights(tok, e, lo, off, first):
        prow = pos_ref[e:e + 1, tok]
        hit = prow - off == ri
        if not first:
            hit = hit & (prow >= lo)
        return jnp.where(hit, aff_ref[e:e + 1, tok], 0.0).astype(BF16)

    qs = []
    for s in range(COMBINE_TILES_PER_STEP):
        blk = (j * COMBINE_TILES_PER_STEP + s) * blocks_per_tile
        tok = slice(s * ROUTE_TILE, (s + 1) * ROUTE_TILE)
        qs.append(jnp.concatenate([weights(tok, e, *_window(base_ref, e, blk, 0, cap), True)
                                   for e in range(N_EXPERTS)], axis=0))
    wait_windows(slot)

    nb, _, r, d = x1_ref.shape
    for s in range(COMBINE_TILES_PER_STEP):
        blk = (j * COMBINE_TILES_PER_STEP + s) * blocks_per_tile
        tok = slice(s * ROUTE_TILE, (s + 1) * ROUTE_TILE)
        f_ref[s] = lax.dot_general(qs[s], win_buf[slot, s], tn, preferred_element_type=F32)

        n_win = [_n_windows(base_ref, e, blk, cap) for e in range(N_EXPERTS)]

        @pl.when(functools.reduce(jnp.maximum, n_win) > 1)
        def _(s=s, blk=blk, tok=tok, n_win=n_win):
            for e in range(N_EXPERTS):
                def extra(w, carry, e=e):
                    lo, off = _window(base_ref, e, blk, w, cap)
                    cp = window_copy(e, off, extra_buf, extra_sem)
                    cp.start()
                    cp.wait()
                    f_ref[s] += lax.dot_general(weights(tok, e, lo, off, False), extra_buf[...], tn,
                                                preferred_element_type=F32)
                    return carry

                lax.fori_loop(1, n_win[e], extra, 0)

        f = f_ref[s].reshape(nb, r, d)
        o_ref[:, s] = x1_ref[:, s] + g2_ref[...] * _rms(f, g_ref[...])

    @pl.when(j == n_steps - 1)
    def _():
        wait_windows(1 - slot)


def _combine(base, eo, pos, aff, x1, mod3, npost, cond_block, cap, nb):
    t, d = x1.shape
    ne = pos.shape[0]
    tps = COMBINE_TILES_PER_STEP
    n_steps = t // (ROUTE_TILE * tps)
    tile = pl.BlockSpec((nb, tps, ROUTE_TILE // nb, d), lambda j, b: (0, j, 0, 0))
    etile = pl.BlockSpec((ne, tps * ROUTE_TILE), lambda j, b: (0, j))
    return pl.pallas_call(
        functools.partial(_combine_kernel, cap=cap, n_steps=n_steps),
        grid_spec=pltpu.PrefetchScalarGridSpec(
            num_scalar_prefetch=1,
            grid=(n_steps,),
            in_specs=[etile, etile, tile,
                      pl.BlockSpec((1, d), lambda j, b: (0, 0)),
                      pl.BlockSpec((nb, 1, d), lambda j, b: (cond_block, 0, 5)),
                      pl.BlockSpec(memory_space=pl.ANY)],
            out_specs=tile,
            scratch_shapes=[pltpu.VMEM((tps, ROUTE_TILE, d), F32),
                            pltpu.VMEM((2, tps, ne * ROUTE_WINDOW, d), BF16), pltpu.SemaphoreType.DMA((2,)),
                            pltpu.VMEM((ROUTE_WINDOW, d), BF16), pltpu.SemaphoreType.DMA(())]),
        out_shape=jax.ShapeDtypeStruct(_tile_view(x1, nb).shape, F32),
        compiler_params=pltpu.CompilerParams(dimension_semantics=("arbitrary",),
                                             vmem_limit_bytes=VMEM_LIMIT),
        name="combine",
    )(base, pos, aff, _tile_view(x1, nb), npost, mod3, eo).reshape(t, d)


def _trunk(x3, mod3, first_cond_row, shared_cond, weights, lgs, *, rope_tabs, init, emit_state):
    (n_pre_mix, n_post_mix, n_pre_ffn, n_post_ffn, w_in, gn_g, conv_w, wro, wco, wo, wr_pad, wg, wu, wd) = weights
    b, seq, d = x3.shape
    t = b * seq
    x = x3.reshape(t, d)
    cond_row_fn = (lambda r: first_cond_row) if shared_cond else (lambda r: first_cond_row + r // seq)
    nb = 1 if shared_cond else b
    assert ROUTE_TILE % nb == 0 and (ROUTE_TILE // nb) % SLOT_ALIGN == 0 and first_cond_row % nb == 0
    proj_a, proj_p, proj_s = _inproj(x, mod3, n_pre_mix, w_in, cond_row_fn, seq, rope_tabs)
    ret = _retention(proj_a, lgs, gn_g, seq, init=init, emit_state=emit_state)
    og = ret[0]
    x1, h2, aff_t = _mix(og, proj_a, proj_p, proj_s, x, mod3, conv_w, wro, wco, wo, wr_pad, n_post_mix,
                         n_pre_ffn, seq, cond_row_fn)
    cap = CAPACITY_FACTOR * t // N_EXPERTS
    pos, base, aff = _route(aff_t, cap, nb)
    eo = _experts(_dispatch(base, h2, pos, cap, nb), wg, wu, wd)
    y = _combine(base, eo, pos, aff, x1, mod3, n_post_ffn, first_cond_row // nb, cap, nb)
    return y.reshape(b, seq, d), ret[1:]


def kernel(x_prompt, x_sample, state_ret_fwd, state_ret_bwd, c, c_ctx, w_ada, b_ada, norm_pre_mix, norm_post_mix,
           norm_pre_ffn, norm_post_ffn, w_in, ret_decay_fwd, ret_decay_bwd, ret_norm_g, conv_w, w_ret_o, w_conv_o,
           w_o, w_router, w_gate, w_up, w_down):
    depth = w_ada.shape[0]
    assert depth == 1
    dec_b, dec_seq = x_sample.shape[0], x_sample.shape[1]
    xp, xs = x_prompt, x_sample
    l = 0
    cond = jnp.zeros((N_COND_ROWS, D_MODEL), F32).at[0:dec_b].set(c).at[CTX_COND_ROW].set(c_ctx)
    mod3 = _ada(cond, w_ada[l], b_ada[l][None, :]).reshape(N_COND_ROWS, 1, 6 * D_MODEL)
    lgs = jnp.stack([-jax.nn.softplus(-ret_decay_fwd[l].astype(F32)),
                     -jax.nn.softplus(-ret_decay_bwd[l].astype(F32))])
    wr_pad = jnp.pad(w_router[l], ((0, 0), (0, ROUTER_LANES - N_EXPERTS)))
    weights = (norm_pre_mix[l][None, :], norm_post_mix[l][None, :], norm_pre_ffn[l][None, :],
               norm_post_ffn[l][None, :], w_in[l].astype(BF16), ret_norm_g[l][None, :], conv_w[l],
               w_ret_o[l].astype(BF16), w_conv_o[l].astype(BF16), w_o[l].astype(BF16), wr_pad,
               w_gate[l], w_up[l], w_down[l])
    yp, (s_f, s_b) = _trunk(xp, mod3, CTX_COND_ROW, True, weights, lgs,
                            rope_tabs=None, init=None, emit_state=True)
    ys, _ = _trunk(xs, mod3, 0, False, weights, lgs,
                   rope_tabs=_rope_tables(dec_seq), init=(state_ret_fwd, state_ret_bwd), emit_state=False)
    return (yp, ys, s_f, s_b)
```
